```python
import math
import jax, jax.numpy as jnp
from jax import lax
import numpy as np

D_MODEL = 1024
BATCH = 8
SEQ = 8192
DEPTH = 4

N_A_LAYERS = DEPTH // 2
N_B_LAYERS = DEPTH - N_A_LAYERS
CONV_WIDTH = 31
N_HEADS = 16
HEAD_DIM = D_MODEL // N_HEADS
D_FF = 4 * D_MODEL
D_PLE = 256
Q_BLOCK = 128
EPS = 1e-6
NEG_BIG = -1e30

kernel_name = "yoco_conformer_fox_hybrid"


def rmsnorm(x, g):
    xf = x.astype(jnp.float32)
    y = xf * lax.rsqrt(jnp.mean(xf * xf, axis=-1, keepdims=True) + EPS)
    return (y * g.astype(jnp.float32)).astype(x.dtype)


def layernorm(x, g, b):
    xf = x.astype(jnp.float32)
    mu = jnp.mean(xf, axis=-1, keepdims=True)
    xc = xf - mu
    var = jnp.mean(xc * xc, axis=-1, keepdims=True)
    y = xc * lax.rsqrt(var + EPS)
    return (y * g.astype(jnp.float32) + b.astype(jnp.float32)).astype(x.dtype)


def conformer_conv(hn, w_pw1, b_pw1, w_dw, b_dw, ln_g, ln_b, w_pw2, b_pw2):
    u = hn @ w_pw1 + b_pw1
    a, g = jnp.split(u, 2, axis=-1)
    u = a * jax.nn.sigmoid(g)
    kern = w_dw[:, None, :].astype(u.dtype)
    u = lax.conv_general_dilated(
        u, kern, window_strides=(1,), padding=((CONV_WIDTH - 1, 0),),
        dimension_numbers=("NWC", "WIO", "NWC"),
        feature_group_count=D_MODEL) + b_dw
    u = layernorm(u, ln_g, ln_b)
    u = jax.nn.silu(u)
    return u @ w_pw2 + b_pw2


def shared_kv(h, kv_norm, w_kvf, b_f):
    B, S, _ = h.shape
    u = rmsnorm(h, kv_norm) @ w_kvf
    k = u[..., :D_MODEL].reshape(B, S, N_HEADS, HEAD_DIM)
    v = u[..., D_MODEL:2 * D_MODEL].reshape(B, S, N_HEADS, HEAD_DIM)
    f_logit = (u[..., 2 * D_MODEL:] + b_f).astype(jnp.float32)
    log_f = jax.nn.log_sigmoid(f_logit)
    c = jnp.cumsum(log_f, axis=1)
    return k, v, jnp.transpose(c, (0, 2, 1))


def fox_attention(hn, w_q, w_o, k, v, c_bhs):
    B, S, _ = hn.shape
    nb = S // Q_BLOCK
    q = (hn @ w_q).reshape(B, S, N_HEADS, HEAD_DIM) * (HEAD_DIM ** -0.5)
    qb = jnp.transpose(q.reshape(B, nb, Q_BLOCK, N_HEADS, HEAD_DIM), (1, 0, 2, 3, 4))
    cb = jnp.transpose(c_bhs.reshape(B, N_HEADS, nb, Q_BLOCK), (2, 0, 1, 3))
    k_pos = jnp.arange(S)

    def one_block(args):
        q_blk, c_blk, i = args
        s = jnp.einsum("bqhd,bkhd->bhqk", q_blk, k,
                       preferred_element_type=jnp.float32)
        bias = c_blk[:, :, :, None] - c_bhs[:, :, None, :]
        q_pos = i * Q_BLOCK + jnp.arange(Q_BLOCK)
        causal = k_pos[None, :] <= q_pos[:, None]
        s = jnp.where(causal, s + bias, NEG_BIG)
        pr = jax.nn.softmax(s, axis=-1)
        return jnp.einsum("bhqk,bkhd->bqhd", pr.astype(v.dtype), v)

    o = lax.map(one_block, (qb, cb, jnp.arange(nb)))
    o = jnp.transpose(o, (1, 0, 2, 3, 4)).reshape(B, S, D_MODEL)
    return o @ w_o


def _fwd_setup_inputs(seed: int = 0) -> dict:
    key = jax.random.key(seed)
    ks = jax.random.split(key, 32)
    f32 = jnp.float32
    nrm = lambda k, shape, scale: (jax.random.normal(k, shape, f32) * scale)
    gain = lambda k, shape: 1.0 + 0.05 * jax.random.normal(k, shape, f32)
    D = D_MODEL
    x = jax.random.normal(ks[0], (BATCH, SEQ, D), f32)
    p = jax.random.normal(ks[1], (DEPTH, BATCH, SEQ, D_PLE), f32)
    mix_norm = gain(ks[2], (DEPTH, D))
    conv_w_pw1 = nrm(ks[3], (N_A_LAYERS, D, 2 * D), D ** -0.5)
    conv_b_pw1 = nrm(ks[4], (N_A_LAYERS, 2 * D), 0.02)
    conv_w_dw = nrm(ks[5], (N_A_LAYERS, CONV_WIDTH, D), CONV_WIDTH ** -0.5)
    conv_b_dw = nrm(ks[6], (N_A_LAYERS, D), 0.02)
    conv_ln_g = gain(ks[7], (N_A_LAYERS, D))
    conv_ln_b = nrm(ks[8], (N_A_LAYERS, D), 0.02)
    conv_w_pw2 = nrm(ks[9], (N_A_LAYERS, D, D), 0.5 * D ** -0.5)
    conv_b_pw2 = nrm(ks[10], (N_A_LAYERS, D), 0.02)
    kv_norm = gain(ks[11], (D,))
    w_kvf = jnp.concatenate([
        nrm(ks[12], (D, 2 * D), D ** -0.5),
        nrm(ks[13], (D, N_HEADS), 0.1 * D ** -0.5),
    ], axis=1)
    b_f = jax.random.uniform(ks[14], (N_HEADS,), f32, 1.0, 6.0)
    attn_w_q = nrm(ks[15], (N_B_LAYERS, D, D), D ** -0.5)
    attn_w_o = nrm(ks[16], (N_B_LAYERS, D, D), 0.5 * D ** -0.5)
    ffn_norm = gain(ks[17], (DEPTH, D))
    ffn_w1 = nrm(ks[18], (DEPTH, D, D_FF), D ** -0.5)
    ffn_w2 = nrm(ks[19], (DEPTH, D_FF, D), 0.5 * D_FF ** -0.5)
    ple_norm = gain(ks[20], (DEPTH, D))
    ple_w_gate = nrm(ks[21], (DEPTH, D, D), D ** -0.5)
    ple_w_proj = nrm(ks[22], (DEPTH, D_PLE, D), 0.5 * D_PLE ** -0.5)
    final_norm = gain(ks[23], (D,))
    return {"x": x, "p": p, "mix_norm": mix_norm,
            "conv_w_pw1": conv_w_pw1, "conv_b_pw1": conv_b_pw1,
            "conv_w_dw": conv_w_dw, "conv_b_dw": conv_b_dw,
            "conv_ln_g": conv_ln_g, "conv_ln_b": conv_ln_b,
            "conv_w_pw2": conv_w_pw2, "conv_b_pw2": conv_b_pw2,
            "kv_norm": kv_norm, "w_kvf": w_kvf, "b_f": b_f,
            "attn_w_q": attn_w_q, "attn_w_o": attn_w_o,
            "ffn_norm": ffn_norm, "ffn_w1": ffn_w1, "ffn_w2": ffn_w2,
            "ple_norm": ple_norm, "ple_w_gate": ple_w_gate, "ple_w_proj": ple_w_proj,
            "final_norm": final_norm}


def _fwd_reference(x, p, mix_norm, conv_w_pw1, conv_b_pw1, conv_w_dw, conv_b_dw,
              conv_ln_g, conv_ln_b, conv_w_pw2, conv_b_pw2, kv_norm, w_kvf, b_f,
              attn_w_q, attn_w_o, ffn_norm, ffn_w1, ffn_w2, ple_norm, ple_w_gate,
              ple_w_proj, final_norm):
    h = x
    k = v = c_bhs = None
    for i in range(DEPTH):
        hn = rmsnorm(h, mix_norm[i])
        if i < N_A_LAYERS:
            h = h + conformer_conv(hn, conv_w_pw1[i], conv_b_pw1[i], conv_w_dw[i],
                                   conv_b_dw[i], conv_ln_g[i], conv_ln_b[i],
                                   conv_w_pw2[i], conv_b_pw2[i])
        else:
            j = i - N_A_LAYERS
            if j == 0:
                k, v, c_bhs = shared_kv(h, kv_norm, w_kvf, b_f)
            h = h + fox_attention(hn, attn_w_q[j], attn_w_o[j], k, v, c_bhs)
        hn = rmsnorm(h, ffn_norm[i])
        h = h + jnp.square(jax.nn.relu(hn @ ffn_w1[i])) @ ffn_w2[i]
        gate = jax.nn.sigmoid(rmsnorm(h, ple_norm[i]) @ ple_w_gate[i])
        h = h + gate * (p[i] @ ple_w_proj[i])
    return rmsnorm(h, final_norm)


import jax as _jax
import jax.numpy as _jnp

TWIN_FORMAT = 'train_step'
FWD_PARAMS = ['x', 'p', 'mix_norm', 'conv_w_pw1', 'conv_b_pw1', 'conv_w_dw', 'conv_b_dw', 'conv_ln_g', 'conv_ln_b', 'conv_w_pw2', 'conv_b_pw2', 'kv_norm', 'w_kvf', 'b_f', 'attn_w_q', 'attn_w_o', 'ffn_norm', 'ffn_w1', 'ffn_w2', 'ple_norm', 'ple_w_gate', 'ple_w_proj', 'final_norm']
TWIN_WEIGHTS = ['mix_norm', 'conv_w_pw1', 'conv_b_pw1', 'conv_w_dw', 'conv_b_dw', 'conv_ln_g', 'conv_ln_b', 'conv_w_pw2', 'conv_b_pw2', 'kv_norm', 'w_kvf', 'b_f', 'attn_w_q', 'attn_w_o', 'ffn_norm', 'ffn_w1', 'ffn_w2', 'ple_norm', 'ple_w_gate', 'ple_w_proj', 'final_norm']
TWIN_DIFF_INPUT = 'x'
TWIN_INPUTS = ['x', 'p', 'mix_norm', 'conv_w_pw1', 'conv_b_pw1', 'conv_w_dw', 'conv_b_dw', 'conv_ln_g', 'conv_ln_b', 'conv_w_pw2', 'conv_b_pw2', 'kv_norm', 'w_kvf', 'b_f', 'attn_w_q', 'attn_w_o', 'ffn_norm', 'ffn_w1', 'ffn_w2', 'ple_norm', 'ple_w_gate', 'ple_w_proj', 'final_norm', 'loss_target', 'm_mix_norm', 'm_conv_w_pw1', 'm_conv_b_pw1', 'm_conv_w_dw', 'm_conv_b_dw', 'm_conv_ln_g', 'm_conv_ln_b', 'm_conv_w_pw2', 'm_conv_b_pw2', 'm_kv_norm', 'm_w_kvf', 'm_b_f', 'm_attn_w_q', 'm_attn_w_o', 'm_ffn_norm', 'm_ffn_w1', 'm_ffn_w2', 'm_ple_norm', 'm_ple_w_gate', 'm_ple_w_proj', 'm_final_norm', 'v_mix_norm', 'v_conv_w_pw1', 'v_conv_b_pw1', 'v_conv_w_dw', 'v_conv_b_dw', 'v_conv_ln_g', 'v_conv_ln_b', 'v_conv_w_pw2', 'v_conv_b_pw2', 'v_kv_norm', 'v_w_kvf', 'v_b_f', 'v_attn_w_q', 'v_attn_w_o', 'v_ffn_norm', 'v_ffn_w1', 'v_ffn_w2', 'v_ple_norm', 'v_ple_w_gate', 'v_ple_w_proj', 'v_final_norm']
TWIN_OUTPUTS = ['loss', 'grad_x', 'grad_mix_norm', 'grad_conv_w_pw1', 'grad_conv_b_pw1', 'grad_conv_w_dw', 'grad_conv_b_dw', 'grad_conv_ln_g', 'grad_conv_ln_b', 'grad_conv_w_pw2', 'grad_conv_b_pw2', 'grad_kv_norm', 'grad_w_kvf', 'grad_b_f', 'grad_attn_w_q', 'grad_attn_w_o', 'grad_ffn_norm', 'grad_ffn_w1', 'grad_ffn_w2', 'grad_ple_norm', 'grad_ple_w_gate', 'grad_ple_w_proj', 'grad_final_norm', 'delta_mix_norm', 'delta_conv_w_pw1', 'delta_conv_b_pw1', 'delta_conv_w_dw', 'delta_conv_b_dw', 'delta_conv_ln_g', 'delta_conv_ln_b', 'delta_conv_w_pw2', 'delta_conv_b_pw2', 'delta_kv_norm', 'delta_w_kvf', 'delta_b_f', 'delta_attn_w_q', 'delta_attn_w_o', 'delta_ffn_norm', 'delta_ffn_w1', 'delta_ffn_w2', 'delta_ple_norm', 'delta_ple_w_gate', 'delta_ple_w_proj', 'delta_final_norm', 'new_m_mix_norm', 'new_m_conv_w_pw1', 'new_m_conv_b_pw1', 'new_m_conv_w_dw', 'new_m_conv_b_dw', 'new_m_conv_ln_g', 'new_m_conv_ln_b', 'new_m_conv_w_pw2', 'new_m_conv_b_pw2', 'new_m_kv_norm', 'new_m_w_kvf', 'new_m_b_f', 'new_m_attn_w_q', 'new_m_attn_w_o', 'new_m_ffn_norm', 'new_m_ffn_w1', 'new_m_ffn_w2', 'new_m_ple_norm', 'new_m_ple_w_gate', 'new_m_ple_w_proj', 'new_m_final_norm', 'new_v_mix_norm', 'new_v_conv_w_pw1', 'new_v_conv_b_pw1', 'new_v_conv_w_dw', 'new_v_conv_b_dw', 'new_v_conv_ln_g', 'new_v_conv_ln_b', 'new_v_conv_w_pw2', 'new_v_conv_b_pw2', 'new_v_kv_norm', 'new_v_w_kvf', 'new_v_b_f', 'new_v_attn_w_q', 'new_v_attn_w_o', 'new_v_ffn_norm', 'new_v_ffn_w1', 'new_v_ffn_w2', 'new_v_ple_norm', 'new_v_ple_w_gate', 'new_v_ple_w_proj', 'new_v_final_norm']
TWIN_LEAF_KINDS = {'loss': 'loss', 'grad_x': 'grad_x', 'grad_mix_norm': 'grad_w', 'grad_conv_w_pw1': 'grad_w', 'grad_conv_b_pw1': 'grad_w', 'grad_conv_w_dw': 'grad_w', 'grad_conv_b_dw': 'grad_w', 'grad_conv_ln_g': 'grad_w', 'grad_conv_ln_b': 'grad_w', 'grad_conv_w_pw2': 'grad_w', 'grad_conv_b_pw2': 'grad_w', 'grad_kv_norm': 'grad_w', 'grad_w_kvf': 'grad_w', 'grad_b_f': 'grad_w', 'grad_attn_w_q': 'grad_w', 'grad_attn_w_o': 'grad_w', 'grad_ffn_norm': 'grad_w', 'grad_ffn_w1': 'grad_w', 'grad_ffn_w2': 'grad_w', 'grad_ple_norm': 'grad_w', 'grad_ple_w_gate': 'grad_w', 'grad_ple_w_proj': 'grad_w', 'grad_final_norm': 'grad_w', 'delta_mix_norm': 'delta_w', 'delta_conv_w_pw1': 'delta_w', 'delta_conv_b_pw1': 'delta_w', 'delta_conv_w_dw': 'delta_w', 'delta_conv_b_dw': 'delta_w', 'delta_conv_ln_g': 'delta_w', 'delta_conv_ln_b': 'delta_w', 'delta_conv_w_pw2': 'delta_w', 'delta_conv_b_pw2': 'delta_w', 'delta_kv_norm': 'delta_w', 'delta_w_kvf': 'delta_w', 'delta_b_f': 'delta_w', 'delta_attn_w_q': 'delta_w', 'delta_attn_w_o': 'delta_w', 'delta_ffn_norm': 'delta_w', 'delta_ffn_w1': 'delta_w', 'delta_ffn_w2': 'delta_w', 'delta_ple_norm': 'delta_w', 'delta_ple_w_gate': 'delta_w', 'delta_ple_w_proj': 'delta_w', 'delta_final_norm': 'delta_w', 'new_m_mix_norm': 'new_m', 'new_m_conv_w_pw1': 'new_m', 'new_m_conv_b_pw1': 'new_m', 'new_m_conv_w_dw': 'new_m', 'new_m_conv_b_dw': 'new_m', 'new_m_conv_ln_g': 'new_m', 'new_m_conv_ln_b': 'new_m', 'new_m_conv_w_pw2': 'new_m', 'new_m_conv_b_pw2': 'new_m', 'new_m_kv_norm': 'new_m', 'new_m_w_kvf': 'new_m', 'new_m_b_f': 'new_m', 'new_m_attn_w_q': 'new_m', 'new_m_attn_w_o': 'new_m', 'new_m_ffn_norm': 'new_m', 'new_m_ffn_w1': 'new_m', 'new_m_ffn_w2': 'new_m', 'new_m_ple_norm': 'new_m', 'new_m_ple_w_gate': 'new_m', 'new_m_ple_w_proj': 'new_m', 'new_m_final_norm': 'new_m', 'new_v_mix_norm': 'new_v', 'new_v_conv_w_pw1': 'new_v', 'new_v_conv_b_pw1': 'new_v', 'new_v_conv_w_dw': 'new_v', 'new_v_conv_b_dw': 'new_v', 'new_v_conv_ln_g': 'new_v', 'new_v_conv_ln_b': 'new_v', 'new_v_conv_w_pw2': 'new_v', 'new_v_conv_b_pw2': 'new_v', 'new_v_kv_norm': 'new_v', 'new_v_w_kvf': 'new_v', 'new_v_b_f': 'new_v', 'new_v_attn_w_q': 'new_v', 'new_v_attn_w_o': 'new_v', 'new_v_ffn_norm': 'new_v', 'new_v_ffn_w1': 'new_v', 'new_v_ffn_w2': 'new_v', 'new_v_ple_norm': 'new_v', 'new_v_ple_w_gate': 'new_v', 'new_v_ple_w_proj': 'new_v', 'new_v_final_norm': 'new_v'}


def _forward(args):
    return _fwd_reference(*[args[k] for k in FWD_PARAMS])


def _output_shape():
    def fwd():
        inp = _fwd_setup_inputs(0)
        return _fwd_reference(*[inp[k] for k in FWD_PARAMS])
    out = _jax.eval_shape(fwd)
    return out.shape, out.dtype

N_MICROBATCH = 1
ADAM_LR = 0.001
ADAM_B1 = 0.9
ADAM_B2 = 0.999
ADAM_EPS = 1e-08
ADAM_WD = 0.01
ADAM_STEP = 10
PER_EXAMPLE_BATCH_AXIS = {'x': 0, 'p': 1, 'loss_target': 0}
SHARED_INPUTS = []
_WEIGHT_DTYPES = {'mix_norm': _jnp.float32, 'conv_w_pw1': _jnp.float32, 'conv_b_pw1': _jnp.float32, 'conv_w_dw': _jnp.float32, 'conv_b_dw': _jnp.float32, 'conv_ln_g': _jnp.float32, 'conv_ln_b': _jnp.float32, 'conv_w_pw2': _jnp.float32, 'conv_b_pw2': _jnp.float32, 'kv_norm': _jnp.float32, 'w_kvf': _jnp.float32, 'b_f': _jnp.float32, 'attn_w_q': _jnp.float32, 'attn_w_o': _jnp.float32, 'ffn_norm': _jnp.float32, 'ffn_w1': _jnp.float32, 'ffn_w2': _jnp.float32, 'ple_norm': _jnp.float32, 'ple_w_gate': _jnp.float32, 'ple_w_proj': _jnp.float32, 'final_norm': _jnp.float32}
MOMENT_SCALE = {'mix_norm': 6.023751e-02, 'conv_w_pw1': 6.024238e-02, 'conv_b_pw1': 2.322552e-01, 'conv_w_dw': 8.391179e-02, 'conv_b_dw': 5.303821e-01, 'conv_ln_g': 2.100692e-01, 'conv_ln_b': 3.221369e-01, 'conv_w_pw2': 2.545660e-01, 'conv_b_pw2': 1.263452e+00, 'kv_norm': 1.893688e-01, 'w_kvf': 1.270772e-01, 'b_f': 2.763913e-01, 'attn_w_q': 1.921398e-02, 'attn_w_o': 2.644349e-01, 'ffn_norm': 1.512467e-01, 'ffn_w1': 7.425077e-02, 'ffn_w2': 5.343670e-01, 'ple_norm': 2.082048e-02, 'ple_w_gate': 2.043519e-02, 'ple_w_proj': 9.497591e-02, 'final_norm': 6.487477e+01}


def _to_microbatches(a, axis):
    t = _jnp.moveaxis(a, axis, 0)
    t = t.reshape((N_MICROBATCH, t.shape[0] // N_MICROBATCH) + t.shape[1:])
    return _jnp.moveaxis(t, 1, axis + 1)


def setup_inputs(seed: int = 0) -> dict:
    inp = _fwd_setup_inputs(seed)
    key = _jax.random.fold_in(_jax.random.key(seed), 7919)
    shape, _ = _output_shape()
    out = dict(inp)
    out["loss_target"] = _jax.random.normal(_jax.random.fold_in(key, 0), shape, _jnp.float32)
    for i, name in enumerate(TWIN_WEIGHTS):
        w = inp[name].astype(_jnp.float32)
        if MOMENT_SCALE is None:
            s = _jnp.sqrt(_jnp.mean(_jnp.square(w)) + 1e-30)
        else:
            s = MOMENT_SCALE[name]
        km, kv = _jax.random.split(_jax.random.fold_in(key, i + 1))
        out[name] = w
        out["m_" + name] = s * _jax.random.normal(km, w.shape, _jnp.float32)
        out["v_" + name] = (s * s) * _jax.random.uniform(kv, w.shape, _jnp.float32, 0.5, 1.5)
    if N_MICROBATCH > 1:
        for name, axis in PER_EXAMPLE_BATCH_AXIS.items():
            out[name] = _to_microbatches(out[name], axis)
    return {'x': out['x'], 'p': out['p'], 'mix_norm': out['mix_norm'], 'conv_w_pw1': out['conv_w_pw1'], 'conv_b_pw1': out['conv_b_pw1'], 'conv_w_dw': out['conv_w_dw'], 'conv_b_dw': out['conv_b_dw'], 'conv_ln_g': out['conv_ln_g'], 'conv_ln_b': out['conv_ln_b'], 'conv_w_pw2': out['conv_w_pw2'], 'conv_b_pw2': out['conv_b_pw2'], 'kv_norm': out['kv_norm'], 'w_kvf': out['w_kvf'], 'b_f': out['b_f'], 'attn_w_q': out['attn_w_q'], 'attn_w_o': out['attn_w_o'], 'ffn_norm': out['ffn_norm'], 'ffn_w1': out['ffn_w1'], 'ffn_w2': out['ffn_w2'], 'ple_norm': out['ple_norm'], 'ple_w_gate': out['ple_w_gate'], 'ple_w_proj': out['ple_w_proj'], 'final_norm': out['final_norm'], 'loss_target': out['loss_target'], 'm_mix_norm': out['m_mix_norm'], 'm_conv_w_pw1': out['m_conv_w_pw1'], 'm_conv_b_pw1': out['m_conv_b_pw1'], 'm_conv_w_dw': out['m_conv_w_dw'], 'm_conv_b_dw': out['m_conv_b_dw'], 'm_conv_ln_g': out['m_conv_ln_g'], 'm_conv_ln_b': out['m_conv_ln_b'], 'm_conv_w_pw2': out['m_conv_w_pw2'], 'm_conv_b_pw2': out['m_conv_b_pw2'], 'm_kv_norm': out['m_kv_norm'], 'm_w_kvf': out['m_w_kvf'], 'm_b_f': out['m_b_f'], 'm_attn_w_q': out['m_attn_w_q'], 'm_attn_w_o': out['m_attn_w_o'], 'm_ffn_norm': out['m_ffn_norm'], 'm_ffn_w1': out['m_ffn_w1'], 'm_ffn_w2': out['m_ffn_w2'], 'm_ple_norm': out['m_ple_norm'], 'm_ple_w_gate': out['m_ple_w_gate'], 'm_ple_w_proj': out['m_ple_w_proj'], 'm_final_norm': out['m_final_norm'], 'v_mix_norm': out['v_mix_norm'], 'v_conv_w_pw1': out['v_conv_w_pw1'], 'v_conv_b_pw1': out['v_conv_b_pw1'], 'v_conv_w_dw': out['v_conv_w_dw'], 'v_conv_b_dw': out['v_conv_b_dw'], 'v_conv_ln_g': out['v_conv_ln_g'], 'v_conv_ln_b': out['v_conv_ln_b'], 'v_conv_w_pw2': out['v_conv_w_pw2'], 'v_conv_b_pw2': out['v_conv_b_pw2'], 'v_kv_norm': out['v_kv_norm'], 'v_w_kvf': out['v_w_kvf'], 'v_b_f': out['v_b_f'], 'v_attn_w_q': out['v_attn_w_q'], 'v_attn_w_o': out['v_attn_w_o'], 'v_ffn_norm': out['v_ffn_norm'], 'v_ffn_w1': out['v_ffn_w1'], 'v_ffn_w2': out['v_ffn_w2'], 'v_ple_norm': out['v_ple_norm'], 'v_ple_w_gate': out['v_ple_w_gate'], 'v_ple_w_proj': out['v_ple_w_proj'], 'v_final_norm': out['v_final_norm']}


def _loss(weights, diff, rest, loss_target):
    with _jax.named_scope("forward"):
        args = {**rest, TWIN_DIFF_INPUT: diff, **{k: w.astype(_WEIGHT_DTYPES[k]) for k, w in weights.items()}}
        y = _forward(args)
    with _jax.named_scope("loss_head"):
        err = _jnp.square(y.astype(_jnp.float32) - loss_target)
        return 0.5 * _jnp.sum(_jnp.mean(err, axis=-1)) if err.ndim else 0.5 * err


def _adamw(w, g, m, v):
    m = ADAM_B1 * m + (1.0 - ADAM_B1) * g
    v = ADAM_B2 * v + (1.0 - ADAM_B2) * _jnp.square(g)
    m_hat = m / (1.0 - ADAM_B1 ** ADAM_STEP)
    v_hat = v / (1.0 - ADAM_B2 ** ADAM_STEP)
    delta = -ADAM_LR * (m_hat / (_jnp.sqrt(v_hat) + ADAM_EPS) + ADAM_WD * w)
    return delta, m, v


def reference(x, p, mix_norm, conv_w_pw1, conv_b_pw1, conv_w_dw, conv_b_dw, conv_ln_g, conv_ln_b, conv_w_pw2, conv_b_pw2, kv_norm, w_kvf, b_f, attn_w_q, attn_w_o, ffn_norm, ffn_w1, ffn_w2, ple_norm, ple_w_gate, ple_w_proj, final_norm, loss_target, m_mix_norm, m_conv_w_pw1, m_conv_b_pw1, m_conv_w_dw, m_conv_b_dw, m_conv_ln_g, m_conv_ln_b, m_conv_w_pw2, m_conv_b_pw2, m_kv_norm, m_w_kvf, m_b_f, m_attn_w_q, m_attn_w_o, m_ffn_norm, m_ffn_w1, m_ffn_w2, m_ple_norm, m_ple_w_gate, m_ple_w_proj, m_final_norm, v_mix_norm, v_conv_w_pw1, v_conv_b_pw1, v_conv_w_dw, v_conv_b_dw, v_conv_ln_g, v_conv_ln_b, v_conv_w_pw2, v_conv_b_pw2, v_kv_norm, v_w_kvf, v_b_f, v_attn_w_q, v_attn_w_o, v_ffn_norm, v_ffn_w1, v_ffn_w2, v_ple_norm, v_ple_w_gate, v_ple_w_proj, v_final_norm):
    given = dict(x=x, p=p, mix_norm=mix_norm, conv_w_pw1=conv_w_pw1, conv_b_pw1=conv_b_pw1, conv_w_dw=conv_w_dw, conv_b_dw=conv_b_dw, conv_ln_g=conv_ln_g, conv_ln_b=conv_ln_b, conv_w_pw2=conv_w_pw2, conv_b_pw2=conv_b_pw2, kv_norm=kv_norm, w_kvf=w_kvf, b_f=b_f, attn_w_q=attn_w_q, attn_w_o=attn_w_o, ffn_norm=ffn_norm, ffn_w1=ffn_w1, ffn_w2=ffn_w2, ple_norm=ple_norm, ple_w_gate=ple_w_gate, ple_w_proj=ple_w_proj, final_norm=final_norm, loss_target=loss_target, m_mix_norm=m_mix_norm, m_conv_w_pw1=m_conv_w_pw1, m_conv_b_pw1=m_conv_b_pw1, m_conv_w_dw=m_conv_w_dw, m_conv_b_dw=m_conv_b_dw, m_conv_ln_g=m_conv_ln_g, m_conv_ln_b=m_conv_ln_b, m_conv_w_pw2=m_conv_w_pw2, m_conv_b_pw2=m_conv_b_pw2, m_kv_norm=m_kv_norm, m_w_kvf=m_w_kvf, m_b_f=m_b_f, m_attn_w_q=m_attn_w_q, m_attn_w_o=m_attn_w_o, m_ffn_norm=m_ffn_norm, m_ffn_w1=m_ffn_w1, m_ffn_w2=m_ffn_w2, m_ple_norm=m_ple_norm, m_ple_w_gate=m_ple_w_gate, m_ple_w_proj=m_ple_w_proj, m_final_norm=m_final_norm, v_mix_norm=v_mix_norm, v_conv_w_pw1=v_conv_w_pw1, v_conv_b_pw1=v_conv_b_pw1, v_conv_w_dw=v_conv_w_dw, v_conv_b_dw=v_conv_b_dw, v_conv_ln_g=v_conv_ln_g, v_conv_ln_b=v_conv_ln_b, v_conv_w_pw2=v_conv_w_pw2, v_conv_b_pw2=v_conv_b_pw2, v_kv_norm=v_kv_norm, v_w_kvf=v_w_kvf, v_b_f=v_b_f, v_attn_w_q=v_attn_w_q, v_attn_w_o=v_attn_w_o, v_ffn_norm=v_ffn_norm, v_ffn_w1=v_ffn_w1, v_ffn_w2=v_ffn_w2, v_ple_norm=v_ple_norm, v_ple_w_gate=v_ple_w_gate, v_ple_w_proj=v_ple_w_proj, v_final_norm=v_final_norm)
    weights = {n: given[n] for n in TWIN_WEIGHTS}
    shared = {n: given[n] for n in SHARED_INPUTS}
    per_example = {n: given[n] for n in ['x', 'p']}
    grad_fn = _jax.value_and_grad(_loss, argnums=(0, 1))

    def one_microbatch(ex, loss_target):
        ex = dict(ex)
        diff = ex.pop(TWIN_DIFF_INPUT)
        return grad_fn(weights, diff, {**shared, **ex}, loss_target)

    if N_MICROBATCH == 1:
        loss, (grad_w, grad_x) = one_microbatch(per_example, given["loss_target"])
    else:
        def body(carry, xs):
            loss_sum, grad_sum = carry
            l_k, (gw_k, gx_k) = one_microbatch(xs[0], xs[1])
            with _jax.named_scope("update"):
                return (loss_sum + l_k, _jax.tree.map(_jnp.add, grad_sum, gw_k)), gx_k

        init = (_jnp.zeros((), _jnp.float32), _jax.tree.map(_jnp.zeros_like, weights))
        (loss, grad_w), grad_x = _jax.lax.scan(body, init, (per_example, given["loss_target"]))
    with _jax.named_scope("update"):
        delta_w, new_m, new_v = {}, {}, {}
        for n in TWIN_WEIGHTS:
            delta_w[n], new_m[n], new_v[n] = _adamw(weights[n], grad_w[n], given["m_" + n], given["v_" + n])
    return (loss, grad_x, *[grad_w[n] for n in TWIN_WEIGHTS], *[delta_w[n] for n in TWIN_WEIGHTS],
            *[new_m[n] for n in TWIN_WEIGHTS], *[new_v[n] for n in TWIN_WEIGHTS])
```

```python
import numpy as np
import jax
import jax.numpy as jnp
from jax import lax
from jax.experimental import pallas as pl
from jax.experimental.pallas import tpu as pltpu

F32 = jnp.float32
MXU_DTYPE = jnp.bfloat16
ACT_DTYPE = jnp.bfloat16
WIRE_DTYPE = jnp.bfloat16

N_DEV = 8
N_CHIP = 4
EPS = 1e-6
NEG_BIG = -1e30
ADAM_LR = 0.001
ADAM_B1 = 0.9
ADAM_B2 = 0.999
ADAM_EPS = 1e-08
ADAM_WD = 0.01
ADAM_STEP = 10

VMEM_LIMIT_BYTES = 56 * 1024 * 1024
PACK_COLS = 1024
PACK_ROW_TILE = 256
HALO = 32
MESH = pl.DeviceIdType.MESH

SHARD_AXIS_BIG = {"conv_w_pw1": 2, "conv_w_pw2": 1, "w_kvf": 1, "attn_w_q": 1, "attn_w_o": 1,
                  "ffn_w1": 2, "ffn_w2": 1, "ple_w_gate": 1, "ple_w_proj": 2}
SHARD_AXIS_SMALL = {"conv_b_pw1": 1, "conv_w_dw": 2, "conv_b_dw": 1, "conv_ln_g": 1, "conv_ln_b": 1,
                    "conv_b_pw2": 1}
REPLICATED = ["mix_norm", "kv_norm", "b_f", "ffn_norm", "ple_norm", "final_norm"]
WEIGHTS = ["mix_norm", "conv_w_pw1", "conv_b_pw1", "conv_w_dw", "conv_b_dw", "conv_ln_g", "conv_ln_b",
           "conv_w_pw2", "conv_b_pw2", "kv_norm", "w_kvf", "b_f", "attn_w_q", "attn_w_o", "ffn_norm",
           "ffn_w1", "ffn_w2", "ple_norm", "ple_w_gate", "ple_w_proj", "final_norm"]


def _mm(a, b):
    return jnp.dot(a.astype(MXU_DTYPE), b.astype(MXU_DTYPE), preferred_element_type=F32)


def _mm_nt(a, b):
    return lax.dot_general(a.astype(MXU_DTYPE), b.astype(MXU_DTYPE), (((1,), (1,)), ((), ())),
                           preferred_element_type=F32)


def _mm_tn(a, b):
    return lax.dot_general(a.astype(MXU_DTYPE), b.astype(MXU_DTYPE), (((0,), (0,)), ((), ())),
                           preferred_element_type=F32)


def _split3(x):
    hi = x.astype(MXU_DTYPE)
    r1 = x - hi.astype(F32)
    mid = r1.astype(MXU_DTYPE)
    lo = (r1 - mid.astype(F32)).astype(MXU_DTYPE)
    return hi, mid, lo


def _tri_mm(tri, x):
    hi, mid, lo = _split3(x)
    return (jnp.dot(tri, lo, preferred_element_type=F32) + jnp.dot(tri, mid, preferred_element_type=F32)
            + jnp.dot(tri, hi, preferred_element_type=F32))


def _colsum8(x):
    tm, n = x.shape
    return jnp.sum(x.reshape(tm // 8, 8, n), axis=0)


def _rms(x, g):
    r = lax.rsqrt(jnp.mean(x * x, axis=-1, keepdims=True) + EPS)
    return x * r * g, r


def _rms_bwd(x, r, g, dn):
    w = dn * g
    dx = r * w - x * (r * r * r) * jnp.mean(w * x, axis=-1, keepdims=True)
    return dx, dn * x * r


def _sigmoid(x):
    return jax.nn.sigmoid(x)


def _params(n_grid):
    return pltpu.CompilerParams(dimension_semantics=("arbitrary",) * n_grid, vmem_limit_bytes=VMEM_LIMIT_BYTES)


def _rows(tm, n):
    return pl.BlockSpec((tm, n), lambda i: (i, 0))


def _rows_rev(tm, n, nt):
    return pl.BlockSpec((tm, n), lambda i: (nt - 1 - i, 0))


def _whole(shape):
    nd = len(shape)
    return pl.BlockSpec(shape, lambda i: (0,) * nd)


def _row_tile(s, want):
    tm = min(s, want)
    assert s % tm == 0 and tm % 8 == 0, (s, tm)
    return tm


def conv_fwd(h, g, w1, b1, wd, bd, lg, lb, w2, b2, tm):
    S, D = h.shape
    CW = wd.shape[0]
    off = HALO - (CW - 1)
    assert 0 <= off and tm >= HALO
    nt = S // tm

    def body(h_ref, g_ref, w1_ref, b1_ref, wd_ref, bd_ref, lg_ref, lb_ref, w2_ref, b2_ref,
             ho_ref, n_ref, u_ref, z_ref, sw_ref, ext):
        @pl.when(pl.program_id(0) == 0)
        def _():
            ext[0:HALO, :] = jnp.zeros((HALO, D), F32)

        x = h_ref[...]
        n, _ = _rms(x, g_ref[...])
        n_ref[...] = n.astype(n_ref.dtype)
        u = _mm(n, w1_ref[...]) + b1_ref[...]
        u_ref[...] = u
        ext[HALO:HALO + tm, :] = u[:, :D] * _sigmoid(u[:, D:])
        z = jnp.broadcast_to(bd_ref[...], (tm, D))
        for k in range(CW):
            z = z + wd_ref[k:k + 1, :] * ext[off + k:off + k + tm, :]
        z_ref[...] = z
        ext[0:HALO, :] = ext[tm:tm + HALO, :]
        mu = jnp.mean(z, axis=-1, keepdims=True)
        zc = z - mu
        y = zc * lax.rsqrt(jnp.mean(zc * zc, axis=-1, keepdims=True) + EPS) * lg_ref[...] + lb_ref[...]
        sw = y * _sigmoid(y)
        sw_ref[...] = sw.astype(sw_ref.dtype)
        ho_ref[...] = x + _mm(sw, w2_ref[...]) + b2_ref[...]

    return pl.pallas_call(
        body, name="conv_fwd", grid=(nt,),
        in_specs=[_rows(tm, D), _whole((1, D)), _whole(w1.shape), _whole((1, 2 * D)), _whole(wd.shape),
                  _whole((1, D)), _whole((1, D)), _whole((1, D)), _whole(w2.shape), _whole((1, D))],
        out_specs=[_rows(tm, D), _rows(tm, D), _rows(tm, 2 * D), _rows(tm, D), _rows(tm, D)],
        out_shape=[jax.ShapeDtypeStruct((S, D), F32), jax.ShapeDtypeStruct((S, D), ACT_DTYPE),
                   jax.ShapeDtypeStruct((S, 2 * D), F32), jax.ShapeDtypeStruct((S, D), F32),
                   jax.ShapeDtypeStruct((S, D), ACT_DTYPE)],
        scratch_shapes=[pltpu.VMEM((HALO + tm, D), F32)],
        compiler_params=_params(1),
    )(h, g, w1, b1, wd, bd, lg, lb, w2, b2)


def ffn_fwd(h, g, w1, w2, tm):
    S, D = h.shape
    FF = w1.shape[1]

    def body(h_ref, g_ref, w1_ref, w2_ref, ho_ref, n_ref, a_ref, s_ref):
        x = h_ref[...]
        n, _ = _rms(x, g_ref[...])
        n_ref[...] = n.astype(n_ref.dtype)
        a = _mm(n, w1_ref[...])
        a_ref[...] = a
        s = jnp.square(jnp.maximum(a, 0.0))
        s_ref[...] = s.astype(s_ref.dtype)
        ho_ref[...] = x + _mm(s, w2_ref[...])

    return pl.pallas_call(
        body, name="ffn_fwd", grid=(S // tm,),
        in_specs=[_rows(tm, D), _whole((1, D)), _whole(w1.shape), _whole(w2.shape)],
        out_specs=[_rows(tm, D), _rows(tm, D), _rows(tm, FF), _rows(tm, FF)],
        out_shape=[jax.ShapeDtypeStruct((S, D), F32), jax.ShapeDtypeStruct((S, D), ACT_DTYPE),
                   jax.ShapeDtypeStruct((S, FF), F32), jax.ShapeDtypeStruct((S, FF), ACT_DTYPE)],
        compiler_params=_params(1),
    )(h, g, w1, w2)


def ple_fwd(h, g, wg, p, wp, tm):
    S, D = h.shape
    E = p.shape[1]

    def body(h_ref, g_ref, wg_ref, p_ref, wp_ref, ho_ref, n_ref, gate_ref):
        x = h_ref[...]
        n, _ = _rms(x, g_ref[...])
        n_ref[...] = n.astype(n_ref.dtype)
        gate = _sigmoid(_mm(n, wg_ref[...]))
        gate_ref[...] = gate
        ho_ref[...] = x + gate * _mm(p_ref[...], wp_ref[...])

    return pl.pallas_call(
        body, name="ple_fwd", grid=(S // tm,),
        in_specs=[_rows(tm, D), _whole((1, D)), _whole(wg.shape), _rows(tm, E), _whole(wp.shape)],
        out_specs=[_rows(tm, D), _rows(tm, D), _rows(tm, D)],
        out_shape=[jax.ShapeDtypeStruct((S, D), F32), jax.ShapeDtypeStruct((S, D), ACT_DTYPE),
                   jax.ShapeDtypeStruct((S, D), F32)],
        compiler_params=_params(1),
    )(h, g, wg, p, wp)


def kv_fwd(h, g, wk, wv, wf, bf, tm):
    S, D = h.shape
    H = wf.shape[1]

    def body(h_ref, g_ref, wk_ref, wv_ref, wf_ref, bf_ref, k_ref, v_ref, n_ref, fl_ref, c_ref, carry):
        @pl.when(pl.program_id(0) == 0)
        def _():
            carry[...] = jnp.zeros_like(carry)

        n, _ = _rms(h_ref[...], g_ref[...])
        n_ref[...] = n.astype(n_ref.dtype)
        k_ref[...] = _mm(n, wk_ref[...]).astype(k_ref.dtype)
        v_ref[...] = _mm(n, wv_ref[...]).astype(v_ref.dtype)
        fl = _mm(n, wf_ref[...]) + bf_ref[...]
        fl_ref[...] = fl
        logf = jnp.minimum(fl, 0.0) - jnp.log1p(jnp.exp(-jnp.abs(fl)))
        row = lax.broadcasted_iota(jnp.int32, (tm, tm), 0)
        col = lax.broadcasted_iota(jnp.int32, (tm, tm), 1)
        tri = (row >= col).astype(MXU_DTYPE)
        c = _tri_mm(tri, logf) + carry[...]
        c_ref[...] = c
        carry[...] = c[tm - 1:tm, :]

    return pl.pallas_call(
        body, name="kv_fwd", grid=(S // tm,),
        in_specs=[_rows(tm, D), _whole((1, D)), _whole(wk.shape), _whole(wv.shape), _whole(wf.shape),
                  _whole((1, H))],
        out_specs=[_rows(tm, D), _rows(tm, D), _rows(tm, D), _rows(tm, H), _rows(tm, H)],
        out_shape=[jax.ShapeDtypeStruct((S, D), ACT_DTYPE), jax.ShapeDtypeStruct((S, D), ACT_DTYPE),
                   jax.ShapeDtypeStruct((S, D), ACT_DTYPE), jax.ShapeDtypeStruct((S, H), F32),
                   jax.ShapeDtypeStruct((S, H), F32)],
        scratch_shapes=[pltpu.VMEM((1, H), F32)],
        compiler_params=_params(1),
    )(h, g, wk, wv, wf, bf)


def q_fwd(h, g, wq, scale, tm):
    S, D = h.shape

    def body(h_ref, g_ref, wq_ref, n_ref, q_ref):
        n, _ = _rms(h_ref[...], g_ref[...])
        n_ref[...] = n.astype(n_ref.dtype)
        q_ref[...] = (_mm(n, wq_ref[...]) * scale).astype(q_ref.dtype)

    return pl.pallas_call(
        body, name="q_fwd", grid=(S // tm,),
        in_specs=[_rows(tm, D), _whole((1, D)), _whole(wq.shape)],
        out_specs=[_rows(tm, D), _rows(tm, D)],
        out_shape=[jax.ShapeDtypeStruct((S, D), ACT_DTYPE), jax.ShapeDtypeStruct((S, D), ACT_DTYPE)],
        compiler_params=_params(1),
    )(h, g, wq)


def attn_out_fwd(h, o, wo, tm):
    S, D = h.shape

    def body(h_ref, o_ref, wo_ref, ho_ref):
        ho_ref[...] = h_ref[...] + _mm(o_ref[...], wo_ref[...])

    return pl.pallas_call(
        body, name="attn_out_fwd", grid=(S // tm,),
        in_specs=[_rows(tm, D), _rows(tm, D), _whole(wo.shape)],
        out_specs=_rows(tm, D),
        out_shape=jax.ShapeDtypeStruct((S, D), F32),
        compiler_params=_params(1),
    )(h, o, wo)


def _causal_pairs(nblk, kv_major):
    if kv_major:
        pairs = [(qi, ki) for ki in range(nblk) for qi in range(ki, nblk)]
    else:
        pairs = [(qi, ki) for qi in range(nblk) for ki in range(qi + 1)]
    return (jnp.asarray(np.array([p[0] for p in pairs], np.int32)),
            jnp.asarray(np.array([p[1] for p in pairs], np.int32)), len(pairs))


def _scores(q, k, cq, ck, on_diag, t):
    s = _mm_nt(q, k) + (cq - ck)
    row = lax.broadcasted_iota(jnp.int32, (t, t), 0)
    col = lax.broadcasted_iota(jnp.int32, (t, t), 1)
    return jnp.where(jnp.logical_or(jnp.logical_not(on_diag), row >= col), s, NEG_BIG)


def flash_fwd(q, k, v, c_col, c_row, dh, t):
    S, D = q.shape
    hg = 128 // dh
    G = D // 128
    nblk = S // t
    qi_tab, ki_tab, npairs = _causal_pairs(nblk, kv_major=False)

    def body(qi_ref, ki_ref, q_ref, k_ref, v_ref, cq_ref, ck_ref, o_ref, o32_ref, lse_ref, m_scr, l_scr, acc_scr):
        j = pl.program_id(1)
        qi = qi_ref[j]
        ki = ki_ref[j]

        @pl.when(ki == 0)
        def _():
            m_scr[...] = jnp.full(m_scr.shape, NEG_BIG, F32)
            l_scr[...] = jnp.zeros(l_scr.shape, F32)
            acc_scr[...] = jnp.zeros(acc_scr.shape, F32)

        for hh in range(hg):
            lanes = slice(hh * dh, (hh + 1) * dh)
            s = _scores(q_ref[:, lanes], k_ref[:, lanes], cq_ref[:, hh:hh + 1], ck_ref[hh:hh + 1, :], ki == qi, t)
            m_old = m_scr[hh]
            m_new = jnp.maximum(m_old, jnp.max(s, axis=-1, keepdims=True))
            alpha = jnp.exp(m_old - m_new)
            p = jnp.exp(s - m_new)
            l_scr[hh] = alpha * l_scr[hh] + jnp.sum(p, axis=-1, keepdims=True)
            p_hi = p.astype(MXU_DTYPE)
            p_lo = p - p_hi.astype(F32)
            acc_scr[hh] = alpha * acc_scr[hh] + (_mm(p_lo, v_ref[:, lanes]) + _mm(p_hi, v_ref[:, lanes]))
            m_scr[hh] = m_new

        @pl.when(ki == qi)
        def _():
            for hh in range(hg):
                oh = acc_scr[hh] / l_scr[hh]
                o_ref[:, hh * dh:(hh + 1) * dh] = oh.astype(o_ref.dtype)
                o32_ref[:, hh * dh:(hh + 1) * dh] = oh
                lse_ref[:, hh:hh + 1] = m_scr[hh] + jnp.log(l_scr[hh])

    grid_spec = pltpu.PrefetchScalarGridSpec(
        num_scalar_prefetch=2, grid=(G, npairs),
        in_specs=[pl.BlockSpec((t, 128), lambda g, j, qt, kt: (qt[j], g)),
                  pl.BlockSpec((t, 128), lambda g, j, qt, kt: (kt[j], g)),
                  pl.BlockSpec((t, 128), lambda g, j, qt, kt: (kt[j], g)),
                  pl.BlockSpec((None, t, hg), lambda g, j, qt, kt: (g, qt[j], 0)),
                  pl.BlockSpec((None, hg, t), lambda g, j, qt, kt: (g, 0, kt[j]))],
        out_specs=[pl.BlockSpec((t, 128), lambda g, j, qt, kt: (qt[j], g)),
                   pl.BlockSpec((t, 128), lambda g, j, qt, kt: (qt[j], g)),
                   pl.BlockSpec((None, t, hg), lambda g, j, qt, kt: (g, qt[j], 0))],
        scratch_shapes=[pltpu.VMEM((hg, t, 1), F32), pltpu.VMEM((hg, t, 1), F32), pltpu.VMEM((hg, t, dh), F32)])
    return pl.pallas_call(
        body, name="flash_fwd", grid_spec=grid_spec,
        out_shape=[jax.ShapeDtypeStruct((S, D), ACT_DTYPE), jax.ShapeDtypeStruct((S, D), F32),
                   jax.ShapeDtypeStruct((G, S, hg), F32)],
        compiler_params=_params(2),
    )(qi_tab, ki_tab, q, k, v, c_col, c_row)


def loss_head(h, g, target, tm):
    S, D = h.shape
    nt = S // tm

    def body(h_ref, g_ref, t_ref, dh_ref, dg_ref, loss_ref, dg_acc, loss_acc):
        i = pl.program_id(0)

        @pl.when(i == 0)
        def _():
            dg_acc[...] = jnp.zeros_like(dg_acc)
            loss_acc[...] = jnp.zeros_like(loss_acc)

        x = h_ref[...]
        gg = g_ref[...]
        y, r = _rms(x, gg)
        e = y - t_ref[...]
        loss_acc[...] += 0.5 * jnp.sum(jnp.mean(e * e, axis=-1, keepdims=True), axis=0, keepdims=True)
        dx, dgr = _rms_bwd(x, r, gg, e / D)
        dh_ref[...] = dx
        dg_acc[...] += _colsum8(dgr)

        @pl.when(i == nt - 1)
        def _():
            dg_ref[...] = jnp.sum(dg_acc[...], axis=0, keepdims=True)
            loss_ref[...] = jnp.broadcast_to(loss_acc[...], loss_ref.shape)

    return pl.pallas_call(
        body, name="loss_head", grid=(nt,),
        in_specs=[_rows(tm, D), _whole((1, D)), _rows(tm, D)],
        out_specs=[_rows(tm, D), _whole((1, D)), _whole((1, 128))],
        out_shape=[jax.ShapeDtypeStruct((S, D), F32), jax.ShapeDtypeStruct((1, D), F32),
                   jax.ShapeDtypeStruct((1, 128), F32)],
        scratch_shapes=[pltpu.VMEM((8, D), F32), pltpu.VMEM((1, 1), F32)],
        compiler_params=_params(1),
    )(h, g, target)


def ple_bwd(d, h, g, wg, gate, p, wp, tm):
    S, D = h.shape
    E = p.shape[1]
    nt = S // tm

    def body(d_ref, h_ref, g_ref, wg_ref, gate_ref, p_ref, wp_ref, di_ref, dz_ref, dpp_ref, dg_ref, dg_acc):
        i = pl.program_id(0)

        @pl.when(i == 0)
        def _():
            dg_acc[...] = jnp.zeros_like(dg_acc)

        dd = d_ref[...]
        x = h_ref[...]
        gg = g_ref[...]
        gt = gate_ref[...]
        pp = _mm(p_ref[...], wp_ref[...])
        dpp_ref[...] = (dd * gt).astype(dpp_ref.dtype)
        dz = dd * pp * gt * (1.0 - gt)
        dz_ref[...] = dz.astype(dz_ref.dtype)
        r = lax.rsqrt(jnp.mean(x * x, axis=-1, keepdims=True) + EPS)
        dx, dgr = _rms_bwd(x, r, gg, _mm_nt(dz, wg_ref[...]))
        di_ref[...] = dd + dx
        dg_acc[...] += _colsum8(dgr)

        @pl.when(i == nt - 1)
        def _():
            dg_ref[...] = jnp.sum(dg_acc[...], axis=0, keepdims=True)

    return pl.pallas_call(
        body, name="ple_bwd", grid=(nt,),
        in_specs=[_rows(tm, D), _rows(tm, D), _whole((1, D)), _whole(wg.shape), _rows(tm, D), _rows(tm, E),
                  _whole(wp.shape)],
        out_specs=[_rows(tm, D), _rows(tm, D), _rows(tm, D), _whole((1, D))],
        out_shape=[jax.ShapeDtypeStruct((S, D), F32), jax.ShapeDtypeStruct((S, D), ACT_DTYPE),
                   jax.ShapeDtypeStruct((S, D), ACT_DTYPE), jax.ShapeDtypeStruct((1, D), F32)],
        scratch_shapes=[pltpu.VMEM((8, D), F32)],
        compiler_params=_params(1),
    )(d, h, g, wg, gate, p, wp)


def ffn_bwd(d, h, g, w1, w2, a, tm):
    S, D = h.shape
    FF = w1.shape[1]
    nt = S // tm

    def body(d_ref, h_ref, g_ref, w1_ref, w2_ref, a_ref, di_ref, da_ref, dg_ref, dg_acc):
        i = pl.program_id(0)

        @pl.when(i == 0)
        def _():
            dg_acc[...] = jnp.zeros_like(dg_acc)

        dd = d_ref[...]
        x = h_ref[...]
        da = _mm_nt(dd, w2_ref[...]) * (2.0 * jnp.maximum(a_ref[...], 0.0))
        da_ref[...] = da.astype(da_ref.dtype)
        r = lax.rsqrt(jnp.mean(x * x, axis=-1, keepdims=True) + EPS)
        dx, dgr = _rms_bwd(x, r, g_ref[...], _mm_nt(da, w1_ref[...]))
        di_ref[...] = dd + dx
        dg_acc[...] += _colsum8(dgr)

        @pl.when(i == nt - 1)
        def _():
            dg_ref[...] = jnp.sum(dg_acc[...], axis=0, keepdims=True)

    return pl.pallas_call(
        body, name="ffn_bwd", grid=(nt,),
        in_specs=[_rows(tm, D), _rows(tm, D), _whole((1, D)), _whole(w1.shape), _whole(w2.shape), _rows(tm, FF)],
        out_specs=[_rows(tm, D), _rows(tm, FF), _whole((1, D))],
        out_shape=[jax.ShapeDtypeStruct((S, D), F32), jax.ShapeDtypeStruct((S, FF), ACT_DTYPE),
                   jax.ShapeDtypeStruct((1, D), F32)],
        scratch_shapes=[pltpu.VMEM((8, D), F32)],
        compiler_params=_params(1),
    )(d, h, g, w1, w2, a)


def attn_out_bwd(d, wo, tm):
    S, D = d.shape

    def body(d_ref, wo_ref, do_ref):
        do_ref[...] = _mm_nt(d_ref[...], wo_ref[...]).astype(do_ref.dtype)

    return pl.pallas_call(
        body, name="attn_out_bwd", grid=(S // tm,),
        in_specs=[_rows(tm, D), _whole(wo.shape)],
        out_specs=_rows(tm, D),
        out_shape=jax.ShapeDtypeStruct((S, D), ACT_DTYPE),
        compiler_params=_params(1),
    )(d, wo)


def flash_bwd(q, k, v, c_col, c_row, o, do, lse, dh, t):
    S, D = q.shape
    hg = 128 // dh
    G = D // 128
    nblk = S // t
    assert hg <= 8
    qi_tab, ki_tab, npairs = _causal_pairs(nblk, kv_major=True)

    def body(qi_ref, ki_ref, q_ref, k_ref, v_ref, cq_ref, ck_ref, o_ref, do_ref, lse_ref,
             dq_ref, dk_ref, dv_ref, dck_ref, dk_acc, dv_acc, dck_acc):
        j = pl.program_id(1)
        qi = qi_ref[j]
        ki = ki_ref[j]

        @pl.when(qi == ki)
        def _():
            dk_acc[...] = jnp.zeros_like(dk_acc)
            dv_acc[...] = jnp.zeros_like(dv_acc)
            dck_acc[...] = jnp.zeros_like(dck_acc)

        rows = pl.ds(pl.multiple_of(qi * t, t), t)
        for hh in range(hg):
            lanes = slice(hh * dh, (hh + 1) * dh)
            qh = q_ref[:, lanes]
            kh = k_ref[:, lanes]
            doh = do_ref[:, lanes]
            s = _scores(qh, kh, cq_ref[:, hh:hh + 1], ck_ref[hh:hh + 1, :], ki == qi, t)
            p = jnp.exp(s - lse_ref[:, hh:hh + 1])
            dv_acc[hh] += _mm_tn(p, doh)
            delta = jnp.sum(doh.astype(F32) * o_ref[:, lanes].astype(F32), axis=-1, keepdims=True)
            ds = p * (_mm_nt(doh, v_ref[:, lanes]) - delta)
            dk_acc[hh] += _mm_tn(ds, qh)
            dck_acc[hh:hh + 1, :] -= jnp.sum(ds, axis=0, keepdims=True)
            dqh = _mm(ds, kh)

            @pl.when(ki == 0)
            def _():
                dq_ref[rows, lanes] = dqh

            @pl.when(ki > 0)
            def _():
                dq_ref[rows, lanes] += dqh

        @pl.when(qi == nblk - 1)
        def _():
            for hh in range(hg):
                dk_ref[:, hh * dh:(hh + 1) * dh] = dk_acc[hh]
                dv_ref[:, hh * dh:(hh + 1) * dh] = dv_acc[hh]
            dck_ref[...] = dck_acc[...]

    grid_spec = pltpu.PrefetchScalarGridSpec(
        num_scalar_prefetch=2, grid=(G, npairs),
        in_specs=[pl.BlockSpec((t, 128), lambda g, j, qt, kt: (qt[j], g)),
                  pl.BlockSpec((t, 128), lambda g, j, qt, kt: (kt[j], g)),
                  pl.BlockSpec((t, 128), lambda g, j, qt, kt: (kt[j], g)),
                  pl.BlockSpec((None, t, hg), lambda g, j, qt, kt: (g, qt[j], 0)),
                  pl.BlockSpec((None, hg, t), lambda g, j, qt, kt: (g, 0, kt[j])),
                  pl.BlockSpec((t, 128), lambda g, j, qt, kt: (qt[j], g)),
                  pl.BlockSpec((t, 128), lambda g, j, qt, kt: (qt[j], g)),
                  pl.BlockSpec((None, t, hg), lambda g, j, qt, kt: (g, qt[j], 0))],
        out_specs=[pl.BlockSpec((S, 128), lambda g, j, qt, kt: (0, g)),
                   pl.BlockSpec((t, 128), lambda g, j, qt, kt: (kt[j], g)),
                   pl.BlockSpec((t, 128), lambda g, j, qt, kt: (kt[j], g)),
                   pl.BlockSpec((None, 8, t), lambda g, j, qt, kt: (g, 0, kt[j]))],
        scratch_shapes=[pltpu.VMEM((hg, t, dh), F32), pltpu.VMEM((hg, t, dh), F32), pltpu.VMEM((8, t), F32)])
    return pl.pallas_call(
        body, name="flash_bwd", grid_spec=grid_spec,
        out_shape=[jax.ShapeDtypeStruct((S, D), F32), jax.ShapeDtypeStruct((S, D), F32),
                   jax.ShapeDtypeStruct((S, D), F32), jax.ShapeDtypeStruct((G, 8, S), F32)],
        compiler_params=_params(2),
    )(qi_tab, ki_tab, q, k, v, c_col, c_row, o, do, lse)


def q_bwd(d, dq, h, g, wq, scale, tm):
    S, D = h.shape
    nt = S // tm

    def body(d_ref, dq_ref, h_ref, g_ref, wq_ref, di_ref, dqs_ref, dg_ref, dg_acc):
        i = pl.program_id(0)

        @pl.when(i == 0)
        def _():
            dg_acc[...] = jnp.zeros_like(dg_acc)

        x = h_ref[...]
        dqs = dq_ref[...] * scale
        dqs_ref[...] = dqs.astype(dqs_ref.dtype)
        r = lax.rsqrt(jnp.mean(x * x, axis=-1, keepdims=True) + EPS)
        dx, dgr = _rms_bwd(x, r, g_ref[...], _mm_nt(dqs, wq_ref[...]))
        di_ref[...] = d_ref[...] + dx
        dg_acc[...] += _colsum8(dgr)

        @pl.when(i == nt - 1)
        def _():
            dg_ref[...] = jnp.sum(dg_acc[...], axis=0, keepdims=True)

    return pl.pallas_call(
        body, name="q_bwd", grid=(nt,),
        in_specs=[_rows(tm, D), _rows(tm, D), _rows(tm, D), _whole((1, D)), _whole(wq.shape)],
        out_specs=[_rows(tm, D), _rows(tm, D), _whole((1, D))],
        out_shape=[jax.ShapeDtypeStruct((S, D), F32), jax.ShapeDtypeStruct((S, D), ACT_DTYPE),
                   jax.ShapeDtypeStruct((1, D), F32)],
        scratch_shapes=[pltpu.VMEM((8, D), F32)],
        compiler_params=_params(1),
    )(d, dq, h, g, wq)


def kv_bwd(d, dks, dvs, dc, fl, h, g, wk, wv, wf, tm):
    S, D = h.shape
    H = wf.shape[1]
    nt = S // tm
    nl = len(dks)

    def body(*refs):
        d_ref = refs[0]
        dk_refs = refs[1:1 + nl]
        dv_refs = refs[1 + nl:1 + 2 * nl]
        (dc_ref, fl_ref, h_ref, g_ref, wk_ref, wv_ref, wf_ref,
         di_ref, dk_ref, dv_ref, dfl_ref, dg_ref, dbf_ref, dg_acc, dbf_acc, carry) = refs[1 + 2 * nl:]
        i = pl.program_id(0)

        @pl.when(i == 0)
        def _():
            dg_acc[...] = jnp.zeros_like(dg_acc)
            dbf_acc[...] = jnp.zeros_like(dbf_acc)
            carry[...] = jnp.zeros_like(carry)

        dk = dk_refs[0][...]
        dv = dv_refs[0][...]
        for l in range(1, nl):
            dk = dk + dk_refs[l][...]
            dv = dv + dv_refs[l][...]
        dk_ref[...] = dk.astype(dk_ref.dtype)
        dv_ref[...] = dv.astype(dv_ref.dtype)
        row = lax.broadcasted_iota(jnp.int32, (tm, tm), 0)
        col = lax.broadcasted_iota(jnp.int32, (tm, tm), 1)
        tri = (col >= row).astype(MXU_DTYPE)
        dlogf = _tri_mm(tri, dc_ref[...]) + carry[...]
        carry[...] = dlogf[0:1, :]
        dfl = dlogf * _sigmoid(-fl_ref[...])
        dfl_ref[...] = dfl
        dbf_acc[...] += jnp.sum(dfl, axis=0, keepdims=True)
        x = h_ref[...]
        dn = _mm_nt(dk, wk_ref[...]) + _mm_nt(dv, wv_ref[...]) + _mm_nt(dfl, wf_ref[...])
        r = lax.rsqrt(jnp.mean(x * x, axis=-1, keepdims=True) + EPS)
        dx, dgr = _rms_bwd(x, r, g_ref[...], dn)
        di_ref[...] = d_ref[...] + dx
        dg_acc[...] += _colsum8(dgr)

        @pl.when(i == nt - 1)
        def _():
            dg_ref[...] = jnp.sum(dg_acc[...], axis=0, keepdims=True)
            dbf_ref[...] = dbf_acc[...]

    rev = lambda n: _rows_rev(tm, n, nt)
    return pl.pallas_call(
        body, name="kv_bwd", grid=(nt,),
        in_specs=([rev(D)] + [rev(D)] * (2 * nl)
                  + [rev(H), rev(H), rev(D), _whole((1, D)), _whole(wk.shape), _whole(wv.shape), _whole(wf.shape)]),
        out_specs=[rev(D), rev(D), rev(D), rev(H), _whole((1, D)), _whole((1, H))],
        out_shape=[jax.ShapeDtypeStruct((S, D), F32), jax.ShapeDtypeStruct((S, D), ACT_DTYPE),
                   jax.ShapeDtypeStruct((S, D), ACT_DTYPE), jax.ShapeDtypeStruct((S, H), F32),
                   jax.ShapeDtypeStruct((1, D), F32), jax.ShapeDtypeStruct((1, H), F32)],
        scratch_shapes=[pltpu.VMEM((8, D), F32), pltpu.VMEM((1, H), F32), pltpu.VMEM((1, H), F32)],
        compiler_params=_params(1),
    )(d, *dks, *dvs, dc, fl, h, g, wk, wv, wf)


def conv_bwd(d, h, g, w1, wd, lg, lb, w2, u, z, tm):
    S, D = h.shape
    CW = wd.shape[0]
    nt = S // tm
    assert tm >= HALO and CW - 1 <= HALO

    def body(d_ref, h_ref, g_ref, w1_ref, wd_ref, lg_ref, lb_ref, w2_ref, u_ref, z_ref,
             di_ref, du_ref, db2_ref, dlg_ref, dlb_ref, dbd_ref, dwd_ref, db1_ref, dg_ref,
             ext, db2_acc, dlg_acc, dlb_acc, dbd_acc, dwd_acc, db1_acc, dg_acc):
        i = pl.program_id(0)

        @pl.when(i == 0)
        def _():
            ext[tm:tm + HALO, :] = jnp.zeros((HALO, D), F32)
            for acc in (db2_acc, dlg_acc, dlb_acc, dbd_acc, dwd_acc, db1_acc, dg_acc):
                acc[...] = jnp.zeros_like(acc)

        dd = d_ref[...]
        db2_acc[...] += _colsum8(dd)
        dsw = _mm_nt(dd, w2_ref[...])
        zz = z_ref[...]
        zc = zz - jnp.mean(zz, axis=-1, keepdims=True)
        rs = lax.rsqrt(jnp.mean(zc * zc, axis=-1, keepdims=True) + EPS)
        xh = zc * rs
        lgv = lg_ref[...]
        y = xh * lgv + lb_ref[...]
        sg = _sigmoid(y)
        dy = dsw * (sg * (1.0 + y * (1.0 - sg)))
        dlg_acc[...] += _colsum8(dy * xh)
        dlb_acc[...] += _colsum8(dy)
        dxh = dy * lgv
        dz = rs * (dxh - jnp.mean(dxh, axis=-1, keepdims=True) - xh * jnp.mean(dxh * xh, axis=-1, keepdims=True))
        dbd_acc[...] += _colsum8(dz)
        ext[0:tm, :] = dz
        uu = u_ref[...]
        a = uu[:, :D]
        sgg = _sigmoid(uu[:, D:])
        glu = a * sgg
        dglu = jnp.zeros((tm, D), F32)
        for k in range(CW):
            sh = ext[CW - 1 - k:CW - 1 - k + tm, :]
            dglu = dglu + wd_ref[k:k + 1, :] * sh
            dwd_acc[k] += _colsum8(glu * sh)
        ext[tm:tm + HALO, :] = ext[0:HALO, :]
        da = dglu * sgg
        dgg = dglu * a * sgg * (1.0 - sgg)
        du_ref[:, :D] = da.astype(du_ref.dtype)
        du_ref[:, D:] = dgg.astype(du_ref.dtype)
        db1_acc[:, :D] += _colsum8(da)
        db1_acc[:, D:] += _colsum8(dgg)
        dn = _mm_nt(da, w1_ref[:, :D]) + _mm_nt(dgg, w1_ref[:, D:])
        x = h_ref[...]
        r = lax.rsqrt(jnp.mean(x * x, axis=-1, keepdims=True) + EPS)
        dx, dgr = _rms_bwd(x, r, g_ref[...], dn)
        di_ref[...] = dd + dx
        dg_acc[...] += _colsum8(dgr)

        @pl.when(i == nt - 1)
        def _():
            db2_ref[...] = jnp.sum(db2_acc[...], axis=0, keepdims=True)
            dlg_ref[...] = jnp.sum(dlg_acc[...], axis=0, keepdims=True)
            dlb_ref[...] = jnp.sum(dlb_acc[...], axis=0, keepdims=True)
            dbd_ref[...] = jnp.sum(dbd_acc[...], axis=0, keepdims=True)
            dwd_ref[...] = jnp.sum(dwd_acc[...], axis=1)
            db1_ref[...] = jnp.sum(db1_acc[...], axis=0, keepdims=True)
            dg_ref[...] = jnp.sum(dg_acc[...], axis=0, keepdims=True)

    rev = lambda n: _rows_rev(tm, n, nt)
    vec = jax.ShapeDtypeStruct((1, D), F32)
    return pl.pallas_call(
        body, name="conv_bwd", grid=(nt,),
        in_specs=[rev(D), rev(D), _whole((1, D)), _whole(w1.shape), _whole(wd.shape), _whole((1, D)),
                  _whole((1, D)), _whole(w2.shape), rev(2 * D), rev(D)],
        out_specs=[rev(D), rev(2 * D), _whole((1, D)), _whole((1, D)), _whole((1, D)), _whole((1, D)),
                   _whole((CW, D)), _whole((1, 2 * D)), _whole((1, D))],
        out_shape=[jax.ShapeDtypeStruct((S, D), F32), jax.ShapeDtypeStruct((S, 2 * D), ACT_DTYPE),
                   vec, vec, vec, vec, jax.ShapeDtypeStruct((CW, D), F32),
                   jax.ShapeDtypeStruct((1, 2 * D), F32), vec],
        scratch_shapes=[pltpu.VMEM((tm + HALO, D), F32), pltpu.VMEM((8, D), F32), pltpu.VMEM((8, D), F32),
                        pltpu.VMEM((8, D), F32), pltpu.VMEM((8, D), F32), pltpu.VMEM((CW, 8, D), F32),
                        pltpu.VMEM((8, 2 * D), F32), pltpu.VMEM((8, D), F32)],
        compiler_params=_params(1),
    )(d, h, g, w1, wd, lg, lb, w2, u, z)


def weight_grad(a, b, ts, name):
    S, M = a.shape
    N = b.shape[1]
    ta = M if M <= 1024 else 1024
    tb = N if N <= 1024 else 1024
    assert M % ta == 0 and N % tb == 0 and S % ts == 0

    def body(a_ref, b_ref, o_ref):
        @pl.when(pl.program_id(2) == 0)
        def _():
            o_ref[...] = jnp.zeros_like(o_ref)

        o_ref[...] += _mm_tn(a_ref[...], b_ref[...])

    return pl.pallas_call(
        body, name=name, grid=(M // ta, N // tb, S // ts),
        in_specs=[pl.BlockSpec((ts, ta), lambda i, j, s: (s, i)), pl.BlockSpec((ts, tb), lambda i, j, s: (s, j))],
        out_specs=pl.BlockSpec((ta, tb), lambda i, j, s: (i, j)),
        out_shape=jax.ShapeDtypeStruct((M, N), F32),
        compiler_params=_params(3),
    )(a, b)


def _position():
    return lax.axis_index("x"), lax.axis_index("y"), lax.axis_index("c")


def all_gather(x, name):
    def body(x_ref, out_ref, send_sems, recv_sems, local_sem):
        x, y, c = _position()
        me, sibling = (x, y, c), (x, y, 1 - c)
        chips = [(1 - x, y), (x, 1 - y), (1 - x, 1 - y)]

        def slot(px, py, pc):
            return out_ref.at[4 * px + 2 * py + pc]

        def copy(k, block, to, src=None):
            return pltpu.make_async_remote_copy(
                src_ref=slot(*block) if src is None else src, dst_ref=slot(*block),
                send_sem=send_sems.at[k], recv_sem=recv_sems.at[k], device_id=to, device_id_type=MESH)

        mine = pltpu.make_async_copy(x_ref, slot(*me), local_sem)
        mine.start()
        first = [copy(0, me, sibling, src=x_ref)]
        first += [copy(1 + j, me, (*chip, c), src=x_ref) for j, chip in enumerate(chips)]
        for cp in first:
            cp.start()
        passed = [copy(4 + j, (*chip, c), sibling) for j, chip in enumerate(chips)]
        for j, chip in enumerate(chips):
            copy(1 + j, (*chip, c), me).wait_recv()
            passed[j].start()
        copy(0, sibling, me).wait_recv()
        for j, chip in enumerate(chips):
            copy(4 + j, (*chip, 1 - c), me).wait_recv()
        for cp in first + passed:
            cp.wait_send()
        mine.wait()

    return pl.pallas_call(
        body, name=name,
        in_specs=[pl.BlockSpec(memory_space=pl.ANY)], out_specs=pl.BlockSpec(memory_space=pl.ANY),
        out_shape=jax.ShapeDtypeStruct((N_DEV,) + x.shape, x.dtype),
        scratch_shapes=[pltpu.SemaphoreType.DMA((7,)), pltpu.SemaphoreType.DMA((7,)), pltpu.SemaphoreType.DMA],
    )(x)


def sibling_exchange(g):
    _, R, C = g.shape

    def body(g_ref, land_ref, send_sems, recv_sems):
        x, y, c = _position()
        copies = [pltpu.make_async_remote_copy(
            src_ref=g_ref.at[2 * j + 1 - c], dst_ref=land_ref.at[j], send_sem=send_sems.at[j],
            recv_sem=recv_sems.at[j], device_id=(x, y, 1 - c), device_id_type=MESH) for j in range(N_CHIP)]
        for cp in copies:
            cp.start()
        for cp in copies:
            cp.wait()

    return pl.pallas_call(
        body, name="grad_sibling_exchange",
        in_specs=[pl.BlockSpec(memory_space=pl.ANY)], out_specs=pl.BlockSpec(memory_space=pl.ANY),
        out_shape=jax.ShapeDtypeStruct((N_CHIP, R, C), g.dtype),
        scratch_shapes=[pltpu.SemaphoreType.DMA((N_CHIP,)), pltpu.SemaphoreType.DMA((N_CHIP,))],
    )(g)


def chip_partial(g, land, core):
    _, R, C = g.shape
    tr = _row_tile(R, PACK_ROW_TILE)

    def body(c_ref, g_ref, l_ref, o_ref):
        o_ref[...] = (g_ref[...] + l_ref[...]).astype(o_ref.dtype)

    grid_spec = pltpu.PrefetchScalarGridSpec(
        num_scalar_prefetch=1, grid=(N_CHIP, R // tr),
        in_specs=[pl.BlockSpec((None, tr, C), lambda j, i, cr: (2 * j + cr[0], i, 0)),
                  pl.BlockSpec((None, tr, C), lambda j, i, cr: (j, i, 0))],
        out_specs=pl.BlockSpec((None, tr, C), lambda j, i, cr: (j, i, 0)))
    return pl.pallas_call(
        body, name="grad_chip_partial", grid_spec=grid_spec,
        out_shape=jax.ShapeDtypeStruct((N_CHIP, R, C), WIRE_DTYPE),
        compiler_params=_params(2),
    )(core, g, land)


def chip_exchange(part):
    def body(p_ref, land_ref, send_sems, recv_sems, local_sem):
        x, y, c = _position()
        mychip = 2 * x + y
        chips = [(1 - x, y), (x, 1 - y), (1 - x, 1 - y)]
        mine = pltpu.make_async_copy(p_ref.at[mychip], land_ref.at[mychip], local_sem)
        mine.start()
        copies = [pltpu.make_async_remote_copy(
            src_ref=p_ref.at[2 * cx + cy], dst_ref=land_ref.at[mychip], send_sem=send_sems.at[k],
            recv_sem=recv_sems.at[k], device_id=(cx, cy, c), device_id_type=MESH)
            for k, (cx, cy) in enumerate(chips)]
        for cp in copies:
            cp.start()
        for k, (cx, cy) in enumerate(chips):
            pltpu.make_async_remote_copy(
                src_ref=p_ref.at[2 * cx + cy], dst_ref=land_ref.at[2 * cx + cy], send_sem=send_sems.at[k],
                recv_sem=recv_sems.at[k], device_id=(cx, cy, c), device_id_type=MESH).wait_recv()
        for cp in copies:
            cp.wait_send()
        mine.wait()

    return pl.pallas_call(
        body, name="grad_chip_exchange",
        in_specs=[pl.BlockSpec(memory_space=pl.ANY)], out_specs=pl.BlockSpec(memory_space=pl.ANY),
        out_shape=jax.ShapeDtypeStruct(part.shape, part.dtype),
        scratch_shapes=[pltpu.SemaphoreType.DMA((3,)), pltpu.SemaphoreType.DMA((3,)), pltpu.SemaphoreType.DMA],
    )(part)


def _adamw(w, g, m, v):
    m = ADAM_B1 * m + (1.0 - ADAM_B1) * g
    v = ADAM_B2 * v + (1.0 - ADAM_B2) * jnp.square(g)
    m_hat = m / (1.0 - ADAM_B1 ** ADAM_STEP)
    v_hat = v / (1.0 - ADAM_B2 ** ADAM_STEP)
    delta = -ADAM_LR * (m_hat / (jnp.sqrt(v_hat) + ADAM_EPS) + ADAM_WD * w)
    return delta, m, v


def adamw_sharded(parts, w, m, v):
    R, C = w.shape
    tr = _row_tile(R, PACK_ROW_TILE)

    def body(p_ref, w_ref, m_ref, v_ref, g_ref, d_ref, nm_ref, nv_ref):
        g = p_ref[0].astype(F32)
        for j in range(1, N_CHIP):
            g = g + p_ref[j].astype(F32)
        g_ref[...] = g
        d_ref[...], nm_ref[...], nv_ref[...] = _adamw(w_ref[...], g, m_ref[...], v_ref[...])

    out = jax.ShapeDtypeStruct((R, C), F32)
    return pl.pallas_call(
        body, name="adamw_sharded", grid=(R // tr,),
        in_specs=[pl.BlockSpec((N_CHIP, tr, C), lambda i: (0, i, 0)), _rows(tr, C), _rows(tr, C), _rows(tr, C)],
        out_specs=[_rows(tr, C)] * 4, out_shape=[out] * 4,
        compiler_params=_params(1),
    )(parts, w, m, v)


def adamw_replicated(gathered, w, m, v):
    R, C = w.shape

    def body(p_ref, w_ref, m_ref, v_ref, g_ref, d_ref, nm_ref, nv_ref):
        g = p_ref[0]
        for j in range(1, N_DEV):
            g = g + p_ref[j]
        g_ref[...] = g
        d_ref[...], nm_ref[...], nv_ref[...] = _adamw(w_ref[...], g, m_ref[...], v_ref[...])

    out = jax.ShapeDtypeStruct((R, C), F32)
    return pl.pallas_call(
        body, name="adamw_replicated", grid=(1,),
        in_specs=[_whole(gathered.shape), _whole((R, C)), _whole((R, C)), _whole((R, C))],
        out_specs=[_whole((R, C))] * 4, out_shape=[out] * 4,
        compiler_params=_params(1),
    )(gathered, w, m, v)


def _pack_rows(flat, rows_multiple):
    n = flat.shape[-1]
    per = PACK_COLS * rows_multiple
    padded = -(-n // per) * per
    flat = jnp.pad(flat, [(0, 0)] * (flat.ndim - 1) + [(0, padded - n)])
    return flat.reshape(flat.shape[:-1] + (padded // PACK_COLS, PACK_COLS))


def _pack(arrays, rows_multiple, dtype=None):
    flat = jnp.concatenate([a.reshape(-1) if dtype is None else a.reshape(-1).astype(dtype) for a in arrays])
    return _pack_rows(flat, rows_multiple)


def _unpack(packed, shapes):
    flat = packed.reshape(-1)
    out, off = [], 0
    for shp in shapes:
        n = int(np.prod(shp))
        out.append(flat[off:off + n].reshape(shp))
        off += n
    return out


def _full_from_gathered(gathered, shard_shapes, axes):
    flat = gathered.reshape(N_DEV, -1)
    out, off = [], 0
    for shp, ax in zip(shard_shapes, axes):
        n = int(np.prod(shp))
        seg = jnp.moveaxis(flat[:, off:off + n].reshape((N_DEV,) + tuple(shp)), 0, ax)
        out.append(seg.reshape(tuple(shp[:ax]) + (N_DEV * shp[ax],) + tuple(shp[ax + 1:])))
        off += n
    return out


def _chunks_from_full(full, ax):
    shp = full.shape
    split = full.reshape(shp[:ax] + (N_DEV, shp[ax] // N_DEV) + shp[ax + 1:])
    return jnp.moveaxis(split, ax, 0).reshape(N_DEV, -1)


def kernel(x, p, mix_norm, conv_w_pw1, conv_b_pw1, conv_w_dw, conv_b_dw, conv_ln_g, conv_ln_b, conv_w_pw2, conv_b_pw2, kv_norm, w_kvf, b_f, attn_w_q, attn_w_o, ffn_norm, ffn_w1, ffn_w2, ple_norm, ple_w_gate, ple_w_proj, final_norm, loss_target, m_mix_norm, m_conv_w_pw1, m_conv_b_pw1, m_conv_w_dw, m_conv_b_dw, m_conv_ln_g, m_conv_ln_b, m_conv_w_pw2, m_conv_b_pw2, m_kv_norm, m_w_kvf, m_b_f, m_attn_w_q, m_attn_w_o, m_ffn_norm, m_ffn_w1, m_ffn_w2, m_ple_norm, m_ple_w_gate, m_ple_w_proj, m_final_norm, v_mix_norm, v_conv_w_pw1, v_conv_b_pw1, v_conv_w_dw, v_conv_b_dw, v_conv_ln_g, v_conv_ln_b, v_conv_w_pw2, v_conv_b_pw2, v_kv_norm, v_w_kvf, v_b_f, v_attn_w_q, v_attn_w_o, v_ffn_norm, v_ffn_w1, v_ffn_w2, v_ple_norm, v_ple_w_gate, v_ple_w_proj, v_final_norm):
    given = dict(locals())
    W = {n: given[n] for n in WEIGHTS}
    M = {n: given["m_" + n] for n in WEIGHTS}
    V = {n: given["v_" + n] for n in WEIGHTS}

    _, S, D = x.shape
    NA = conv_w_pw1.shape[0]
    NB = attn_w_q.shape[0]
    DEPTH = NA + NB
    H = b_f.shape[0]
    dh = D // H
    hg = 128 // dh
    G = D // 128
    scale = dh ** -0.5
    tm = _row_tile(S, 256)
    t_attn = _row_tile(S, 512)
    ts = _row_tile(S, 512)
    xs = x[0]
    tgt = loss_target[0]
    ps = p[:, 0]
    row = lambda a: a.reshape(1, -1)

    big_names = list(SHARD_AXIS_BIG)
    small_names = list(SHARD_AXIS_SMALL)
    big = _full_from_gathered(
        all_gather(_pack([W[n] for n in big_names], 16, MXU_DTYPE), "weights_all_gather"),
        [W[n].shape for n in big_names], [SHARD_AXIS_BIG[n] for n in big_names])
    small = _full_from_gathered(
        all_gather(_pack([W[n] for n in small_names], 8), "vectors_all_gather"),
        [W[n].shape for n in small_names], [SHARD_AXIS_SMALL[n] for n in small_names])
    FW = dict(zip(big_names + small_names, big + small))
    wk, wv, wf = FW["w_kvf"][:, :D], FW["w_kvf"][:, D:2 * D], FW["w_kvf"][:, 2 * D:]

    saved = []
    h = xs
    kv = None
    for i in range(DEPTH):
        rec = {"h_in": h}
        if i < NA:
            h, rec["n"], rec["u"], rec["z"], rec["sw"] = conv_fwd(
                h, row(mix_norm[i]), FW["conv_w_pw1"][i], row(FW["conv_b_pw1"][i]), FW["conv_w_dw"][i],
                row(FW["conv_b_dw"][i]), row(FW["conv_ln_g"][i]), row(FW["conv_ln_b"][i]),
                FW["conv_w_pw2"][i], row(FW["conv_b_pw2"][i]), tm)
        else:
            j = i - NA
            if j == 0:
                k_, v_, nkv, fl, c = kv_fwd(h, row(kv_norm), wk, wv, wf, row(b_f), tm)
                cg = c.reshape(S, G, hg)
                kv = dict(k=k_, v=v_, n=nkv, fl=fl, h=h, c_col=jnp.transpose(cg, (1, 0, 2)),
                          c_row=jnp.transpose(cg, (1, 2, 0)))
            rec["n"], rec["q"] = q_fwd(h, row(mix_norm[i]), FW["attn_w_q"][j], scale, tm)
            rec["o"], rec["o32"], rec["lse"] = flash_fwd(rec["q"], kv["k"], kv["v"], kv["c_col"], kv["c_row"], dh, t_attn)
            h = attn_out_fwd(h, rec["o"], FW["attn_w_o"][j], tm)
        rec["h_ffn"] = h
        h, rec["n_ffn"], rec["a"], rec["s"] = ffn_fwd(h, row(ffn_norm[i]), FW["ffn_w1"][i], FW["ffn_w2"][i], tm)
        rec["h_ple"] = h
        h, rec["n_ple"], rec["gate"] = ple_fwd(h, row(ple_norm[i]), FW["ple_w_gate"][i], ps[i],
                                               FW["ple_w_proj"][i], tm)
        saved.append(rec)

    d, g_final, loss_part = loss_head(h, row(final_norm), tgt, tm)
    GW = {n: [None] * W[n].shape[0] for n in WEIGHTS if W[n].ndim > 1 and n != "w_kvf"}
    dks, dvs, dcs = [], [], []
    for i in reversed(range(DEPTH)):
        rec = saved[i]
        d_out = d
        d, dz, dpp, GW["ple_norm"][i] = ple_bwd(d_out, rec["h_ple"], row(ple_norm[i]), FW["ple_w_gate"][i],
                                                rec["gate"], ps[i], FW["ple_w_proj"][i], tm)
        GW["ple_w_gate"][i] = weight_grad(rec["n_ple"], dz, ts, "grad_ple_w_gate")
        GW["ple_w_proj"][i] = weight_grad(ps[i], dpp, ts, "grad_ple_w_proj")
        d_out = d
        d, da, GW["ffn_norm"][i] = ffn_bwd(d_out, rec["h_ffn"], row(ffn_norm[i]), FW["ffn_w1"][i],
                                           FW["ffn_w2"][i], rec["a"], tm)
        GW["ffn_w2"][i] = weight_grad(rec["s"], d_out, ts, "grad_ffn_w2")
        GW["ffn_w1"][i] = weight_grad(rec["n_ffn"], da, ts, "grad_ffn_w1")
        d_out = d
        if i >= NA:
            j = i - NA
            GW["attn_w_o"][j] = weight_grad(rec["o"], d_out, ts, "grad_attn_w_o")
            do = attn_out_bwd(d_out, FW["attn_w_o"][j], tm)
            dq, dk, dv, dck = flash_bwd(rec["q"], kv["k"], kv["v"], kv["c_col"], kv["c_row"], rec["o32"], do,
                                        rec["lse"], dh, t_attn)
            dks.append(dk)
            dvs.append(dv)
            dcs.append(jnp.transpose(dck[:, :hg, :], (2, 0, 1)).reshape(S, H))
            d, dqs, GW["mix_norm"][i] = q_bwd(d_out, dq, rec["h_in"], row(mix_norm[i]), FW["attn_w_q"][j], scale, tm)
            GW["attn_w_q"][j] = weight_grad(rec["n"], dqs, ts, "grad_attn_w_q")
            if j == 0:
                dc = dcs[0]
                for extra in dcs[1:]:
                    dc = dc + extra
                d, dk_sum, dv_sum, dfl, g_kv_norm, g_b_f = kv_bwd(d, dks, dvs, dc, kv["fl"], kv["h"],
                                                                  row(kv_norm), wk, wv, wf, tm)
                g_w_kvf = jnp.concatenate([weight_grad(kv["n"], dk_sum, ts, "grad_w_k"),
                                           weight_grad(kv["n"], dv_sum, ts, "grad_w_v"),
                                           weight_grad(kv["n"], dfl, ts, "grad_w_f")], axis=1)
        else:
            GW["conv_w_pw2"][i] = weight_grad(rec["sw"], d_out, ts, "grad_conv_w_pw2")
            (d, du, GW["conv_b_pw2"][i], GW["conv_ln_g"][i], GW["conv_ln_b"][i], GW["conv_b_dw"][i],
             GW["conv_w_dw"][i], GW["conv_b_pw1"][i], GW["mix_norm"][i]) = conv_bwd(
                d_out, rec["h_in"], row(mix_norm[i]), FW["conv_w_pw1"][i], FW["conv_w_dw"][i],
                row(FW["conv_ln_g"][i]), row(FW["conv_ln_b"][i]), FW["conv_w_pw2"][i], rec["u"], rec["z"], tm)
            GW["conv_w_pw1"][i] = weight_grad(rec["n"], du, ts, "grad_conv_w_pw1")
    grad_x = d[None]

    def stacked(n):
        return jnp.concatenate(GW[n], axis=0) if W[n].ndim == 2 else jnp.stack(GW[n])

    full_grads = {n: stacked(n) for n in GW if n not in REPLICATED}
    full_grads["w_kvf"] = g_w_kvf

    sharded_names = big_names + small_names
    axes = {**SHARD_AXIS_BIG, **SHARD_AXIS_SMALL}
    chunks = jnp.concatenate([_chunks_from_full(full_grads[n], axes[n]) for n in sharded_names], axis=1)
    chunks = _pack_rows(chunks, PACK_ROW_TILE)
    core = lax.axis_index("c").astype(jnp.int32).reshape(1)
    parts = chip_exchange(chip_partial(chunks, sibling_exchange(chunks), core))

    pack_sh = lambda src: _pack([src[n] for n in sharded_names], PACK_ROW_TILE)
    outs_sh = adamw_sharded(parts, pack_sh(W), pack_sh(M), pack_sh(V))
    shard_shapes = [W[n].shape for n in sharded_names]
    res = {}
    for kind, packed in zip(("grad", "delta", "new_m", "new_v"), outs_sh):
        for n, a in zip(sharded_names, _unpack(packed, shard_shapes)):
            res[kind, n] = a

    rep_grads = {"mix_norm": jnp.concatenate(GW["mix_norm"], axis=0), "kv_norm": g_kv_norm,
                 "b_f": g_b_f, "ffn_norm": jnp.concatenate(GW["ffn_norm"], axis=0),
                 "ple_norm": jnp.concatenate(GW["ple_norm"], axis=0), "final_norm": g_final}

    def pack_rep(src, extra=None):
        rows_ = [jnp.pad(src[n].reshape(-1, src[n].shape[-1]), ((0, 0), (0, D - src[n].shape[-1])))
                 for n in REPLICATED]
        if extra is not None:
            rows_.append(jnp.pad(extra, ((0, 0), (0, D - extra.shape[-1]))))
        else:
            rows_.append(jnp.zeros((1, D), F32))
        flat = jnp.concatenate(rows_, axis=0)
        return jnp.pad(flat, ((0, -flat.shape[0] % 8), (0, 0)))

    rep_g = all_gather(pack_rep(rep_grads, loss_part), "replicated_all_gather")
    outs_rep = adamw_replicated(rep_g, pack_rep(W), pack_rep(M), pack_rep(V))
    n_rep_rows = sum(int(np.prod(W[n].shape[:-1])) for n in REPLICATED)
    for kind, packed in zip(("grad", "delta", "new_m", "new_v"), outs_rep):
        r0 = 0
        for n in REPLICATED:
            nr = int(np.prod(W[n].shape[:-1]))
            res[kind, n] = packed[r0:r0 + nr, :W[n].shape[-1]].reshape(W[n].shape)
            r0 += nr
    loss = outs_rep[0][n_rep_rows, 0]

    return (loss, grad_x, *[res["grad", n] for n in WEIGHTS], *[res["delta", n] for n in WEIGHTS],
            *[res["new_m", n] for n in WEIGHTS], *[res["new_v", n] for n in WEIGHTS])
```

```python
import numpy as np
import jax
import jax.numpy as jnp
from jax import lax
from jax.experimental import pallas as pl
from jax.experimental.pallas import tpu as pltpu

F32 = jnp.float32
MXU_DTYPE = jnp.bfloat16
ACT_DTYPE = jnp.bfloat16
WIRE_DTYPE = jnp.bfloat16

N_DEV = 8
N_CHIP = 4
EPS = 1e-6
NEG_BIG = -1e30
ADAM_LR = 0.001
ADAM_B1 = 0.9
ADAM_B2 = 0.999
ADAM_EPS = 1e-08
ADAM_WD = 0.01
ADAM_STEP = 10

VMEM_LIMIT_BYTES = 56 * 1024 * 1024
PACK_COLS = 1024
PACK_ROW_TILE = 256
HALO = 32
MESH = pl.DeviceIdType.MESH

SHARD_AXIS_BIG = {"conv_w_pw1": 2, "conv_w_pw2": 1, "w_kvf": 1, "attn_w_q": 1, "attn_w_o": 1,
                  "ffn_w1": 2, "ffn_w2": 1, "ple_w_gate": 1, "ple_w_proj": 2}
SHARD_AXIS_SMALL = {"conv_b_pw1": 1, "conv_w_dw": 2, "conv_b_dw": 1, "conv_ln_g": 1, "conv_ln_b": 1,
                    "conv_b_pw2": 1}
REPLICATED = ["mix_norm", "kv_norm", "b_f", "ffn_norm", "ple_norm", "final_norm"]
WEIGHTS = ["mix_norm", "conv_w_pw1", "conv_b_pw1", "conv_w_dw", "conv_b_dw", "conv_ln_g", "conv_ln_b",
           "conv_w_pw2", "conv_b_pw2", "kv_norm", "w_kvf", "b_f", "attn_w_q", "attn_w_o", "ffn_norm",
           "ffn_w1", "ffn_w2", "ple_norm", "ple_w_gate", "ple_w_proj", "final_norm"]


def _mm(a, b):
    return jnp.dot(a.astype(MXU_DTYPE), b.astype(MXU_DTYPE), preferred_element_type=F32)


def _mm_nt(a, b):
    return lax.dot_general(a.astype(MXU_DTYPE), b.astype(MXU_DTYPE), (((1,), (1,)), ((), ())),
                           preferred_element_type=F32)


def _mm_tn(a, b):
    return lax.dot_general(a.astype(MXU_DTYPE), b.astype(MXU_DTYPE), (((0,), (0,)), ((), ())),
                           preferred_element_type=F32)


def _split3(x):
    hi = x.astype(MXU_DTYPE)
    r1 = x - hi.astype(F32)
    mid = r1.astype(MXU_DTYPE)
    lo = (r1 - mid.astype(F32)).astype(MXU_DTYPE)
    return hi, mid, lo


def _tri_mm(tri, x):
    hi, mid, lo = _split3(x)
    return (jnp.dot(tri, lo, preferred_element_type=F32) + jnp.dot(tri, mid, preferred_element_type=F32)
            + jnp.dot(tri, hi, preferred_element_type=F32))


def _colsum8(x):
    tm, n = x.shape
    return jnp.sum(x.reshape(tm // 8, 8, n), axis=0)


def _rms(x, g):
    r = lax.rsqrt(jnp.mean(x * x, axis=-1, keepdims=True) + EPS)
    return x * r * g, r


def _rms_bwd(x, r, g, dn):
    w = dn * g
    dx = r * w - x * (r * r * r) * jnp.mean(w * x, axis=-1, keepdims=True)
    return dx, dn * x * r


def _sigmoid(x):
    return jax.nn.sigmoid(x)


def _params(n_grid):
    return pltpu.CompilerParams(dimension_semantics=("arbitrary",) * n_grid, vmem_limit_bytes=VMEM_LIMIT_BYTES)


def _rows(tm, n):
    return pl.BlockSpec((tm, n), lambda i: (i, 0))


def _rows_rev(tm, n, nt):
    return pl.BlockSpec((tm, n), lambda i: (nt - 1 - i, 0))


def _whole(shape):
    nd = len(shape)
    return pl.BlockSpec(shape, lambda i: (0,) * nd)


def _row_tile(s, want):
    tm = min(s, want)
    assert s % tm == 0 and tm % 8 == 0, (s, tm)
    return tm


def conv_fwd(h, g, w1, b1, wd, bd, lg, lb, w2, b2, tm):
    S, D = h.shape
    CW = wd.shape[0]
    off = HALO - (CW - 1)
    assert 0 <= off and tm >= HALO
    nt = S // tm

    def body(h_ref, g_ref, w1_ref, b1_ref, wd_ref, bd_ref, lg_ref, lb_ref, w2_ref, b2_ref,
             ho_ref, n_ref, u_ref, z_ref, sw_ref, ext):
        @pl.when(pl.program_id(0) == 0)
        def _():
            ext[0:HALO, :] = jnp.zeros((HALO, D), F32)

        x = h_ref[...]
        n, _ = _rms(x, g_ref[...])
        n_ref[...] = n.astype(n_ref.dtype)
        u = _mm(n, w1_ref[...]) + b1_ref[...]
        u_ref[...] = u
        ext[HALO:HALO + tm, :] = u[:, :D] * _sigmoid(u[:, D:])
        z = jnp.broadcast_to(bd_ref[...], (tm, D))
        for k in range(CW):
            z = z + wd_ref[k:k + 1, :] * ext[off + k:off + k + tm, :]
        z_ref[...] = z
        ext[0:HALO, :] = ext[tm:tm + HALO, :]
        mu = jnp.mean(z, axis=-1, keepdims=True)
        zc = z - mu
        y = zc * lax.rsqrt(jnp.mean(zc * zc, axis=-1, keepdims=True) + EPS) * lg_ref[...] + lb_ref[...]
        sw = y * _sigmoid(y)
        sw_ref[...] = sw.astype(sw_ref.dtype)
        ho_ref[...] = x + _mm(sw, w2_ref[...]) + b2_ref[...]

    return pl.pallas_call(
        body, name="conv_fwd", grid=(nt,),
        in_specs=[_rows(tm, D), _whole((1, D)), _whole(w1.shape), _whole((1, 2 * D)), _whole(wd.shape),
                  _whole((1, D)), _whole((1, D)), _whole((1, D)), _whole(w2.shape), _whole((1, D))],
        out_specs=[_rows(tm, D), _rows(tm, D), _rows(tm, 2 * D), _rows(tm, D), _rows(tm, D)],
        out_shape=[jax.ShapeDtypeStruct((S, D), F32), jax.ShapeDtypeStruct((S, D), ACT_DTYPE),
                   jax.ShapeDtypeStruct((S, 2 * D), F32), jax.ShapeDtypeStruct((S, D), F32),
                   jax.ShapeDtypeStruct((S, D), ACT_DTYPE)],
        scratch_shapes=[pltpu.VMEM((HALO + tm, D), F32)],
        compiler_params=_params(1),
    )(h, g, w1, b1, wd, bd, lg, lb, w2, b2)


def ffn_fwd(h, g, w1, w2, tm):
    S, D = h.shape
    FF = w1.shape[1]

    def body(h_ref, g_ref, w1_ref, w2_ref, ho_ref, n_ref, a_ref, s_ref):
        x = h_ref[...]
        n, _ = _rms(x, g_ref[...])
        n_ref[...] = n.astype(n_ref.dtype)
        a = _mm(n, w1_ref[...])
        a_ref[...] = a
        s = jnp.square(jnp.maximum(a, 0.0))
        s_ref[...] = s.astype(s_ref.dtype)
        ho_ref[...] = x + _mm(s, w2_ref[...])

    return pl.pallas_call(
        body, name="ffn_fwd", grid=(S // tm,),
        in_specs=[_rows(tm, D), _whole((1, D)), _whole(w1.shape), _whole(w2.shape)],
        out_specs=[_rows(tm, D), _rows(tm, D), _rows(tm, FF), _rows(tm, FF)],
        out_shape=[jax.ShapeDtypeStruct((S, D), F32), jax.ShapeDtypeStruct((S, D), ACT_DTYPE),
                   jax.ShapeDtypeStruct((S, FF), F32), jax.ShapeDtypeStruct((S, FF), ACT_DTYPE)],
        compiler_params=_params(1),
    )(h, g, w1, w2)


def ple_fwd(h, g, wg, p, wp, tm):
    S, D = h.shape
    E = p.shape[1]

    def body(h_ref, g_ref, wg_ref, p_ref, wp_ref, ho_ref, n_ref, gate_ref):
        x = h_ref[...]
        n, _ = _rms(x, g_ref[...])
        n_ref[...] = n.astype(n_ref.dtype)
        gate = _sigmoid(_mm(n, wg_ref[...]))
        gate_ref[...] = gate
        ho_ref[...] = x + gate * _mm(p_ref[...], wp_ref[...])

    return pl.pallas_call(
        body, name="ple_fwd", grid=(S // tm,),
        in_specs=[_rows(tm, D), _whole((1, D)), _whole(wg.shape), _rows(tm, E), _whole(wp.shape)],
        out_specs=[_rows(tm, D), _rows(tm, D), _rows(tm, D)],
        out_shape=[jax.ShapeDtypeStruct((S, D), F32), jax.ShapeDtypeStruct((S, D), ACT_DTYPE),
                   jax.ShapeDtypeStruct((S, D), F32)],
        compiler_params=_params(1),
    )(h, g, wg, p, wp)


def kv_fwd(h, g, wk, wv, wf, bf, tm):
    S, D = h.shape
    H = wf.shape[1]

    def body(h_ref, g_ref, wk_ref, wv_ref, wf_ref, bf_ref, k_ref, v_ref, n_ref, fl_ref, c_ref, carry):
        @pl.when(pl.program_id(0) == 0)
        def _():
            carry[...] = jnp.zeros_like(carry)

        n, _ = _rms(h_ref[...], g_ref[...])
        n_ref[...] = n.astype(n_ref.dtype)
        k_ref[...] = _mm(n, wk_ref[...]).astype(k_ref.dtype)
        v_ref[...] = _mm(n, wv_ref[...]).astype(v_ref.dtype)
        fl = _mm(n, wf_ref[...]) + bf_ref[...]
        fl_ref[...] = fl
        logf = jnp.minimum(fl, 0.0) - jnp.log1p(jnp.exp(-jnp.abs(fl)))
        row = lax.broadcasted_iota(jnp.int32, (tm, tm), 0)
        col = lax.broadcasted_iota(jnp.int32, (tm, tm), 1)
        tri = (row >= col).astype(MXU_DTYPE)
        c = _tri_mm(tri, logf) + carry[...]
        c_ref[...] = c
        carry[...] = c[tm - 1:tm, :]

    return pl.pallas_call(
        body, name="kv_fwd", grid=(S // tm,),
        in_specs=[_rows(tm, D), _whole((1, D)), _whole(wk.shape), _whole(wv.shape), _whole(wf.shape),
                  _whole((1, H))],
        out_specs=[_rows(tm, D), _rows(tm, D), _rows(tm, D), _rows(tm, H), _rows(tm, H)],
        out_shape=[jax.ShapeDtypeStruct((S, D), ACT_DTYPE), jax.ShapeDtypeStruct((S, D), ACT_DTYPE),
                   jax.ShapeDtypeStruct((S, D), ACT_DTYPE), jax.ShapeDtypeStruct((S, H), F32),
                   jax.ShapeDtypeStruct((S, H), F32)],
        scratch_shapes=[pltpu.VMEM((1, H), F32)],
        compiler_params=_params(1),
    )(h, g, wk, wv, wf, bf)


def q_fwd(h, g, wq, scale, tm):
    S, D = h.shape

    def body(h_ref, g_ref, wq_ref, n_ref, q_ref):
        n, _ = _rms(h_ref[...], g_ref[...])
        n_ref[...] = n.astype(n_ref.dtype)
        q_ref[...] = (_mm(n, wq_ref[...]) * scale).astype(q_ref.dtype)

    return pl.pallas_call(
        body, name="q_fwd", grid=(S // tm,),
        in_specs=[_rows(tm, D), _whole((1, D)), _whole(wq.shape)],
        out_specs=[_rows(tm, D), _rows(tm, D)],
        out_shape=[jax.ShapeDtypeStruct((S, D), ACT_DTYPE), jax.ShapeDtypeStruct((S, D), ACT_DTYPE)],
        compiler_params=_params(1),
    )(h, g, wq)


def attn_out_fwd(h, o, wo, tm):
    S, D = h.shape

    def body(h_ref, o_ref, wo_ref, ho_ref):
        ho_ref[...] = h_ref[...] + _mm(o_ref[...], wo_ref[...])

    return pl.pallas_call(
        body, name="attn_out_fwd", grid=(S // tm,),
        in_specs=[_rows(tm, D), _rows(tm, D), _whole(wo.shape)],
        out_specs=_rows(tm, D),
        out_shape=jax.ShapeDtypeStruct((S, D), F32),
        compiler_params=_params(1),
    )(h, o, wo)


def _causal_mask(key0, qry0, shape, key_axis):
    key = key0 + lax.broadcasted_iota(jnp.int32, shape, key_axis)
    qry = qry0 + lax.broadcasted_iota(jnp.int32, shape, 1 - key_axis)
    return key <= qry


def flash_fwd(q, k, vT, c_col, c_row, dh, tq, tkc):
    S, D = q.shape
    hg = 128 // dh
    G = D // 128
    per = tq // tkc
    assert tq % tkc == 0 and S % tq == 0

    def body(q_ref, k_ref, vT_ref, ccol_ref, crow_ref, o_ref, o32_ref, lse_ref, m_scr, l_scr, acc_scr):
        i = pl.program_id(1)
        m_scr[...] = jnp.full(m_scr.shape, NEG_BIG, F32)
        l_scr[...] = jnp.zeros(l_scr.shape, F32)
        acc_scr[...] = jnp.zeros(acc_scr.shape, F32)

        def chunk(j, masked):
            keys = pl.ds(pl.multiple_of(j * tkc, tkc), tkc)
            for hh in range(hg):
                lanes = slice(hh * dh, (hh + 1) * dh)
                c0 = ccol_ref[pl.ds(j * tkc, 1), hh:hh + 1]
                r = crow_ref[hh:hh + 1, :] - c0
                s = _mm_nt(k_ref[keys, lanes], q_ref[:, lanes]) - (ccol_ref[keys, hh:hh + 1] - c0)
                if masked:
                    s = jnp.where(_causal_mask(j * tkc, i * tq, (tkc, tq), 0), s, NEG_BIG)
                m_old = m_scr[hh]
                m_new = jnp.maximum(m_old, jnp.max(s, axis=0, keepdims=True) + r)
                alpha = jnp.exp(m_old - m_new)
                p = jnp.exp(s - (m_new - r))
                l_scr[hh] = alpha * l_scr[hh] + jnp.sum(p, axis=0, keepdims=True)
                p_hi = p.astype(MXU_DTYPE)
                p_lo = p - p_hi.astype(F32)
                vc = vT_ref[lanes, keys]
                acc_scr[lanes, :] = alpha * acc_scr[lanes, :] + (_mm(vc, p_lo) + _mm(vc, p_hi))
                m_scr[hh] = m_new

        def full_chunk(j, carry):
            chunk(j, False)
            return carry

        lax.fori_loop(0, i * per, full_chunk, 0)
        for jj in range(per):
            chunk(i * per + jj, True)
        for hh in range(hg):
            lanes = slice(hh * dh, (hh + 1) * dh)
            acc_scr[lanes, :] = acc_scr[lanes, :] / l_scr[hh]
            lse_ref[hh:hh + 1, :] = m_scr[hh] + jnp.log(l_scr[hh])
        o = acc_scr[...].T
        o_ref[...] = o.astype(o_ref.dtype)
        o32_ref[...] = o

    return pl.pallas_call(
        body, name="flash_fwd", grid=(G, S // tq),
        in_specs=[pl.BlockSpec((tq, 128), lambda g, i: (i, g)),
                  pl.BlockSpec((S, 128), lambda g, i: (0, g)),
                  pl.BlockSpec((128, S), lambda g, i: (g, 0)),
                  pl.BlockSpec((None, S, hg), lambda g, i: (g, 0, 0)),
                  pl.BlockSpec((None, hg, tq), lambda g, i: (g, 0, i))],
        out_specs=[pl.BlockSpec((tq, 128), lambda g, i: (i, g)),
                   pl.BlockSpec((tq, 128), lambda g, i: (i, g)),
                   pl.BlockSpec((None, hg, tq), lambda g, i: (g, 0, i))],
        out_shape=[jax.ShapeDtypeStruct((S, D), ACT_DTYPE), jax.ShapeDtypeStruct((S, D), F32),
                   jax.ShapeDtypeStruct((G, hg, S), F32)],
        scratch_shapes=[pltpu.VMEM((hg, 1, tq), F32), pltpu.VMEM((hg, 1, tq), F32), pltpu.VMEM((128, tq), F32)],
        compiler_params=_params(2),
    )(q, k, vT, c_col, c_row)


def loss_head(h, g, target, tm):
    S, D = h.shape
    nt = S // tm

    def body(h_ref, g_ref, t_ref, dh_ref, dg_ref, loss_ref, dg_acc, loss_acc):
        i = pl.program_id(0)

        @pl.when(i == 0)
        def _():
            dg_acc[...] = jnp.zeros_like(dg_acc)
            loss_acc[...] = jnp.zeros_like(loss_acc)

        x = h_ref[...]
        gg = g_ref[...]
        y, r = _rms(x, gg)
        e = y - t_ref[...]
        loss_acc[...] += 0.5 * jnp.sum(jnp.mean(e * e, axis=-1, keepdims=True), axis=0, keepdims=True)
        dx, dgr = _rms_bwd(x, r, gg, e / D)
        dh_ref[...] = dx
        dg_acc[...] += _colsum8(dgr)

        @pl.when(i == nt - 1)
        def _():
            dg_ref[...] = jnp.sum(dg_acc[...], axis=0, keepdims=True)
            loss_ref[...] = jnp.broadcast_to(loss_acc[...], loss_ref.shape)

    return pl.pallas_call(
        body, name="loss_head", grid=(nt,),
        in_specs=[_rows(tm, D), _whole((1, D)), _rows(tm, D)],
        out_specs=[_rows(tm, D), _whole((1, D)), _whole((1, 128))],
        out_shape=[jax.ShapeDtypeStruct((S, D), F32), jax.ShapeDtypeStruct((1, D), F32),
                   jax.ShapeDtypeStruct((1, 128), F32)],
        scratch_shapes=[pltpu.VMEM((8, D), F32), pltpu.VMEM((1, 1), F32)],
        compiler_params=_params(1),
    )(h, g, target)


def ple_bwd(d, h, g, wg, gate, p, wp, tm):
    S, D = h.shape
    E = p.shape[1]
    nt = S // tm

    def body(d_ref, h_ref, g_ref, wg_ref, gate_ref, p_ref, wp_ref, di_ref, dz_ref, dpp_ref, dg_ref, dg_acc):
        i = pl.program_id(0)

        @pl.when(i == 0)
        def _():
            dg_acc[...] = jnp.zeros_like(dg_acc)

        dd = d_ref[...]
        x = h_ref[...]
        gg = g_ref[...]
        gt = gate_ref[...]
        pp = _mm(p_ref[...], wp_ref[...])
        dpp_ref[...] = (dd * gt).astype(dpp_ref.dtype)
        dz = dd * pp * gt * (1.0 - gt)
        dz_ref[...] = dz.astype(dz_ref.dtype)
        r = lax.rsqrt(jnp.mean(x * x, axis=-1, keepdims=True) + EPS)
        dx, dgr = _rms_bwd(x, r, gg, _mm_nt(dz, wg_ref[...]))
        di_ref[...] = dd + dx
        dg_acc[...] += _colsum8(dgr)

        @pl.when(i == nt - 1)
        def _():
            dg_ref[...] = jnp.sum(dg_acc[...], axis=0, keepdims=True)

    return pl.pallas_call(
        body, name="ple_bwd", grid=(nt,),
        in_specs=[_rows(tm, D), _rows(tm, D), _whole((1, D)), _whole(wg.shape), _rows(tm, D), _rows(tm, E),
                  _whole(wp.shape)],
        out_specs=[_rows(tm, D), _rows(tm, D), _rows(tm, D), _whole((1, D))],
        out_shape=[jax.ShapeDtypeStruct((S, D), F32), jax.ShapeDtypeStruct((S, D), ACT_DTYPE),
                   jax.ShapeDtypeStruct((S, D), ACT_DTYPE), jax.ShapeDtypeStruct((1, D), F32)],
        scratch_shapes=[pltpu.VMEM((8, D), F32)],
        compiler_params=_params(1),
    )(d, h, g, wg, gate, p, wp)


def ffn_bwd(d, h, g, w1, w2, a, tm):
    S, D = h.shape
    FF = w1.shape[1]
    nt = S // tm

    def body(d_ref, h_ref, g_ref, w1_ref, w2_ref, a_ref, di_ref, da_ref, dg_ref, dg_acc):
        i = pl.program_id(0)

        @pl.when(i == 0)
        def _():
            dg_acc[...] = jnp.zeros_like(dg_acc)

        dd = d_ref[...]
        x = h_ref[...]
        da = _mm_nt(dd, w2_ref[...]) * (2.0 * jnp.maximum(a_ref[...], 0.0))
        da_ref[...] = da.astype(da_ref.dtype)
        r = lax.rsqrt(jnp.mean(x * x, axis=-1, keepdims=True) + EPS)
        dx, dgr = _rms_bwd(x, r, g_ref[...], _mm_nt(da, w1_ref[...]))
        di_ref[...] = dd + dx
        dg_acc[...] += _colsum8(dgr)

        @pl.when(i == nt - 1)
        def _():
            dg_ref[...] = jnp.sum(dg_acc[...], axis=0, keepdims=True)

    return pl.pallas_call(
        body, name="ffn_bwd", grid=(nt,),
        in_specs=[_rows(tm, D), _rows(tm, D), _whole((1, D)), _whole(w1.shape), _whole(w2.shape), _rows(tm, FF)],
        out_specs=[_rows(tm, D), _rows(tm, FF), _whole((1, D))],
        out_shape=[jax.ShapeDtypeStruct((S, D), F32), jax.ShapeDtypeStruct((S, FF), ACT_DTYPE),
                   jax.ShapeDtypeStruct((1, D), F32)],
        scratch_shapes=[pltpu.VMEM((8, D), F32)],
        compiler_params=_params(1),
    )(d, h, g, w1, w2, a)


def attn_out_bwd(d, wo, o32, dh, tm):
    S, D = d.shape
    H = D // dh

    def body(d_ref, wo_ref, o_ref, do_ref, delta_ref):
        do = _mm_nt(d_ref[...], wo_ref[...]).astype(do_ref.dtype)
        do_ref[...] = do
        lane_head = lax.broadcasted_iota(jnp.int32, (D, H), 0) // dh
        seg = (lane_head == lax.broadcasted_iota(jnp.int32, (D, H), 1)).astype(MXU_DTYPE)
        hi, mid, lo = _split3(do.astype(F32) * o_ref[...])
        delta_ref[...] = (jnp.dot(lo, seg, preferred_element_type=F32) + jnp.dot(mid, seg, preferred_element_type=F32)
                          + jnp.dot(hi, seg, preferred_element_type=F32))

    return pl.pallas_call(
        body, name="attn_out_bwd", grid=(S // tm,),
        in_specs=[_rows(tm, D), _whole(wo.shape), _rows(tm, D)],
        out_specs=[_rows(tm, D), _rows(tm, H)],
        out_shape=[jax.ShapeDtypeStruct((S, D), ACT_DTYPE), jax.ShapeDtypeStruct((S, H), F32)],
        compiler_params=_params(1),
    )(d, wo, o32)


def flash_bwd(q, qT, k, v, c_row, do, doT, qstat, dh, tk, tqc):
    S, D = q.shape
    hg = 128 // dh
    G = D // 128
    per = tk // tqc
    nchunk = S // tqc
    assert hg <= 8 and tk % tqc == 0 and S % tk == 0

    def body(q_ref, qT_ref, k_ref, v_ref, crow_ref, do_ref, doT_ref, st_ref,
             dq_ref, dkT_ref, dvT_ref, dck_ref):
        ki = pl.program_id(1)

        @pl.when(ki == 0)
        def _():
            dq_ref[...] = jnp.zeros_like(dq_ref)

        dck_ref[...] = jnp.zeros_like(dck_ref)
        dkT_ref[...] = jnp.zeros_like(dkT_ref)
        dvT_ref[...] = jnp.zeros_like(dvT_ref)
        def chunk(jq, masked):
            rows = pl.ds(pl.multiple_of(jq * tqc, tqc), tqc)
            st = st_ref[rows, :]
            for hh in range(hg):
                lanes = slice(hh * dh, (hh + 1) * dh)
                kh = k_ref[:, lanes]
                ck = crow_ref[hh:hh + 1, :]
                c0 = ck[:, 0:1]
                u = (st[:, hh:hh + 1] - c0) - st[:, hg + hh:hg + hh + 1]
                s = (_mm_nt(q_ref[rows, lanes], kh) - (ck - c0)) + u
                if masked:
                    s = jnp.where(_causal_mask(ki * tk, jq * tqc, (tqc, tk), 1), s, NEG_BIG)
                p = jnp.exp(s)
                dvT_ref[lanes, :] += _mm(doT_ref[lanes, rows], p)
                ds = p * (_mm_nt(do_ref[rows, lanes], v_ref[:, lanes]) - st[:, 2 * hg + hh:2 * hg + hh + 1])
                dkT_ref[lanes, :] += _mm(qT_ref[lanes, rows], ds)
                dck_ref[hh:hh + 1, :] -= jnp.sum(ds, axis=0, keepdims=True)
                dq_ref[rows, lanes] += _mm(ds, kh)

        for jj in range(per):
            chunk(ki * per + jj, True)

        def full_chunk(jq, carry):
            chunk(jq, False)
            return carry

        lax.fori_loop((ki + 1) * per, nchunk, full_chunk, 0)

    return pl.pallas_call(
        body, name="flash_bwd", grid=(G, S // tk),
        in_specs=[pl.BlockSpec((S, 128), lambda g, j: (0, g)),
                  pl.BlockSpec((128, S), lambda g, j: (g, 0)),
                  pl.BlockSpec((tk, 128), lambda g, j: (j, g)),
                  pl.BlockSpec((tk, 128), lambda g, j: (j, g)),
                  pl.BlockSpec((None, hg, tk), lambda g, j: (g, 0, j)),
                  pl.BlockSpec((S, 128), lambda g, j: (0, g)),
                  pl.BlockSpec((128, S), lambda g, j: (g, 0)),
                  pl.BlockSpec((None, S, 3 * hg), lambda g, j: (g, 0, 0))],
        out_specs=[pl.BlockSpec((S, 128), lambda g, j: (0, g)),
                   pl.BlockSpec((128, tk), lambda g, j: (g, j)),
                   pl.BlockSpec((128, tk), lambda g, j: (g, j)),
                   pl.BlockSpec((None, 8, tk), lambda g, j: (g, 0, j))],
        out_shape=[jax.ShapeDtypeStruct((S, D), F32), jax.ShapeDtypeStruct((D, S), F32),
                   jax.ShapeDtypeStruct((D, S), F32), jax.ShapeDtypeStruct((G, 8, S), F32)],
        compiler_params=_params(2),
    )(q, qT, k, v, c_row, do, doT, qstat)


def q_bwd(d, dq, h, g, wq, scale, tm):
    S, D = h.shape
    nt = S // tm

    def body(d_ref, dq_ref, h_ref, g_ref, wq_ref, di_ref, dqs_ref, dg_ref, dg_acc):
        i = pl.program_id(0)

        @pl.when(i == 0)
        def _():
            dg_acc[...] = jnp.zeros_like(dg_acc)

        x = h_ref[...]
        dqs = dq_ref[...] * scale
        dqs_ref[...] = dqs.astype(dqs_ref.dtype)
        r = lax.rsqrt(jnp.mean(x * x, axis=-1, keepdims=True) + EPS)
        dx, dgr = _rms_bwd(x, r, g_ref[...], _mm_nt(dqs, wq_ref[...]))
        di_ref[...] = d_ref[...] + dx
        dg_acc[...] += _colsum8(dgr)

        @pl.when(i == nt - 1)
        def _():
            dg_ref[...] = jnp.sum(dg_acc[...], axis=0, keepdims=True)

    return pl.pallas_call(
        body, name="q_bwd", grid=(nt,),
        in_specs=[_rows(tm, D), _rows(tm, D), _rows(tm, D), _whole((1, D)), _whole(wq.shape)],
        out_specs=[_rows(tm, D), _rows(tm, D), _whole((1, D))],
        out_shape=[jax.ShapeDtypeStruct((S, D), F32), jax.ShapeDtypeStruct((S, D), ACT_DTYPE),
                   jax.ShapeDtypeStruct((1, D), F32)],
        scratch_shapes=[pltpu.VMEM((8, D), F32)],
        compiler_params=_params(1),
    )(d, dq, h, g, wq)


def kv_bwd(d, dks, dvs, dc, fl, h, g, wk, wv, wf, tm):
    S, D = h.shape
    H = wf.shape[1]
    nt = S // tm
    nl = len(dks)

    def body(*refs):
        d_ref = refs[0]
        dk_refs = refs[1:1 + nl]
        dv_refs = refs[1 + nl:1 + 2 * nl]
        (dc_ref, fl_ref, h_ref, g_ref, wk_ref, wv_ref, wf_ref,
         di_ref, dk_ref, dv_ref, dfl_ref, dg_ref, dbf_ref, dg_acc, dbf_acc, carry) = refs[1 + 2 * nl:]
        i = pl.program_id(0)

        @pl.when(i == 0)
        def _():
            dg_acc[...] = jnp.zeros_like(dg_acc)
            dbf_acc[...] = jnp.zeros_like(dbf_acc)
            carry[...] = jnp.zeros_like(carry)

        dk = dk_refs[0][...]
        dv = dv_refs[0][...]
        for l in range(1, nl):
            dk = dk + dk_refs[l][...]
            dv = dv + dv_refs[l][...]
        dk_ref[...] = dk.astype(dk_ref.dtype)
        dv_ref[...] = dv.astype(dv_ref.dtype)
        row = lax.broadcasted_iota(jnp.int32, (tm, tm), 0)
        col = lax.broadcasted_iota(jnp.int32, (tm, tm), 1)
        tri = (col >= row).astype(MXU_DTYPE)
        dlogf = _tri_mm(tri, dc_ref[...]) + carry[...]
        carry[...] = dlogf[0:1, :]
        dfl = dlogf * _sigmoid(-fl_ref[...])
        dfl_ref[...] = dfl
        dbf_acc[...] += jnp.sum(dfl, axis=0, keepdims=True)
        x = h_ref[...]
        dn = _mm_nt(dk, wk_ref[...]) + _mm_nt(dv, wv_ref[...]) + _mm_nt(dfl, wf_ref[...])
        r = lax.rsqrt(jnp.mean(x * x, axis=-1, keepdims=True) + EPS)
        dx, dgr = _rms_bwd(x, r, g_ref[...], dn)
        di_ref[...] = d_ref[...] + dx
        dg_acc[...] += _colsum8(dgr)

        @pl.when(i == nt - 1)
        def _():
            dg_ref[...] = jnp.sum(dg_acc[...], axis=0, keepdims=True)
            dbf_ref[...] = dbf_acc[...]

    rev = lambda n: _rows_rev(tm, n, nt)
    return pl.pallas_call(
        body, name="kv_bwd", grid=(nt,),
        in_specs=([rev(D)] + [rev(D)] * (2 * nl)
                  + [rev(H), rev(H), rev(D), _whole((1, D)), _whole(wk.shape), _whole(wv.shape), _whole(wf.shape)]),
        out_specs=[rev(D), rev(D), rev(D), rev(H), _whole((1, D)), _whole((1, H))],
        out_shape=[jax.ShapeDtypeStruct((S, D), F32), jax.ShapeDtypeStruct((S, D), ACT_DTYPE),
                   jax.ShapeDtypeStruct((S, D), ACT_DTYPE), jax.ShapeDtypeStruct((S, H), F32),
                   jax.ShapeDtypeStruct((1, D), F32), jax.ShapeDtypeStruct((1, H), F32)],
        scratch_shapes=[pltpu.VMEM((8, D), F32), pltpu.VMEM((1, H), F32), pltpu.VMEM((1, H), F32)],
        compiler_params=_params(1),
    )(d, *dks, *dvs, dc, fl, h, g, wk, wv, wf)


def conv_bwd(d, h, g, w1, wd, lg, lb, w2, u, z, tm):
    S, D = h.shape
    CW = wd.shape[0]
    nt = S // tm
    assert tm >= HALO and CW - 1 <= HALO

    def body(d_ref, h_ref, g_ref, w1_ref, wd_ref, lg_ref, lb_ref, w2_ref, u_ref, z_ref,
             di_ref, du_ref, db2_ref, dlg_ref, dlb_ref, dbd_ref, dwd_ref, db1_ref, dg_ref,
             ext, db2_acc, dlg_acc, dlb_acc, dbd_acc, dwd_acc, db1_acc, dg_acc):
        i = pl.program_id(0)

        @pl.when(i == 0)
        def _():
            ext[tm:tm + HALO, :] = jnp.zeros((HALO, D), F32)
            for acc in (db2_acc, dlg_acc, dlb_acc, dbd_acc, dwd_acc, db1_acc, dg_acc):
                acc[...] = jnp.zeros_like(acc)

        dd = d_ref[...]
        db2_acc[...] += _colsum8(dd)
        dsw = _mm_nt(dd, w2_ref[...])
        zz = z_ref[...]
        zc = zz - jnp.mean(zz, axis=-1, keepdims=True)
        rs = lax.rsqrt(jnp.mean(zc * zc, axis=-1, keepdims=True) + EPS)
        xh = zc * rs
        lgv = lg_ref[...]
        y = xh * lgv + lb_ref[...]
        sg = _sigmoid(y)
        dy = dsw * (sg * (1.0 + y * (1.0 - sg)))
        dlg_acc[...] += _colsum8(dy * xh)
        dlb_acc[...] += _colsum8(dy)
        dxh = dy * lgv
        dz = rs * (dxh - jnp.mean(dxh, axis=-1, keepdims=True) - xh * jnp.mean(dxh * xh, axis=-1, keepdims=True))
        dbd_acc[...] += _colsum8(dz)
        ext[0:tm, :] = dz
        uu = u_ref[...]
        a = uu[:, :D]
        sgg = _sigmoid(uu[:, D:])
        glu = a * sgg
        dglu = jnp.zeros((tm, D), F32)
        for k in range(CW):
            sh = ext[CW - 1 - k:CW - 1 - k + tm, :]
            dglu = dglu + wd_ref[k:k + 1, :] * sh
            dwd_acc[k] += _colsum8(glu * sh)
        ext[tm:tm + HALO, :] = ext[0:HALO, :]
        da = dglu * sgg
        dgg = dglu * a * sgg * (1.0 - sgg)
        du_ref[:, :D] = da.astype(du_ref.dtype)
        du_ref[:, D:] = dgg.astype(du_ref.dtype)
        db1_acc[:, :D] += _colsum8(da)
        db1_acc[:, D:] += _colsum8(dgg)
        dn = _mm_nt(da, w1_ref[:, :D]) + _mm_nt(dgg, w1_ref[:, D:])
        x = h_ref[...]
        r = lax.rsqrt(jnp.mean(x * x, axis=-1, keepdims=True) + EPS)
        dx, dgr = _rms_bwd(x, r, g_ref[...], dn)
        di_ref[...] = dd + dx
        dg_acc[...] += _colsum8(dgr)

        @pl.when(i == nt - 1)
        def _():
            db2_ref[...] = jnp.sum(db2_acc[...], axis=0, keepdims=True)
            dlg_ref[...] = jnp.sum(dlg_acc[...], axis=0, keepdims=True)
            dlb_ref[...] = jnp.sum(dlb_acc[...], axis=0, keepdims=True)
            dbd_ref[...] = jnp.sum(dbd_acc[...], axis=0, keepdims=True)
            dwd_ref[...] = jnp.sum(dwd_acc[...], axis=1)
            db1_ref[...] = jnp.sum(db1_acc[...], axis=0, keepdims=True)
            dg_ref[...] = jnp.sum(dg_acc[...], axis=0, keepdims=True)

    rev = lambda n: _rows_rev(tm, n, nt)
    vec = jax.ShapeDtypeStruct((1, D), F32)
    return pl.pallas_call(
        body, name="conv_bwd", grid=(nt,),
        in_specs=[rev(D), rev(D), _whole((1, D)), _whole(w1.shape), _whole(wd.shape), _whole((1, D)),
                  _whole((1, D)), _whole(w2.shape), rev(2 * D), rev(D)],
        out_specs=[rev(D), rev(2 * D), _whole((1, D)), _whole((1, D)), _whole((1, D)), _whole((1, D)),
                   _whole((CW, D)), _whole((1, 2 * D)), _whole((1, D))],
        out_shape=[jax.ShapeDtypeStruct((S, D), F32), jax.ShapeDtypeStruct((S, 2 * D), ACT_DTYPE),
                   vec, vec, vec, vec, jax.ShapeDtypeStruct((CW, D), F32),
                   jax.ShapeDtypeStruct((1, 2 * D), F32), vec],
        scratch_shapes=[pltpu.VMEM((tm + HALO, D), F32), pltpu.VMEM((8, D), F32), pltpu.VMEM((8, D), F32),
                        pltpu.VMEM((8, D), F32), pltpu.VMEM((8, D), F32), pltpu.VMEM((CW, 8, D), F32),
                        pltpu.VMEM((8, 2 * D), F32), pltpu.VMEM((8, D), F32)],
        compiler_params=_params(1),
    )(d, h, g, w1, wd, lg, lb, w2, u, z)


def weight_grad(a, b, ts, name):
    S, M = a.shape
    N = b.shape[1]
    ta = M if M <= 1024 else 1024
    tb = N if N <= 1024 else 1024
    assert M % ta == 0 and N % tb == 0 and S % ts == 0

    def body(a_ref, b_ref, o_ref):
        @pl.when(pl.program_id(2) == 0)
        def _():
            o_ref[...] = jnp.zeros_like(o_ref)

        o_ref[...] += _mm_tn(a_ref[...], b_ref[...])

    return pl.pallas_call(
        body, name=name, grid=(M // ta, N // tb, S // ts),
        in_specs=[pl.BlockSpec((ts, ta), lambda i, j, s: (s, i)), pl.BlockSpec((ts, tb), lambda i, j, s: (s, j))],
        out_specs=pl.BlockSpec((ta, tb), lambda i, j, s: (i, j)),
        out_shape=jax.ShapeDtypeStruct((M, N), F32),
        compiler_params=_params(3),
    )(a, b)


def _position():
    return lax.axis_index("x"), lax.axis_index("y"), lax.axis_index("c")


def all_gather(x, name):
    def body(x_ref, out_ref, send_sems, recv_sems, local_sem):
        x, y, c = _position()
        me, sibling = (x, y, c), (x, y, 1 - c)
        chips = [(1 - x, y), (x, 1 - y), (1 - x, 1 - y)]

        def slot(px, py, pc):
            return out_ref.at[4 * px + 2 * py + pc]

        def copy(k, block, to, src=None):
            return pltpu.make_async_remote_copy(
                src_ref=slot(*block) if src is None else src, dst_ref=slot(*block),
                send_sem=send_sems.at[k], recv_sem=recv_sems.at[k], device_id=to, device_id_type=MESH)

        mine = pltpu.make_async_copy(x_ref, slot(*me), local_sem)
        mine.start()
        first = [copy(0, me, sibling, src=x_ref)]
        first += [copy(1 + j, me, (*chip, c), src=x_ref) for j, chip in enumerate(chips)]
        for cp in first:
            cp.start()
        passed = [copy(4 + j, (*chip, c), sibling) for j, chip in enumerate(chips)]
        for j, chip in enumerate(chips):
            copy(1 + j, (*chip, c), me).wait_recv()
            passed[j].start()
        copy(0, sibling, me).wait_recv()
        for j, chip in enumerate(chips):
            copy(4 + j, (*chip, 1 - c), me).wait_recv()
        for cp in first + passed:
            cp.wait_send()
        mine.wait()

    return pl.pallas_call(
        body, name=name,
        in_specs=[pl.BlockSpec(memory_space=pl.ANY)], out_specs=pl.BlockSpec(memory_space=pl.ANY),
        out_shape=jax.ShapeDtypeStruct((N_DEV,) + x.shape, x.dtype),
        scratch_shapes=[pltpu.SemaphoreType.DMA((7,)), pltpu.SemaphoreType.DMA((7,)), pltpu.SemaphoreType.DMA],
    )(x)


def sibling_exchange(g):
    _, R, C = g.shape

    def body(g_ref, land_ref, send_sems, recv_sems):
        x, y, c = _position()
        copies = [pltpu.make_async_remote_copy(
            src_ref=g_ref.at[2 * j + 1 - c], dst_ref=land_ref.at[j], send_sem=send_sems.at[j],
            recv_sem=recv_sems.at[j], device_id=(x, y, 1 - c), device_id_type=MESH) for j in range(N_CHIP)]
        for cp in copies:
            cp.start()
        for cp in copies:
            cp.wait()

    return pl.pallas_call(
        body, name="grad_sibling_exchange",
        in_specs=[pl.BlockSpec(memory_space=pl.ANY)], out_specs=pl.BlockSpec(memory_space=pl.ANY),
        out_shape=jax.ShapeDtypeStruct((N_CHIP, R, C), g.dtype),
        scratch_shapes=[pltpu.SemaphoreType.DMA((N_CHIP,)), pltpu.SemaphoreType.DMA((N_CHIP,))],
    )(g)


def chip_partial(g, land, core):
    _, R, C = g.shape
    tr = _row_tile(R, PACK_ROW_TILE)

    def body(c_ref, g_ref, l_ref, o_ref):
        o_ref[...] = (g_ref[...] + l_ref[...]).astype(o_ref.dtype)

    grid_spec = pltpu.PrefetchScalarGridSpec(
        num_scalar_prefetch=1, grid=(N_CHIP, R // tr),
        in_specs=[pl.BlockSpec((None, tr, C), lambda j, i, cr: (2 * j + cr[0], i, 0)),
                  pl.BlockSpec((None, tr, C), lambda j, i, cr: (j, i, 0))],
        out_specs=pl.BlockSpec((None, tr, C), lambda j, i, cr: (j, i, 0)))
    return pl.pallas_call(
        body, name="grad_chip_partial", grid_spec=grid_spec,
        out_shape=jax.ShapeDtypeStruct((N_CHIP, R, C), WIRE_DTYPE),
        compiler_params=_params(2),
    )(core, g, land)


def chip_exchange(part):
    def body(p_ref, land_ref, send_sems, recv_sems, local_sem):
        x, y, c = _position()
        mychip = 2 * x + y
        chips = [(1 - x, y), (x, 1 - y), (1 - x, 1 - y)]
        mine = pltpu.make_async_copy(p_ref.at[mychip], land_ref.at[mychip], local_sem)
        mine.start()
        copies = [pltpu.make_async_remote_copy(
            src_ref=p_ref.at[2 * cx + cy], dst_ref=land_ref.at[mychip], send_sem=send_sems.at[k],
            recv_sem=recv_sems.at[k], device_id=(cx, cy, c), device_id_type=MESH)
            for k, (cx, cy) in enumerate(chips)]
        for cp in copies:
            cp.start()
        for k, (cx, cy) in enumerate(chips):
            pltpu.make_async_remote_copy(
                src_ref=p_ref.at[2 * cx + cy], dst_ref=land_ref.at[2 * cx + cy], send_sem=send_sems.at[k],
                recv_sem=recv_sems.at[k], device_id=(cx, cy, c), device_id_type=MESH).wait_recv()
        for cp in copies:
            cp.wait_send()
        mine.wait()

    return pl.pallas_call(
        body, name="grad_chip_exchange",
        in_specs=[pl.BlockSpec(memory_space=pl.ANY)], out_specs=pl.BlockSpec(memory_space=pl.ANY),
        out_shape=jax.ShapeDtypeStruct(part.shape, part.dtype),
        scratch_shapes=[pltpu.SemaphoreType.DMA((3,)), pltpu.SemaphoreType.DMA((3,)), pltpu.SemaphoreType.DMA],
    )(part)


def _adamw(w, g, m, v):
    m = ADAM_B1 * m + (1.0 - ADAM_B1) * g
    v = ADAM_B2 * v + (1.0 - ADAM_B2) * jnp.square(g)
    m_hat = m / (1.0 - ADAM_B1 ** ADAM_STEP)
    v_hat = v / (1.0 - ADAM_B2 ** ADAM_STEP)
    delta = -ADAM_LR * (m_hat / (jnp.sqrt(v_hat) + ADAM_EPS) + ADAM_WD * w)
    return delta, m, v


def adamw_sharded(parts, w, m, v):
    R, C = w.shape
    tr = _row_tile(R, PACK_ROW_TILE)

    def body(p_ref, w_ref, m_ref, v_ref, g_ref, d_ref, nm_ref, nv_ref):
        g = p_ref[0].astype(F32)
        for j in range(1, N_CHIP):
            g = g + p_ref[j].astype(F32)
        g_ref[...] = g
        d_ref[...], nm_ref[...], nv_ref[...] = _adamw(w_ref[...], g, m_ref[...], v_ref[...])

    out = jax.ShapeDtypeStruct((R, C), F32)
    return pl.pallas_call(
        body, name="adamw_sharded", grid=(R // tr,),
        in_specs=[pl.BlockSpec((N_CHIP, tr, C), lambda i: (0, i, 0)), _rows(tr, C), _rows(tr, C), _rows(tr, C)],
        out_specs=[_rows(tr, C)] * 4, out_shape=[out] * 4,
        compiler_params=_params(1),
    )(parts, w, m, v)


def adamw_replicated(gathered, w, m, v):
    R, C = w.shape

    def body(p_ref, w_ref, m_ref, v_ref, g_ref, d_ref, nm_ref, nv_ref):
        g = p_ref[0]
        for j in range(1, N_DEV):
            g = g + p_ref[j]
        g_ref[...] = g
        d_ref[...], nm_ref[...], nv_ref[...] = _adamw(w_ref[...], g, m_ref[...], v_ref[...])

    out = jax.ShapeDtypeStruct((R, C), F32)
    return pl.pallas_call(
        body, name="adamw_replicated", grid=(1,),
        in_specs=[_whole(gathered.shape), _whole((R, C)), _whole((R, C)), _whole((R, C))],
        out_specs=[_whole((R, C))] * 4, out_shape=[out] * 4,
        compiler_params=_params(1),
    )(gathered, w, m, v)


def _pack_rows(flat, rows_multiple):
    n = flat.shape[-1]
    per = PACK_COLS * rows_multiple
    padded = -(-n // per) * per
    flat = jnp.pad(flat, [(0, 0)] * (flat.ndim - 1) + [(0, padded - n)])
    return flat.reshape(flat.shape[:-1] + (padded // PACK_COLS, PACK_COLS))


def _pack(arrays, rows_multiple, dtype=None):
    flat = jnp.concatenate([a.reshape(-1) if dtype is None else a.reshape(-1).astype(dtype) for a in arrays])
    return _pack_rows(flat, rows_multiple)


def _unpack(packed, shapes):
    flat = packed.reshape(-1)
    out, off = [], 0
    for shp in shapes:
        n = int(np.prod(shp))
        out.append(flat[off:off + n].reshape(shp))
        off += n
    return out


def _full_from_gathered(gathered, shard_shapes, axes):
    flat = gathered.reshape(N_DEV, -1)
    out, off = [], 0
    for shp, ax in zip(shard_shapes, axes):
        n = int(np.prod(shp))
        seg = jnp.moveaxis(flat[:, off:off + n].reshape((N_DEV,) + tuple(shp)), 0, ax)
        out.append(seg.reshape(tuple(shp[:ax]) + (N_DEV * shp[ax],) + tuple(shp[ax + 1:])))
        off += n
    return out


def _chunks_from_full(full, ax):
    shp = full.shape
    split = full.reshape(shp[:ax] + (N_DEV, shp[ax] // N_DEV) + shp[ax + 1:])
    return jnp.moveaxis(split, ax, 0).reshape(N_DEV, -1)


def kernel(x, p, mix_norm, conv_w_pw1, conv_b_pw1, conv_w_dw, conv_b_dw, conv_ln_g, conv_ln_b, conv_w_pw2, conv_b_pw2, kv_norm, w_kvf, b_f, attn_w_q, attn_w_o, ffn_norm, ffn_w1, ffn_w2, ple_norm, ple_w_gate, ple_w_proj, final_norm, loss_target, m_mix_norm, m_conv_w_pw1, m_conv_b_pw1, m_conv_w_dw, m_conv_b_dw, m_conv_ln_g, m_conv_ln_b, m_conv_w_pw2, m_conv_b_pw2, m_kv_norm, m_w_kvf, m_b_f, m_attn_w_q, m_attn_w_o, m_ffn_norm, m_ffn_w1, m_ffn_w2, m_ple_norm, m_ple_w_gate, m_ple_w_proj, m_final_norm, v_mix_norm, v_conv_w_pw1, v_conv_b_pw1, v_conv_w_dw, v_conv_b_dw, v_conv_ln_g, v_conv_ln_b, v_conv_w_pw2, v_conv_b_pw2, v_kv_norm, v_w_kvf, v_b_f, v_attn_w_q, v_attn_w_o, v_ffn_norm, v_ffn_w1, v_ffn_w2, v_ple_norm, v_ple_w_gate, v_ple_w_proj, v_final_norm):
    given = dict(locals())
    W = {n: given[n] for n in WEIGHTS}
    M = {n: given["m_" + n] for n in WEIGHTS}
    V = {n: given["v_" + n] for n in WEIGHTS}

    _, S, D = x.shape
    NA = conv_w_pw1.shape[0]
    NB = attn_w_q.shape[0]
    DEPTH = NA + NB
    H = b_f.shape[0]
    dh = D // H
    hg = 128 // dh
    G = D // 128
    scale = dh ** -0.5
    tm = _row_tile(S, 256)
    t_attn = _row_tile(S, 512)
    t_chunk = _row_tile(t_attn, 256)
    ts = _row_tile(S, 512)
    xs = x[0]
    tgt = loss_target[0]
    ps = p[:, 0]
    row = lambda a: a.reshape(1, -1)

    big_names = list(SHARD_AXIS_BIG)
    small_names = list(SHARD_AXIS_SMALL)
    big = _full_from_gathered(
        all_gather(_pack([W[n] for n in big_names], 16, MXU_DTYPE), "weights_all_gather"),
        [W[n].shape for n in big_names], [SHARD_AXIS_BIG[n] for n in big_names])
    small = _full_from_gathered(
        all_gather(_pack([W[n] for n in small_names], 8), "vectors_all_gather"),
        [W[n].shape for n in small_names], [SHARD_AXIS_SMALL[n] for n in small_names])
    FW = dict(zip(big_names + small_names, big + small))
    wk, wv, wf = FW["w_kvf"][:, :D], FW["w_kvf"][:, D:2 * D], FW["w_kvf"][:, 2 * D:]

    saved = []
    h = xs
    kv = None
    for i in range(DEPTH):
        rec = {"h_in": h}
        if i < NA:
            h, rec["n"], rec["u"], rec["z"], rec["sw"] = conv_fwd(
                h, row(mix_norm[i]), FW["conv_w_pw1"][i], row(FW["conv_b_pw1"][i]), FW["conv_w_dw"][i],
                row(FW["conv_b_dw"][i]), row(FW["conv_ln_g"][i]), row(FW["conv_ln_b"][i]),
                FW["conv_w_pw2"][i], row(FW["conv_b_pw2"][i]), tm)
        else:
            j = i - NA
            if j == 0:
                k_, v_, nkv, fl, c = kv_fwd(h, row(kv_norm), wk, wv, wf, row(b_f), tm)
                cg = c.reshape(S, G, hg)
                kv = dict(k=k_, v=v_, vT=v_.T, n=nkv, fl=fl, h=h, c_col=jnp.transpose(cg, (1, 0, 2)),
                          c_row=jnp.transpose(cg, (1, 2, 0)))
            rec["n"], rec["q"] = q_fwd(h, row(mix_norm[i]), FW["attn_w_q"][j], scale, tm)
            rec["o"], rec["o32"], rec["lse"] = flash_fwd(rec["q"], kv["k"], kv["vT"], kv["c_col"], kv["c_row"], dh,
                                                         t_attn, t_chunk)
            h = attn_out_fwd(h, rec["o"], FW["attn_w_o"][j], tm)
        rec["h_ffn"] = h
        h, rec["n_ffn"], rec["a"], rec["s"] = ffn_fwd(h, row(ffn_norm[i]), FW["ffn_w1"][i], FW["ffn_w2"][i], tm)
        rec["h_ple"] = h
        h, rec["n_ple"], rec["gate"] = ple_fwd(h, row(ple_norm[i]), FW["ple_w_gate"][i], ps[i],
                                               FW["ple_w_proj"][i], tm)
        saved.append(rec)

    d, g_final, loss_part = loss_head(h, row(final_norm), tgt, tm)
    GW = {n: [None] * W[n].shape[0] for n in WEIGHTS if W[n].ndim > 1 and n != "w_kvf"}
    dks, dvs, dcs = [], [], []
    for i in reversed(range(DEPTH)):
        rec = saved[i]
        d_out = d
        d, dz, dpp, GW["ple_norm"][i] = ple_bwd(d_out, rec["h_ple"], row(ple_norm[i]), FW["ple_w_gate"][i],
                                                rec["gate"], ps[i], FW["ple_w_proj"][i], tm)
        GW["ple_w_gate"][i] = weight_grad(rec["n_ple"], dz, ts, "grad_ple_w_gate")
        GW["ple_w_proj"][i] = weight_grad(ps[i], dpp, ts, "grad_ple_w_proj")
        d_out = d
        d, da, GW["ffn_norm"][i] = ffn_bwd(d_out, rec["h_ffn"], row(ffn_norm[i]), FW["ffn_w1"][i],
                                           FW["ffn_w2"][i], rec["a"], tm)
        GW["ffn_w2"][i] = weight_grad(rec["s"], d_out, ts, "grad_ffn_w2")
        GW["ffn_w1"][i] = weight_grad(rec["n_ffn"], da, ts, "grad_ffn_w1")
        d_out = d
        if i >= NA:
            j = i - NA
            GW["attn_w_o"][j] = weight_grad(rec["o"], d_out, ts, "grad_attn_w_o")
            do, delta = attn_out_bwd(d_out, FW["attn_w_o"][j], rec["o32"], dh, tm)
            qstat = jnp.concatenate([kv["c_col"], jnp.transpose(rec["lse"], (0, 2, 1)),
                                     jnp.transpose(delta.reshape(S, G, hg), (1, 0, 2))], axis=2)
            dq, dkT, dvT, dck = flash_bwd(rec["q"], rec["q"].T, kv["k"], kv["v"], kv["c_row"], do, do.T, qstat,
                                          dh, t_attn, t_chunk)
            dks.append(dkT.T)
            dvs.append(dvT.T)
            dcs.append(jnp.transpose(dck[:, :hg, :], (2, 0, 1)).reshape(S, H))
            d, dqs, GW["mix_norm"][i] = q_bwd(d_out, dq, rec["h_in"], row(mix_norm[i]), FW["attn_w_q"][j], scale, tm)
            GW["attn_w_q"][j] = weight_grad(rec["n"], dqs, ts, "grad_attn_w_q")
            if j == 0:
                dc = dcs[0]
                for extra in dcs[1:]:
                    dc = dc + extra
                d, dk_sum, dv_sum, dfl, g_kv_norm, g_b_f = kv_bwd(d, dks, dvs, dc, kv["fl"], kv["h"],
                                                                  row(kv_norm), wk, wv, wf, tm)
                g_w_kvf = jnp.concatenate([weight_grad(kv["n"], dk_sum, ts, "grad_w_k"),
                                           weight_grad(kv["n"], dv_sum, ts, "grad_w_v"),
                                           weight_grad(kv["n"], dfl, ts, "grad_w_f")], axis=1)
        else:
            GW["conv_w_pw2"][i] = weight_grad(rec["sw"], d_out, ts, "grad_conv_w_pw2")
            (d, du, GW["conv_b_pw2"][i], GW["conv_ln_g"][i], GW["conv_ln_b"][i], GW["conv_b_dw"][i],
             GW["conv_w_dw"][i], GW["conv_b_pw1"][i], GW["mix_norm"][i]) = conv_bwd(
                d_out, rec["h_in"], row(mix_norm[i]), FW["conv_w_pw1"][i], FW["conv_w_dw"][i],
                row(FW["conv_ln_g"][i]), row(FW["conv_ln_b"][i]), FW["conv_w_pw2"][i], rec["u"], rec["z"], tm)
            GW["conv_w_pw1"][i] = weight_grad(rec["n"], du, ts, "grad_conv_w_pw1")
    grad_x = d[None]

    def stacked(n):
        return jnp.concatenate(GW[n], axis=0) if W[n].ndim == 2 else jnp.stack(GW[n])

    full_grads = {n: stacked(n) for n in GW if n not in REPLICATED}
    full_grads["w_kvf"] = g_w_kvf

    sharded_names = big_names + small_names
    axes = {**SHARD_AXIS_BIG, **SHARD_AXIS_SMALL}
    chunks = jnp.concatenate([_chunks_from_full(full_grads[n], axes[n]) for n in sharded_names], axis=1)
    chunks = _pack_rows(chunks, PACK_ROW_TILE)
    core = lax.axis_index("c").astype(jnp.int32).reshape(1)
    parts = chip_exchange(chip_partial(chunks, sibling_exchange(chunks), core))

    pack_sh = lambda src: _pack([src[n] for n in sharded_names], PACK_ROW_TILE)
    outs_sh = adamw_sharded(parts, pack_sh(W), pack_sh(M), pack_sh(V))
    shard_shapes = [W[n].shape for n in sharded_names]
    res = {}
    for kind, packed in zip(("grad", "delta", "new_m", "new_v"), outs_sh):
        for n, a in zip(sharded_names, _unpack(packed, shard_shapes)):
            res[kind, n] = a

    rep_grads = {"mix_norm": jnp.concatenate(GW["mix_norm"], axis=0), "kv_norm": g_kv_norm,
                 "b_f": g_b_f, "ffn_norm": jnp.concatenate(GW["ffn_norm"], axis=0),
                 "ple_norm": jnp.concatenate(GW["ple_norm"], axis=0), "final_norm": g_final}

    def pack_rep(src, extra=None):
        rows_ = [jnp.pad(src[n].reshape(-1, src[n].shape[-1]), ((0, 0), (0, D - src[n].shape[-1])))
                 for n in REPLICATED]
        if extra is not None:
            rows_.append(jnp.pad(extra, ((0, 0), (0, D - extra.shape[-1]))))
        else:
            rows_.append(jnp.zeros((1, D), F32))
        flat = jnp.concatenate(rows_, axis=0)
        return jnp.pad(flat, ((0, -flat.shape[0] % 8), (0, 0)))

    rep_g = all_gather(pack_rep(rep_grads, loss_part), "replicated_all_gather")
    outs_rep = adamw_replicated(rep_g, pack_rep(W), pack_rep(M), pack_rep(V))
    n_rep_rows = sum(int(np.prod(W[n].shape[:-1])) for n in REPLICATED)
    for kind, packed in zip(("grad", "delta", "new_m", "new_v"), outs_rep):
        r0 = 0
        for n in REPLICATED:
            nr = int(np.prod(W[n].shape[:-1]))
            res[kind, n] = packed[r0:r0 + nr, :W[n].shape[-1]].reshape(W[n].shape)
            r0 += nr
    loss = outs_rep[0][n_rep_rows, 0]

    return (loss, grad_x, *[res["grad", n] for n in WEIGHTS], *[res["delta", n] for n in WEIGHTS],
            *[res["new_m", n] for n in WEIGHTS], *[res["new_v", n] for n in WEIGHTS])
```

```python
import numpy as np
import jax
import jax.numpy as jnp
from jax import lax
from jax.experimental import pallas as pl
from jax.experimental.pallas import tpu as pltpu

F32 = jnp.float32
MXU_DTYPE = jnp.bfloat16
ACT_DTYPE = jnp.bfloat16
WIRE_DTYPE = jnp.bfloat16

N_DEV = 8
N_CHIP = 4
EPS = 1e-6
NEG_BIG = -1e30
ADAM_LR = 0.001
ADAM_B1 = 0.9
ADAM_B2 = 0.999
ADAM_EPS = 1e-08
ADAM_WD = 0.01
ADAM_STEP = 10

VMEM_LIMIT_BYTES = 56 * 1024 * 1024
PACK_COLS = 1024
PACK_ROW_TILE = 256
FLASH_FWD_TILE = (1024, 512)
FLASH_BWD_TILE = (1024, 512)
HALO = 32
MESH = pl.DeviceIdType.MESH

SHARD_AXIS_BIG = {"conv_w_pw1": 2, "conv_w_pw2": 1, "w_kvf": 1, "attn_w_q": 1, "attn_w_o": 1,
                  "ffn_w1": 2, "ffn_w2": 1, "ple_w_gate": 1, "ple_w_proj": 2}
SHARD_AXIS_SMALL = {"conv_b_pw1": 1, "conv_w_dw": 2, "conv_b_dw": 1, "conv_ln_g": 1, "conv_ln_b": 1,
                    "conv_b_pw2": 1}
REPLICATED = ["mix_norm", "kv_norm", "b_f", "ffn_norm", "ple_norm", "final_norm"]
WEIGHTS = ["mix_norm", "conv_w_pw1", "conv_b_pw1", "conv_w_dw", "conv_b_dw", "conv_ln_g", "conv_ln_b",
           "conv_w_pw2", "conv_b_pw2", "kv_norm", "w_kvf", "b_f", "attn_w_q", "attn_w_o", "ffn_norm",
           "ffn_w1", "ffn_w2", "ple_norm", "ple_w_gate", "ple_w_proj", "final_norm"]


def _mm(a, b):
    return jnp.dot(a.astype(MXU_DTYPE), b.astype(MXU_DTYPE), preferred_element_type=F32)


def _mm_nt(a, b):
    return lax.dot_general(a.astype(MXU_DTYPE), b.astype(MXU_DTYPE), (((1,), (1,)), ((), ())),
                           preferred_element_type=F32)


def _mm_tn(a, b):
    return lax.dot_general(a.astype(MXU_DTYPE), b.astype(MXU_DTYPE), (((0,), (0,)), ((), ())),
                           preferred_element_type=F32)


def _split3(x):
    hi = x.astype(MXU_DTYPE)
    r1 = x - hi.astype(F32)
    mid = r1.astype(MXU_DTYPE)
    lo = (r1 - mid.astype(F32)).astype(MXU_DTYPE)
    return hi, mid, lo


def _tri_mm(tri, x):
    hi, mid, lo = _split3(x)
    return (jnp.dot(tri, lo, preferred_element_type=F32) + jnp.dot(tri, mid, preferred_element_type=F32)
            + jnp.dot(tri, hi, preferred_element_type=F32))


def _colsum8(x):
    tm, n = x.shape
    return jnp.sum(x.reshape(tm // 8, 8, n), axis=0)


def _rms(x, g):
    r = lax.rsqrt(jnp.mean(x * x, axis=-1, keepdims=True) + EPS)
    return x * r * g, r


def _rms_bwd(x, r, g, dn):
    w = dn * g
    dx = r * w - x * (r * r * r) * jnp.mean(w * x, axis=-1, keepdims=True)
    return dx, dn * x * r


def _sigmoid(x):
    return jax.nn.sigmoid(x)


def _params(n_grid):
    return pltpu.CompilerParams(dimension_semantics=("arbitrary",) * n_grid, vmem_limit_bytes=VMEM_LIMIT_BYTES)


def _rows(tm, n):
    return pl.BlockSpec((tm, n), lambda i: (i, 0))


def _rows_rev(tm, n, nt):
    return pl.BlockSpec((tm, n), lambda i: (nt - 1 - i, 0))


def _whole(shape):
    nd = len(shape)
    return pl.BlockSpec(shape, lambda i: (0,) * nd)


def _row_tile(s, want):
    tm = min(s, want)
    assert s % tm == 0 and tm % 8 == 0, (s, tm)
    return tm


def conv_fwd(h, g, w1, b1, wd, bd, lg, lb, w2, b2, tm):
    S, D = h.shape
    CW = wd.shape[0]
    off = HALO - (CW - 1)
    assert 0 <= off and tm >= HALO
    nt = S // tm

    def body(h_ref, g_ref, w1_ref, b1_ref, wd_ref, bd_ref, lg_ref, lb_ref, w2_ref, b2_ref,
             ho_ref, n_ref, u_ref, z_ref, sw_ref, ext):
        @pl.when(pl.program_id(0) == 0)
        def _():
            ext[0:HALO, :] = jnp.zeros((HALO, D), F32)

        x = h_ref[...]
        n, _ = _rms(x, g_ref[...])
        n_ref[...] = n.astype(n_ref.dtype)
        u = _mm(n, w1_ref[...]) + b1_ref[...]
        u_ref[...] = u
        ext[HALO:HALO + tm, :] = u[:, :D] * _sigmoid(u[:, D:])
        z = jnp.broadcast_to(bd_ref[...], (tm, D))
        for k in range(CW):
            z = z + wd_ref[k:k + 1, :] * ext[off + k:off + k + tm, :]
        z_ref[...] = z
        ext[0:HALO, :] = ext[tm:tm + HALO, :]
        mu = jnp.mean(z, axis=-1, keepdims=True)
        zc = z - mu
        y = zc * lax.rsqrt(jnp.mean(zc * zc, axis=-1, keepdims=True) + EPS) * lg_ref[...] + lb_ref[...]
        sw = y * _sigmoid(y)
        sw_ref[...] = sw.astype(sw_ref.dtype)
        ho_ref[...] = x + _mm(sw, w2_ref[...]) + b2_ref[...]

    return pl.pallas_call(
        body, name="conv_fwd", grid=(nt,),
        in_specs=[_rows(tm, D), _whole((1, D)), _whole(w1.shape), _whole((1, 2 * D)), _whole(wd.shape),
                  _whole((1, D)), _whole((1, D)), _whole((1, D)), _whole(w2.shape), _whole((1, D))],
        out_specs=[_rows(tm, D), _rows(tm, D), _rows(tm, 2 * D), _rows(tm, D), _rows(tm, D)],
        out_shape=[jax.ShapeDtypeStruct((S, D), F32), jax.ShapeDtypeStruct((S, D), ACT_DTYPE),
                   jax.ShapeDtypeStruct((S, 2 * D), F32), jax.ShapeDtypeStruct((S, D), F32),
                   jax.ShapeDtypeStruct((S, D), ACT_DTYPE)],
        scratch_shapes=[pltpu.VMEM((HALO + tm, D), F32)],
        compiler_params=_params(1),
    )(h, g, w1, b1, wd, bd, lg, lb, w2, b2)


def ffn_fwd(h, g, w1, w2, tm):
    S, D = h.shape
    FF = w1.shape[1]

    def body(h_ref, g_ref, w1_ref, w2_ref, ho_ref, n_ref, a_ref, s_ref):
        x = h_ref[...]
        n, _ = _rms(x, g_ref[...])
        n_ref[...] = n.astype(n_ref.dtype)
        a = _mm(n, w1_ref[...])
        a_ref[...] = a
        s = jnp.square(jnp.maximum(a, 0.0))
        s_ref[...] = s.astype(s_ref.dtype)
        ho_ref[...] = x + _mm(s, w2_ref[...])

    return pl.pallas_call(
        body, name="ffn_fwd", grid=(S // tm,),
        in_specs=[_rows(tm, D), _whole((1, D)), _whole(w1.shape), _whole(w2.shape)],
        out_specs=[_rows(tm, D), _rows(tm, D), _rows(tm, FF), _rows(tm, FF)],
        out_shape=[jax.ShapeDtypeStruct((S, D), F32), jax.ShapeDtypeStruct((S, D), ACT_DTYPE),
                   jax.ShapeDtypeStruct((S, FF), F32), jax.ShapeDtypeStruct((S, FF), ACT_DTYPE)],
        compiler_params=_params(1),
    )(h, g, w1, w2)


def ple_fwd(h, g, wg, p, wp, tm):
    S, D = h.shape
    E = p.shape[1]

    def body(h_ref, g_ref, wg_ref, p_ref, wp_ref, ho_ref, n_ref, gate_ref):
        x = h_ref[...]
        n, _ = _rms(x, g_ref[...])
        n_ref[...] = n.astype(n_ref.dtype)
        gate = _sigmoid(_mm(n, wg_ref[...]))
        gate_ref[...] = gate
        ho_ref[...] = x + gate * _mm(p_ref[...], wp_ref[...])

    return pl.pallas_call(
        body, name="ple_fwd", grid=(S // tm,),
        in_specs=[_rows(tm, D), _whole((1, D)), _whole(wg.shape), _rows(tm, E), _whole(wp.shape)],
        out_specs=[_rows(tm, D), _rows(tm, D), _rows(tm, D)],
        out_shape=[jax.ShapeDtypeStruct((S, D), F32), jax.ShapeDtypeStruct((S, D), ACT_DTYPE),
                   jax.ShapeDtypeStruct((S, D), F32)],
        compiler_params=_params(1),
    )(h, g, wg, p, wp)


def kv_fwd(h, g, wk, wv, wf, bf, tm):
    S, D = h.shape
    H = wf.shape[1]

    def body(h_ref, g_ref, wk_ref, wv_ref, wf_ref, bf_ref, k_ref, v_ref, n_ref, fl_ref, c_ref, carry):
        @pl.when(pl.program_id(0) == 0)
        def _():
            carry[...] = jnp.zeros_like(carry)

        n, _ = _rms(h_ref[...], g_ref[...])
        n_ref[...] = n.astype(n_ref.dtype)
        k_ref[...] = _mm(n, wk_ref[...]).astype(k_ref.dtype)
        v_ref[...] = _mm(n, wv_ref[...]).astype(v_ref.dtype)
        fl = _mm(n, wf_ref[...]) + bf_ref[...]
        fl_ref[...] = fl
        logf = jnp.minimum(fl, 0.0) - jnp.log1p(jnp.exp(-jnp.abs(fl)))
        row = lax.broadcasted_iota(jnp.int32, (tm, tm), 0)
        col = lax.broadcasted_iota(jnp.int32, (tm, tm), 1)
        tri = (row >= col).astype(MXU_DTYPE)
        c = _tri_mm(tri, logf) + carry[...]
        c_ref[...] = c
        carry[...] = c[tm - 1:tm, :]

    return pl.pallas_call(
        body, name="kv_fwd", grid=(S // tm,),
        in_specs=[_rows(tm, D), _whole((1, D)), _whole(wk.shape), _whole(wv.shape), _whole(wf.shape),
                  _whole((1, H))],
        out_specs=[_rows(tm, D), _rows(tm, D), _rows(tm, D), _rows(tm, H), _rows(tm, H)],
        out_shape=[jax.ShapeDtypeStruct((S, D), ACT_DTYPE), jax.ShapeDtypeStruct((S, D), ACT_DTYPE),
                   jax.ShapeDtypeStruct((S, D), ACT_DTYPE), jax.ShapeDtypeStruct((S, H), F32),
                   jax.ShapeDtypeStruct((S, H), F32)],
        scratch_shapes=[pltpu.VMEM((1, H), F32)],
        compiler_params=_params(1),
    )(h, g, wk, wv, wf, bf)


def q_fwd(h, g, wq, scale, tm):
    S, D = h.shape

    def body(h_ref, g_ref, wq_ref, n_ref, q_ref):
        n, _ = _rms(h_ref[...], g_ref[...])
        n_ref[...] = n.astype(n_ref.dtype)
        q_ref[...] = (_mm(n, wq_ref[...]) * scale).astype(q_ref.dtype)

    return pl.pallas_call(
        body, name="q_fwd", grid=(S // tm,),
        in_specs=[_rows(tm, D), _whole((1, D)), _whole(wq.shape)],
        out_specs=[_rows(tm, D), _rows(tm, D)],
        out_shape=[jax.ShapeDtypeStruct((S, D), ACT_DTYPE), jax.ShapeDtypeStruct((S, D), ACT_DTYPE)],
        compiler_params=_params(1),
    )(h, g, wq)


def attn_out_fwd(h, o, wo, tm):
    S, D = h.shape

    def body(h_ref, o_ref, wo_ref, ho_ref):
        ho_ref[...] = h_ref[...] + _mm(o_ref[...], wo_ref[...])

    return pl.pallas_call(
        body, name="attn_out_fwd", grid=(S // tm,),
        in_specs=[_rows(tm, D), _rows(tm, D), _whole(wo.shape)],
        out_specs=_rows(tm, D),
        out_shape=jax.ShapeDtypeStruct((S, D), F32),
        compiler_params=_params(1),
    )(h, o, wo)


def _causal_mask(key0, qry0, shape, key_axis):
    key = key0 + lax.broadcasted_iota(jnp.int32, shape, key_axis)
    qry = qry0 + lax.broadcasted_iota(jnp.int32, shape, 1 - key_axis)
    return key <= qry


def flash_fwd(qT, k, vT, c_col, c_row, dh, tq, tkc):
    D, S = qT.shape
    hg = 128 // dh
    G = D // 128
    per = tq // tkc
    assert tq % tkc == 0 and S % tq == 0

    def body(qT_ref, k_ref, vT_ref, ccol_ref, crow_ref, o_ref, o32_ref, lse_ref, m_scr, l_scr, acc_scr):
        i = pl.program_id(1)
        m_scr[...] = jnp.full(m_scr.shape, NEG_BIG, F32)
        l_scr[...] = jnp.zeros(l_scr.shape, F32)
        acc_scr[...] = jnp.zeros(acc_scr.shape, F32)

        def chunk(j, masked):
            keys = pl.ds(pl.multiple_of(j * tkc, tkc), tkc)
            for hh in range(hg):
                lanes = slice(hh * dh, (hh + 1) * dh)
                c0 = ccol_ref[pl.ds(j * tkc, 1), hh:hh + 1]
                r = crow_ref[hh:hh + 1, :] - c0
                s = _mm(k_ref[keys, lanes], qT_ref[lanes, :]) - (ccol_ref[keys, hh:hh + 1] - c0)
                if masked:
                    s = jnp.where(_causal_mask(j * tkc, i * tq, (tkc, tq), 0), s, NEG_BIG)
                m_old = m_scr[hh]
                m_new = jnp.maximum(m_old, jnp.max(s, axis=0, keepdims=True) + r)
                alpha = jnp.exp(m_old - m_new)
                p = jnp.exp(s - (m_new - r))
                l_scr[hh] = alpha * l_scr[hh] + jnp.sum(p, axis=0, keepdims=True)
                p_hi = p.astype(MXU_DTYPE)
                p_lo = p - p_hi.astype(F32)
                vc = vT_ref[lanes, keys]
                acc_scr[lanes, :] = alpha * acc_scr[lanes, :] + (_mm(vc, p_lo) + _mm(vc, p_hi))
                m_scr[hh] = m_new

        def full_chunk(j, carry):
            chunk(j, False)
            return carry

        lax.fori_loop(0, i * per, full_chunk, 0)
        for jj in range(per):
            chunk(i * per + jj, True)
        for hh in range(hg):
            lanes = slice(hh * dh, (hh + 1) * dh)
            acc_scr[lanes, :] = acc_scr[lanes, :] / l_scr[hh]
            lse_ref[hh:hh + 1, :] = m_scr[hh] + jnp.log(l_scr[hh])
        o = acc_scr[...].T
        o_ref[...] = o.astype(o_ref.dtype)
        o32_ref[...] = o

    return pl.pallas_call(
        body, name="flash_fwd", grid=(G, S // tq),
        in_specs=[pl.BlockSpec((128, tq), lambda g, i: (g, i)),
                  pl.BlockSpec((S, 128), lambda g, i: (0, g)),
                  pl.BlockSpec((128, S), lambda g, i: (g, 0)),
                  pl.BlockSpec((None, S, hg), lambda g, i: (g, 0, 0)),
                  pl.BlockSpec((None, hg, tq), lambda g, i: (g, 0, i))],
        out_specs=[pl.BlockSpec((tq, 128), lambda g, i: (i, g)),
                   pl.BlockSpec((tq, 128), lambda g, i: (i, g)),
                   pl.BlockSpec((None, hg, tq), lambda g, i: (g, 0, i))],
        out_shape=[jax.ShapeDtypeStruct((S, D), ACT_DTYPE), jax.ShapeDtypeStruct((S, D), F32),
                   jax.ShapeDtypeStruct((G, hg, S), F32)],
        scratch_shapes=[pltpu.VMEM((hg, 1, tq), F32), pltpu.VMEM((hg, 1, tq), F32), pltpu.VMEM((128, tq), F32)],
        compiler_params=_params(2),
    )(qT, k, vT, c_col, c_row)


def loss_head(h, g, target, tm):
    S, D = h.shape
    nt = S // tm

    def body(h_ref, g_ref, t_ref, dh_ref, dg_ref, loss_ref, dg_acc, loss_acc):
        i = pl.program_id(0)

        @pl.when(i == 0)
        def _():
            dg_acc[...] = jnp.zeros_like(dg_acc)
            loss_acc[...] = jnp.zeros_like(loss_acc)

        x = h_ref[...]
        gg = g_ref[...]
        y, r = _rms(x, gg)
        e = y - t_ref[...]
        loss_acc[...] += 0.5 * jnp.sum(jnp.mean(e * e, axis=-1, keepdims=True), axis=0, keepdims=True)
        dx, dgr = _rms_bwd(x, r, gg, e / D)
        dh_ref[...] = dx
        dg_acc[...] += _colsum8(dgr)

        @pl.when(i == nt - 1)
        def _():
            dg_ref[...] = jnp.sum(dg_acc[...], axis=0, keepdims=True)
            loss_ref[...] = jnp.broadcast_to(loss_acc[...], loss_ref.shape)

    return pl.pallas_call(
        body, name="loss_head", grid=(nt,),
        in_specs=[_rows(tm, D), _whole((1, D)), _rows(tm, D)],
        out_specs=[_rows(tm, D), _whole((1, D)), _whole((1, 128))],
        out_shape=[jax.ShapeDtypeStruct((S, D), F32), jax.ShapeDtypeStruct((1, D), F32),
                   jax.ShapeDtypeStruct((1, 128), F32)],
        scratch_shapes=[pltpu.VMEM((8, D), F32), pltpu.VMEM((1, 1), F32)],
        compiler_params=_params(1),
    )(h, g, target)


def ple_bwd(d, h, g, wg, gate, p, wp, tm):
    S, D = h.shape
    E = p.shape[1]
    nt = S // tm

    def body(d_ref, h_ref, g_ref, wg_ref, gate_ref, p_ref, wp_ref, di_ref, dz_ref, dpp_ref, dg_ref, dg_acc):
        i = pl.program_id(0)

        @pl.when(i == 0)
        def _():
            dg_acc[...] = jnp.zeros_like(dg_acc)

        dd = d_ref[...]
        x = h_ref[...]
        gg = g_ref[...]
        gt = gate_ref[...]
        pp = _mm(p_ref[...], wp_ref[...])
        dpp_ref[...] = (dd * gt).astype(dpp_ref.dtype)
        dz = dd * pp * gt * (1.0 - gt)
        dz_ref[...] = dz.astype(dz_ref.dtype)
        r = lax.rsqrt(jnp.mean(x * x, axis=-1, keepdims=True) + EPS)
        dx, dgr = _rms_bwd(x, r, gg, _mm_nt(dz, wg_ref[...]))
        di_ref[...] = dd + dx
        dg_acc[...] += _colsum8(dgr)

        @pl.when(i == nt - 1)
        def _():
            dg_ref[...] = jnp.sum(dg_acc[...], axis=0, keepdims=True)

    return pl.pallas_call(
        body, name="ple_bwd", grid=(nt,),
        in_specs=[_rows(tm, D), _rows(tm, D), _whole((1, D)), _whole(wg.shape), _rows(tm, D), _rows(tm, E),
                  _whole(wp.shape)],
        out_specs=[_rows(tm, D), _rows(tm, D), _rows(tm, D), _whole((1, D))],
        out_shape=[jax.ShapeDtypeStruct((S, D), F32), jax.ShapeDtypeStruct((S, D), ACT_DTYPE),
                   jax.ShapeDtypeStruct((S, D), ACT_DTYPE), jax.ShapeDtypeStruct((1, D), F32)],
        scratch_shapes=[pltpu.VMEM((8, D), F32)],
        compiler_params=_params(1),
    )(d, h, g, wg, gate, p, wp)


def ffn_bwd(d, h, g, w1, w2, a, tm):
    S, D = h.shape
    FF = w1.shape[1]
    nt = S // tm

    def body(d_ref, h_ref, g_ref, w1_ref, w2_ref, a_ref, di_ref, da_ref, dg_ref, dg_acc):
        i = pl.program_id(0)

        @pl.when(i == 0)
        def _():
            dg_acc[...] = jnp.zeros_like(dg_acc)

        dd = d_ref[...]
        x = h_ref[...]
        da = _mm_nt(dd, w2_ref[...]) * (2.0 * jnp.maximum(a_ref[...], 0.0))
        da_ref[...] = da.astype(da_ref.dtype)
        r = lax.rsqrt(jnp.mean(x * x, axis=-1, keepdims=True) + EPS)
        dx, dgr = _rms_bwd(x, r, g_ref[...], _mm_nt(da, w1_ref[...]))
        di_ref[...] = dd + dx
        dg_acc[...] += _colsum8(dgr)

        @pl.when(i == nt - 1)
        def _():
            dg_ref[...] = jnp.sum(dg_acc[...], axis=0, keepdims=True)

    return pl.pallas_call(
        body, name="ffn_bwd", grid=(nt,),
        in_specs=[_rows(tm, D), _rows(tm, D), _whole((1, D)), _whole(w1.shape), _whole(w2.shape), _rows(tm, FF)],
        out_specs=[_rows(tm, D), _rows(tm, FF), _whole((1, D))],
        out_shape=[jax.ShapeDtypeStruct((S, D), F32), jax.ShapeDtypeStruct((S, FF), ACT_DTYPE),
                   jax.ShapeDtypeStruct((1, D), F32)],
        scratch_shapes=[pltpu.VMEM((8, D), F32)],
        compiler_params=_params(1),
    )(d, h, g, w1, w2, a)


def attn_out_bwd(d, wo, o32, dh, tm):
    S, D = d.shape
    H = D // dh

    def body(d_ref, wo_ref, o_ref, do_ref, delta_ref):
        do = _mm_nt(d_ref[...], wo_ref[...]).astype(do_ref.dtype)
        do_ref[...] = do
        lane_head = lax.broadcasted_iota(jnp.int32, (D, H), 0) // dh
        seg = (lane_head == lax.broadcasted_iota(jnp.int32, (D, H), 1)).astype(MXU_DTYPE)
        hi, mid, lo = _split3(do.astype(F32) * o_ref[...])
        delta_ref[...] = (jnp.dot(lo, seg, preferred_element_type=F32) + jnp.dot(mid, seg, preferred_element_type=F32)
                          + jnp.dot(hi, seg, preferred_element_type=F32))

    return pl.pallas_call(
        body, name="attn_out_bwd", grid=(S // tm,),
        in_specs=[_rows(tm, D), _whole(wo.shape), _rows(tm, D)],
        out_specs=[_rows(tm, D), _rows(tm, H)],
        out_shape=[jax.ShapeDtypeStruct((S, D), ACT_DTYPE), jax.ShapeDtypeStruct((S, H), F32)],
        compiler_params=_params(1),
    )(d, wo, o32)


def flash_bwd(q, qT, k, kT, vT, c_row, do, doT, qstat, dh, tk, tqc):
    S, D = q.shape
    hg = 128 // dh
    G = D // 128
    per = tk // tqc
    nchunk = S // tqc
    assert hg <= 8 and tk % tqc == 0 and S % tk == 0

    def body(q_ref, qT_ref, k_ref, kT_ref, vT_ref, crow_ref, do_ref, doT_ref, st_ref,
             dq_ref, dkT_ref, dvT_ref, dck_ref):
        ki = pl.program_id(1)

        @pl.when(ki == 0)
        def _():
            dq_ref[...] = jnp.zeros_like(dq_ref)

        dck_ref[...] = jnp.zeros_like(dck_ref)
        dkT_ref[...] = jnp.zeros_like(dkT_ref)
        dvT_ref[...] = jnp.zeros_like(dvT_ref)
        def chunk(jq, masked):
            rows = pl.ds(pl.multiple_of(jq * tqc, tqc), tqc)
            st = st_ref[rows, :]
            for hh in range(hg):
                lanes = slice(hh * dh, (hh + 1) * dh)
                kh = k_ref[:, lanes]
                ck = crow_ref[hh:hh + 1, :]
                c0 = ck[:, 0:1]
                u = (st[:, hh:hh + 1] - c0) - st[:, hg + hh:hg + hh + 1]
                s = (_mm(q_ref[rows, lanes], kT_ref[lanes, :]) - (ck - c0)) + u
                if masked:
                    s = jnp.where(_causal_mask(ki * tk, jq * tqc, (tqc, tk), 1), s, NEG_BIG)
                p = jnp.exp(s)
                dvT_ref[lanes, :] += _mm(doT_ref[lanes, rows], p)
                ds = p * (_mm(do_ref[rows, lanes], vT_ref[lanes, :]) - st[:, 2 * hg + hh:2 * hg + hh + 1])
                dkT_ref[lanes, :] += _mm(qT_ref[lanes, rows], ds)
                dck_ref[hh:hh + 1, :] -= jnp.sum(ds, axis=0, keepdims=True)
                dq_ref[rows, lanes] += _mm(ds, kh)

        for jj in range(per):
            chunk(ki * per + jj, True)

        def full_chunk(jq, carry):
            chunk(jq, False)
            return carry

        lax.fori_loop((ki + 1) * per, nchunk, full_chunk, 0)

    return pl.pallas_call(
        body, name="flash_bwd", grid=(G, S // tk),
        in_specs=[pl.BlockSpec((S, 128), lambda g, j: (0, g)),
                  pl.BlockSpec((128, S), lambda g, j: (g, 0)),
                  pl.BlockSpec((tk, 128), lambda g, j: (j, g)),
                  pl.BlockSpec((128, tk), lambda g, j: (g, j)),
                  pl.BlockSpec((128, tk), lambda g, j: (g, j)),
                  pl.BlockSpec((None, hg, tk), lambda g, j: (g, 0, j)),
                  pl.BlockSpec((S, 128), lambda g, j: (0, g)),
                  pl.BlockSpec((128, S), lambda g, j: (g, 0)),
                  pl.BlockSpec((None, S, 3 * hg), lambda g, j: (g, 0, 0))],
        out_specs=[pl.BlockSpec((S, 128), lambda g, j: (0, g)),
                   pl.BlockSpec((128, tk), lambda g, j: (g, j)),
                   pl.BlockSpec((128, tk), lambda g, j: (g, j)),
                   pl.BlockSpec((None, 8, tk), lambda g, j: (g, 0, j))],
        out_shape=[jax.ShapeDtypeStruct((S, D), F32), jax.ShapeDtypeStruct((D, S), F32),
                   jax.ShapeDtypeStruct((D, S), F32), jax.ShapeDtypeStruct((G, 8, S), F32)],
        compiler_params=_params(2),
    )(q, qT, k, kT, vT, c_row, do, doT, qstat)


def q_bwd(d, dq, h, g, wq, scale, tm):
    S, D = h.shape
    nt = S // tm

    def body(d_ref, dq_ref, h_ref, g_ref, wq_ref, di_ref, dqs_ref, dg_ref, dg_acc):
        i = pl.program_id(0)

        @pl.when(i == 0)
        def _():
            dg_acc[...] = jnp.zeros_like(dg_acc)

        x = h_ref[...]
        dqs = dq_ref[...] * scale
        dqs_ref[...] = dqs.astype(dqs_ref.dtype)
        r = lax.rsqrt(jnp.mean(x * x, axis=-1, keepdims=True) + EPS)
        dx, dgr = _rms_bwd(x, r, g_ref[...], _mm_nt(dqs, wq_ref[...]))
        di_ref[...] = d_ref[...] + dx
        dg_acc[...] += _colsum8(dgr)

        @pl.when(i == nt - 1)
        def _():
            dg_ref[...] = jnp.sum(dg_acc[...], axis=0, keepdims=True)

    return pl.pallas_call(
        body, name="q_bwd", grid=(nt,),
        in_specs=[_rows(tm, D), _rows(tm, D), _rows(tm, D), _whole((1, D)), _whole(wq.shape)],
        out_specs=[_rows(tm, D), _rows(tm, D), _whole((1, D))],
        out_shape=[jax.ShapeDtypeStruct((S, D), F32), jax.ShapeDtypeStruct((S, D), ACT_DTYPE),
                   jax.ShapeDtypeStruct((1, D), F32)],
        scratch_shapes=[pltpu.VMEM((8, D), F32)],
        compiler_params=_params(1),
    )(d, dq, h, g, wq)


def kv_bwd(d, dks, dvs, dc, fl, h, g, wk, wv, wf, tm):
    S, D = h.shape
    H = wf.shape[1]
    nt = S // tm
    nl = len(dks)

    def body(*refs):
        d_ref = refs[0]
        dk_refs = refs[1:1 + nl]
        dv_refs = refs[1 + nl:1 + 2 * nl]
        (dc_ref, fl_ref, h_ref, g_ref, wk_ref, wv_ref, wf_ref,
         di_ref, dk_ref, dv_ref, dfl_ref, dg_ref, dbf_ref, dg_acc, dbf_acc, carry) = refs[1 + 2 * nl:]
        i = pl.program_id(0)

        @pl.when(i == 0)
        def _():
            dg_acc[...] = jnp.zeros_like(dg_acc)
            dbf_acc[...] = jnp.zeros_like(dbf_acc)
            carry[...] = jnp.zeros_like(carry)

        dk = dk_refs[0][...]
        dv = dv_refs[0][...]
        for l in range(1, nl):
            dk = dk + dk_refs[l][...]
            dv = dv + dv_refs[l][...]
        dk_ref[...] = dk.astype(dk_ref.dtype)
        dv_ref[...] = dv.astype(dv_ref.dtype)
        row = lax.broadcasted_iota(jnp.int32, (tm, tm), 0)
        col = lax.broadcasted_iota(jnp.int32, (tm, tm), 1)
        tri = (col >= row).astype(MXU_DTYPE)
        dlogf = _tri_mm(tri, dc_ref[...]) + carry[...]
        carry[...] = dlogf[0:1, :]
        dfl = dlogf * _sigmoid(-fl_ref[...])
        dfl_ref[...] = dfl
        dbf_acc[...] += jnp.sum(dfl, axis=0, keepdims=True)
        x = h_ref[...]
        dn = _mm_nt(dk, wk_ref[...]) + _mm_nt(dv, wv_ref[...]) + _mm_nt(dfl, wf_ref[...])
        r = lax.rsqrt(jnp.mean(x * x, axis=-1, keepdims=True) + EPS)
        dx, dgr = _rms_bwd(x, r, g_ref[...], dn)
        di_ref[...] = d_ref[...] + dx
        dg_acc[...] += _colsum8(dgr)

        @pl.when(i == nt - 1)
        def _():
            dg_ref[...] = jnp.sum(dg_acc[...], axis=0, keepdims=True)
            dbf_ref[...] = dbf_acc[...]

    rev = lambda n: _rows_rev(tm, n, nt)
    return pl.pallas_call(
        body, name="kv_bwd", grid=(nt,),
        in_specs=([rev(D)] + [rev(D)] * (2 * nl)
                  + [rev(H), rev(H), rev(D), _whole((1, D)), _whole(wk.shape), _whole(wv.shape), _whole(wf.shape)]),
        out_specs=[rev(D), rev(D), rev(D), rev(H), _whole((1, D)), _whole((1, H))],
        out_shape=[jax.ShapeDtypeStruct((S, D), F32), jax.ShapeDtypeStruct((S, D), ACT_DTYPE),
                   jax.ShapeDtypeStruct((S, D), ACT_DTYPE), jax.ShapeDtypeStruct((S, H), F32),
                   jax.ShapeDtypeStruct((1, D), F32), jax.ShapeDtypeStruct((1, H), F32)],
        scratch_shapes=[pltpu.VMEM((8, D), F32), pltpu.VMEM((1, H), F32), pltpu.VMEM((1, H), F32)],
        compiler_params=_params(1),
    )(d, *dks, *dvs, dc, fl, h, g, wk, wv, wf)


def conv_bwd(d, h, g, w1, wd, lg, lb, w2, u, z, tm):
    S, D = h.shape
    CW = wd.shape[0]
    nt = S // tm
    assert tm >= HALO and CW - 1 <= HALO

    def body(d_ref, h_ref, g_ref, w1_ref, wd_ref, lg_ref, lb_ref, w2_ref, u_ref, z_ref,
             di_ref, du_ref, db2_ref, dlg_ref, dlb_ref, dbd_ref, dwd_ref, db1_ref, dg_ref,
             ext, db2_acc, dlg_acc, dlb_acc, dbd_acc, dwd_acc, db1_acc, dg_acc):
        i = pl.program_id(0)

        @pl.when(i == 0)
        def _():
            ext[tm:tm + HALO, :] = jnp.zeros((HALO, D), F32)
            for acc in (db2_acc, dlg_acc, dlb_acc, dbd_acc, dwd_acc, db1_acc, dg_acc):
                acc[...] = jnp.zeros_like(acc)

        dd = d_ref[...]
        db2_acc[...] += _colsum8(dd)
        dsw = _mm_nt(dd, w2_ref[...])
        zz = z_ref[...]
        zc = zz - jnp.mean(zz, axis=-1, keepdims=True)
        rs = lax.rsqrt(jnp.mean(zc * zc, axis=-1, keepdims=True) + EPS)
        xh = zc * rs
        lgv = lg_ref[...]
        y = xh * lgv + lb_ref[...]
        sg = _sigmoid(y)
        dy = dsw * (sg * (1.0 + y * (1.0 - sg)))
        dlg_acc[...] += _colsum8(dy * xh)
        dlb_acc[...] += _colsum8(dy)
        dxh = dy * lgv
        dz = rs * (dxh - jnp.mean(dxh, axis=-1, keepdims=True) - xh * jnp.mean(dxh * xh, axis=-1, keepdims=True))
        dbd_acc[...] += _colsum8(dz)
        ext[0:tm, :] = dz
        uu = u_ref[...]
        a = uu[:, :D]
        sgg = _sigmoid(uu[:, D:])
        glu = a * sgg
        dglu = jnp.zeros((tm, D), F32)
        for k in range(CW):
            sh = ext[CW - 1 - k:CW - 1 - k + tm, :]
            dglu = dglu + wd_ref[k:k + 1, :] * sh
            dwd_acc[k] += _colsum8(glu * sh)
        ext[tm:tm + HALO, :] = ext[0:HALO, :]
        da = dglu * sgg
        dgg = dglu * a * sgg * (1.0 - sgg)
        du_ref[:, :D] = da.astype(du_ref.dtype)
        du_ref[:, D:] = dgg.astype(du_ref.dtype)
        db1_acc[:, :D] += _colsum8(da)
        db1_acc[:, D:] += _colsum8(dgg)
        dn = _mm_nt(da, w1_ref[:, :D]) + _mm_nt(dgg, w1_ref[:, D:])
        x = h_ref[...]
        r = lax.rsqrt(jnp.mean(x * x, axis=-1, keepdims=True) + EPS)
        dx, dgr = _rms_bwd(x, r, g_ref[...], dn)
        di_ref[...] = dd + dx
        dg_acc[...] += _colsum8(dgr)

        @pl.when(i == nt - 1)
        def _():
            db2_ref[...] = jnp.sum(db2_acc[...], axis=0, keepdims=True)
            dlg_ref[...] = jnp.sum(dlg_acc[...], axis=0, keepdims=True)
            dlb_ref[...] = jnp.sum(dlb_acc[...], axis=0, keepdims=True)
            dbd_ref[...] = jnp.sum(dbd_acc[...], axis=0, keepdims=True)
            dwd_ref[...] = jnp.sum(dwd_acc[...], axis=1)
            db1_ref[...] = jnp.sum(db1_acc[...], axis=0, keepdims=True)
            dg_ref[...] = jnp.sum(dg_acc[...], axis=0, keepdims=True)

    rev = lambda n: _rows_rev(tm, n, nt)
    vec = jax.ShapeDtypeStruct((1, D), F32)
    return pl.pallas_call(
        body, name="conv_bwd", grid=(nt,),
        in_specs=[rev(D), rev(D), _whole((1, D)), _whole(w1.shape), _whole(wd.shape), _whole((1, D)),
                  _whole((1, D)), _whole(w2.shape), rev(2 * D), rev(D)],
        out_specs=[rev(D), rev(2 * D), _whole((1, D)), _whole((1, D)), _whole((1, D)), _whole((1, D)),
                   _whole((CW, D)), _whole((1, 2 * D)), _whole((1, D))],
        out_shape=[jax.ShapeDtypeStruct((S, D), F32), jax.ShapeDtypeStruct((S, 2 * D), ACT_DTYPE),
                   vec, vec, vec, vec, jax.ShapeDtypeStruct((CW, D), F32),
                   jax.ShapeDtypeStruct((1, 2 * D), F32), vec],
        scratch_shapes=[pltpu.VMEM((tm + HALO, D), F32), pltpu.VMEM((8, D), F32), pltpu.VMEM((8, D), F32),
                        pltpu.VMEM((8, D), F32), pltpu.VMEM((8, D), F32), pltpu.VMEM((CW, 8, D), F32),
                        pltpu.VMEM((8, 2 * D), F32), pltpu.VMEM((8, D), F32)],
        compiler_params=_params(1),
    )(d, h, g, w1, wd, lg, lb, w2, u, z)


def weight_grad(a, b, ts, name):
    S, M = a.shape
    N = b.shape[1]
    ta = M if M <= 1024 else 1024
    tb = N if N <= 1024 else 1024
    assert M % ta == 0 and N % tb == 0 and S % ts == 0

    def body(a_ref, b_ref, o_ref):
        @pl.when(pl.program_id(2) == 0)
        def _():
            o_ref[...] = jnp.zeros_like(o_ref)

        o_ref[...] += _mm_tn(a_ref[...], b_ref[...])

    return pl.pallas_call(
        body, name=name, grid=(M // ta, N // tb, S // ts),
        in_specs=[pl.BlockSpec((ts, ta), lambda i, j, s: (s, i)), pl.BlockSpec((ts, tb), lambda i, j, s: (s, j))],
        out_specs=pl.BlockSpec((ta, tb), lambda i, j, s: (i, j)),
        out_shape=jax.ShapeDtypeStruct((M, N), F32),
        compiler_params=_params(3),
    )(a, b)


def _position():
    return lax.axis_index("x"), lax.axis_index("y"), lax.axis_index("c")


def all_gather(x, name):
    def body(x_ref, out_ref, send_sems, recv_sems, local_sem):
        x, y, c = _position()
        me, sibling = (x, y, c), (x, y, 1 - c)
        chips = [(1 - x, y), (x, 1 - y), (1 - x, 1 - y)]

        def slot(px, py, pc):
            return out_ref.at[4 * px + 2 * py + pc]

        def copy(k, block, to, src=None):
            return pltpu.make_async_remote_copy(
                src_ref=slot(*block) if src is None else src, dst_ref=slot(*block),
                send_sem=send_sems.at[k], recv_sem=recv_sems.at[k], device_id=to, device_id_type=MESH)

        mine = pltpu.make_async_copy(x_ref, slot(*me), local_sem)
        mine.start()
        first = [copy(0, me, sibling, src=x_ref)]
        first += [copy(1 + j, me, (*chip, c), src=x_ref) for j, chip in enumerate(chips)]
        for cp in first:
            cp.start()
        passed = [copy(4 + j, (*chip, c), sibling) for j, chip in enumerate(chips)]
        for j, chip in enumerate(chips):
            copy(1 + j, (*chip, c), me).wait_recv()
            passed[j].start()
        copy(0, sibling, me).wait_recv()
        for j, chip in enumerate(chips):
            copy(4 + j, (*chip, 1 - c), me).wait_recv()
        for cp in first + passed:
            cp.wait_send()
        mine.wait()

    return pl.pallas_call(
        body, name=name,
        in_specs=[pl.BlockSpec(memory_space=pl.ANY)], out_specs=pl.BlockSpec(memory_space=pl.ANY),
        out_shape=jax.ShapeDtypeStruct((N_DEV,) + x.shape, x.dtype),
        scratch_shapes=[pltpu.SemaphoreType.DMA((7,)), pltpu.SemaphoreType.DMA((7,)), pltpu.SemaphoreType.DMA],
    )(x)


def sibling_exchange(g):
    _, R, C = g.shape

    def body(g_ref, land_ref, send_sems, recv_sems):
        x, y, c = _position()
        copies = [pltpu.make_async_remote_copy(
            src_ref=g_ref.at[2 * j + 1 - c], dst_ref=land_ref.at[j], send_sem=send_sems.at[j],
            recv_sem=recv_sems.at[j], device_id=(x, y, 1 - c), device_id_type=MESH) for j in range(N_CHIP)]
        for cp in copies:
            cp.start()
        for cp in copies:
            cp.wait()

    return pl.pallas_call(
        body, name="grad_sibling_exchange",
        in_specs=[pl.BlockSpec(memory_space=pl.ANY)], out_specs=pl.BlockSpec(memory_space=pl.ANY),
        out_shape=jax.ShapeDtypeStruct((N_CHIP, R, C), g.dtype),
        scratch_shapes=[pltpu.SemaphoreType.DMA((N_CHIP,)), pltpu.SemaphoreType.DMA((N_CHIP,))],
    )(g)


def chip_partial(g, land, core):
    _, R, C = g.shape
    tr = _row_tile(R, PACK_ROW_TILE)

    def body(c_ref, g_ref, l_ref, o_ref):
        o_ref[...] = (g_ref[...] + l_ref[...]).astype(o_ref.dtype)

    grid_spec = pltpu.PrefetchScalarGridSpec(
        num_scalar_prefetch=1, grid=(N_CHIP, R // tr),
        in_specs=[pl.BlockSpec((None, tr, C), lambda j, i, cr: (2 * j + cr[0], i, 0)),
                  pl.BlockSpec((None, tr, C), lambda j, i, cr: (j, i, 0))],
        out_specs=pl.BlockSpec((None, tr, C), lambda j, i, cr: (j, i, 0)))
    return pl.pallas_call(
        body, name="grad_chip_partial", grid_spec=grid_spec,
        out_shape=jax.ShapeDtypeStruct((N_CHIP, R, C), WIRE_DTYPE),
        compiler_params=_params(2),
    )(core, g, land)


def chip_exchange(part):
    def body(p_ref, land_ref, send_sems, recv_sems, local_sem):
        x, y, c = _position()
        mychip = 2 * x + y
        chips = [(1 - x, y), (x, 1 - y), (1 - x, 1 - y)]
        mine = pltpu.make_async_copy(p_ref.at[mychip], land_ref.at[mychip], local_sem)
        mine.start()
        copies = [pltpu.make_async_remote_copy(
            src_ref=p_ref.at[2 * cx + cy], dst_ref=land_ref.at[mychip], send_sem=send_sems.at[k],
            recv_sem=recv_sems.at[k], device_id=(cx, cy, c), device_id_type=MESH)
            for k, (cx, cy) in enumerate(chips)]
        for cp in copies:
            cp.start()
        for k, (cx, cy) in enumerate(chips):
            pltpu.make_async_remote_copy(
                src_ref=p_ref.at[2 * cx + cy], dst_ref=land_ref.at[2 * cx + cy], send_sem=send_sems.at[k],
                recv_sem=recv_sems.at[k], device_id=(cx, cy, c), device_id_type=MESH).wait_recv()
        for cp in copies:
            cp.wait_send()
        mine.wait()

    return pl.pallas_call(
        body, name="grad_chip_exchange",
        in_specs=[pl.BlockSpec(memory_space=pl.ANY)], out_specs=pl.BlockSpec(memory_space=pl.ANY),
        out_shape=jax.ShapeDtypeStruct(part.shape, part.dtype),
        scratch_shapes=[pltpu.SemaphoreType.DMA((3,)), pltpu.SemaphoreType.DMA((3,)), pltpu.SemaphoreType.DMA],
    )(part)


def _adamw(w, g, m, v):
    m = ADAM_B1 * m + (1.0 - ADAM_B1) * g
    v = ADAM_B2 * v + (1.0 - ADAM_B2) * jnp.square(g)
    m_hat = m / (1.0 - ADAM_B1 ** ADAM_STEP)
    v_hat = v / (1.0 - ADAM_B2 ** ADAM_STEP)
    delta = -ADAM_LR * (m_hat / (jnp.sqrt(v_hat) + ADAM_EPS) + ADAM_WD * w)
    return delta, m, v


def adamw_sharded(parts, w, m, v):
    R, C = w.shape
    tr = _row_tile(R, PACK_ROW_TILE)

    def body(p_ref, w_ref, m_ref, v_ref, g_ref, d_ref, nm_ref, nv_ref):
        g = p_ref[0].astype(F32)
        for j in range(1, N_CHIP):
            g = g + p_ref[j].astype(F32)
        g_ref[...] = g
        d_ref[...], nm_ref[...], nv_ref[...] = _adamw(w_ref[...], g, m_ref[...], v_ref[...])

    out = jax.ShapeDtypeStruct((R, C), F32)
    return pl.pallas_call(
        body, name="adamw_sharded", grid=(R // tr,),
        in_specs=[pl.BlockSpec((N_CHIP, tr, C), lambda i: (0, i, 0)), _rows(tr, C), _rows(tr, C), _rows(tr, C)],
        out_specs=[_rows(tr, C)] * 4, out_shape=[out] * 4,
        compiler_params=_params(1),
    )(parts, w, m, v)


def adamw_replicated(gathered, w, m, v):
    R, C = w.shape

    def body(p_ref, w_ref, m_ref, v_ref, g_ref, d_ref, nm_ref, nv_ref):
        g = p_ref[0]
        for j in range(1, N_DEV):
            g = g + p_ref[j]
        g_ref[...] = g
        d_ref[...], nm_ref[...], nv_ref[...] = _adamw(w_ref[...], g, m_ref[...], v_ref[...])

    out = jax.ShapeDtypeStruct((R, C), F32)
    return pl.pallas_call(
        body, name="adamw_replicated", grid=(1,),
        in_specs=[_whole(gathered.shape), _whole((R, C)), _whole((R, C)), _whole((R, C))],
        out_specs=[_whole((R, C))] * 4, out_shape=[out] * 4,
        compiler_params=_params(1),
    )(gathered, w, m, v)


def _pack_rows(flat, rows_multiple):
    n = flat.shape[-1]
    per = PACK_COLS * rows_multiple
    padded = -(-n // per) * per
    flat = jnp.pad(flat, [(0, 0)] * (flat.ndim - 1) + [(0, padded - n)])
    return flat.reshape(flat.shape[:-1] + (padded // PACK_COLS, PACK_COLS))


def _pack(arrays, rows_multiple, dtype=None):
    flat = jnp.concatenate([a.reshape(-1) if dtype is None else a.reshape(-1).astype(dtype) for a in arrays])
    return _pack_rows(flat, rows_multiple)


def _unpack(packed, shapes):
    flat = packed.reshape(-1)
    out, off = [], 0
    for shp in shapes:
        n = int(np.prod(shp))
        out.append(flat[off:off + n].reshape(shp))
        off += n
    return out


def _full_from_gathered(gathered, shard_shapes, axes):
    flat = gathered.reshape(N_DEV, -1)
    out, off = [], 0
    for shp, ax in zip(shard_shapes, axes):
        n = int(np.prod(shp))
        seg = jnp.moveaxis(flat[:, off:off + n].reshape((N_DEV,) + tuple(shp)), 0, ax)
        out.append(seg.reshape(tuple(shp[:ax]) + (N_DEV * shp[ax],) + tuple(shp[ax + 1:])))
        off += n
    return out


def _chunks_from_full(full, ax):
    shp = full.shape
    split = full.reshape(shp[:ax] + (N_DEV, shp[ax] // N_DEV) + shp[ax + 1:])
    return jnp.moveaxis(split, ax, 0).reshape(N_DEV, -1)


def kernel(x, p, mix_norm, conv_w_pw1, conv_b_pw1, conv_w_dw, conv_b_dw, conv_ln_g, conv_ln_b, conv_w_pw2, conv_b_pw2, kv_norm, w_kvf, b_f, attn_w_q, attn_w_o, ffn_norm, ffn_w1, ffn_w2, ple_norm, ple_w_gate, ple_w_proj, final_norm, loss_target, m_mix_norm, m_conv_w_pw1, m_conv_b_pw1, m_conv_w_dw, m_conv_b_dw, m_conv_ln_g, m_conv_ln_b, m_conv_w_pw2, m_conv_b_pw2, m_kv_norm, m_w_kvf, m_b_f, m_attn_w_q, m_attn_w_o, m_ffn_norm, m_ffn_w1, m_ffn_w2, m_ple_norm, m_ple_w_gate, m_ple_w_proj, m_final_norm, v_mix_norm, v_conv_w_pw1, v_conv_b_pw1, v_conv_w_dw, v_conv_b_dw, v_conv_ln_g, v_conv_ln_b, v_conv_w_pw2, v_conv_b_pw2, v_kv_norm, v_w_kvf, v_b_f, v_attn_w_q, v_attn_w_o, v_ffn_norm, v_ffn_w1, v_ffn_w2, v_ple_norm, v_ple_w_gate, v_ple_w_proj, v_final_norm):
    given = dict(locals())
    W = {n: given[n] for n in WEIGHTS}
    M = {n: given["m_" + n] for n in WEIGHTS}
    V = {n: given["v_" + n] for n in WEIGHTS}

    _, S, D = x.shape
    NA = conv_w_pw1.shape[0]
    NB = attn_w_q.shape[0]
    DEPTH = NA + NB
    H = b_f.shape[0]
    dh = D // H
    hg = 128 // dh
    G = D // 128
    scale = dh ** -0.5
    tm = _row_tile(S, 256)
    tq_f = _row_tile(S, FLASH_FWD_TILE[0])
    tkc_f = _row_tile(tq_f, FLASH_FWD_TILE[1])
    tk_b = _row_tile(S, FLASH_BWD_TILE[0])
    tqc_b = _row_tile(tk_b, FLASH_BWD_TILE[1])
    ts = _row_tile(S, 512)
    xs = x[0]
    tgt = loss_target[0]
    ps = p[:, 0]
    row = lambda a: a.reshape(1, -1)

    big_names = list(SHARD_AXIS_BIG)
    small_names = list(SHARD_AXIS_SMALL)
    big = _full_from_gathered(
        all_gather(_pack([W[n] for n in big_names], 16, MXU_DTYPE), "weights_all_gather"),
        [W[n].shape for n in big_names], [SHARD_AXIS_BIG[n] for n in big_names])
    small = _full_from_gathered(
        all_gather(_pack([W[n] for n in small_names], 8), "vectors_all_gather"),
        [W[n].shape for n in small_names], [SHARD_AXIS_SMALL[n] for n in small_names])
    FW = dict(zip(big_names + small_names, big + small))
    wk, wv, wf = FW["w_kvf"][:, :D], FW["w_kvf"][:, D:2 * D], FW["w_kvf"][:, 2 * D:]

    saved = []
    h = xs
    kv = None
    for i in range(DEPTH):
        rec = {"h_in": h}
        if i < NA:
            h, rec["n"], rec["u"], rec["z"], rec["sw"] = conv_fwd(
                h, row(mix_norm[i]), FW["conv_w_pw1"][i], row(FW["conv_b_pw1"][i]), FW["conv_w_dw"][i],
                row(FW["conv_b_dw"][i]), row(FW["conv_ln_g"][i]), row(FW["conv_ln_b"][i]),
                FW["conv_w_pw2"][i], row(FW["conv_b_pw2"][i]), tm)
        else:
            j = i - NA
            if j == 0:
                k_, v_, nkv, fl, c = kv_fwd(h, row(kv_norm), wk, wv, wf, row(b_f), tm)
                cg = c.reshape(S, G, hg)
                kv = dict(k=k_, kT=k_.T, vT=v_.T, n=nkv, fl=fl, h=h, c_col=jnp.transpose(cg, (1, 0, 2)),
                          c_row=jnp.transpose(cg, (1, 2, 0)))
            rec["n"], rec["q"] = q_fwd(h, row(mix_norm[i]), FW["attn_w_q"][j], scale, tm)
            rec["qT"] = rec["q"].T
            rec["o"], rec["o32"], rec["lse"] = flash_fwd(rec["qT"], kv["k"], kv["vT"], kv["c_col"], kv["c_row"], dh,
                                                         tq_f, tkc_f)
            h = attn_out_fwd(h, rec["o"], FW["attn_w_o"][j], tm)
        rec["h_ffn"] = h
        h, rec["n_ffn"], rec["a"], rec["s"] = ffn_fwd(h, row(ffn_norm[i]), FW["ffn_w1"][i], FW["ffn_w2"][i], tm)
        rec["h_ple"] = h
        h, rec["n_ple"], rec["gate"] = ple_fwd(h, row(ple_norm[i]), FW["ple_w_gate"][i], ps[i],
                                               FW["ple_w_proj"][i], tm)
        saved.append(rec)

    d, g_final, loss_part = loss_head(h, row(final_norm), tgt, tm)
    GW = {n: [None] * W[n].shape[0] for n in WEIGHTS if W[n].ndim > 1 and n != "w_kvf"}
    dks, dvs, dcs = [], [], []
    for i in reversed(range(DEPTH)):
        rec = saved[i]
        d_out = d
        d, dz, dpp, GW["ple_norm"][i] = ple_bwd(d_out, rec["h_ple"], row(ple_norm[i]), FW["ple_w_gate"][i],
                                                rec["gate"], ps[i], FW["ple_w_proj"][i], tm)
        GW["ple_w_gate"][i] = weight_grad(rec["n_ple"], dz, ts, "grad_ple_w_gate")
        GW["ple_w_proj"][i] = weight_grad(ps[i], dpp, ts, "grad_ple_w_proj")
        d_out = d
        d, da, GW["ffn_norm"][i] = ffn_bwd(d_out, rec["h_ffn"], row(ffn_norm[i]), FW["ffn_w1"][i],
                                           FW["ffn_w2"][i], rec["a"], tm)
        GW["ffn_w2"][i] = weight_grad(rec["s"], d_out, ts, "grad_ffn_w2")
        GW["ffn_w1"][i] = weight_grad(rec["n_ffn"], da, ts, "grad_ffn_w1")
        d_out = d
        if i >= NA:
            j = i - NA
            GW["attn_w_o"][j] = weight_grad(rec["o"], d_out, ts, "grad_attn_w_o")
            do, delta = attn_out_bwd(d_out, FW["attn_w_o"][j], rec["o32"], dh, tm)
            qstat = jnp.concatenate([kv["c_col"], jnp.transpose(rec["lse"], (0, 2, 1)),
                                     jnp.transpose(delta.reshape(S, G, hg), (1, 0, 2))], axis=2)
            dq, dkT, dvT, dck = flash_bwd(rec["q"], rec["qT"], kv["k"], kv["kT"], kv["vT"], kv["c_row"], do, do.T,
                                          qstat, dh, tk_b, tqc_b)
            dks.append(dkT.T)
            dvs.append(dvT.T)
            dcs.append(jnp.transpose(dck[:, :hg, :], (2, 0, 1)).reshape(S, H))
            d, dqs, GW["mix_norm"][i] = q_bwd(d_out, dq, rec["h_in"], row(mix_norm[i]), FW["attn_w_q"][j], scale, tm)
            GW["attn_w_q"][j] = weight_grad(rec["n"], dqs, ts, "grad_attn_w_q")
            if j == 0:
                dc = dcs[0]
                for extra in dcs[1:]:
                    dc = dc + extra
                d, dk_sum, dv_sum, dfl, g_kv_norm, g_b_f = kv_bwd(d, dks, dvs, dc, kv["fl"], kv["h"],
                                                                  row(kv_norm), wk, wv, wf, tm)
                g_w_kvf = jnp.concatenate([weight_grad(kv["n"], dk_sum, ts, "grad_w_k"),
                                           weight_grad(kv["n"], dv_sum, ts, "grad_w_v"),
                                           weight_grad(kv["n"], dfl, ts, "grad_w_f")], axis=1)
        else:
            GW["conv_w_pw2"][i] = weight_grad(rec["sw"], d_out, ts, "grad_conv_w_pw2")
            (d, du, GW["conv_b_pw2"][i], GW["conv_ln_g"][i], GW["conv_ln_b"][i], GW["conv_b_dw"][i],
             GW["conv_w_dw"][i], GW["conv_b_pw1"][i], GW["mix_norm"][i]) = conv_bwd(
                d_out, rec["h_in"], row(mix_norm[i]), FW["conv_w_pw1"][i], FW["conv_w_dw"][i],
                row(FW["conv_ln_g"][i]), row(FW["conv_ln_b"][i]), FW["conv_w_pw2"][i], rec["u"], rec["z"], tm)
            GW["conv_w_pw1"][i] = weight_grad(rec["n"], du, ts, "grad_conv_w_pw1")
    grad_x = d[None]

    def stacked(n):
        return jnp.concatenate(GW[n], axis=0) if W[n].ndim == 2 else jnp.stack(GW[n])

    full_grads = {n: stacked(n) for n in GW if n not in REPLICATED}
    full_grads["w_kvf"] = g_w_kvf

    sharded_names = big_names + small_names
    axes = {**SHARD_AXIS_BIG, **SHARD_AXIS_SMALL}
    chunks = jnp.concatenate([_chunks_from_full(full_grads[n], axes[n]) for n in sharded_names], axis=1)
    chunks = _pack_rows(chunks, PACK_ROW_TILE)
    core = lax.axis_index("c").astype(jnp.int32).reshape(1)
    parts = chip_exchange(chip_partial(chunks, sibling_exchange(chunks), core))

    pack_sh = lambda src: _pack([src[n] for n in sharded_names], PACK_ROW_TILE)
    outs_sh = adamw_sharded(parts, pack_sh(W), pack_sh(M), pack_sh(V))
    shard_shapes = [W[n].shape for n in sharded_names]
    res = {}
    for kind, packed in zip(("grad", "delta", "new_m", "new_v"), outs_sh):
        for n, a in zip(sharded_names, _unpack(packed, shard_shapes)):
            res[kind, n] = a

    rep_grads = {"mix_norm": jnp.concatenate(GW["mix_norm"], axis=0), "kv_norm": g_kv_norm,
                 "b_f": g_b_f, "ffn_norm": jnp.concatenate(GW["ffn_norm"], axis=0),
                 "ple_norm": jnp.concatenate(GW["ple_norm"], axis=0), "final_norm": g_final}

    def pack_rep(src, extra=None):
        rows_ = [jnp.pad(src[n].reshape(-1, src[n].shape[-1]), ((0, 0), (0, D - src[n].shape[-1])))
                 for n in REPLICATED]
        if extra is not None:
            rows_.append(jnp.pad(extra, ((0, 0), (0, D - extra.shape[-1]))))
        else:
            rows_.append(jnp.zeros((1, D), F32))
        flat = jnp.concatenate(rows_, axis=0)
        return jnp.pad(flat, ((0, -flat.shape[0] % 8), (0, 0)))

    rep_g = all_gather(pack_rep(rep_grads, loss_part), "replicated_all_gather")
    outs_rep = adamw_replicated(rep_g, pack_rep(W), pack_rep(M), pack_rep(V))
    n_rep_rows = sum(int(np.prod(W[n].shape[:-1])) for n in REPLICATED)
    for kind, packed in zip(("grad", "delta", "new_m", "new_v"), outs_rep):
        r0 = 0
        for n in REPLICATED:
            nr = int(np.prod(W[n].shape[:-1]))
            res[kind, n] = packed[r0:r0 + nr, :W[n].shape[-1]].reshape(W[n].shape)
            r0 += nr
    loss = outs_rep[0][n_rep_rows, 0]

    return (loss, grad_x, *[res["grad", n] for n in WEIGHTS], *[res["delta", n] for n in WEIGHTS],
            *[res["new_m", n] for n in WEIGHTS], *[res["new_v", n] for n in WEIGHTS])
```

```python
import numpy as np
import jax
import jax.numpy as jnp
from jax import lax
from jax.experimental import pallas as pl
from jax.experimental.pallas import tpu as pltpu

F32 = jnp.float32
MXU_DTYPE = jnp.bfloat16
ACT_DTYPE = jnp.bfloat16
WIRE_DTYPE = jnp.bfloat16

N_DEV = 8
N_CHIP = 4
EPS = 1e-6
NEG_BIG = -1e30
ADAM_LR = 0.001
ADAM_B1 = 0.9
ADAM_B2 = 0.999
ADAM_EPS = 1e-08
ADAM_WD = 0.01
ADAM_STEP = 10

VMEM_LIMIT_BYTES = 56 * 1024 * 1024
PACK_COLS = 1024
PACK_ROW_TILE = 256
FLASH_FWD_TILE = (1024, 512)
FLASH_BWD_TILE = (1024, 512)
HALO = 32
MESH = pl.DeviceIdType.MESH

SHARD_AXIS_BIG = {"conv_w_pw1": 2, "conv_w_pw2": 1, "w_kvf": 1, "attn_w_q": 1, "attn_w_o": 1,
                  "ffn_w1": 2, "ffn_w2": 1, "ple_w_gate": 1, "ple_w_proj": 2}
SHARD_AXIS_SMALL = {"conv_b_pw1": 1, "conv_w_dw": 2, "conv_b_dw": 1, "conv_ln_g": 1, "conv_ln_b": 1,
                    "conv_b_pw2": 1}
REPLICATED = ["mix_norm", "kv_norm", "b_f", "ffn_norm", "ple_norm", "final_norm"]
WEIGHTS = ["mix_norm", "conv_w_pw1", "conv_b_pw1", "conv_w_dw", "conv_b_dw", "conv_ln_g", "conv_ln_b",
           "conv_w_pw2", "conv_b_pw2", "kv_norm", "w_kvf", "b_f", "attn_w_q", "attn_w_o", "ffn_norm",
           "ffn_w1", "ffn_w2", "ple_norm", "ple_w_gate", "ple_w_proj", "final_norm"]


def _mm(a, b):
    return jnp.dot(a.astype(MXU_DTYPE), b.astype(MXU_DTYPE), preferred_element_type=F32)


def _mm_nt(a, b):
    return lax.dot_general(a.astype(MXU_DTYPE), b.astype(MXU_DTYPE), (((1,), (1,)), ((), ())),
                           preferred_element_type=F32)


def _mm_tn(a, b):
    return lax.dot_general(a.astype(MXU_DTYPE), b.astype(MXU_DTYPE), (((0,), (0,)), ((), ())),
                           preferred_element_type=F32)


def _split3(x):
    hi = x.astype(MXU_DTYPE)
    r1 = x - hi.astype(F32)
    mid = r1.astype(MXU_DTYPE)
    lo = (r1 - mid.astype(F32)).astype(MXU_DTYPE)
    return hi, mid, lo


def _tri_mm(tri, x):
    hi, mid, lo = _split3(x)
    return (jnp.dot(tri, lo, preferred_element_type=F32) + jnp.dot(tri, mid, preferred_element_type=F32)
            + jnp.dot(tri, hi, preferred_element_type=F32))


def _colsum8(x):
    tm, n = x.shape
    return jnp.sum(x.reshape(tm // 8, 8, n), axis=0)


def _rms(x, g):
    r = lax.rsqrt(jnp.mean(x * x, axis=-1, keepdims=True) + EPS)
    return x * r * g, r


def _rms_bwd(x, r, g, dn):
    w = dn * g
    dx = r * w - x * (r * r * r) * jnp.mean(w * x, axis=-1, keepdims=True)
    return dx, dn * x * r


def _sigmoid(x):
    return jax.nn.sigmoid(x)


def _params(n_grid):
    return pltpu.CompilerParams(dimension_semantics=("arbitrary",) * n_grid, vmem_limit_bytes=VMEM_LIMIT_BYTES)


def _rows(tm, n):
    return pl.BlockSpec((tm, n), lambda i: (i, 0))


def _rows_rev(tm, n, nt):
    return pl.BlockSpec((tm, n), lambda i: (nt - 1 - i, 0))


def _cols(n, tm):
    return pl.BlockSpec((n, tm), lambda i: (0, i))


def _cols_rev(n, tm, nt):
    return pl.BlockSpec((n, tm), lambda i: (0, nt - 1 - i))


def _whole(shape):
    nd = len(shape)
    return pl.BlockSpec(shape, lambda i: (0,) * nd)


def _row_tile(s, want):
    tm = min(s, want)
    assert s % tm == 0 and tm % 8 == 0, (s, tm)
    return tm


def conv_fwd(h, g, w1, b1, wd, bd, lg, lb, w2, b2, tm):
    S, D = h.shape
    CW = wd.shape[0]
    off = HALO - (CW - 1)
    assert 0 <= off and tm >= HALO
    nt = S // tm

    def body(h_ref, g_ref, w1_ref, b1_ref, wd_ref, bd_ref, lg_ref, lb_ref, w2_ref, b2_ref,
             ho_ref, n_ref, u_ref, z_ref, sw_ref, ext, win):
        @pl.when(pl.program_id(0) == 0)
        def _():
            ext[0:HALO, :] = jnp.zeros((HALO, D), F32)

        x = h_ref[...]
        n, _ = _rms(x, g_ref[...])
        n_ref[...] = n.astype(n_ref.dtype)
        u = _mm(n, w1_ref[...]) + b1_ref[...]
        u_ref[...] = u
        ext[HALO:HALO + tm, :] = u[:, :D] * _sigmoid(u[:, D:])
        z = jnp.broadcast_to(bd_ref[...], (tm, D))
        for b in range(8):
            amax = (CW - 1 - b) // 8
            win[0:tm + 8 * amax, :] = ext[off + b:off + b + tm + 8 * amax, :]
            for a8 in range(amax + 1):
                z = z + wd_ref[8 * a8 + b:8 * a8 + b + 1, :] * win[8 * a8:8 * a8 + tm, :]
        z_ref[...] = z
        ext[0:HALO, :] = ext[tm:tm + HALO, :]
        mu = jnp.mean(z, axis=-1, keepdims=True)
        zc = z - mu
        y = zc * lax.rsqrt(jnp.mean(zc * zc, axis=-1, keepdims=True) + EPS) * lg_ref[...] + lb_ref[...]
        sw = y * _sigmoid(y)
        sw_ref[...] = sw.astype(sw_ref.dtype)
        ho_ref[...] = x + _mm(sw, w2_ref[...]) + b2_ref[...]

    return pl.pallas_call(
        body, name="conv_fwd", grid=(nt,),
        in_specs=[_rows(tm, D), _whole((1, D)), _whole(w1.shape), _whole((1, 2 * D)), _whole(wd.shape),
                  _whole((1, D)), _whole((1, D)), _whole((1, D)), _whole(w2.shape), _whole((1, D))],
        out_specs=[_rows(tm, D), _rows(tm, D), _rows(tm, 2 * D), _rows(tm, D), _rows(tm, D)],
        out_shape=[jax.ShapeDtypeStruct((S, D), F32), jax.ShapeDtypeStruct((S, D), ACT_DTYPE),
                   jax.ShapeDtypeStruct((S, 2 * D), F32), jax.ShapeDtypeStruct((S, D), F32),
                   jax.ShapeDtypeStruct((S, D), ACT_DTYPE)],
        scratch_shapes=[pltpu.VMEM((HALO + tm, D), F32), pltpu.VMEM((HALO + tm, D), F32)],
        compiler_params=_params(1),
    )(h, g, w1, b1, wd, bd, lg, lb, w2, b2)


def ffn_fwd(h, g, w1, w2, tm):
    S, D = h.shape
    FF = w1.shape[1]

    def body(h_ref, g_ref, w1_ref, w2_ref, ho_ref, n_ref, a_ref, s_ref):
        x = h_ref[...]
        n, _ = _rms(x, g_ref[...])
        n_ref[...] = n.astype(n_ref.dtype)
        a = _mm(n, w1_ref[...])
        a_ref[...] = a
        s = jnp.square(jnp.maximum(a, 0.0))
        s_ref[...] = s.astype(s_ref.dtype)
        ho_ref[...] = x + _mm(s, w2_ref[...])

    return pl.pallas_call(
        body, name="ffn_fwd", grid=(S // tm,),
        in_specs=[_rows(tm, D), _whole((1, D)), _whole(w1.shape), _whole(w2.shape)],
        out_specs=[_rows(tm, D), _rows(tm, D), _rows(tm, FF), _rows(tm, FF)],
        out_shape=[jax.ShapeDtypeStruct((S, D), F32), jax.ShapeDtypeStruct((S, D), ACT_DTYPE),
                   jax.ShapeDtypeStruct((S, FF), F32), jax.ShapeDtypeStruct((S, FF), ACT_DTYPE)],
        compiler_params=_params(1),
    )(h, g, w1, w2)


def ple_fwd(h, g, wg, p, wp, tm):
    S, D = h.shape
    E = p.shape[1]

    def body(h_ref, g_ref, wg_ref, p_ref, wp_ref, ho_ref, n_ref, gate_ref):
        x = h_ref[...]
        n, _ = _rms(x, g_ref[...])
        n_ref[...] = n.astype(n_ref.dtype)
        gate = _sigmoid(_mm(n, wg_ref[...]))
        gate_ref[...] = gate
        ho_ref[...] = x + gate * _mm(p_ref[...], wp_ref[...])

    return pl.pallas_call(
        body, name="ple_fwd", grid=(S // tm,),
        in_specs=[_rows(tm, D), _whole((1, D)), _whole(wg.shape), _rows(tm, E), _whole(wp.shape)],
        out_specs=[_rows(tm, D), _rows(tm, D), _rows(tm, D)],
        out_shape=[jax.ShapeDtypeStruct((S, D), F32), jax.ShapeDtypeStruct((S, D), ACT_DTYPE),
                   jax.ShapeDtypeStruct((S, D), F32)],
        compiler_params=_params(1),
    )(h, g, wg, p, wp)


def kv_fwd(h, g, wk, wv, wf, bf, tm):
    S, D = h.shape
    H = wf.shape[1]

    def body(h_ref, g_ref, wk_ref, wv_ref, wf_ref, bf_ref, k_ref, kT_ref, vT_ref, n_ref, fl_ref, c_ref, carry):
        @pl.when(pl.program_id(0) == 0)
        def _():
            carry[...] = jnp.zeros_like(carry)

        n, _ = _rms(h_ref[...], g_ref[...])
        n_ref[...] = n.astype(n_ref.dtype)
        k = _mm(n, wk_ref[...])
        k_ref[...] = k.astype(k_ref.dtype)
        kT_ref[...] = k.T.astype(kT_ref.dtype)
        vT_ref[...] = _mm(n, wv_ref[...]).T.astype(vT_ref.dtype)
        fl = _mm(n, wf_ref[...]) + bf_ref[...]
        fl_ref[...] = fl
        logf = jnp.minimum(fl, 0.0) - jnp.log1p(jnp.exp(-jnp.abs(fl)))
        row = lax.broadcasted_iota(jnp.int32, (tm, tm), 0)
        col = lax.broadcasted_iota(jnp.int32, (tm, tm), 1)
        tri = (row >= col).astype(MXU_DTYPE)
        c = _tri_mm(tri, logf) + carry[...]
        c_ref[...] = c
        carry[...] = c[tm - 1:tm, :]

    return pl.pallas_call(
        body, name="kv_fwd", grid=(S // tm,),
        in_specs=[_rows(tm, D), _whole((1, D)), _whole(wk.shape), _whole(wv.shape), _whole(wf.shape),
                  _whole((1, H))],
        out_specs=[_rows(tm, D), _cols(D, tm), _cols(D, tm), _rows(tm, D), _rows(tm, H), _rows(tm, H)],
        out_shape=[jax.ShapeDtypeStruct((S, D), ACT_DTYPE), jax.ShapeDtypeStruct((D, S), ACT_DTYPE),
                   jax.ShapeDtypeStruct((D, S), ACT_DTYPE), jax.ShapeDtypeStruct((S, D), ACT_DTYPE),
                   jax.ShapeDtypeStruct((S, H), F32), jax.ShapeDtypeStruct((S, H), F32)],
        scratch_shapes=[pltpu.VMEM((1, H), F32)],
        compiler_params=_params(1),
    )(h, g, wk, wv, wf, bf)


def q_fwd(h, g, wq, scale, tm):
    S, D = h.shape

    def body(h_ref, g_ref, wq_ref, n_ref, q_ref, qT_ref):
        n, _ = _rms(h_ref[...], g_ref[...])
        n_ref[...] = n.astype(n_ref.dtype)
        q = _mm(n, wq_ref[...]) * scale
        q_ref[...] = q.astype(q_ref.dtype)
        qT_ref[...] = q.T.astype(qT_ref.dtype)

    return pl.pallas_call(
        body, name="q_fwd", grid=(S // tm,),
        in_specs=[_rows(tm, D), _whole((1, D)), _whole(wq.shape)],
        out_specs=[_rows(tm, D), _rows(tm, D), _cols(D, tm)],
        out_shape=[jax.ShapeDtypeStruct((S, D), ACT_DTYPE), jax.ShapeDtypeStruct((S, D), ACT_DTYPE),
                   jax.ShapeDtypeStruct((D, S), ACT_DTYPE)],
        compiler_params=_params(1),
    )(h, g, wq)


def attn_out_fwd(h, o, wo, tm):
    S, D = h.shape

    def body(h_ref, o_ref, wo_ref, ho_ref):
        ho_ref[...] = h_ref[...] + _mm(o_ref[...], wo_ref[...])

    return pl.pallas_call(
        body, name="attn_out_fwd", grid=(S // tm,),
        in_specs=[_rows(tm, D), _rows(tm, D), _whole(wo.shape)],
        out_specs=_rows(tm, D),
        out_shape=jax.ShapeDtypeStruct((S, D), F32),
        compiler_params=_params(1),
    )(h, o, wo)


def _causal_mask(key0, qry0, shape, key_axis):
    key = key0 + lax.broadcasted_iota(jnp.int32, shape, key_axis)
    qry = qry0 + lax.broadcasted_iota(jnp.int32, shape, 1 - key_axis)
    return key <= qry


def flash_fwd(qT, k, vT, c_col, c_row, dh, tq, tkc):
    D, S = qT.shape
    hg = 128 // dh
    G = D // 128
    per = tq // tkc
    assert tq % tkc == 0 and S % tq == 0

    def body(qT_ref, k_ref, vT_ref, ccol_ref, crow_ref, o_ref, o32_ref, lse_ref, m_scr, l_scr, acc_scr):
        i = pl.program_id(1)
        m_scr[...] = jnp.full(m_scr.shape, NEG_BIG, F32)
        l_scr[...] = jnp.zeros(l_scr.shape, F32)
        acc_scr[...] = jnp.zeros(acc_scr.shape, F32)

        def chunk(j, masked):
            keys = pl.ds(pl.multiple_of(j * tkc, tkc), tkc)
            for hh in range(hg):
                lanes = slice(hh * dh, (hh + 1) * dh)
                c0 = ccol_ref[pl.ds(j * tkc, 1), hh:hh + 1]
                r = crow_ref[hh:hh + 1, :] - c0
                s = _mm(k_ref[keys, lanes], qT_ref[lanes, :]) - (ccol_ref[keys, hh:hh + 1] - c0)
                if masked:
                    s = jnp.where(_causal_mask(j * tkc, i * tq, (tkc, tq), 0), s, NEG_BIG)
                m_old = m_scr[hh]
                m_new = jnp.maximum(m_old, jnp.max(s, axis=0, keepdims=True) + r)
                alpha = jnp.exp(m_old - m_new)
                p = jnp.exp(s - (m_new - r))
                l_scr[hh] = alpha * l_scr[hh] + jnp.sum(p, axis=0, keepdims=True)
                p_hi = p.astype(MXU_DTYPE)
                p_lo = p - p_hi.astype(F32)
                vc = vT_ref[lanes, keys]
                acc_scr[lanes, :] = alpha * acc_scr[lanes, :] + (_mm(vc, p_lo) + _mm(vc, p_hi))
                m_scr[hh] = m_new

        def full_chunk(j, carry):
            chunk(j, False)
            return carry

        lax.fori_loop(0, i * per, full_chunk, 0)
        for jj in range(per):
            chunk(i * per + jj, True)
        for hh in range(hg):
            lanes = slice(hh * dh, (hh + 1) * dh)
            acc_scr[lanes, :] = acc_scr[lanes, :] / l_scr[hh]
            lse_ref[hh:hh + 1, :] = m_scr[hh] + jnp.log(l_scr[hh])
        o = acc_scr[...].T
        o_ref[...] = o.astype(o_ref.dtype)
        o32_ref[...] = o

    return pl.pallas_call(
        body, name="flash_fwd", grid=(G, S // tq),
        in_specs=[pl.BlockSpec((128, tq), lambda g, i: (g, i)),
                  pl.BlockSpec((S, 128), lambda g, i: (0, g)),
                  pl.BlockSpec((128, S), lambda g, i: (g, 0)),
                  pl.BlockSpec((None, S, hg), lambda g, i: (g, 0, 0)),
                  pl.BlockSpec((None, hg, tq), lambda g, i: (g, 0, i))],
        out_specs=[pl.BlockSpec((tq, 128), lambda g, i: (i, g)),
                   pl.BlockSpec((tq, 128), lambda g, i: (i, g)),
                   pl.BlockSpec((None, hg, tq), lambda g, i: (g, 0, i))],
        out_shape=[jax.ShapeDtypeStruct((S, D), ACT_DTYPE), jax.ShapeDtypeStruct((S, D), F32),
                   jax.ShapeDtypeStruct((G, hg, S), F32)],
        scratch_shapes=[pltpu.VMEM((hg, 1, tq), F32), pltpu.VMEM((hg, 1, tq), F32), pltpu.VMEM((128, tq), F32)],
        compiler_params=_params(2),
    )(qT, k, vT, c_col, c_row)


def loss_head(h, g, target, tm):
    S, D = h.shape
    nt = S // tm

    def body(h_ref, g_ref, t_ref, dh_ref, dg_ref, loss_ref, dg_acc, loss_acc):
        i = pl.program_id(0)

        @pl.when(i == 0)
        def _():
            dg_acc[...] = jnp.zeros_like(dg_acc)
            loss_acc[...] = jnp.zeros_like(loss_acc)

        x = h_ref[...]
        gg = g_ref[...]
        y, r = _rms(x, gg)
        e = y - t_ref[...]
        loss_acc[...] += 0.5 * jnp.sum(jnp.mean(e * e, axis=-1, keepdims=True), axis=0, keepdims=True)
        dx, dgr = _rms_bwd(x, r, gg, e / D)
        dh_ref[...] = dx
        dg_acc[...] += _colsum8(dgr)

        @pl.when(i == nt - 1)
        def _():
            dg_ref[...] = jnp.sum(dg_acc[...], axis=0, keepdims=True)
            loss_ref[...] = jnp.broadcast_to(loss_acc[...], loss_ref.shape)

    return pl.pallas_call(
        body, name="loss_head", grid=(nt,),
        in_specs=[_rows(tm, D), _whole((1, D)), _rows(tm, D)],
        out_specs=[_rows(tm, D), _whole((1, D)), _whole((1, 128))],
        out_shape=[jax.ShapeDtypeStruct((S, D), F32), jax.ShapeDtypeStruct((1, D), F32),
                   jax.ShapeDtypeStruct((1, 128), F32)],
        scratch_shapes=[pltpu.VMEM((8, D), F32), pltpu.VMEM((1, 1), F32)],
        compiler_params=_params(1),
    )(h, g, target)


def ple_bwd(d, h, g, wg, gate, p, wp, tm):
    S, D = h.shape
    E = p.shape[1]
    nt = S // tm

    def body(d_ref, h_ref, g_ref, wg_ref, gate_ref, p_ref, wp_ref, di_ref, dz_ref, dpp_ref, dg_ref, dg_acc):
        i = pl.program_id(0)

        @pl.when(i == 0)
        def _():
            dg_acc[...] = jnp.zeros_like(dg_acc)

        dd = d_ref[...]
        x = h_ref[...]
        gg = g_ref[...]
        gt = gate_ref[...]
        pp = _mm(p_ref[...], wp_ref[...])
        dpp_ref[...] = (dd * gt).astype(dpp_ref.dtype)
        dz = dd * pp * gt * (1.0 - gt)
        dz_ref[...] = dz.astype(dz_ref.dtype)
        r = lax.rsqrt(jnp.mean(x * x, axis=-1, keepdims=True) + EPS)
        dx, dgr = _rms_bwd(x, r, gg, _mm_nt(dz, wg_ref[...]))
        di_ref[...] = dd + dx
        dg_acc[...] += _colsum8(dgr)

        @pl.when(i == nt - 1)
        def _():
            dg_ref[...] = jnp.sum(dg_acc[...], axis=0, keepdims=True)

    return pl.pallas_call(
        body, name="ple_bwd", grid=(nt,),
        in_specs=[_rows(tm, D), _rows(tm, D), _whole((1, D)), _whole(wg.shape), _rows(tm, D), _rows(tm, E),
                  _whole(wp.shape)],
        out_specs=[_rows(tm, D), _rows(tm, D), _rows(tm, D), _whole((1, D))],
        out_shape=[jax.ShapeDtypeStruct((S, D), F32), jax.ShapeDtypeStruct((S, D), ACT_DTYPE),
                   jax.ShapeDtypeStruct((S, D), ACT_DTYPE), jax.ShapeDtypeStruct((1, D), F32)],
        scratch_shapes=[pltpu.VMEM((8, D), F32)],
        compiler_params=_params(1),
    )(d, h, g, wg, gate, p, wp)


def ffn_bwd(d, h, g, w1, w2, a, tm):
    S, D = h.shape
    FF = w1.shape[1]
    nt = S // tm

    def body(d_ref, h_ref, g_ref, w1_ref, w2_ref, a_ref, di_ref, da_ref, dg_ref, dg_acc):
        i = pl.program_id(0)

        @pl.when(i == 0)
        def _():
            dg_acc[...] = jnp.zeros_like(dg_acc)

        dd = d_ref[...]
        x = h_ref[...]
        da = _mm_nt(dd, w2_ref[...]) * (2.0 * jnp.maximum(a_ref[...], 0.0))
        da_ref[...] = da.astype(da_ref.dtype)
        r = lax.rsqrt(jnp.mean(x * x, axis=-1, keepdims=True) + EPS)
        dx, dgr = _rms_bwd(x, r, g_ref[...], _mm_nt(da, w1_ref[...]))
        di_ref[...] = dd + dx
        dg_acc[...] += _colsum8(dgr)

        @pl.when(i == nt - 1)
        def _():
            dg_ref[...] = jnp.sum(dg_acc[...], axis=0, keepdims=True)

    return pl.pallas_call(
        body, name="ffn_bwd", grid=(nt,),
        in_specs=[_rows(tm, D), _rows(tm, D), _whole((1, D)), _whole(w1.shape), _whole(w2.shape), _rows(tm, FF)],
        out_specs=[_rows(tm, D), _rows(tm, FF), _whole((1, D))],
        out_shape=[jax.ShapeDtypeStruct((S, D), F32), jax.ShapeDtypeStruct((S, FF), ACT_DTYPE),
                   jax.ShapeDtypeStruct((1, D), F32)],
        scratch_shapes=[pltpu.VMEM((8, D), F32)],
        compiler_params=_params(1),
    )(d, h, g, w1, w2, a)


def attn_out_bwd(d, wo, o32, dh, tm):
    S, D = d.shape
    H = D // dh

    def body(d_ref, wo_ref, o_ref, do_ref, doT_ref, delta_ref):
        do32 = _mm_nt(d_ref[...], wo_ref[...])
        do = do32.astype(do_ref.dtype)
        do_ref[...] = do
        doT_ref[...] = do32.T.astype(doT_ref.dtype)
        lane_head = lax.broadcasted_iota(jnp.int32, (D, H), 0) // dh
        seg = (lane_head == lax.broadcasted_iota(jnp.int32, (D, H), 1)).astype(MXU_DTYPE)
        hi, mid, lo = _split3(do.astype(F32) * o_ref[...])
        delta_ref[...] = (jnp.dot(lo, seg, preferred_element_type=F32) + jnp.dot(mid, seg, preferred_element_type=F32)
                          + jnp.dot(hi, seg, preferred_element_type=F32))

    return pl.pallas_call(
        body, name="attn_out_bwd", grid=(S // tm,),
        in_specs=[_rows(tm, D), _whole(wo.shape), _rows(tm, D)],
        out_specs=[_rows(tm, D), _cols(D, tm), _rows(tm, H)],
        out_shape=[jax.ShapeDtypeStruct((S, D), ACT_DTYPE), jax.ShapeDtypeStruct((D, S), ACT_DTYPE),
                   jax.ShapeDtypeStruct((S, H), F32)],
        compiler_params=_params(1),
    )(d, wo, o32)


def flash_bwd(q, qT, k, kT, vT, c_row, do, doT, qstat, dh, tk, tqc):
    S, D = q.shape
    hg = 128 // dh
    G = D // 128
    per = tk // tqc
    nchunk = S // tqc
    assert hg <= 8 and tk % tqc == 0 and S % tk == 0

    def body(q_ref, qT_ref, k_ref, kT_ref, vT_ref, crow_ref, do_ref, doT_ref, st_ref,
             dq_ref, dkT_ref, dvT_ref, dck_ref):
        ki = pl.program_id(1)

        @pl.when(ki == 0)
        def _():
            dq_ref[...] = jnp.zeros_like(dq_ref)

        dck_ref[...] = jnp.zeros_like(dck_ref)
        dkT_ref[...] = jnp.zeros_like(dkT_ref)
        dvT_ref[...] = jnp.zeros_like(dvT_ref)
        def chunk(jq, masked):
            rows = pl.ds(pl.multiple_of(jq * tqc, tqc), tqc)
            st = st_ref[rows, :]
            for hh in range(hg):
                lanes = slice(hh * dh, (hh + 1) * dh)
                kh = k_ref[:, lanes]
                ck = crow_ref[hh:hh + 1, :]
                c0 = ck[:, 0:1]
                u = (st[:, hh:hh + 1] - c0) - st[:, hg + hh:hg + hh + 1]
                s = (_mm(q_ref[rows, lanes], kT_ref[lanes, :]) - (ck - c0)) + u
                if masked:
                    s = jnp.where(_causal_mask(ki * tk, jq * tqc, (tqc, tk), 1), s, NEG_BIG)
                p = jnp.exp(s)
                dvT_ref[lanes, :] += _mm(doT_ref[lanes, rows], p)
                ds = p * (_mm(do_ref[rows, lanes], vT_ref[lanes, :]) - st[:, 2 * hg + hh:2 * hg + hh + 1])
                dkT_ref[lanes, :] += _mm(qT_ref[lanes, rows], ds)
                dck_ref[hh:hh + 1, :] -= jnp.sum(ds, axis=0, keepdims=True)
                dq_ref[rows, lanes] += _mm(ds, kh)

        for jj in range(per):
            chunk(ki * per + jj, True)

        def full_chunk(jq, carry):
            chunk(jq, False)
            return carry

        lax.fori_loop((ki + 1) * per, nchunk, full_chunk, 0)

    return pl.pallas_call(
        body, name="flash_bwd", grid=(G, S // tk),
        in_specs=[pl.BlockSpec((S, 128), lambda g, j: (0, g)),
                  pl.BlockSpec((128, S), lambda g, j: (g, 0)),
                  pl.BlockSpec((tk, 128), lambda g, j: (j, g)),
                  pl.BlockSpec((128, tk), lambda g, j: (g, j)),
                  pl.BlockSpec((128, tk), lambda g, j: (g, j)),
                  pl.BlockSpec((None, hg, tk), lambda g, j: (g, 0, j)),
                  pl.BlockSpec((S, 128), lambda g, j: (0, g)),
                  pl.BlockSpec((128, S), lambda g, j: (g, 0)),
                  pl.BlockSpec((None, S, 3 * hg), lambda g, j: (g, 0, 0))],
        out_specs=[pl.BlockSpec((S, 128), lambda g, j: (0, g)),
                   pl.BlockSpec((128, tk), lambda g, j: (g, j)),
                   pl.BlockSpec((128, tk), lambda g, j: (g, j)),
                   pl.BlockSpec((None, 8, tk), lambda g, j: (g, 0, j))],
        out_shape=[jax.ShapeDtypeStruct((S, D), F32), jax.ShapeDtypeStruct((D, S), F32),
                   jax.ShapeDtypeStruct((D, S), F32), jax.ShapeDtypeStruct((G, 8, S), F32)],
        compiler_params=_params(2),
    )(q, qT, k, kT, vT, c_row, do, doT, qstat)


def q_bwd(d, dq, h, g, wq, scale, tm):
    S, D = h.shape
    nt = S // tm

    def body(d_ref, dq_ref, h_ref, g_ref, wq_ref, di_ref, dqs_ref, dg_ref, dg_acc):
        i = pl.program_id(0)

        @pl.when(i == 0)
        def _():
            dg_acc[...] = jnp.zeros_like(dg_acc)

        x = h_ref[...]
        dqs = dq_ref[...] * scale
        dqs_ref[...] = dqs.astype(dqs_ref.dtype)
        r = lax.rsqrt(jnp.mean(x * x, axis=-1, keepdims=True) + EPS)
        dx, dgr = _rms_bwd(x, r, g_ref[...], _mm_nt(dqs, wq_ref[...]))
        di_ref[...] = d_ref[...] + dx
        dg_acc[...] += _colsum8(dgr)

        @pl.when(i == nt - 1)
        def _():
            dg_ref[...] = jnp.sum(dg_acc[...], axis=0, keepdims=True)

    return pl.pallas_call(
        body, name="q_bwd", grid=(nt,),
        in_specs=[_rows(tm, D), _rows(tm, D), _rows(tm, D), _whole((1, D)), _whole(wq.shape)],
        out_specs=[_rows(tm, D), _rows(tm, D), _whole((1, D))],
        out_shape=[jax.ShapeDtypeStruct((S, D), F32), jax.ShapeDtypeStruct((S, D), ACT_DTYPE),
                   jax.ShapeDtypeStruct((1, D), F32)],
        scratch_shapes=[pltpu.VMEM((8, D), F32)],
        compiler_params=_params(1),
    )(d, dq, h, g, wq)


def kv_bwd(d, dks, dvs, dc, fl, h, g, wk, wv, wf, tm):
    S, D = h.shape
    H = wf.shape[1]
    nt = S // tm
    nl = len(dks)

    def body(*refs):
        d_ref = refs[0]
        dk_refs = refs[1:1 + nl]
        dv_refs = refs[1 + nl:1 + 2 * nl]
        (dc_ref, fl_ref, h_ref, g_ref, wk_ref, wv_ref, wf_ref,
         di_ref, dk_ref, dv_ref, dfl_ref, dg_ref, dbf_ref, dg_acc, dbf_acc, carry) = refs[1 + 2 * nl:]
        i = pl.program_id(0)

        @pl.when(i == 0)
        def _():
            dg_acc[...] = jnp.zeros_like(dg_acc)
            dbf_acc[...] = jnp.zeros_like(dbf_acc)
            carry[...] = jnp.zeros_like(carry)

        dkT = dk_refs[0][...]
        dvT = dv_refs[0][...]
        for l in range(1, nl):
            dkT = dkT + dk_refs[l][...]
            dvT = dvT + dv_refs[l][...]
        dk = dkT.T
        dv = dvT.T
        dk_ref[...] = dk.astype(dk_ref.dtype)
        dv_ref[...] = dv.astype(dv_ref.dtype)
        row = lax.broadcasted_iota(jnp.int32, (tm, tm), 0)
        col = lax.broadcasted_iota(jnp.int32, (tm, tm), 1)
        tri = (col >= row).astype(MXU_DTYPE)
        dlogf = _tri_mm(tri, dc_ref[...]) + carry[...]
        carry[...] = dlogf[0:1, :]
        dfl = dlogf * _sigmoid(-fl_ref[...])
        dfl_ref[...] = dfl
        dbf_acc[...] += jnp.sum(dfl, axis=0, keepdims=True)
        x = h_ref[...]
        dn = _mm_nt(dk, wk_ref[...]) + _mm_nt(dv, wv_ref[...]) + _mm_nt(dfl, wf_ref[...])
        r = lax.rsqrt(jnp.mean(x * x, axis=-1, keepdims=True) + EPS)
        dx, dgr = _rms_bwd(x, r, g_ref[...], dn)
        di_ref[...] = d_ref[...] + dx
        dg_acc[...] += _colsum8(dgr)

        @pl.when(i == nt - 1)
        def _():
            dg_ref[...] = jnp.sum(dg_acc[...], axis=0, keepdims=True)
            dbf_ref[...] = dbf_acc[...]

    rev = lambda n: _rows_rev(tm, n, nt)
    return pl.pallas_call(
        body, name="kv_bwd", grid=(nt,),
        in_specs=([rev(D)] + [_cols_rev(D, tm, nt)] * (2 * nl)
                  + [rev(H), rev(H), rev(D), _whole((1, D)), _whole(wk.shape), _whole(wv.shape), _whole(wf.shape)]),
        out_specs=[rev(D), rev(D), rev(D), rev(H), _whole((1, D)), _whole((1, H))],
        out_shape=[jax.ShapeDtypeStruct((S, D), F32), jax.ShapeDtypeStruct((S, D), ACT_DTYPE),
                   jax.ShapeDtypeStruct((S, D), ACT_DTYPE), jax.ShapeDtypeStruct((S, H), F32),
                   jax.ShapeDtypeStruct((1, D), F32), jax.ShapeDtypeStruct((1, H), F32)],
        scratch_shapes=[pltpu.VMEM((8, D), F32), pltpu.VMEM((1, H), F32), pltpu.VMEM((1, H), F32)],
        compiler_params=_params(1),
    )(d, *dks, *dvs, dc, fl, h, g, wk, wv, wf)


def conv_bwd(d, h, g, w1, wd, lg, lb, w2, u, z, tm):
    S, D = h.shape
    CW = wd.shape[0]
    nt = S // tm
    assert tm >= HALO and CW - 1 <= HALO

    def body(d_ref, h_ref, g_ref, w1_ref, wd_ref, lg_ref, lb_ref, w2_ref, u_ref, z_ref,
             di_ref, du_ref, db2_ref, dlg_ref, dlb_ref, dbd_ref, dwd_ref, db1_ref, dg_ref,
             ext, win, db2_acc, dlg_acc, dlb_acc, dbd_acc, dwd_acc, db1_acc, dg_acc):
        i = pl.program_id(0)

        @pl.when(i == 0)
        def _():
            ext[tm:tm + HALO, :] = jnp.zeros((HALO, D), F32)
            for acc in (db2_acc, dlg_acc, dlb_acc, dbd_acc, dwd_acc, db1_acc, dg_acc):
                acc[...] = jnp.zeros_like(acc)

        dd = d_ref[...]
        db2_acc[...] += _colsum8(dd)
        dsw = _mm_nt(dd, w2_ref[...])
        zz = z_ref[...]
        zc = zz - jnp.mean(zz, axis=-1, keepdims=True)
        rs = lax.rsqrt(jnp.mean(zc * zc, axis=-1, keepdims=True) + EPS)
        xh = zc * rs
        lgv = lg_ref[...]
        y = xh * lgv + lb_ref[...]
        sg = _sigmoid(y)
        dy = dsw * (sg * (1.0 + y * (1.0 - sg)))
        dlg_acc[...] += _colsum8(dy * xh)
        dlb_acc[...] += _colsum8(dy)
        dxh = dy * lgv
        dz = rs * (dxh - jnp.mean(dxh, axis=-1, keepdims=True) - xh * jnp.mean(dxh * xh, axis=-1, keepdims=True))
        dbd_acc[...] += _colsum8(dz)
        ext[0:tm, :] = dz
        uu = u_ref[...]
        a = uu[:, :D]
        sgg = _sigmoid(uu[:, D:])
        glu = a * sgg
        dglu = jnp.zeros((tm, D), F32)
        for b in range(8):
            amax = (CW - 1 - b) // 8
            win[0:tm + 8 * amax, :] = ext[b:b + tm + 8 * amax, :]
            for a8 in range(amax + 1):
                k = CW - 1 - (8 * a8 + b)
                sh = win[8 * a8:8 * a8 + tm, :]
                dglu = dglu + wd_ref[k:k + 1, :] * sh
                dwd_acc[k] += _colsum8(glu * sh)
        ext[tm:tm + HALO, :] = ext[0:HALO, :]
        da = dglu * sgg
        dgg = dglu * a * sgg * (1.0 - sgg)
        du_ref[:, :D] = da.astype(du_ref.dtype)
        du_ref[:, D:] = dgg.astype(du_ref.dtype)
        db1_acc[:, :D] += _colsum8(da)
        db1_acc[:, D:] += _colsum8(dgg)
        dn = _mm_nt(da, w1_ref[:, :D]) + _mm_nt(dgg, w1_ref[:, D:])
        x = h_ref[...]
        r = lax.rsqrt(jnp.mean(x * x, axis=-1, keepdims=True) + EPS)
        dx, dgr = _rms_bwd(x, r, g_ref[...], dn)
        di_ref[...] = dd + dx
        dg_acc[...] += _colsum8(dgr)

        @pl.when(i == nt - 1)
        def _():
            db2_ref[...] = jnp.sum(db2_acc[...], axis=0, keepdims=True)
            dlg_ref[...] = jnp.sum(dlg_acc[...], axis=0, keepdims=True)
            dlb_ref[...] = jnp.sum(dlb_acc[...], axis=0, keepdims=True)
            dbd_ref[...] = jnp.sum(dbd_acc[...], axis=0, keepdims=True)
            dwd_ref[...] = jnp.sum(dwd_acc[...], axis=1)
            db1_ref[...] = jnp.sum(db1_acc[...], axis=0, keepdims=True)
            dg_ref[...] = jnp.sum(dg_acc[...], axis=0, keepdims=True)

    rev = lambda n: _rows_rev(tm, n, nt)
    vec = jax.ShapeDtypeStruct((1, D), F32)
    return pl.pallas_call(
        body, name="conv_bwd", grid=(nt,),
        in_specs=[rev(D), rev(D), _whole((1, D)), _whole(w1.shape), _whole(wd.shape), _whole((1, D)),
                  _whole((1, D)), _whole(w2.shape), rev(2 * D), rev(D)],
        out_specs=[rev(D), rev(2 * D), _whole((1, D)), _whole((1, D)), _whole((1, D)), _whole((1, D)),
                   _whole((CW, D)), _whole((1, 2 * D)), _whole((1, D))],
        out_shape=[jax.ShapeDtypeStruct((S, D), F32), jax.ShapeDtypeStruct((S, 2 * D), ACT_DTYPE),
                   vec, vec, vec, vec, jax.ShapeDtypeStruct((CW, D), F32),
                   jax.ShapeDtypeStruct((1, 2 * D), F32), vec],
        scratch_shapes=[pltpu.VMEM((tm + HALO, D), F32), pltpu.VMEM((tm + HALO, D), F32),
                        pltpu.VMEM((8, D), F32), pltpu.VMEM((8, D), F32),
                        pltpu.VMEM((8, D), F32), pltpu.VMEM((8, D), F32), pltpu.VMEM((CW, 8, D), F32),
                        pltpu.VMEM((8, 2 * D), F32), pltpu.VMEM((8, D), F32)],
        compiler_params=_params(1),
    )(d, h, g, w1, wd, lg, lb, w2, u, z)


def weight_grad(a, b, ts, name):
    S, M = a.shape
    N = b.shape[1]
    ta = M if M <= 1024 else 1024
    tb = N if N <= 1024 else 1024
    assert M % ta == 0 and N % tb == 0 and S % ts == 0

    def body(a_ref, b_ref, o_ref):
        @pl.when(pl.program_id(2) == 0)
        def _():
            o_ref[...] = jnp.zeros_like(o_ref)

        o_ref[...] += _mm_tn(a_ref[...], b_ref[...])

    return pl.pallas_call(
        body, name=name, grid=(M // ta, N // tb, S // ts),
        in_specs=[pl.BlockSpec((ts, ta), lambda i, j, s: (s, i)), pl.BlockSpec((ts, tb), lambda i, j, s: (s, j))],
        out_specs=pl.BlockSpec((ta, tb), lambda i, j, s: (i, j)),
        out_shape=jax.ShapeDtypeStruct((M, N), F32),
        compiler_params=_params(3),
    )(a, b)


def _position():
    return lax.axis_index("x"), lax.axis_index("y"), lax.axis_index("c")


def all_gather(x, name):
    def body(x_ref, out_ref, send_sems, recv_sems, local_sem):
        x, y, c = _position()
        me, sibling = (x, y, c), (x, y, 1 - c)
        chips = [(1 - x, y), (x, 1 - y), (1 - x, 1 - y)]

        def slot(px, py, pc):
            return out_ref.at[4 * px + 2 * py + pc]

        def copy(k, block, to, src=None):
            return pltpu.make_async_remote_copy(
                src_ref=slot(*block) if src is None else src, dst_ref=slot(*block),
                send_sem=send_sems.at[k], recv_sem=recv_sems.at[k], device_id=to, device_id_type=MESH)

        mine = pltpu.make_async_copy(x_ref, slot(*me), local_sem)
        mine.start()
        first = [copy(0, me, sibling, src=x_ref)]
        first += [copy(1 + j, me, (*chip, c), src=x_ref) for j, chip in enumerate(chips)]
        for cp in first:
            cp.start()
        passed = [copy(4 + j, (*chip, c), sibling) for j, chip in enumerate(chips)]
        for j, chip in enumerate(chips):
            copy(1 + j, (*chip, c), me).wait_recv()
            passed[j].start()
        copy(0, sibling, me).wait_recv()
        for j, chip in enumerate(chips):
            copy(4 + j, (*chip, 1 - c), me).wait_recv()
        for cp in first + passed:
            cp.wait_send()
        mine.wait()

    return pl.pallas_call(
        body, name=name,
        in_specs=[pl.BlockSpec(memory_space=pl.ANY)], out_specs=pl.BlockSpec(memory_space=pl.ANY),
        out_shape=jax.ShapeDtypeStruct((N_DEV,) + x.shape, x.dtype),
        scratch_shapes=[pltpu.SemaphoreType.DMA((7,)), pltpu.SemaphoreType.DMA((7,)), pltpu.SemaphoreType.DMA],
    )(x)


def sibling_exchange(g):
    _, R, C = g.shape

    def body(g_ref, land_ref, send_sems, recv_sems):
        x, y, c = _position()
        copies = [pltpu.make_async_remote_copy(
            src_ref=g_ref.at[2 * j + 1 - c], dst_ref=land_ref.at[j], send_sem=send_sems.at[j],
            recv_sem=recv_sems.at[j], device_id=(x, y, 1 - c), device_id_type=MESH) for j in range(N_CHIP)]
        for cp in copies:
            cp.start()
        for cp in copies:
            cp.wait()

    return pl.pallas_call(
        body, name="grad_sibling_exchange",
        in_specs=[pl.BlockSpec(memory_space=pl.ANY)], out_specs=pl.BlockSpec(memory_space=pl.ANY),
        out_shape=jax.ShapeDtypeStruct((N_CHIP, R, C), g.dtype),
        scratch_shapes=[pltpu.SemaphoreType.DMA((N_CHIP,)), pltpu.SemaphoreType.DMA((N_CHIP,))],
    )(g)


def chip_partial(g, land, core):
    _, R, C = g.shape
    tr = _row_tile(R, PACK_ROW_TILE)

    def body(c_ref, g_ref, l_ref, o_ref):
        o_ref[...] = (g_ref[...] + l_ref[...]).astype(o_ref.dtype)

    grid_spec = pltpu.PrefetchScalarGridSpec(
        num_scalar_prefetch=1, grid=(N_CHIP, R // tr),
        in_specs=[pl.BlockSpec((None, tr, C), lambda j, i, cr: (2 * j + cr[0], i, 0)),
                  pl.BlockSpec((None, tr, C), lambda j, i, cr: (j, i, 0))],
        out_specs=pl.BlockSpec((None, tr, C), lambda j, i, cr: (j, i, 0)))
    return pl.pallas_call(
        body, name="grad_chip_partial", grid_spec=grid_spec,
        out_shape=jax.ShapeDtypeStruct((N_CHIP, R, C), WIRE_DTYPE),
        compiler_params=_params(2),
    )(core, g, land)


def chip_exchange(part):
    def body(p_ref, land_ref, send_sems, recv_sems, local_sem):
        x, y, c = _position()
        mychip = 2 * x + y
        chips = [(1 - x, y), (x, 1 - y), (1 - x, 1 - y)]
        mine = pltpu.make_async_copy(p_ref.at[mychip], land_ref.at[mychip], local_sem)
        mine.start()
        copies = [pltpu.make_async_remote_copy(
            src_ref=p_ref.at[2 * cx + cy], dst_ref=land_ref.at[mychip], send_sem=send_sems.at[k],
            recv_sem=recv_sems.at[k], device_id=(cx, cy, c), device_id_type=MESH)
            for k, (cx, cy) in enumerate(chips)]
        for cp in copies:
            cp.start()
        for k, (cx, cy) in enumerate(chips):
            pltpu.make_async_remote_copy(
                src_ref=p_ref.at[2 * cx + cy], dst_ref=land_ref.at[2 * cx + cy], send_sem=send_sems.at[k],
                recv_sem=recv_sems.at[k], device_id=(cx, cy, c), device_id_type=MESH).wait_recv()
        for cp in copies:
            cp.wait_send()
        mine.wait()

    return pl.pallas_call(
        body, name="grad_chip_exchange",
        in_specs=[pl.BlockSpec(memory_space=pl.ANY)], out_specs=pl.BlockSpec(memory_space=pl.ANY),
        out_shape=jax.ShapeDtypeStruct(part.shape, part.dtype),
        scratch_shapes=[pltpu.SemaphoreType.DMA((3,)), pltpu.SemaphoreType.DMA((3,)), pltpu.SemaphoreType.DMA],
    )(part)


def _adamw(w, g, m, v):
    m = ADAM_B1 * m + (1.0 - ADAM_B1) * g
    v = ADAM_B2 * v + (1.0 - ADAM_B2) * jnp.square(g)
    m_hat = m / (1.0 - ADAM_B1 ** ADAM_STEP)
    v_hat = v / (1.0 - ADAM_B2 ** ADAM_STEP)
    delta = -ADAM_LR * (m_hat / (jnp.sqrt(v_hat) + ADAM_EPS) + ADAM_WD * w)
    return delta, m, v


def adamw_sharded(parts, w, m, v):
    R, C = w.shape
    tr = _row_tile(R, PACK_ROW_TILE)

    def body(p_ref, w_ref, m_ref, v_ref, g_ref, d_ref, nm_ref, nv_ref):
        g = p_ref[0].astype(F32)
        for j in range(1, N_CHIP):
            g = g + p_ref[j].astype(F32)
        g_ref[...] = g
        d_ref[...], nm_ref[...], nv_ref[...] = _adamw(w_ref[...], g, m_ref[...], v_ref[...])

    out = jax.ShapeDtypeStruct((R, C), F32)
    return pl.pallas_call(
        body, name="adamw_sharded", grid=(R // tr,),
        in_specs=[pl.BlockSpec((N_CHIP, tr, C), lambda i: (0, i, 0)), _rows(tr, C), _rows(tr, C), _rows(tr, C)],
        out_specs=[_rows(tr, C)] * 4, out_shape=[out] * 4,
        compiler_params=_params(1),
    )(parts, w, m, v)


def adamw_replicated(gathered, w, m, v):
    R, C = w.shape

    def body(p_ref, w_ref, m_ref, v_ref, g_ref, d_ref, nm_ref, nv_ref):
        g = p_ref[0]
        for j in range(1, N_DEV):
            g = g + p_ref[j]
        g_ref[...] = g
        d_ref[...], nm_ref[...], nv_ref[...] = _adamw(w_ref[...], g, m_ref[...], v_ref[...])

    out = jax.ShapeDtypeStruct((R, C), F32)
    return pl.pallas_call(
        body, name="adamw_replicated", grid=(1,),
        in_specs=[_whole(gathered.shape), _whole((R, C)), _whole((R, C)), _whole((R, C))],
        out_specs=[_whole((R, C))] * 4, out_shape=[out] * 4,
        compiler_params=_params(1),
    )(gathered, w, m, v)


def _pack_rows(flat, rows_multiple):
    n = flat.shape[-1]
    per = PACK_COLS * rows_multiple
    padded = -(-n // per) * per
    flat = jnp.pad(flat, [(0, 0)] * (flat.ndim - 1) + [(0, padded - n)])
    return flat.reshape(flat.shape[:-1] + (padded // PACK_COLS, PACK_COLS))


def _pack(arrays, rows_multiple, dtype=None):
    flat = jnp.concatenate([a.reshape(-1) if dtype is None else a.reshape(-1).astype(dtype) for a in arrays])
    return _pack_rows(flat, rows_multiple)


def _unpack(packed, shapes):
    flat = packed.reshape(-1)
    out, off = [], 0
    for shp in shapes:
        n = int(np.prod(shp))
        out.append(flat[off:off + n].reshape(shp))
        off += n
    return out


def _full_from_gathered(gathered, shard_shapes, axes):
    flat = gathered.reshape(N_DEV, -1)
    out, off = [], 0
    for shp, ax in zip(shard_shapes, axes):
        n = int(np.prod(shp))
        seg = jnp.moveaxis(flat[:, off:off + n].reshape((N_DEV,) + tuple(shp)), 0, ax)
        out.append(seg.reshape(tuple(shp[:ax]) + (N_DEV * shp[ax],) + tuple(shp[ax + 1:])))
        off += n
    return out


def _chunks_from_full(full, ax):
    shp = full.shape
    split = full.reshape(shp[:ax] + (N_DEV, shp[ax] // N_DEV) + shp[ax + 1:])
    return jnp.moveaxis(split, ax, 0).reshape(N_DEV, -1)


def kernel(x, p, mix_norm, conv_w_pw1, conv_b_pw1, conv_w_dw, conv_b_dw, conv_ln_g, conv_ln_b, conv_w_pw2, conv_b_pw2, kv_norm, w_kvf, b_f, attn_w_q, attn_w_o, ffn_norm, ffn_w1, ffn_w2, ple_norm, ple_w_gate, ple_w_proj, final_norm, loss_target, m_mix_norm, m_conv_w_pw1, m_conv_b_pw1, m_conv_w_dw, m_conv_b_dw, m_conv_ln_g, m_conv_ln_b, m_conv_w_pw2, m_conv_b_pw2, m_kv_norm, m_w_kvf, m_b_f, m_attn_w_q, m_attn_w_o, m_ffn_norm, m_ffn_w1, m_ffn_w2, m_ple_norm, m_ple_w_gate, m_ple_w_proj, m_final_norm, v_mix_norm, v_conv_w_pw1, v_conv_b_pw1, v_conv_w_dw, v_conv_b_dw, v_conv_ln_g, v_conv_ln_b, v_conv_w_pw2, v_conv_b_pw2, v_kv_norm, v_w_kvf, v_b_f, v_attn_w_q, v_attn_w_o, v_ffn_norm, v_ffn_w1, v_ffn_w2, v_ple_norm, v_ple_w_gate, v_ple_w_proj, v_final_norm):
    given = dict(locals())
    W = {n: given[n] for n in WEIGHTS}
    M = {n: given["m_" + n] for n in WEIGHTS}
    V = {n: given["v_" + n] for n in WEIGHTS}

    _, S, D = x.shape
    NA = conv_w_pw1.shape[0]
    NB = attn_w_q.shape[0]
    DEPTH = NA + NB
    H = b_f.shape[0]
    dh = D // H
    hg = 128 // dh
    G = D // 128
    scale = dh ** -0.5
    tm = _row_tile(S, 256)
    tq_f = _row_tile(S, FLASH_FWD_TILE[0])
    tkc_f = _row_tile(tq_f, FLASH_FWD_TILE[1])
    tk_b = _row_tile(S, FLASH_BWD_TILE[0])
    tqc_b = _row_tile(tk_b, FLASH_BWD_TILE[1])
    ts = _row_tile(S, 512)
    xs = x[0]
    tgt = loss_target[0]
    ps = p[:, 0]
    row = lambda a: a.reshape(1, -1)

    big_names = list(SHARD_AXIS_BIG)
    small_names = list(SHARD_AXIS_SMALL)
    big = _full_from_gathered(
        all_gather(_pack([W[n] for n in big_names], 16, MXU_DTYPE), "weights_all_gather"),
        [W[n].shape for n in big_names], [SHARD_AXIS_BIG[n] for n in big_names])
    small = _full_from_gathered(
        all_gather(_pack([W[n] for n in small_names], 8), "vectors_all_gather"),
        [W[n].shape for n in small_names], [SHARD_AXIS_SMALL[n] for n in small_names])
    FW = dict(zip(big_names + small_names, big + small))
    wk, wv, wf = FW["w_kvf"][:, :D], FW["w_kvf"][:, D:2 * D], FW["w_kvf"][:, 2 * D:]

    saved = []
    h = xs
    kv = None
    for i in range(DEPTH):
        rec = {"h_in": h}
        if i < NA:
            h, rec["n"], rec["u"], rec["z"], rec["sw"] = conv_fwd(
                h, row(mix_norm[i]), FW["conv_w_pw1"][i], row(FW["conv_b_pw1"][i]), FW["conv_w_dw"][i],
                row(FW["conv_b_dw"][i]), row(FW["conv_ln_g"][i]), row(FW["conv_ln_b"][i]),
                FW["conv_w_pw2"][i], row(FW["conv_b_pw2"][i]), tm)
        else:
            j = i - NA
            if j == 0:
                k_, kT_, vT_, nkv, fl, c = kv_fwd(h, row(kv_norm), wk, wv, wf, row(b_f), tm)
                cg = c.reshape(S, G, hg)
                kv = dict(k=k_, kT=kT_, vT=vT_, n=nkv, fl=fl, h=h, c_col=jnp.transpose(cg, (1, 0, 2)),
                          c_row=jnp.transpose(cg, (1, 2, 0)))
            rec["n"], rec["q"], rec["qT"] = q_fwd(h, row(mix_norm[i]), FW["attn_w_q"][j], scale, tm)
            rec["o"], rec["o32"], rec["lse"] = flash_fwd(rec["qT"], kv["k"], kv["vT"], kv["c_col"], kv["c_row"], dh,
                                                         tq_f, tkc_f)
            h = attn_out_fwd(h, rec["o"], FW["attn_w_o"][j], tm)
        rec["h_ffn"] = h
        h, rec["n_ffn"], rec["a"], rec["s"] = ffn_fwd(h, row(ffn_norm[i]), FW["ffn_w1"][i], FW["ffn_w2"][i], tm)
        rec["h_ple"] = h
        h, rec["n_ple"], rec["gate"] = ple_fwd(h, row(ple_norm[i]), FW["ple_w_gate"][i], ps[i],
                                               FW["ple_w_proj"][i], tm)
        saved.append(rec)

    d, g_final, loss_part = loss_head(h, row(final_norm), tgt, tm)
    GW = {n: [None] * W[n].shape[0] for n in WEIGHTS if W[n].ndim > 1 and n != "w_kvf"}
    dks, dvs, dcs = [], [], []
    for i in reversed(range(DEPTH)):
        rec = saved[i]
        d_out = d
        d, dz, dpp, GW["ple_norm"][i] = ple_bwd(d_out, rec["h_ple"], row(ple_norm[i]), FW["ple_w_gate"][i],
                                                rec["gate"], ps[i], FW["ple_w_proj"][i], tm)
        GW["ple_w_gate"][i] = weight_grad(rec["n_ple"], dz, ts, "grad_ple_w_gate")
        GW["ple_w_proj"][i] = weight_grad(ps[i], dpp, ts, "grad_ple_w_proj")
        d_out = d
        d, da, GW["ffn_norm"][i] = ffn_bwd(d_out, rec["h_ffn"], row(ffn_norm[i]), FW["ffn_w1"][i],
                                           FW["ffn_w2"][i], rec["a"], tm)
        GW["ffn_w2"][i] = weight_grad(rec["s"], d_out, ts, "grad_ffn_w2")
        GW["ffn_w1"][i] = weight_grad(rec["n_ffn"], da, ts, "grad_ffn_w1")
        d_out = d
        if i >= NA:
            j = i - NA
            GW["attn_w_o"][j] = weight_grad(rec["o"], d_out, ts, "grad_attn_w_o")
            do, doT, delta = attn_out_bwd(d_out, FW["attn_w_o"][j], rec["o32"], dh, tm)
            qstat = jnp.concatenate([kv["c_col"], jnp.transpose(rec["lse"], (0, 2, 1)),
                                     jnp.transpose(delta.reshape(S, G, hg), (1, 0, 2))], axis=2)
            dq, dkT, dvT, dck = flash_bwd(rec["q"], rec["qT"], kv["k"], kv["kT"], kv["vT"], kv["c_row"], do, doT,
                                          qstat, dh, tk_b, tqc_b)
            dks.append(dkT)
            dvs.append(dvT)
            dcs.append(jnp.transpose(dck[:, :hg, :], (2, 0, 1)).reshape(S, H))
            d, dqs, GW["mix_norm"][i] = q_bwd(d_out, dq, rec["h_in"], row(mix_norm[i]), FW["attn_w_q"][j], scale, tm)
            GW["attn_w_q"][j] = weight_grad(rec["n"], dqs, ts, "grad_attn_w_q")
            if j == 0:
                dc = dcs[0]
                for extra in dcs[1:]:
                    dc = dc + extra
                d, dk_sum, dv_sum, dfl, g_kv_norm, g_b_f = kv_bwd(d, dks, dvs, dc, kv["fl"], kv["h"],
                                                                  row(kv_norm), wk, wv, wf, tm)
                g_w_kvf = jnp.concatenate([weight_grad(kv["n"], dk_sum, ts, "grad_w_k"),
                                           weight_grad(kv["n"], dv_sum, ts, "grad_w_v"),
                                           weight_grad(kv["n"], dfl, ts, "grad_w_f")], axis=1)
        else:
            GW["conv_w_pw2"][i] = weight_grad(rec["sw"], d_out, ts, "grad_conv_w_pw2")
            (d, du, GW["conv_b_pw2"][i], GW["conv_ln_g"][i], GW["conv_ln_b"][i], GW["conv_b_dw"][i],
             GW["conv_w_dw"][i], GW["conv_b_pw1"][i], GW["mix_norm"][i]) = conv_bwd(
                d_out, rec["h_in"], row(mix_norm[i]), FW["conv_w_pw1"][i], FW["conv_w_dw"][i],
                row(FW["conv_ln_g"][i]), row(FW["conv_ln_b"][i]), FW["conv_w_pw2"][i], rec["u"], rec["z"], tm)
            GW["conv_w_pw1"][i] = weight_grad(rec["n"], du, ts, "grad_conv_w_pw1")
    grad_x = d[None]

    def stacked(n):
        return jnp.concatenate(GW[n], axis=0) if W[n].ndim == 2 else jnp.stack(GW[n])

    full_grads = {n: stacked(n) for n in GW if n not in REPLICATED}
    full_grads["w_kvf"] = g_w_kvf

    sharded_names = big_names + small_names
    axes = {**SHARD_AXIS_BIG, **SHARD_AXIS_SMALL}
    chunks = jnp.concatenate([_chunks_from_full(full_grads[n], axes[n]) for n in sharded_names], axis=1)
    chunks = _pack_rows(chunks, PACK_ROW_TILE)
    core = lax.axis_index("c").astype(jnp.int32).reshape(1)
    parts = chip_exchange(chip_partial(chunks, sibling_exchange(chunks), core))

    pack_sh = lambda src: _pack([src[n] for n in sharded_names], PACK_ROW_TILE)
    outs_sh = adamw_sharded(parts, pack_sh(W), pack_sh(M), pack_sh(V))
    shard_shapes = [W[n].shape for n in sharded_names]
    res = {}
    for kind, packed in zip(("grad", "delta", "new_m", "new_v"), outs_sh):
        for n, a in zip(sharded_names, _unpack(packed, shard_shapes)):
            res[kind, n] = a

    rep_grads = {"mix_norm": jnp.concatenate(GW["mix_norm"], axis=0), "kv_norm": g_kv_norm,
                 "b_f": g_b_f, "ffn_norm": jnp.concatenate(GW["ffn_norm"], axis=0),
                 "ple_norm": jnp.concatenate(GW["ple_norm"], axis=0), "final_norm": g_final}

    def pack_rep(src, extra=None):
        rows_ = [jnp.pad(src[n].reshape(-1, src[n].shape[-1]), ((0, 0), (0, D - src[n].shape[-1])))
                 for n in REPLICATED]
        if extra is not None:
            rows_.append(jnp.pad(extra, ((0, 0), (0, D - extra.shape[-1]))))
        else:
            rows_.append(jnp.zeros((1, D), F32))
        flat = jnp.concatenate(rows_, axis=0)
        return jnp.pad(flat, ((0, -flat.shape[0] % 8), (0, 0)))

    rep_g = all_gather(pack_rep(rep_grads, loss_part), "replicated_all_gather")
    outs_rep = adamw_replicated(rep_g, pack_rep(W), pack_rep(M), pack_rep(V))
    n_rep_rows = sum(int(np.prod(W[n].shape[:-1])) for n in REPLICATED)
    for kind, packed in zip(("grad", "delta", "new_m", "new_v"), outs_rep):
        r0 = 0
        for n in REPLICATED:
            nr = int(np.prod(W[n].shape[:-1]))
            res[kind, n] = packed[r0:r0 + nr, :W[n].shape[-1]].reshape(W[n].shape)
            r0 += nr
    loss = outs_rep[0][n_rep_rows, 0]

    return (loss, grad_x, *[res["grad", n] for n in WEIGHTS], *[res["delta", n] for n in WEIGHTS],
            *[res["new_m", n] for n in WEIGHTS], *[res["new_v", n] for n in WEIGHTS])
```

```python
import numpy as np
import jax
import jax.numpy as jnp
from jax import lax
from jax.experimental import pallas as pl
from jax.experimental.pallas import tpu as pltpu

F32 = jnp.float32
MXU_DTYPE = jnp.bfloat16
ACT_DTYPE = jnp.bfloat16
WIRE_DTYPE = jnp.bfloat16

N_DEV = 8
N_CHIP = 4
EPS = 1e-6
NEG_BIG = -1e30
ADAM_LR = 0.001
ADAM_B1 = 0.9
ADAM_B2 = 0.999
ADAM_EPS = 1e-08
ADAM_WD = 0.01
ADAM_STEP = 10

VMEM_LIMIT_BYTES = 56 * 1024 * 1024
PACK_COLS = 1024
PACK_ROW_TILE = 256
FLASH_FWD_TILE = (1024, 512)
FLASH_BWD_TILE = (1024, 512)
FLASH_TILE = 128
HALO = 32
MESH = pl.DeviceIdType.MESH

SHARD_AXIS_BIG = {"conv_w_pw1": 2, "conv_w_pw2": 1, "w_kvf": 1, "attn_w_q": 1, "attn_w_o": 1,
                  "ffn_w1": 2, "ffn_w2": 1, "ple_w_gate": 1, "ple_w_proj": 2}
SHARD_AXIS_SMALL = {"conv_b_pw1": 1, "conv_w_dw": 2, "conv_b_dw": 1, "conv_ln_g": 1, "conv_ln_b": 1,
                    "conv_b_pw2": 1}
REPLICATED = ["mix_norm", "kv_norm", "b_f", "ffn_norm", "ple_norm", "final_norm"]
WEIGHTS = ["mix_norm", "conv_w_pw1", "conv_b_pw1", "conv_w_dw", "conv_b_dw", "conv_ln_g", "conv_ln_b",
           "conv_w_pw2", "conv_b_pw2", "kv_norm", "w_kvf", "b_f", "attn_w_q", "attn_w_o", "ffn_norm",
           "ffn_w1", "ffn_w2", "ple_norm", "ple_w_gate", "ple_w_proj", "final_norm"]


def _mm(a, b):
    return jnp.dot(a.astype(MXU_DTYPE), b.astype(MXU_DTYPE), preferred_element_type=F32)


def _mm_nt(a, b):
    return lax.dot_general(a.astype(MXU_DTYPE), b.astype(MXU_DTYPE), (((1,), (1,)), ((), ())),
                           preferred_element_type=F32)


def _mm_tn(a, b):
    return lax.dot_general(a.astype(MXU_DTYPE), b.astype(MXU_DTYPE), (((0,), (0,)), ((), ())),
                           preferred_element_type=F32)


def _split3(x):
    hi = x.astype(MXU_DTYPE)
    r1 = x - hi.astype(F32)
    mid = r1.astype(MXU_DTYPE)
    lo = (r1 - mid.astype(F32)).astype(MXU_DTYPE)
    return hi, mid, lo


def _tri_mm(tri, x):
    hi, mid, lo = _split3(x)
    return (jnp.dot(tri, lo, preferred_element_type=F32) + jnp.dot(tri, mid, preferred_element_type=F32)
            + jnp.dot(tri, hi, preferred_element_type=F32))


def _colsum8(x):
    tm, n = x.shape
    return jnp.sum(x.reshape(tm // 8, 8, n), axis=0)


def _rms(x, g):
    r = lax.rsqrt(jnp.mean(x * x, axis=-1, keepdims=True) + EPS)
    return x * r * g, r


def _rms_bwd(x, r, g, dn):
    w = dn * g
    dx = r * w - x * (r * r * r) * jnp.mean(w * x, axis=-1, keepdims=True)
    return dx, dn * x * r


def _sigmoid(x):
    return jax.nn.sigmoid(x)


def _params(n_grid):
    return pltpu.CompilerParams(dimension_semantics=("arbitrary",) * n_grid, vmem_limit_bytes=VMEM_LIMIT_BYTES)


def _rows(tm, n):
    return pl.BlockSpec((tm, n), lambda i: (i, 0))


def _rows_rev(tm, n, nt):
    return pl.BlockSpec((tm, n), lambda i: (nt - 1 - i, 0))


def _cols(n, tm):
    return pl.BlockSpec((n, tm), lambda i: (0, i))


def _cols_rev(n, tm, nt):
    return pl.BlockSpec((n, tm), lambda i: (0, nt - 1 - i))


def _whole(shape):
    nd = len(shape)
    return pl.BlockSpec(shape, lambda i: (0,) * nd)


def _row_tile(s, want):
    tm = min(s, want)
    assert s % tm == 0 and tm % 8 == 0, (s, tm)
    return tm


def conv_fwd(h, g, w1, b1, wd, bd, lg, lb, w2, b2, tm):
    S, D = h.shape
    CW = wd.shape[0]
    off = HALO - (CW - 1)
    assert 0 <= off and tm >= HALO
    nt = S // tm

    def body(h_ref, g_ref, w1_ref, b1_ref, wd_ref, bd_ref, lg_ref, lb_ref, w2_ref, b2_ref,
             ho_ref, n_ref, u_ref, z_ref, sw_ref, ext, win):
        @pl.when(pl.program_id(0) == 0)
        def _():
            ext[0:HALO, :] = jnp.zeros((HALO, D), F32)

        x = h_ref[...]
        n, _ = _rms(x, g_ref[...])
        n_ref[...] = n.astype(n_ref.dtype)
        u = _mm(n, w1_ref[...]) + b1_ref[...]
        u_ref[...] = u
        ext[HALO:HALO + tm, :] = u[:, :D] * _sigmoid(u[:, D:])
        z = jnp.broadcast_to(bd_ref[...], (tm, D))
        for b in range(8):
            amax = (CW - 1 - b) // 8
            win[0:tm + 8 * amax, :] = ext[off + b:off + b + tm + 8 * amax, :]
            for a8 in range(amax + 1):
                z = z + wd_ref[8 * a8 + b:8 * a8 + b + 1, :] * win[8 * a8:8 * a8 + tm, :]
        z_ref[...] = z
        ext[0:HALO, :] = ext[tm:tm + HALO, :]
        mu = jnp.mean(z, axis=-1, keepdims=True)
        zc = z - mu
        y = zc * lax.rsqrt(jnp.mean(zc * zc, axis=-1, keepdims=True) + EPS) * lg_ref[...] + lb_ref[...]
        sw = y * _sigmoid(y)
        sw_ref[...] = sw.astype(sw_ref.dtype)
        ho_ref[...] = x + _mm(sw, w2_ref[...]) + b2_ref[...]

    return pl.pallas_call(
        body, name="conv_fwd", grid=(nt,),
        in_specs=[_rows(tm, D), _whole((1, D)), _whole(w1.shape), _whole((1, 2 * D)), _whole(wd.shape),
                  _whole((1, D)), _whole((1, D)), _whole((1, D)), _whole(w2.shape), _whole((1, D))],
        out_specs=[_rows(tm, D), _rows(tm, D), _rows(tm, 2 * D), _rows(tm, D), _rows(tm, D)],
        out_shape=[jax.ShapeDtypeStruct((S, D), F32), jax.ShapeDtypeStruct((S, D), ACT_DTYPE),
                   jax.ShapeDtypeStruct((S, 2 * D), F32), jax.ShapeDtypeStruct((S, D), F32),
                   jax.ShapeDtypeStruct((S, D), ACT_DTYPE)],
        scratch_shapes=[pltpu.VMEM((HALO + tm, D), F32), pltpu.VMEM((HALO + tm, D), F32)],
        compiler_params=_params(1),
    )(h, g, w1, b1, wd, bd, lg, lb, w2, b2)


def ffn_fwd(h, g, w1, w2, tm):
    S, D = h.shape
    FF = w1.shape[1]

    def body(h_ref, g_ref, w1_ref, w2_ref, ho_ref, n_ref, a_ref, s_ref):
        x = h_ref[...]
        n, _ = _rms(x, g_ref[...])
        n_ref[...] = n.astype(n_ref.dtype)
        a = _mm(n, w1_ref[...])
        a_ref[...] = a
        s = jnp.square(jnp.maximum(a, 0.0))
        s_ref[...] = s.astype(s_ref.dtype)
        ho_ref[...] = x + _mm(s, w2_ref[...])

    return pl.pallas_call(
        body, name="ffn_fwd", grid=(S // tm,),
        in_specs=[_rows(tm, D), _whole((1, D)), _whole(w1.shape), _whole(w2.shape)],
        out_specs=[_rows(tm, D), _rows(tm, D), _rows(tm, FF), _rows(tm, FF)],
        out_shape=[jax.ShapeDtypeStruct((S, D), F32), jax.ShapeDtypeStruct((S, D), ACT_DTYPE),
                   jax.ShapeDtypeStruct((S, FF), F32), jax.ShapeDtypeStruct((S, FF), ACT_DTYPE)],
        compiler_params=_params(1),
    )(h, g, w1, w2)


def ple_fwd(h, g, wg, p, wp, tm):
    S, D = h.shape
    E = p.shape[1]

    def body(h_ref, g_ref, wg_ref, p_ref, wp_ref, ho_ref, n_ref, gate_ref):
        x = h_ref[...]
        n, _ = _rms(x, g_ref[...])
        n_ref[...] = n.astype(n_ref.dtype)
        gate = _sigmoid(_mm(n, wg_ref[...]))
        gate_ref[...] = gate
        ho_ref[...] = x + gate * _mm(p_ref[...], wp_ref[...])

    return pl.pallas_call(
        body, name="ple_fwd", grid=(S // tm,),
        in_specs=[_rows(tm, D), _whole((1, D)), _whole(wg.shape), _rows(tm, E), _whole(wp.shape)],
        out_specs=[_rows(tm, D), _rows(tm, D), _rows(tm, D)],
        out_shape=[jax.ShapeDtypeStruct((S, D), F32), jax.ShapeDtypeStruct((S, D), ACT_DTYPE),
                   jax.ShapeDtypeStruct((S, D), F32)],
        compiler_params=_params(1),
    )(h, g, wg, p, wp)


def kv_fwd(h, g, wk, wv, wf, bf, tm):
    S, D = h.shape
    H = wf.shape[1]

    def body(h_ref, g_ref, wk_ref, wv_ref, wf_ref, bf_ref, k_ref, kT_ref, vT_ref, n_ref, fl_ref, c_ref, carry):
        @pl.when(pl.program_id(0) == 0)
        def _():
            carry[...] = jnp.zeros_like(carry)

        n, _ = _rms(h_ref[...], g_ref[...])
        n_ref[...] = n.astype(n_ref.dtype)
        k = _mm(n, wk_ref[...])
        k_ref[...] = k.astype(k_ref.dtype)
        kT_ref[...] = k.T.astype(kT_ref.dtype)
        vT_ref[...] = _mm(n, wv_ref[...]).T.astype(vT_ref.dtype)
        fl = _mm(n, wf_ref[...]) + bf_ref[...]
        fl_ref[...] = fl
        logf = jnp.minimum(fl, 0.0) - jnp.log1p(jnp.exp(-jnp.abs(fl)))
        row = lax.broadcasted_iota(jnp.int32, (tm, tm), 0)
        col = lax.broadcasted_iota(jnp.int32, (tm, tm), 1)
        tri = (row >= col).astype(MXU_DTYPE)
        c = _tri_mm(tri, logf) + carry[...]
        c_ref[...] = c
        carry[...] = c[tm - 1:tm, :]

    return pl.pallas_call(
        body, name="kv_fwd", grid=(S // tm,),
        in_specs=[_rows(tm, D), _whole((1, D)), _whole(wk.shape), _whole(wv.shape), _whole(wf.shape),
                  _whole((1, H))],
        out_specs=[_rows(tm, D), _cols(D, tm), _cols(D, tm), _rows(tm, D), _rows(tm, H), _rows(tm, H)],
        out_shape=[jax.ShapeDtypeStruct((S, D), ACT_DTYPE), jax.ShapeDtypeStruct((D, S), ACT_DTYPE),
                   jax.ShapeDtypeStruct((D, S), ACT_DTYPE), jax.ShapeDtypeStruct((S, D), ACT_DTYPE),
                   jax.ShapeDtypeStruct((S, H), F32), jax.ShapeDtypeStruct((S, H), F32)],
        scratch_shapes=[pltpu.VMEM((1, H), F32)],
        compiler_params=_params(1),
    )(h, g, wk, wv, wf, bf)


def q_fwd(h, g, wq, scale, tm):
    S, D = h.shape

    def body(h_ref, g_ref, wq_ref, n_ref, q_ref, qT_ref):
        n, _ = _rms(h_ref[...], g_ref[...])
        n_ref[...] = n.astype(n_ref.dtype)
        q = _mm(n, wq_ref[...]) * scale
        q_ref[...] = q.astype(q_ref.dtype)
        qT_ref[...] = q.T.astype(qT_ref.dtype)

    return pl.pallas_call(
        body, name="q_fwd", grid=(S // tm,),
        in_specs=[_rows(tm, D), _whole((1, D)), _whole(wq.shape)],
        out_specs=[_rows(tm, D), _rows(tm, D), _cols(D, tm)],
        out_shape=[jax.ShapeDtypeStruct((S, D), ACT_DTYPE), jax.ShapeDtypeStruct((S, D), ACT_DTYPE),
                   jax.ShapeDtypeStruct((D, S), ACT_DTYPE)],
        compiler_params=_params(1),
    )(h, g, wq)


def attn_out_fwd(h, o, wo, tm):
    S, D = h.shape

    def body(h_ref, o_ref, wo_ref, ho_ref):
        ho_ref[...] = h_ref[...] + _mm(o_ref[...], wo_ref[...])

    return pl.pallas_call(
        body, name="attn_out_fwd", grid=(S // tm,),
        in_specs=[_rows(tm, D), _rows(tm, D), _whole(wo.shape)],
        out_specs=_rows(tm, D),
        out_shape=jax.ShapeDtypeStruct((S, D), F32),
        compiler_params=_params(1),
    )(h, o, wo)


def _causal_mask(key0, qry0, shape, key_axis):
    key = key0 + lax.broadcasted_iota(jnp.int32, shape, key_axis)
    qry = qry0 + lax.broadcasted_iota(jnp.int32, shape, 1 - key_axis)
    return key <= qry


def flash_fwd(qT, k, vT, c_col, c_row, dh, tq, tkc):
    D, S = qT.shape
    hg = 128 // dh
    G = D // 128
    per = tq // tkc
    assert tq % tkc == 0 and S % tq == 0

    def body(qT_ref, k_ref, vT_ref, ccol_ref, crow_ref, o_ref, o32_ref, lse_ref, m_scr, l_scr, acc_scr, mx_scr,
             *scratch):
        i = pl.program_id(1)
        m_scr[...] = jnp.full(m_scr.shape, NEG_BIG, F32)
        l_scr[...] = jnp.zeros(l_scr.shape, F32)
        acc_scr[...] = jnp.zeros(acc_scr.shape, F32)

        def head_shift(j, hh):
            c0 = ccol_ref[pl.ds(j * tkc, 1), hh:hh + 1]
            return c0, crow_ref[hh:hh + 1, :] - c0

        def scores(j, masked, slot):
            keys = pl.ds(pl.multiple_of(j * tkc, tkc), tkc)
            s_scr = scratch[slot]
            for hh in range(hg):
                lanes = slice(hh * dh, (hh + 1) * dh)
                c0, r = head_shift(j, hh)
                s = _mm(k_ref[keys, lanes], qT_ref[lanes, :]) - (ccol_ref[keys, hh:hh + 1] - c0)
                if masked:
                    s = jnp.where(_causal_mask(j * tkc, i * tq, (tkc, tq), 0), s, NEG_BIG)
                s_scr[hh] = s
                mx_scr[slot * hg + hh] = jnp.max(s, axis=0, keepdims=True) + r

        def update(j, slot):
            keys = pl.ds(pl.multiple_of(j * tkc, tkc), tkc)
            s_scr = scratch[slot]
            for hh in range(hg):
                lanes = slice(hh * dh, (hh + 1) * dh)
                _, r = head_shift(j, hh)
                m_old = m_scr[hh]
                m_new = jnp.maximum(m_old, mx_scr[slot * hg + hh])
                alpha = jnp.exp(m_old - m_new)
                p = jnp.exp(s_scr[hh] - (m_new - r))
                l_scr[hh] = alpha * l_scr[hh] + jnp.sum(p, axis=0, keepdims=True)
                acc_scr[lanes, :] = alpha * acc_scr[lanes, :] + _mm(vT_ref[lanes, keys], p)
                m_scr[hh] = m_new

        def chunks(j0, masked):
            for jj in range(per):
                scores(j0 + jj, masked, jj)
            for jj in range(per):
                update(j0 + jj, jj)

        def full_chunks(jb, carry):
            chunks(jb * per, False)
            return carry

        lax.fori_loop(0, i, full_chunks, 0)
        chunks(i * per, True)
        for hh in range(hg):
            lanes = slice(hh * dh, (hh + 1) * dh)
            acc_scr[lanes, :] = acc_scr[lanes, :] / l_scr[hh]
            lse_ref[hh:hh + 1, :] = m_scr[hh] + jnp.log(l_scr[hh])
        o = acc_scr[...].T
        o_ref[...] = o.astype(o_ref.dtype)
        o32_ref[...] = o

    return pl.pallas_call(
        body, name="flash_fwd", grid=(G, S // tq),
        in_specs=[pl.BlockSpec((128, tq), lambda g, i: (g, i)),
                  pl.BlockSpec((S, 128), lambda g, i: (0, g)),
                  pl.BlockSpec((128, S), lambda g, i: (g, 0)),
                  pl.BlockSpec((None, S, hg), lambda g, i: (g, 0, 0)),
                  pl.BlockSpec((None, hg, tq), lambda g, i: (g, 0, i))],
        out_specs=[pl.BlockSpec((tq, 128), lambda g, i: (i, g)),
                   pl.BlockSpec((tq, 128), lambda g, i: (i, g)),
                   pl.BlockSpec((None, hg, tq), lambda g, i: (g, 0, i))],
        out_shape=[jax.ShapeDtypeStruct((S, D), ACT_DTYPE), jax.ShapeDtypeStruct((S, D), F32),
                   jax.ShapeDtypeStruct((G, hg, S), F32)],
        scratch_shapes=([pltpu.VMEM((hg, 1, tq), F32), pltpu.VMEM((hg, 1, tq), F32), pltpu.VMEM((128, tq), F32),
                         pltpu.VMEM((per * hg, 1, tq), F32)]
                        + [pltpu.VMEM((hg, tkc, tq), F32)] * per),
        compiler_params=_params(2),
    )(qT, k, vT, c_col, c_row)


def loss_head(h, g, target, tm):
    S, D = h.shape
    nt = S // tm

    def body(h_ref, g_ref, t_ref, dh_ref, dg_ref, loss_ref, dg_acc, loss_acc):
        i = pl.program_id(0)

        @pl.when(i == 0)
        def _():
            dg_acc[...] = jnp.zeros_like(dg_acc)
            loss_acc[...] = jnp.zeros_like(loss_acc)

        x = h_ref[...]
        gg = g_ref[...]
        y, r = _rms(x, gg)
        e = y - t_ref[...]
        loss_acc[...] += 0.5 * jnp.sum(jnp.mean(e * e, axis=-1, keepdims=True), axis=0, keepdims=True)
        dx, dgr = _rms_bwd(x, r, gg, e / D)
        dh_ref[...] = dx
        dg_acc[...] += _colsum8(dgr)

        @pl.when(i == nt - 1)
        def _():
            dg_ref[...] = jnp.sum(dg_acc[...], axis=0, keepdims=True)
            loss_ref[...] = jnp.broadcast_to(loss_acc[...], loss_ref.shape)

    return pl.pallas_call(
        body, name="loss_head", grid=(nt,),
        in_specs=[_rows(tm, D), _whole((1, D)), _rows(tm, D)],
        out_specs=[_rows(tm, D), _whole((1, D)), _whole((1, 128))],
        out_shape=[jax.ShapeDtypeStruct((S, D), F32), jax.ShapeDtypeStruct((1, D), F32),
                   jax.ShapeDtypeStruct((1, 128), F32)],
        scratch_shapes=[pltpu.VMEM((8, D), F32), pltpu.VMEM((1, 1), F32)],
        compiler_params=_params(1),
    )(h, g, target)


def ple_bwd(d, h, g, wg, gate, p, wp, tm):
    S, D = h.shape
    E = p.shape[1]
    nt = S // tm

    def body(d_ref, h_ref, g_ref, wg_ref, gate_ref, p_ref, wp_ref, di_ref, dz_ref, dpp_ref, dg_ref, dg_acc):
        i = pl.program_id(0)

        @pl.when(i == 0)
        def _():
            dg_acc[...] = jnp.zeros_like(dg_acc)

        dd = d_ref[...]
        x = h_ref[...]
        gg = g_ref[...]
        gt = gate_ref[...]
        pp = _mm(p_ref[...], wp_ref[...])
        dpp_ref[...] = (dd * gt).astype(dpp_ref.dtype)
        dz = dd * pp * gt * (1.0 - gt)
        dz_ref[...] = dz.astype(dz_ref.dtype)
        r = lax.rsqrt(jnp.mean(x * x, axis=-1, keepdims=True) + EPS)
        dx, dgr = _rms_bwd(x, r, gg, _mm_nt(dz, wg_ref[...]))
        di_ref[...] = dd + dx
        dg_acc[...] += _colsum8(dgr)

        @pl.when(i == nt - 1)
        def _():
            dg_ref[...] = jnp.sum(dg_acc[...], axis=0, keepdims=True)

    return pl.pallas_call(
        body, name="ple_bwd", grid=(nt,),
        in_specs=[_rows(tm, D), _rows(tm, D), _whole((1, D)), _whole(wg.shape), _rows(tm, D), _rows(tm, E),
                  _whole(wp.shape)],
        out_specs=[_rows(tm, D), _rows(tm, D), _rows(tm, D), _whole((1, D))],
        out_shape=[jax.ShapeDtypeStruct((S, D), F32), jax.ShapeDtypeStruct((S, D), ACT_DTYPE),
                   jax.ShapeDtypeStruct((S, D), ACT_DTYPE), jax.ShapeDtypeStruct((1, D), F32)],
        scratch_shapes=[pltpu.VMEM((8, D), F32)],
        compiler_params=_params(1),
    )(d, h, g, wg, gate, p, wp)


def ffn_bwd(d, h, g, w1, w2, a, tm):
    S, D = h.shape
    FF = w1.shape[1]
    nt = S // tm

    def body(d_ref, h_ref, g_ref, w1_ref, w2_ref, a_ref, di_ref, da_ref, dg_ref, dg_acc):
        i = pl.program_id(0)

        @pl.when(i == 0)
        def _():
            dg_acc[...] = jnp.zeros_like(dg_acc)

        dd = d_ref[...]
        x = h_ref[...]
        da = _mm_nt(dd, w2_ref[...]) * (2.0 * jnp.maximum(a_ref[...], 0.0))
        da_ref[...] = da.astype(da_ref.dtype)
        r = lax.rsqrt(jnp.mean(x * x, axis=-1, keepdims=True) + EPS)
        dx, dgr = _rms_bwd(x, r, g_ref[...], _mm_nt(da, w1_ref[...]))
        di_ref[...] = dd + dx
        dg_acc[...] += _colsum8(dgr)

        @pl.when(i == nt - 1)
        def _():
            dg_ref[...] = jnp.sum(dg_acc[...], axis=0, keepdims=True)

    return pl.pallas_call(
        body, name="ffn_bwd", grid=(nt,),
        in_specs=[_rows(tm, D), _rows(tm, D), _whole((1, D)), _whole(w1.shape), _whole(w2.shape), _rows(tm, FF)],
        out_specs=[_rows(tm, D), _rows(tm, FF), _whole((1, D))],
        out_shape=[jax.ShapeDtypeStruct((S, D), F32), jax.ShapeDtypeStruct((S, FF), ACT_DTYPE),
                   jax.ShapeDtypeStruct((1, D), F32)],
        scratch_shapes=[pltpu.VMEM((8, D), F32)],
        compiler_params=_params(1),
    )(d, h, g, w1, w2, a)


def attn_out_bwd(d, wo, o32, dh, tm):
    S, D = d.shape
    H = D // dh

    def body(d_ref, wo_ref, o_ref, do_ref, doT_ref, delta_ref):
        do32 = _mm_nt(d_ref[...], wo_ref[...])
        do = do32.astype(do_ref.dtype)
        do_ref[...] = do
        doT_ref[...] = do32.T.astype(doT_ref.dtype)
        lane_head = lax.broadcasted_iota(jnp.int32, (D, H), 0) // dh
        seg = (lane_head == lax.broadcasted_iota(jnp.int32, (D, H), 1)).astype(MXU_DTYPE)
        hi, mid, lo = _split3(do.astype(F32) * o_ref[...])
        delta_ref[...] = (jnp.dot(lo, seg, preferred_element_type=F32) + jnp.dot(mid, seg, preferred_element_type=F32)
                          + jnp.dot(hi, seg, preferred_element_type=F32))

    return pl.pallas_call(
        body, name="attn_out_bwd", grid=(S // tm,),
        in_specs=[_rows(tm, D), _whole(wo.shape), _rows(tm, D)],
        out_specs=[_rows(tm, D), _cols(D, tm), _rows(tm, H)],
        out_shape=[jax.ShapeDtypeStruct((S, D), ACT_DTYPE), jax.ShapeDtypeStruct((D, S), ACT_DTYPE),
                   jax.ShapeDtypeStruct((S, H), F32)],
        compiler_params=_params(1),
    )(d, wo, o32)


def flash_bwd(q, qT, k, kT, vT, c_row, do, doT, qstat, dh, tk, tqc):
    S, D = q.shape
    hg = 128 // dh
    G = D // 128
    per = tk // tqc
    nchunk = S // tqc
    assert hg <= 8 and tk % tqc == 0 and S % tk == 0

    def body(q_ref, qT_ref, k_ref, kT_ref, vT_ref, crow_ref, do_ref, doT_ref, st_ref,
             dq_ref, dkT_ref, dvT_ref, dck_ref, dcq_ref):
        ki = pl.program_id(1)

        @pl.when(ki == 0)
        def _():
            dq_ref[...] = jnp.zeros_like(dq_ref)
            dcq_ref[...] = jnp.zeros_like(dcq_ref)

        dck_ref[...] = jnp.zeros_like(dck_ref)
        dkT_ref[...] = jnp.zeros_like(dkT_ref)
        dvT_ref[...] = jnp.zeros_like(dvT_ref)
        def chunk(jq, masked):
            rows = pl.ds(pl.multiple_of(jq * tqc, tqc), tqc)
            st = st_ref[rows, :]
            for hh in range(hg):
                lanes = slice(hh * dh, (hh + 1) * dh)
                kh = k_ref[:, lanes]
                ck = crow_ref[hh:hh + 1, :]
                c0 = ck[:, 0:1]
                u = (st[:, hh:hh + 1] - c0) - st[:, hg + hh:hg + hh + 1]
                s = (_mm(q_ref[rows, lanes], kT_ref[lanes, :]) - (ck - c0)) + u
                if masked:
                    s = jnp.where(_causal_mask(ki * tk, jq * tqc, (tqc, tk), 1), s, NEG_BIG)
                p = jnp.exp(s)
                dvT_ref[lanes, :] += _mm(doT_ref[lanes, rows], p)
                ds = p * (_mm(do_ref[rows, lanes], vT_ref[lanes, :]) - st[:, 2 * hg + hh:2 * hg + hh + 1])
                dkT_ref[lanes, :] += _mm(qT_ref[lanes, rows], ds)
                dck_ref[hh:hh + 1, :] -= jnp.sum(ds, axis=0, keepdims=True)
                dcq_ref[rows, hh:hh + 1] += jnp.sum(ds, axis=1, keepdims=True)
                dq_ref[rows, lanes] += _mm(ds, kh)

        for jj in range(per):
            chunk(ki * per + jj, True)

        def full_chunk(jq, carry):
            chunk(jq, False)
            return carry

        lax.fori_loop((ki + 1) * per, nchunk, full_chunk, 0)

    return pl.pallas_call(
        body, name="flash_bwd", grid=(G, S // tk),
        in_specs=[pl.BlockSpec((S, 128), lambda g, j: (0, g)),
                  pl.BlockSpec((128, S), lambda g, j: (g, 0)),
                  pl.BlockSpec((tk, 128), lambda g, j: (j, g)),
                  pl.BlockSpec((128, tk), lambda g, j: (g, j)),
                  pl.BlockSpec((128, tk), lambda g, j: (g, j)),
                  pl.BlockSpec((None, hg, tk), lambda g, j: (g, 0, j)),
                  pl.BlockSpec((S, 128), lambda g, j: (0, g)),
                  pl.BlockSpec((128, S), lambda g, j: (g, 0)),
                  pl.BlockSpec((None, S, 3 * hg), lambda g, j: (g, 0, 0))],
        out_specs=[pl.BlockSpec((S, 128), lambda g, j: (0, g)),
                   pl.BlockSpec((128, tk), lambda g, j: (g, j)),
                   pl.BlockSpec((128, tk), lambda g, j: (g, j)),
                   pl.BlockSpec((None, 8, tk), lambda g, j: (g, 0, j)),
                   pl.BlockSpec((None, S, hg), lambda g, j: (g, 0, 0))],
        out_shape=[jax.ShapeDtypeStruct((S, D), F32), jax.ShapeDtypeStruct((D, S), F32),
                   jax.ShapeDtypeStruct((D, S), F32), jax.ShapeDtypeStruct((G, 8, S), F32),
                   jax.ShapeDtypeStruct((G, S, hg), F32)],
        compiler_params=_params(2),
    )(q, qT, k, kT, vT, c_row, do, doT, qstat)


def q_bwd(d, dq, h, g, wq, scale, tm):
    S, D = h.shape
    nt = S // tm

    def body(d_ref, dq_ref, h_ref, g_ref, wq_ref, di_ref, dqs_ref, dg_ref, dg_acc):
        i = pl.program_id(0)

        @pl.when(i == 0)
        def _():
            dg_acc[...] = jnp.zeros_like(dg_acc)

        x = h_ref[...]
        dqs = dq_ref[...] * scale
        dqs_ref[...] = dqs.astype(dqs_ref.dtype)
        r = lax.rsqrt(jnp.mean(x * x, axis=-1, keepdims=True) + EPS)
        dx, dgr = _rms_bwd(x, r, g_ref[...], _mm_nt(dqs, wq_ref[...]))
        di_ref[...] = d_ref[...] + dx
        dg_acc[...] += _colsum8(dgr)

        @pl.when(i == nt - 1)
        def _():
            dg_ref[...] = jnp.sum(dg_acc[...], axis=0, keepdims=True)

    return pl.pallas_call(
        body, name="q_bwd", grid=(nt,),
        in_specs=[_rows(tm, D), _rows(tm, D), _rows(tm, D), _whole((1, D)), _whole(wq.shape)],
        out_specs=[_rows(tm, D), _rows(tm, D), _whole((1, D))],
        out_shape=[jax.ShapeDtypeStruct((S, D), F32), jax.ShapeDtypeStruct((S, D), ACT_DTYPE),
                   jax.ShapeDtypeStruct((1, D), F32)],
        scratch_shapes=[pltpu.VMEM((8, D), F32)],
        compiler_params=_params(1),
    )(d, dq, h, g, wq)


def kv_bwd(d, dks, dvs, dcs, fl, h, g, wk, wv, wf, tm):
    S, D = h.shape
    H = wf.shape[1]
    nt = S // tm
    nl = len(dks)
    nc = len(dcs)

    def body(*refs):
        d_ref = refs[0]
        dk_refs = refs[1:1 + nl]
        dv_refs = refs[1 + nl:1 + 2 * nl]
        dc_refs = refs[1 + 2 * nl:1 + 2 * nl + nc]
        (fl_ref, h_ref, g_ref, wk_ref, wv_ref, wf_ref,
         di_ref, dk_ref, dv_ref, dfl_ref, dg_ref, dbf_ref, dg_acc, dbf_acc, carry) = refs[1 + 2 * nl + nc:]
        i = pl.program_id(0)

        @pl.when(i == 0)
        def _():
            dg_acc[...] = jnp.zeros_like(dg_acc)
            dbf_acc[...] = jnp.zeros_like(dbf_acc)
            carry[...] = jnp.zeros_like(carry)

        dkT = dk_refs[0][...]
        dvT = dv_refs[0][...]
        for l in range(1, nl):
            dkT = dkT + dk_refs[l][...]
            dvT = dvT + dv_refs[l][...]
        dk = dkT.T
        dv = dvT.T
        dk_ref[...] = dk.astype(dk_ref.dtype)
        dv_ref[...] = dv.astype(dv_ref.dtype)
        row = lax.broadcasted_iota(jnp.int32, (tm, tm), 0)
        col = lax.broadcasted_iota(jnp.int32, (tm, tm), 1)
        tri = (col >= row).astype(MXU_DTYPE)
        dc = dc_refs[0][...]
        for l in range(1, nc):
            dc = dc + dc_refs[l][...]
        dlogf = _tri_mm(tri, dc) + carry[...]
        carry[...] = dlogf[0:1, :]
        dfl = dlogf * _sigmoid(-fl_ref[...])
        dfl_ref[...] = dfl
        dbf_acc[...] += jnp.sum(dfl, axis=0, keepdims=True)
        x = h_ref[...]
        dn = _mm_nt(dk, wk_ref[...]) + _mm_nt(dv, wv_ref[...]) + _mm_nt(dfl, wf_ref[...])
        r = lax.rsqrt(jnp.mean(x * x, axis=-1, keepdims=True) + EPS)
        dx, dgr = _rms_bwd(x, r, g_ref[...], dn)
        di_ref[...] = d_ref[...] + dx
        dg_acc[...] += _colsum8(dgr)

        @pl.when(i == nt - 1)
        def _():
            dg_ref[...] = jnp.sum(dg_acc[...], axis=0, keepdims=True)
            dbf_ref[...] = dbf_acc[...]

    rev = lambda n: _rows_rev(tm, n, nt)
    return pl.pallas_call(
        body, name="kv_bwd", grid=(nt,),
        in_specs=([rev(D)] + [_cols_rev(D, tm, nt)] * (2 * nl)
                  + [rev(H)] * nc
                  + [rev(H), rev(D), _whole((1, D)), _whole(wk.shape), _whole(wv.shape), _whole(wf.shape)]),
        out_specs=[rev(D), rev(D), rev(D), rev(H), _whole((1, D)), _whole((1, H))],
        out_shape=[jax.ShapeDtypeStruct((S, D), F32), jax.ShapeDtypeStruct((S, D), ACT_DTYPE),
                   jax.ShapeDtypeStruct((S, D), ACT_DTYPE), jax.ShapeDtypeStruct((S, H), F32),
                   jax.ShapeDtypeStruct((1, D), F32), jax.ShapeDtypeStruct((1, H), F32)],
        scratch_shapes=[pltpu.VMEM((8, D), F32), pltpu.VMEM((1, H), F32), pltpu.VMEM((1, H), F32)],
        compiler_params=_params(1),
    )(d, *dks, *dvs, *dcs, fl, h, g, wk, wv, wf)


def conv_bwd(d, h, g, w1, wd, lg, lb, w2, u, z, tm):
    S, D = h.shape
    CW = wd.shape[0]
    nt = S // tm
    assert tm >= HALO and CW - 1 <= HALO

    def body(d_ref, h_ref, g_ref, w1_ref, wd_ref, lg_ref, lb_ref, w2_ref, u_ref, z_ref,
             di_ref, du_ref, db2_ref, dlg_ref, dlb_ref, dbd_ref, dwd_ref, db1_ref, dg_ref,
             ext, win, db2_acc, dlg_acc, dlb_acc, dbd_acc, dwd_acc, db1_acc, dg_acc):
        i = pl.program_id(0)

        @pl.when(i == 0)
        def _():
            ext[tm:tm + HALO, :] = jnp.zeros((HALO, D), F32)
            for acc in (db2_acc, dlg_acc, dlb_acc, dbd_acc, dwd_acc, db1_acc, dg_acc):
                acc[...] = jnp.zeros_like(acc)

        dd = d_ref[...]
        db2_acc[...] += _colsum8(dd)
        dsw = _mm_nt(dd, w2_ref[...])
        zz = z_ref[...]
        zc = zz - jnp.mean(zz, axis=-1, keepdims=True)
        rs = lax.rsqrt(jnp.mean(zc * zc, axis=-1, keepdims=True) + EPS)
        xh = zc * rs
        lgv = lg_ref[...]
        y = xh * lgv + lb_ref[...]
        sg = _sigmoid(y)
        dy = dsw * (sg * (1.0 + y * (1.0 - sg)))
        dlg_acc[...] += _colsum8(dy * xh)
        dlb_acc[...] += _colsum8(dy)
        dxh = dy * lgv
        dz = rs * (dxh - jnp.mean(dxh, axis=-1, keepdims=True) - xh * jnp.mean(dxh * xh, axis=-1, keepdims=True))
        dbd_acc[...] += _colsum8(dz)
        ext[0:tm, :] = dz
        uu = u_ref[...]
        a = uu[:, :D]
        sgg = _sigmoid(uu[:, D:])
        glu = a * sgg
        dglu = jnp.zeros((tm, D), F32)
        for b in range(8):
            amax = (CW - 1 - b) // 8
            win[0:tm + 8 * amax, :] = ext[b:b + tm + 8 * amax, :]
            for a8 in range(amax + 1):
                k = CW - 1 - (8 * a8 + b)
                sh = win[8 * a8:8 * a8 + tm, :]
                dglu = dglu + wd_ref[k:k + 1, :] * sh
                dwd_acc[k] += _colsum8(glu * sh)
        ext[tm:tm + HALO, :] = ext[0:HALO, :]
        da = dglu * sgg
        dgg = dglu * a * sgg * (1.0 - sgg)
        du_ref[:, :D] = da.astype(du_ref.dtype)
        du_ref[:, D:] = dgg.astype(du_ref.dtype)
        db1_acc[:, :D] += _colsum8(da)
        db1_acc[:, D:] += _colsum8(dgg)
        dn = _mm_nt(da, w1_ref[:, :D]) + _mm_nt(dgg, w1_ref[:, D:])
        x = h_ref[...]
        r = lax.rsqrt(jnp.mean(x * x, axis=-1, keepdims=True) + EPS)
        dx, dgr = _rms_bwd(x, r, g_ref[...], dn)
        di_ref[...] = dd + dx
        dg_acc[...] += _colsum8(dgr)

        @pl.when(i == nt - 1)
        def _():
            db2_ref[...] = jnp.sum(db2_acc[...], axis=0, keepdims=True)
            dlg_ref[...] = jnp.sum(dlg_acc[...], axis=0, keepdims=True)
            dlb_ref[...] = jnp.sum(dlb_acc[...], axis=0, keepdims=True)
            dbd_ref[...] = jnp.sum(dbd_acc[...], axis=0, keepdims=True)
            dwd_ref[...] = jnp.sum(dwd_acc[...], axis=1)
            db1_ref[...] = jnp.sum(db1_acc[...], axis=0, keepdims=True)
            dg_ref[...] = jnp.sum(dg_acc[...], axis=0, keepdims=True)

    rev = lambda n: _rows_rev(tm, n, nt)
    vec = jax.ShapeDtypeStruct((1, D), F32)
    return pl.pallas_call(
        body, name="conv_bwd", grid=(nt,),
        in_specs=[rev(D), rev(D), _whole((1, D)), _whole(w1.shape), _whole(wd.shape), _whole((1, D)),
                  _whole((1, D)), _whole(w2.shape), rev(2 * D), rev(D)],
        out_specs=[rev(D), rev(2 * D), _whole((1, D)), _whole((1, D)), _whole((1, D)), _whole((1, D)),
                   _whole((CW, D)), _whole((1, 2 * D)), _whole((1, D))],
        out_shape=[jax.ShapeDtypeStruct((S, D), F32), jax.ShapeDtypeStruct((S, 2 * D), ACT_DTYPE),
                   vec, vec, vec, vec, jax.ShapeDtypeStruct((CW, D), F32),
                   jax.ShapeDtypeStruct((1, 2 * D), F32), vec],
        scratch_shapes=[pltpu.VMEM((tm + HALO, D), F32), pltpu.VMEM((tm + HALO, D), F32),
                        pltpu.VMEM((8, D), F32), pltpu.VMEM((8, D), F32),
                        pltpu.VMEM((8, D), F32), pltpu.VMEM((8, D), F32), pltpu.VMEM((CW, 8, D), F32),
                        pltpu.VMEM((8, 2 * D), F32), pltpu.VMEM((8, D), F32)],
        compiler_params=_params(1),
    )(d, h, g, w1, wd, lg, lb, w2, u, z)


def weight_grad(a, b, ts, name):
    S, M = a.shape
    N = b.shape[1]
    ta = M if M <= 1024 else 1024
    tb = N if N <= 1024 else 1024
    assert M % ta == 0 and N % tb == 0 and S % ts == 0

    def body(a_ref, b_ref, o_ref):
        @pl.when(pl.program_id(2) == 0)
        def _():
            o_ref[...] = jnp.zeros_like(o_ref)

        o_ref[...] += _mm_tn(a_ref[...], b_ref[...])

    return pl.pallas_call(
        body, name=name, grid=(M // ta, N // tb, S // ts),
        in_specs=[pl.BlockSpec((ts, ta), lambda i, j, s: (s, i)), pl.BlockSpec((ts, tb), lambda i, j, s: (s, j))],
        out_specs=pl.BlockSpec((ta, tb), lambda i, j, s: (i, j)),
        out_shape=jax.ShapeDtypeStruct((M, N), F32),
        compiler_params=_params(3),
    )(a, b)


def _position():
    return lax.axis_index("x"), lax.axis_index("y"), lax.axis_index("c")


def all_gather(x, name):
    def body(x_ref, out_ref, send_sems, recv_sems, local_sem):
        x, y, c = _position()
        me, sibling = (x, y, c), (x, y, 1 - c)
        chips = [(1 - x, y), (x, 1 - y), (1 - x, 1 - y)]

        def slot(px, py, pc):
            return out_ref.at[4 * px + 2 * py + pc]

        def copy(k, block, to, src=None):
            return pltpu.make_async_remote_copy(
                src_ref=slot(*block) if src is None else src, dst_ref=slot(*block),
                send_sem=send_sems.at[k], recv_sem=recv_sems.at[k], device_id=to, device_id_type=MESH)

        mine = pltpu.make_async_copy(x_ref, slot(*me), local_sem)
        mine.start()
        first = [copy(0, me, sibling, src=x_ref)]
        first += [copy(1 + j, me, (*chip, c), src=x_ref) for j, chip in enumerate(chips)]
        for cp in first:
            cp.start()
        passed = [copy(4 + j, (*chip, c), sibling) for j, chip in enumerate(chips)]
        for j, chip in enumerate(chips):
            copy(1 + j, (*chip, c), me).wait_recv()
            passed[j].start()
        copy(0, sibling, me).wait_recv()
        for j, chip in enumerate(chips):
            copy(4 + j, (*chip, 1 - c), me).wait_recv()
        for cp in first + passed:
            cp.wait_send()
        mine.wait()

    return pl.pallas_call(
        body, name=name,
        in_specs=[pl.BlockSpec(memory_space=pl.ANY)], out_specs=pl.BlockSpec(memory_space=pl.ANY),
        out_shape=jax.ShapeDtypeStruct((N_DEV,) + x.shape, x.dtype),
        scratch_shapes=[pltpu.SemaphoreType.DMA((7,)), pltpu.SemaphoreType.DMA((7,)), pltpu.SemaphoreType.DMA],
    )(x)


def sibling_exchange(g):
    _, R, C = g.shape

    def body(g_ref, land_ref, send_sems, recv_sems):
        x, y, c = _position()
        copies = [pltpu.make_async_remote_copy(
            src_ref=g_ref.at[2 * j + 1 - c], dst_ref=land_ref.at[j], send_sem=send_sems.at[j],
            recv_sem=recv_sems.at[j], device_id=(x, y, 1 - c), device_id_type=MESH) for j in range(N_CHIP)]
        for cp in copies:
            cp.start()
        for cp in copies:
            cp.wait()

    return pl.pallas_call(
        body, name="grad_sibling_exchange",
        in_specs=[pl.BlockSpec(memory_space=pl.ANY)], out_specs=pl.BlockSpec(memory_space=pl.ANY),
        out_shape=jax.ShapeDtypeStruct((N_CHIP, R, C), g.dtype),
        scratch_shapes=[pltpu.SemaphoreType.DMA((N_CHIP,)), pltpu.SemaphoreType.DMA((N_CHIP,))],
    )(g)


def chip_partial(g, land, core):
    _, R, C = g.shape
    tr = _row_tile(R, PACK_ROW_TILE)

    def body(c_ref, g_ref, l_ref, o_ref):
        o_ref[...] = (g_ref[...] + l_ref[...]).astype(o_ref.dtype)

    grid_spec = pltpu.PrefetchScalarGridSpec(
        num_scalar_prefetch=1, grid=(N_CHIP, R // tr),
        in_specs=[pl.BlockSpec((None, tr, C), lambda j, i, cr: (2 * j + cr[0], i, 0)),
                  pl.BlockSpec((None, tr, C), lambda j, i, cr: (j, i, 0))],
        out_specs=pl.BlockSpec((None, tr, C), lambda j, i, cr: (j, i, 0)))
    return pl.pallas_call(
        body, name="grad_chip_partial", grid_spec=grid_spec,
        out_shape=jax.ShapeDtypeStruct((N_CHIP, R, C), WIRE_DTYPE),
        compiler_params=_params(2),
    )(core, g, land)


def chip_exchange(part):
    def body(p_ref, land_ref, send_sems, recv_sems, local_sem):
        x, y, c = _position()
        mychip = 2 * x + y
        chips = [(1 - x, y), (x, 1 - y), (1 - x, 1 - y)]
        mine = pltpu.make_async_copy(p_ref.at[mychip], land_ref.at[mychip], local_sem)
        mine.start()
        copies = [pltpu.make_async_remote_copy(
            src_ref=p_ref.at[2 * cx + cy], dst_ref=land_ref.at[mychip], send_sem=send_sems.at[k],
            recv_sem=recv_sems.at[k], device_id=(cx, cy, c), device_id_type=MESH)
            for k, (cx, cy) in enumerate(chips)]
        for cp in copies:
            cp.start()
        for k, (cx, cy) in enumerate(chips):
            pltpu.make_async_remote_copy(
                src_ref=p_ref.at[2 * cx + cy], dst_ref=land_ref.at[2 * cx + cy], send_sem=send_sems.at[k],
                recv_sem=recv_sems.at[k], device_id=(cx, cy, c), device_id_type=MESH).wait_recv()
        for cp in copies:
            cp.wait_send()
        mine.wait()

    return pl.pallas_call(
        body, name="grad_chip_exchange",
        in_specs=[pl.BlockSpec(memory_space=pl.ANY)], out_specs=pl.BlockSpec(memory_space=pl.ANY),
        out_shape=jax.ShapeDtypeStruct(part.shape, part.dtype),
        scratch_shapes=[pltpu.SemaphoreType.DMA((3,)), pltpu.SemaphoreType.DMA((3,)), pltpu.SemaphoreType.DMA],
    )(part)


def _adamw(w, g, m, v):
    m = ADAM_B1 * m + (1.0 - ADAM_B1) * g
    v = ADAM_B2 * v + (1.0 - ADAM_B2) * jnp.square(g)
    m_hat = m / (1.0 - ADAM_B1 ** ADAM_STEP)
    v_hat = v / (1.0 - ADAM_B2 ** ADAM_STEP)
    delta = -ADAM_LR * (m_hat / (jnp.sqrt(v_hat) + ADAM_EPS) + ADAM_WD * w)
    return delta, m, v


def adamw_sharded(parts, w, m, v):
    R, C = w.shape
    tr = _row_tile(R, PACK_ROW_TILE)

    def body(p_ref, w_ref, m_ref, v_ref, g_ref, d_ref, nm_ref, nv_ref):
        g = p_ref[0].astype(F32)
        for j in range(1, N_CHIP):
            g = g + p_ref[j].astype(F32)
        g_ref[...] = g
        d_ref[...], nm_ref[...], nv_ref[...] = _adamw(w_ref[...], g, m_ref[...], v_ref[...])

    out = jax.ShapeDtypeStruct((R, C), F32)
    return pl.pallas_call(
        body, name="adamw_sharded", grid=(R // tr,),
        in_specs=[pl.BlockSpec((N_CHIP, tr, C), lambda i: (0, i, 0)), _rows(tr, C), _rows(tr, C), _rows(tr, C)],
        out_specs=[_rows(tr, C)] * 4, out_shape=[out] * 4,
        compiler_params=_params(1),
    )(parts, w, m, v)


def adamw_replicated(gathered, w, m, v):
    R, C = w.shape

    def body(p_ref, w_ref, m_ref, v_ref, g_ref, d_ref, nm_ref, nv_ref):
        g = p_ref[0]
        for j in range(1, N_DEV):
            g = g + p_ref[j]
        g_ref[...] = g
        d_ref[...], nm_ref[...], nv_ref[...] = _adamw(w_ref[...], g, m_ref[...], v_ref[...])

    out = jax.ShapeDtypeStruct((R, C), F32)
    return pl.pallas_call(
        body, name="adamw_replicated", grid=(1,),
        in_specs=[_whole(gathered.shape), _whole((R, C)), _whole((R, C)), _whole((R, C))],
        out_specs=[_whole((R, C))] * 4, out_shape=[out] * 4,
        compiler_params=_params(1),
    )(gathered, w, m, v)


def _pack_rows(flat, rows_multiple):
    n = flat.shape[-1]
    per = PACK_COLS * rows_multiple
    padded = -(-n // per) * per
    flat = jnp.pad(flat, [(0, 0)] * (flat.ndim - 1) + [(0, padded - n)])
    return flat.reshape(flat.shape[:-1] + (padded // PACK_COLS, PACK_COLS))


def _pack(arrays, rows_multiple, dtype=None):
    flat = jnp.concatenate([a.reshape(-1) if dtype is None else a.reshape(-1).astype(dtype) for a in arrays])
    return _pack_rows(flat, rows_multiple)


def _unpack(packed, shapes):
    flat = packed.reshape(-1)
    out, off = [], 0
    for shp in shapes:
        n = int(np.prod(shp))
        out.append(flat[off:off + n].reshape(shp))
        off += n
    return out


def _full_from_gathered(gathered, shard_shapes, axes):
    flat = gathered.reshape(N_DEV, -1)
    out, off = [], 0
    for shp, ax in zip(shard_shapes, axes):
        n = int(np.prod(shp))
        seg = jnp.moveaxis(flat[:, off:off + n].reshape((N_DEV,) + tuple(shp)), 0, ax)
        out.append(seg.reshape(tuple(shp[:ax]) + (N_DEV * shp[ax],) + tuple(shp[ax + 1:])))
        off += n
    return out


def _chunks_from_full(full, ax):
    shp = full.shape
    split = full.reshape(shp[:ax] + (N_DEV, shp[ax] // N_DEV) + shp[ax + 1:])
    return jnp.moveaxis(split, ax, 0).reshape(N_DEV, -1)


def kernel(x, p, mix_norm, conv_w_pw1, conv_b_pw1, conv_w_dw, conv_b_dw, conv_ln_g, conv_ln_b, conv_w_pw2, conv_b_pw2, kv_norm, w_kvf, b_f, attn_w_q, attn_w_o, ffn_norm, ffn_w1, ffn_w2, ple_norm, ple_w_gate, ple_w_proj, final_norm, loss_target, m_mix_norm, m_conv_w_pw1, m_conv_b_pw1, m_conv_w_dw, m_conv_b_dw, m_conv_ln_g, m_conv_ln_b, m_conv_w_pw2, m_conv_b_pw2, m_kv_norm, m_w_kvf, m_b_f, m_attn_w_q, m_attn_w_o, m_ffn_norm, m_ffn_w1, m_ffn_w2, m_ple_norm, m_ple_w_gate, m_ple_w_proj, m_final_norm, v_mix_norm, v_conv_w_pw1, v_conv_b_pw1, v_conv_w_dw, v_conv_b_dw, v_conv_ln_g, v_conv_ln_b, v_conv_w_pw2, v_conv_b_pw2, v_kv_norm, v_w_kvf, v_b_f, v_attn_w_q, v_attn_w_o, v_ffn_norm, v_ffn_w1, v_ffn_w2, v_ple_norm, v_ple_w_gate, v_ple_w_proj, v_final_norm):
    given = dict(locals())
    W = {n: given[n] for n in WEIGHTS}
    M = {n: given["m_" + n] for n in WEIGHTS}
    V = {n: given["v_" + n] for n in WEIGHTS}

    _, S, D = x.shape
    NA = conv_w_pw1.shape[0]
    NB = attn_w_q.shape[0]
    DEPTH = NA + NB
    H = b_f.shape[0]
    dh = D // H
    hg = 128 // dh
    G = D // 128
    scale = dh ** -0.5
    tm = _row_tile(S, 256)
    tq_f = _row_tile(S, FLASH_FWD_TILE[0])
    tkc_f = _row_tile(tq_f, FLASH_FWD_TILE[1])
    tk_b = _row_tile(S, FLASH_BWD_TILE[0])
    tqc_b = _row_tile(tk_b, FLASH_BWD_TILE[1])
    ts = _row_tile(S, 512)
    xs = x[0]
    tgt = loss_target[0]
    ps = p[:, 0]
    row = lambda a: a.reshape(1, -1)

    big_names = list(SHARD_AXIS_BIG)
    small_names = list(SHARD_AXIS_SMALL)
    big = _full_from_gathered(
        all_gather(_pack([W[n] for n in big_names], 16, MXU_DTYPE), "weights_all_gather"),
        [W[n].shape for n in big_names], [SHARD_AXIS_BIG[n] for n in big_names])
    small = _full_from_gathered(
        all_gather(_pack([W[n] for n in small_names], 8), "vectors_all_gather"),
        [W[n].shape for n in small_names], [SHARD_AXIS_SMALL[n] for n in small_names])
    FW = dict(zip(big_names + small_names, big + small))
    wk, wv, wf = FW["w_kvf"][:, :D], FW["w_kvf"][:, D:2 * D], FW["w_kvf"][:, 2 * D:]

    saved = []
    h = xs
    kv = None
    for i in range(DEPTH):
        rec = {"h_in": h}
        if i < NA:
            h, rec["n"], rec["u"], rec["z"], rec["sw"] = conv_fwd(
                h, row(mix_norm[i]), FW["conv_w_pw1"][i], row(FW["conv_b_pw1"][i]), FW["conv_w_dw"][i],
                row(FW["conv_b_dw"][i]), row(FW["conv_ln_g"][i]), row(FW["conv_ln_b"][i]),
                FW["conv_w_pw2"][i], row(FW["conv_b_pw2"][i]), tm)
        else:
            j = i - NA
            if j == 0:
                k_, kT_, vT_, nkv, fl, c = kv_fwd(h, row(kv_norm), wk, wv, wf, row(b_f), tm)
                cg = c.reshape(S, G, hg)
                kv = dict(k=k_, kT=kT_, vT=vT_, n=nkv, fl=fl, h=h, c_col=jnp.transpose(cg, (1, 0, 2)),
                          c_row=jnp.transpose(cg, (1, 2, 0)))
            rec["n"], rec["q"], rec["qT"] = q_fwd(h, row(mix_norm[i]), FW["attn_w_q"][j], scale, tm)
            rec["o"], rec["o32"], rec["lse"] = flash_fwd(rec["qT"], kv["k"], kv["vT"], kv["c_col"], kv["c_row"], dh,
                                                         tq_f, tkc_f)
            h = attn_out_fwd(h, rec["o"], FW["attn_w_o"][j], tm)
        rec["h_ffn"] = h
        h, rec["n_ffn"], rec["a"], rec["s"] = ffn_fwd(h, row(ffn_norm[i]), FW["ffn_w1"][i], FW["ffn_w2"][i], tm)
        rec["h_ple"] = h
        h, rec["n_ple"], rec["gate"] = ple_fwd(h, row(ple_norm[i]), FW["ple_w_gate"][i], ps[i],
                                               FW["ple_w_proj"][i], tm)
        saved.append(rec)

    d, g_final, loss_part = loss_head(h, row(final_norm), tgt, tm)
    GW = {n: [None] * W[n].shape[0] for n in WEIGHTS if W[n].ndim > 1 and n != "w_kvf"}
    dks, dvs, dcs = [], [], []
    for i in reversed(range(DEPTH)):
        rec = saved[i]
        d_out = d
        d, dz, dpp, GW["ple_norm"][i] = ple_bwd(d_out, rec["h_ple"], row(ple_norm[i]), FW["ple_w_gate"][i],
                                                rec["gate"], ps[i], FW["ple_w_proj"][i], tm)
        GW["ple_w_gate"][i] = weight_grad(rec["n_ple"], dz, ts, "grad_ple_w_gate")
        GW["ple_w_proj"][i] = weight_grad(ps[i], dpp, ts, "grad_ple_w_proj")
        d_out = d
        d, da, GW["ffn_norm"][i] = ffn_bwd(d_out, rec["h_ffn"], row(ffn_norm[i]), FW["ffn_w1"][i],
                                           FW["ffn_w2"][i], rec["a"], tm)
        GW["ffn_w2"][i] = weight_grad(rec["s"], d_out, ts, "grad_ffn_w2")
        GW["ffn_w1"][i] = weight_grad(rec["n_ffn"], da, ts, "grad_ffn_w1")
        d_out = d
        if i >= NA:
            j = i - NA
            GW["attn_w_o"][j] = weight_grad(rec["o"], d_out, ts, "grad_attn_w_o")
            do, doT, delta = attn_out_bwd(d_out, FW["attn_w_o"][j], rec["o32"], dh, tm)
            qstat = jnp.concatenate([kv["c_col"], jnp.transpose(rec["lse"], (0, 2, 1)),
                                     jnp.transpose(delta.reshape(S, G, hg), (1, 0, 2))], axis=2)
            dq, dkT, dvT, dck, dcq = flash_bwd(rec["q"], rec["qT"], kv["k"], kv["kT"], kv["vT"], kv["c_row"], do, doT,
                                          qstat, dh, tk_b, tqc_b)
            dks.append(dkT)
            dvs.append(dvT)
            dcs.append(jnp.transpose(dck[:, :hg, :], (2, 0, 1)).reshape(S, H))
            dcs.append(jnp.transpose(dcq, (1, 0, 2)).reshape(S, H))
            d, dqs, GW["mix_norm"][i] = q_bwd(d_out, dq, rec["h_in"], row(mix_norm[i]), FW["attn_w_q"][j], scale, tm)
            GW["attn_w_q"][j] = weight_grad(rec["n"], dqs, ts, "grad_attn_w_q")
            if j == 0:
                d, dk_sum, dv_sum, dfl, g_kv_norm, g_b_f = kv_bwd(d, dks, dvs, dcs, kv["fl"], kv["h"],
                                                                  row(kv_norm), wk, wv, wf, tm)
                g_w_kvf = jnp.concatenate([weight_grad(kv["n"], dk_sum, ts, "grad_w_k"),
                                           weight_grad(kv["n"], dv_sum, ts, "grad_w_v"),
                                           weight_grad(kv["n"], dfl, ts, "grad_w_f")], axis=1)
        else:
            GW["conv_w_pw2"][i] = weight_grad(rec["sw"], d_out, ts, "grad_conv_w_pw2")
            (d, du, GW["conv_b_pw2"][i], GW["conv_ln_g"][i], GW["conv_ln_b"][i], GW["conv_b_dw"][i],
             GW["conv_w_dw"][i], GW["conv_b_pw1"][i], GW["mix_norm"][i]) = conv_bwd(
                d_out, rec["h_in"], row(mix_norm[i]), FW["conv_w_pw1"][i], FW["conv_w_dw"][i],
                row(FW["conv_ln_g"][i]), row(FW["conv_ln_b"][i]), FW["conv_w_pw2"][i], rec["u"], rec["z"], tm)
            GW["conv_w_pw1"][i] = weight_grad(rec["n"], du, ts, "grad_conv_w_pw1")
    grad_x = d[None]

    def stacked(n):
        return jnp.concatenate(GW[n], axis=0) if W[n].ndim == 2 else jnp.stack(GW[n])

    full_grads = {n: stacked(n) for n in GW if n not in REPLICATED}
    full_grads["w_kvf"] = g_w_kvf

    sharded_names = big_names + small_names
    axes = {**SHARD_AXIS_BIG, **SHARD_AXIS_SMALL}
    chunks = jnp.concatenate([_chunks_from_full(full_grads[n], axes[n]) for n in sharded_names], axis=1)
    chunks = _pack_rows(chunks, PACK_ROW_TILE)
    core = lax.axis_index("c").astype(jnp.int32).reshape(1)
    parts = chip_exchange(chip_partial(chunks, sibling_exchange(chunks), core))

    pack_sh = lambda src: _pack([src[n] for n in sharded_names], PACK_ROW_TILE)
    outs_sh = adamw_sharded(parts, pack_sh(W), pack_sh(M), pack_sh(V))
    shard_shapes = [W[n].shape for n in sharded_names]
    res = {}
    for kind, packed in zip(("grad", "delta", "new_m", "new_v"), outs_sh):
        for n, a in zip(sharded_names, _unpack(packed, shard_shapes)):
            res[kind, n] = a

    rep_grads = {"mix_norm": jnp.concatenate(GW["mix_norm"], axis=0), "kv_norm": g_kv_norm,
                 "b_f": g_b_f, "ffn_norm": jnp.concatenate(GW["ffn_norm"], axis=0),
                 "ple_norm": jnp.concatenate(GW["ple_norm"], axis=0), "final_norm": g_final}

    def pack_rep(src, extra=None):
        rows_ = [jnp.pad(src[n].reshape(-1, src[n].shape[-1]), ((0, 0), (0, D - src[n].shape[-1])))
                 for n in REPLICATED]
        if extra is not None:
            rows_.append(jnp.pad(extra, ((0, 0), (0, D - extra.shape[-1]))))
        else:
            rows_.append(jnp.zeros((1, D), F32))
        flat = jnp.concatenate(rows_, axis=0)
        return jnp.pad(flat, ((0, -flat.shape[0] % 8), (0, 0)))

    rep_g = all_gather(pack_rep(rep_grads, loss_part), "replicated_all_gather")
    outs_rep = adamw_replicated(rep_g, pack_rep(W), pack_rep(M), pack_rep(V))
    n_rep_rows = sum(int(np.prod(W[n].shape[:-1])) for n in REPLICATED)
    for kind, packed in zip(("grad", "delta", "new_m", "new_v"), outs_rep):
        r0 = 0
        for n in REPLICATED:
            nr = int(np.prod(W[n].shape[:-1]))
            res[kind, n] = packed[r0:r0 + nr, :W[n].shape[-1]].reshape(W[n].shape)
            r0 += nr
    loss = outs_rep[0][n_rep_rows, 0]

    return (loss, grad_x, *[res["grad", n] for n in WEIGHTS], *[res["delta", n] for n in WEIGHTS],
            *[res["new_m", n] for n in WEIGHTS], *[res["new_v", n] for n in WEIGHTS])
```

```python
import numpy as np
import jax
import jax.numpy as jnp
from jax import lax
from jax.experimental import pallas as pl
from jax.experimental.pallas import tpu as pltpu

F32 = jnp.float32
MXU_DTYPE = jnp.bfloat16
ACT_DTYPE = jnp.bfloat16
WIRE_DTYPE = jnp.bfloat16

N_DEV = 8
N_CHIP = 4
EPS = 1e-6
NEG_BIG = -1e30
ADAM_LR = 0.001
ADAM_B1 = 0.9
ADAM_B2 = 0.999
ADAM_EPS = 1e-08
ADAM_WD = 0.01
ADAM_STEP = 10

VMEM_LIMIT_BYTES = 56 * 1024 * 1024
PACK_COLS = 1024
PACK_ROW_TILE = 256
FLASH_FWD_TILE = (1024, 512)
FLASH_BWD_TILE = (1024, 512)
FLASH_TILE = 128
HALO = 32
MESH = pl.DeviceIdType.MESH

SHARD_AXIS_BIG = {"conv_w_pw1": 2, "conv_w_pw2": 1, "w_kvf": 1, "attn_w_q": 1, "attn_w_o": 1,
                  "ffn_w1": 2, "ffn_w2": 1, "ple_w_gate": 1, "ple_w_proj": 2}
SHARD_AXIS_SMALL = {"conv_b_pw1": 1, "conv_w_dw": 2, "conv_b_dw": 1, "conv_ln_g": 1, "conv_ln_b": 1,
                    "conv_b_pw2": 1}
REPLICATED = ["mix_norm", "kv_norm", "b_f", "ffn_norm", "ple_norm", "final_norm"]
WEIGHTS = ["mix_norm", "conv_w_pw1", "conv_b_pw1", "conv_w_dw", "conv_b_dw", "conv_ln_g", "conv_ln_b",
           "conv_w_pw2", "conv_b_pw2", "kv_norm", "w_kvf", "b_f", "attn_w_q", "attn_w_o", "ffn_norm",
           "ffn_w1", "ffn_w2", "ple_norm", "ple_w_gate", "ple_w_proj", "final_norm"]


def _mm(a, b):
    return jnp.dot(a.astype(MXU_DTYPE), b.astype(MXU_DTYPE), preferred_element_type=F32)


def _mm_nt(a, b):
    return lax.dot_general(a.astype(MXU_DTYPE), b.astype(MXU_DTYPE), (((1,), (1,)), ((), ())),
                           preferred_element_type=F32)


def _mm_tn(a, b):
    return lax.dot_general(a.astype(MXU_DTYPE), b.astype(MXU_DTYPE), (((0,), (0,)), ((), ())),
                           preferred_element_type=F32)


def _split3(x):
    hi = x.astype(MXU_DTYPE)
    r1 = x - hi.astype(F32)
    mid = r1.astype(MXU_DTYPE)
    lo = (r1 - mid.astype(F32)).astype(MXU_DTYPE)
    return hi, mid, lo


def _tri_mm(tri, x):
    hi, mid, lo = _split3(x)
    return (jnp.dot(tri, lo, preferred_element_type=F32) + jnp.dot(tri, mid, preferred_element_type=F32)
            + jnp.dot(tri, hi, preferred_element_type=F32))


def _colsum8(x):
    tm, n = x.shape
    return jnp.sum(x.reshape(tm // 8, 8, n), axis=0)


def _rms(x, g):
    r = lax.rsqrt(jnp.mean(x * x, axis=-1, keepdims=True) + EPS)
    return x * r * g, r


def _rms_bwd(x, r, g, dn):
    w = dn * g
    dx = r * w - x * (r * r * r) * jnp.mean(w * x, axis=-1, keepdims=True)
    return dx, dn * x * r


def _sigmoid(x):
    return jax.nn.sigmoid(x)


def _params(n_grid):
    return pltpu.CompilerParams(dimension_semantics=("arbitrary",) * n_grid, vmem_limit_bytes=VMEM_LIMIT_BYTES)


def _rows(tm, n):
    return pl.BlockSpec((tm, n), lambda i: (i, 0))


def _rows_rev(tm, n, nt):
    return pl.BlockSpec((tm, n), lambda i: (nt - 1 - i, 0))


def _cols(n, tm):
    return pl.BlockSpec((n, tm), lambda i: (0, i))


def _cols_rev(n, tm, nt):
    return pl.BlockSpec((n, tm), lambda i: (0, nt - 1 - i))


def _whole(shape):
    nd = len(shape)
    return pl.BlockSpec(shape, lambda i: (0,) * nd)


def _row_tile(s, want):
    tm = min(s, want)
    assert s % tm == 0 and tm % 8 == 0, (s, tm)
    return tm


def conv_fwd(h, g, w1, b1, wd, bd, lg, lb, w2, b2, tm):
    S, D = h.shape
    CW = wd.shape[0]
    off = HALO - (CW - 1)
    assert 0 <= off and tm >= HALO
    nt = S // tm

    def body(h_ref, g_ref, w1_ref, b1_ref, wd_ref, bd_ref, lg_ref, lb_ref, w2_ref, b2_ref,
             ho_ref, n_ref, u_ref, z_ref, sw_ref, ext, win):
        @pl.when(pl.program_id(0) == 0)
        def _():
            ext[0:HALO, :] = jnp.zeros((HALO, D), F32)

        x = h_ref[...]
        n, _ = _rms(x, g_ref[...])
        n_ref[...] = n.astype(n_ref.dtype)
        u = _mm(n, w1_ref[...]) + b1_ref[...]
        u_ref[...] = u
        ext[HALO:HALO + tm, :] = u[:, :D] * _sigmoid(u[:, D:])
        z = jnp.broadcast_to(bd_ref[...], (tm, D))
        for b in range(8):
            amax = (CW - 1 - b) // 8
            win[0:tm + 8 * amax, :] = ext[off + b:off + b + tm + 8 * amax, :]
            for a8 in range(amax + 1):
                z = z + wd_ref[8 * a8 + b:8 * a8 + b + 1, :] * win[8 * a8:8 * a8 + tm, :]
        z_ref[...] = z
        ext[0:HALO, :] = ext[tm:tm + HALO, :]
        mu = jnp.mean(z, axis=-1, keepdims=True)
        zc = z - mu
        y = zc * lax.rsqrt(jnp.mean(zc * zc, axis=-1, keepdims=True) + EPS) * lg_ref[...] + lb_ref[...]
        sw = y * _sigmoid(y)
        sw_ref[...] = sw.astype(sw_ref.dtype)
        ho_ref[...] = x + _mm(sw, w2_ref[...]) + b2_ref[...]

    return pl.pallas_call(
        body, name="conv_fwd", grid=(nt,),
        in_specs=[_rows(tm, D), _whole((1, D)), _whole(w1.shape), _whole((1, 2 * D)), _whole(wd.shape),
                  _whole((1, D)), _whole((1, D)), _whole((1, D)), _whole(w2.shape), _whole((1, D))],
        out_specs=[_rows(tm, D), _rows(tm, D), _rows(tm, 2 * D), _rows(tm, D), _rows(tm, D)],
        out_shape=[jax.ShapeDtypeStruct((S, D), F32), jax.ShapeDtypeStruct((S, D), ACT_DTYPE),
                   jax.ShapeDtypeStruct((S, 2 * D), F32), jax.ShapeDtypeStruct((S, D), F32),
                   jax.ShapeDtypeStruct((S, D), ACT_DTYPE)],
        scratch_shapes=[pltpu.VMEM((HALO + tm, D), F32), pltpu.VMEM((HALO + tm, D), F32)],
        compiler_params=_params(1),
    )(h, g, w1, b1, wd, bd, lg, lb, w2, b2)


def ffn_fwd(h, g, w1, w2, tm):
    S, D = h.shape
    FF = w1.shape[1]

    def body(h_ref, g_ref, w1_ref, w2_ref, ho_ref, n_ref, a_ref, s_ref):
        x = h_ref[...]
        n, _ = _rms(x, g_ref[...])
        n_ref[...] = n.astype(n_ref.dtype)
        a = _mm(n, w1_ref[...])
        a_ref[...] = a
        s = jnp.square(jnp.maximum(a, 0.0))
        s_ref[...] = s.astype(s_ref.dtype)
        ho_ref[...] = x + _mm(s, w2_ref[...])

    return pl.pallas_call(
        body, name="ffn_fwd", grid=(S // tm,),
        in_specs=[_rows(tm, D), _whole((1, D)), _whole(w1.shape), _whole(w2.shape)],
        out_specs=[_rows(tm, D), _rows(tm, D), _rows(tm, FF), _rows(tm, FF)],
        out_shape=[jax.ShapeDtypeStruct((S, D), F32), jax.ShapeDtypeStruct((S, D), ACT_DTYPE),
                   jax.ShapeDtypeStruct((S, FF), F32), jax.ShapeDtypeStruct((S, FF), ACT_DTYPE)],
        compiler_params=_params(1),
    )(h, g, w1, w2)


def ple_fwd(h, g, wg, p, wp, tm):
    S, D = h.shape
    E = p.shape[1]

    def body(h_ref, g_ref, wg_ref, p_ref, wp_ref, ho_ref, n_ref, gate_ref):
        x = h_ref[...]
        n, _ = _rms(x, g_ref[...])
        n_ref[...] = n.astype(n_ref.dtype)
        gate = _sigmoid(_mm(n, wg_ref[...]))
        gate_ref[...] = gate
        ho_ref[...] = x + gate * _mm(p_ref[...], wp_ref[...])

    return pl.pallas_call(
        body, name="ple_fwd", grid=(S // tm,),
        in_specs=[_rows(tm, D), _whole((1, D)), _whole(wg.shape), _rows(tm, E), _whole(wp.shape)],
        out_specs=[_rows(tm, D), _rows(tm, D), _rows(tm, D)],
        out_shape=[jax.ShapeDtypeStruct((S, D), F32), jax.ShapeDtypeStruct((S, D), ACT_DTYPE),
                   jax.ShapeDtypeStruct((S, D), F32)],
        compiler_params=_params(1),
    )(h, g, wg, p, wp)


def kv_fwd(h, g, wk, wv, wf, bf, tm):
    S, D = h.shape
    H = wf.shape[1]

    def body(h_ref, g_ref, wk_ref, wv_ref, wf_ref, bf_ref, k_ref, kT_ref, vT_ref, n_ref, fl_ref, c_ref, carry):
        @pl.when(pl.program_id(0) == 0)
        def _():
            carry[...] = jnp.zeros_like(carry)

        n, _ = _rms(h_ref[...], g_ref[...])
        n_ref[...] = n.astype(n_ref.dtype)
        k = _mm(n, wk_ref[...])
        k_ref[...] = k.astype(k_ref.dtype)
        kT_ref[...] = k.T.astype(kT_ref.dtype)
        vT_ref[...] = _mm(n, wv_ref[...]).T.astype(vT_ref.dtype)
        fl = _mm(n, wf_ref[...]) + bf_ref[...]
        fl_ref[...] = fl
        logf = jnp.minimum(fl, 0.0) - jnp.log1p(jnp.exp(-jnp.abs(fl)))
        row = lax.broadcasted_iota(jnp.int32, (tm, tm), 0)
        col = lax.broadcasted_iota(jnp.int32, (tm, tm), 1)
        tri = (row >= col).astype(MXU_DTYPE)
        c = _tri_mm(tri, logf) + carry[...]
        c_ref[...] = c
        carry[...] = c[tm - 1:tm, :]

    return pl.pallas_call(
        body, name="kv_fwd", grid=(S // tm,),
        in_specs=[_rows(tm, D), _whole((1, D)), _whole(wk.shape), _whole(wv.shape), _whole(wf.shape),
                  _whole((1, H))],
        out_specs=[_rows(tm, D), _cols(D, tm), _cols(D, tm), _rows(tm, D), _rows(tm, H), _rows(tm, H)],
        out_shape=[jax.ShapeDtypeStruct((S, D), ACT_DTYPE), jax.ShapeDtypeStruct((D, S), ACT_DTYPE),
                   jax.ShapeDtypeStruct((D, S), ACT_DTYPE), jax.ShapeDtypeStruct((S, D), ACT_DTYPE),
                   jax.ShapeDtypeStruct((S, H), F32), jax.ShapeDtypeStruct((S, H), F32)],
        scratch_shapes=[pltpu.VMEM((1, H), F32)],
        compiler_params=_params(1),
    )(h, g, wk, wv, wf, bf)


def q_fwd(h, g, wq, scale, tm):
    S, D = h.shape

    def body(h_ref, g_ref, wq_ref, n_ref, q_ref, qT_ref):
        n, _ = _rms(h_ref[...], g_ref[...])
        n_ref[...] = n.astype(n_ref.dtype)
        q = _mm(n, wq_ref[...]) * scale
        q_ref[...] = q.astype(q_ref.dtype)
        qT_ref[...] = q.T.astype(qT_ref.dtype)

    return pl.pallas_call(
        body, name="q_fwd", grid=(S // tm,),
        in_specs=[_rows(tm, D), _whole((1, D)), _whole(wq.shape)],
        out_specs=[_rows(tm, D), _rows(tm, D), _cols(D, tm)],
        out_shape=[jax.ShapeDtypeStruct((S, D), ACT_DTYPE), jax.ShapeDtypeStruct((S, D), ACT_DTYPE),
                   jax.ShapeDtypeStruct((D, S), ACT_DTYPE)],
        compiler_params=_params(1),
    )(h, g, wq)


def attn_out_fwd(h, o, wo, tm):
    S, D = h.shape

    def body(h_ref, o_ref, wo_ref, ho_ref):
        ho_ref[...] = h_ref[...] + _mm(o_ref[...], wo_ref[...])

    return pl.pallas_call(
        body, name="attn_out_fwd", grid=(S // tm,),
        in_specs=[_rows(tm, D), _rows(tm, D), _whole(wo.shape)],
        out_specs=_rows(tm, D),
        out_shape=jax.ShapeDtypeStruct((S, D), F32),
        compiler_params=_params(1),
    )(h, o, wo)


def _causal_mask(key0, qry0, shape, key_axis):
    key = key0 + lax.broadcasted_iota(jnp.int32, shape, key_axis)
    qry = qry0 + lax.broadcasted_iota(jnp.int32, shape, 1 - key_axis)
    return key <= qry


def flash_fwd(qT, k, vT, c_col, c_row, dh, tq, tkc):
    D, S = qT.shape
    hg = 128 // dh
    G = D // 128
    per = tq // tkc
    assert tq % tkc == 0 and S % tq == 0

    def body(qT_ref, k_ref, vT_ref, ccol_ref, crow_ref, o_ref, o32_ref, lse_ref, m_scr, l_scr, acc_scr, mx_scr,
             *scratch):
        i = pl.program_id(1)
        m_scr[...] = jnp.full(m_scr.shape, NEG_BIG, F32)
        l_scr[...] = jnp.zeros(l_scr.shape, F32)
        acc_scr[...] = jnp.zeros(acc_scr.shape, F32)

        def head_shift(j, hh):
            c0 = ccol_ref[pl.ds(j * tkc, 1), hh:hh + 1]
            return c0, crow_ref[hh:hh + 1, :] - c0

        def scores(j, masked, slot):
            keys = pl.ds(pl.multiple_of(j * tkc, tkc), tkc)
            s_scr = scratch[slot]
            for hh in range(hg):
                lanes = slice(hh * dh, (hh + 1) * dh)
                c0, r = head_shift(j, hh)
                s = _mm(k_ref[keys, lanes], qT_ref[lanes, :]) - (ccol_ref[keys, hh:hh + 1] - c0)
                if masked:
                    s = jnp.where(_causal_mask(j * tkc, i * tq, (tkc, tq), 0), s, NEG_BIG)
                s_scr[hh] = s
                mx_scr[slot * hg + hh] = jnp.max(s, axis=0, keepdims=True) + r

        def update(j, slot):
            keys = pl.ds(pl.multiple_of(j * tkc, tkc), tkc)
            s_scr = scratch[slot]
            for hh in range(hg):
                lanes = slice(hh * dh, (hh + 1) * dh)
                _, r = head_shift(j, hh)
                m_old = m_scr[hh]
                m_new = jnp.maximum(m_old, mx_scr[slot * hg + hh])
                alpha = jnp.exp(m_old - m_new)
                p = jnp.exp(s_scr[hh] - (m_new - r))
                l_scr[hh] = alpha * l_scr[hh] + jnp.sum(p, axis=0, keepdims=True)
                acc_scr[lanes, :] = alpha * acc_scr[lanes, :] + _mm(vT_ref[lanes, keys], p)
                m_scr[hh] = m_new

        def chunks(j0, masked):
            for jj in range(per):
                scores(j0 + jj, masked, jj)
            for jj in range(per):
                update(j0 + jj, jj)

        def full_chunks(jb, carry):
            chunks(jb * per, False)
            return carry

        lax.fori_loop(0, i, full_chunks, 0)
        chunks(i * per, True)
        for hh in range(hg):
            lanes = slice(hh * dh, (hh + 1) * dh)
            acc_scr[lanes, :] = acc_scr[lanes, :] / l_scr[hh]
            lse_ref[hh:hh + 1, :] = m_scr[hh] + jnp.log(l_scr[hh])
        o = acc_scr[...].T
        o_ref[...] = o.astype(o_ref.dtype)
        o32_ref[...] = o

    return pl.pallas_call(
        body, name="flash_fwd", grid=(G, S // tq),
        in_specs=[pl.BlockSpec((128, tq), lambda g, i: (g, i)),
                  pl.BlockSpec((S, 128), lambda g, i: (0, g)),
                  pl.BlockSpec((128, S), lambda g, i: (g, 0)),
                  pl.BlockSpec((None, S, hg), lambda g, i: (g, 0, 0)),
                  pl.BlockSpec((None, hg, tq), lambda g, i: (g, 0, i))],
        out_specs=[pl.BlockSpec((tq, 128), lambda g, i: (i, g)),
                   pl.BlockSpec((tq, 128), lambda g, i: (i, g)),
                   pl.BlockSpec((None, hg, tq), lambda g, i: (g, 0, i))],
        out_shape=[jax.ShapeDtypeStruct((S, D), ACT_DTYPE), jax.ShapeDtypeStruct((S, D), F32),
                   jax.ShapeDtypeStruct((G, hg, S), F32)],
        scratch_shapes=([pltpu.VMEM((hg, 1, tq), F32), pltpu.VMEM((hg, 1, tq), F32), pltpu.VMEM((128, tq), F32),
                         pltpu.VMEM((per * hg, 1, tq), F32)]
                        + [pltpu.VMEM((hg, tkc, tq), F32)] * per),
        compiler_params=_params(2),
    )(qT, k, vT, c_col, c_row)


def loss_head(h, g, target, tm):
    S, D = h.shape
    nt = S // tm

    def body(h_ref, g_ref, t_ref, dh_ref, dg_ref, loss_ref, dg_acc, loss_acc):
        i = pl.program_id(0)

        @pl.when(i == 0)
        def _():
            dg_acc[...] = jnp.zeros_like(dg_acc)
            loss_acc[...] = jnp.zeros_like(loss_acc)

        x = h_ref[...]
        gg = g_ref[...]
        y, r = _rms(x, gg)
        e = y - t_ref[...]
        loss_acc[...] += 0.5 * jnp.sum(jnp.mean(e * e, axis=-1, keepdims=True), axis=0, keepdims=True)
        dx, dgr = _rms_bwd(x, r, gg, e / D)
        dh_ref[...] = dx
        dg_acc[...] += _colsum8(dgr)

        @pl.when(i == nt - 1)
        def _():
            dg_ref[...] = jnp.sum(dg_acc[...], axis=0, keepdims=True)
            loss_ref[...] = jnp.broadcast_to(loss_acc[...], loss_ref.shape)

    return pl.pallas_call(
        body, name="loss_head", grid=(nt,),
        in_specs=[_rows(tm, D), _whole((1, D)), _rows(tm, D)],
        out_specs=[_rows(tm, D), _whole((1, D)), _whole((1, 128))],
        out_shape=[jax.ShapeDtypeStruct((S, D), F32), jax.ShapeDtypeStruct((1, D), F32),
                   jax.ShapeDtypeStruct((1, 128), F32)],
        scratch_shapes=[pltpu.VMEM((8, D), F32), pltpu.VMEM((1, 1), F32)],
        compiler_params=_params(1),
    )(h, g, target)


def ple_bwd(d, h, g, wg, gate, p, wp, tm):
    S, D = h.shape
    E = p.shape[1]
    nt = S // tm

    def body(d_ref, h_ref, g_ref, wg_ref, gate_ref, p_ref, wp_ref, di_ref, dz_ref, dpp_ref, dg_ref, dg_acc):
        i = pl.program_id(0)

        @pl.when(i == 0)
        def _():
            dg_acc[...] = jnp.zeros_like(dg_acc)

        dd = d_ref[...]
        x = h_ref[...]
        gg = g_ref[...]
        gt = gate_ref[...]
        pp = _mm(p_ref[...], wp_ref[...])
        dpp_ref[...] = (dd * gt).astype(dpp_ref.dtype)
        dz = dd * pp * gt * (1.0 - gt)
        dz_ref[...] = dz.astype(dz_ref.dtype)
        r = lax.rsqrt(jnp.mean(x * x, axis=-1, keepdims=True) + EPS)
        dx, dgr = _rms_bwd(x, r, gg, _mm_nt(dz, wg_ref[...]))
        di_ref[...] = dd + dx
        dg_acc[...] += _colsum8(dgr)

        @pl.when(i == nt - 1)
        def _():
            dg_ref[...] = jnp.sum(dg_acc[...], axis=0, keepdims=True)

    return pl.pallas_call(
        body, name="ple_bwd", grid=(nt,),
        in_specs=[_rows(tm, D), _rows(tm, D), _whole((1, D)), _whole(wg.shape), _rows(tm, D), _rows(tm, E),
                  _whole(wp.shape)],
        out_specs=[_rows(tm, D), _rows(tm, D), _rows(tm, D), _whole((1, D))],
        out_shape=[jax.ShapeDtypeStruct((S, D), F32), jax.ShapeDtypeStruct((S, D), ACT_DTYPE),
                   jax.ShapeDtypeStruct((S, D), ACT_DTYPE), jax.ShapeDtypeStruct((1, D), F32)],
        scratch_shapes=[pltpu.VMEM((8, D), F32)],
        compiler_params=_params(1),
    )(d, h, g, wg, gate, p, wp)


def ffn_bwd(d, h, g, w1, w2, a, tm):
    S, D = h.shape
    FF = w1.shape[1]
    nt = S // tm

    def body(d_ref, h_ref, g_ref, w1_ref, w2_ref, a_ref, di_ref, da_ref, dg_ref, dg_acc):
        i = pl.program_id(0)

        @pl.when(i == 0)
        def _():
            dg_acc[...] = jnp.zeros_like(dg_acc)

        dd = d_ref[...]
        x = h_ref[...]
        da = _mm_nt(dd, w2_ref[...]) * (2.0 * jnp.maximum(a_ref[...], 0.0))
        da_ref[...] = da.astype(da_ref.dtype)
        r = lax.rsqrt(jnp.mean(x * x, axis=-1, keepdims=True) + EPS)
        dx, dgr = _rms_bwd(x, r, g_ref[...], _mm_nt(da, w1_ref[...]))
        di_ref[...] = dd + dx
        dg_acc[...] += _colsum8(dgr)

        @pl.when(i == nt - 1)
        def _():
            dg_ref[...] = jnp.sum(dg_acc[...], axis=0, keepdims=True)

    return pl.pallas_call(
        body, name="ffn_bwd", grid=(nt,),
        in_specs=[_rows(tm, D), _rows(tm, D), _whole((1, D)), _whole(w1.shape), _whole(w2.shape), _rows(tm, FF)],
        out_specs=[_rows(tm, D), _rows(tm, FF), _whole((1, D))],
        out_shape=[jax.ShapeDtypeStruct((S, D), F32), jax.ShapeDtypeStruct((S, FF), ACT_DTYPE),
                   jax.ShapeDtypeStruct((1, D), F32)],
        scratch_shapes=[pltpu.VMEM((8, D), F32)],
        compiler_params=_params(1),
    )(d, h, g, w1, w2, a)


def attn_out_bwd(d, wo, o32, dh, tm):
    S, D = d.shape
    H = D // dh

    def body(d_ref, wo_ref, o_ref, do_ref, doT_ref, delta_ref):
        do32 = _mm_nt(d_ref[...], wo_ref[...])
        do = do32.astype(do_ref.dtype)
        do_ref[...] = do
        doT_ref[...] = do32.T.astype(doT_ref.dtype)
        lane_head = lax.broadcasted_iota(jnp.int32, (D, H), 0) // dh
        seg = (lane_head == lax.broadcasted_iota(jnp.int32, (D, H), 1)).astype(MXU_DTYPE)
        hi, mid, lo = _split3(do.astype(F32) * o_ref[...])
        delta_ref[...] = (jnp.dot(lo, seg, preferred_element_type=F32) + jnp.dot(mid, seg, preferred_element_type=F32)
                          + jnp.dot(hi, seg, preferred_element_type=F32))

    return pl.pallas_call(
        body, name="attn_out_bwd", grid=(S // tm,),
        in_specs=[_rows(tm, D), _whole(wo.shape), _rows(tm, D)],
        out_specs=[_rows(tm, D), _cols(D, tm), _rows(tm, H)],
        out_shape=[jax.ShapeDtypeStruct((S, D), ACT_DTYPE), jax.ShapeDtypeStruct((D, S), ACT_DTYPE),
                   jax.ShapeDtypeStruct((S, H), F32)],
        compiler_params=_params(1),
    )(d, wo, o32)


def flash_bwd(q, qT, k, kT, vT, c_row, do, doT, qstat, dh, tk, tqc):
    S, D = q.shape
    hg = 128 // dh
    G = D // 128
    per = tk // tqc
    nchunk = S // tqc
    assert hg <= 8 and tk % tqc == 0 and S % tk == 0

    def body(q_ref, qT_ref, k_ref, kT_ref, vT_ref, crow_ref, do_ref, doT_ref, st_ref,
             dq_ref, dkT_ref, dvT_ref, dck_ref, dcq_ref):
        ki = pl.program_id(1)

        @pl.when(ki == 0)
        def _():
            dq_ref[...] = jnp.zeros_like(dq_ref)
            dcq_ref[...] = jnp.zeros_like(dcq_ref)

        dck_ref[...] = jnp.zeros_like(dck_ref)
        dkT_ref[...] = jnp.zeros_like(dkT_ref)
        dvT_ref[...] = jnp.zeros_like(dvT_ref)
        def chunk(jq, masked):
            rows = pl.ds(pl.multiple_of(jq * tqc, tqc), tqc)
            st = st_ref[rows, :]
            for hh in range(hg):
                lanes = slice(hh * dh, (hh + 1) * dh)
                kh = k_ref[:, lanes]
                ck = crow_ref[hh:hh + 1, :]
                c0 = ck[:, 0:1]
                u = (st[:, hh:hh + 1] - c0) - st[:, hg + hh:hg + hh + 1]
                s = (_mm(q_ref[rows, lanes], kT_ref[lanes, :]) - (ck - c0)) + u
                if masked:
                    s = jnp.where(_causal_mask(ki * tk, jq * tqc, (tqc, tk), 1), s, NEG_BIG)
                p = jnp.exp(s)
                dvT_ref[lanes, :] += _mm(doT_ref[lanes, rows], p)
                ds = p * (_mm(do_ref[rows, lanes], vT_ref[lanes, :]) - st[:, 2 * hg + hh:2 * hg + hh + 1])
                dkT_ref[lanes, :] += _mm(qT_ref[lanes, rows], ds)
                dck_ref[hh:hh + 1, :] -= jnp.sum(ds, axis=0, keepdims=True)
                dcq_ref[rows, hh:hh + 1] += jnp.sum(ds, axis=1, keepdims=True)
                dq_ref[rows, lanes] += _mm(ds, kh)

        for jj in range(per):
            chunk(ki * per + jj, True)

        def full_chunk(jq, carry):
            chunk(jq, False)
            return carry

        lax.fori_loop((ki + 1) * per, nchunk, full_chunk, 0)

    return pl.pallas_call(
        body, name="flash_bwd", grid=(G, S // tk),
        in_specs=[pl.BlockSpec((S, 128), lambda g, j: (0, g)),
                  pl.BlockSpec((128, S), lambda g, j: (g, 0)),
                  pl.BlockSpec((tk, 128), lambda g, j: (j, g)),
                  pl.BlockSpec((128, tk), lambda g, j: (g, j)),
                  pl.BlockSpec((128, tk), lambda g, j: (g, j)),
                  pl.BlockSpec((None, hg, tk), lambda g, j: (g, 0, j)),
                  pl.BlockSpec((S, 128), lambda g, j: (0, g)),
                  pl.BlockSpec((128, S), lambda g, j: (g, 0)),
                  pl.BlockSpec((None, S, 3 * hg), lambda g, j: (g, 0, 0))],
        out_specs=[pl.BlockSpec((S, 128), lambda g, j: (0, g)),
                   pl.BlockSpec((128, tk), lambda g, j: (g, j)),
                   pl.BlockSpec((128, tk), lambda g, j: (g, j)),
                   pl.BlockSpec((None, 8, tk), lambda g, j: (g, 0, j)),
                   pl.BlockSpec((None, S, hg), lambda g, j: (g, 0, 0))],
        out_shape=[jax.ShapeDtypeStruct((S, D), F32), jax.ShapeDtypeStruct((D, S), F32),
                   jax.ShapeDtypeStruct((D, S), F32), jax.ShapeDtypeStruct((G, 8, S), F32),
                   jax.ShapeDtypeStruct((G, S, hg), F32)],
        compiler_params=_params(2),
    )(q, qT, k, kT, vT, c_row, do, doT, qstat)


def q_bwd(d, dq, h, g, wq, scale, tm):
    S, D = h.shape
    nt = S // tm

    def body(d_ref, dq_ref, h_ref, g_ref, wq_ref, di_ref, dqs_ref, dg_ref, dg_acc):
        i = pl.program_id(0)

        @pl.when(i == 0)
        def _():
            dg_acc[...] = jnp.zeros_like(dg_acc)

        x = h_ref[...]
        dqs = dq_ref[...] * scale
        dqs_ref[...] = dqs.astype(dqs_ref.dtype)
        r = lax.rsqrt(jnp.mean(x * x, axis=-1, keepdims=True) + EPS)
        dx, dgr = _rms_bwd(x, r, g_ref[...], _mm_nt(dqs, wq_ref[...]))
        di_ref[...] = d_ref[...] + dx
        dg_acc[...] += _colsum8(dgr)

        @pl.when(i == nt - 1)
        def _():
            dg_ref[...] = jnp.sum(dg_acc[...], axis=0, keepdims=True)

    return pl.pallas_call(
        body, name="q_bwd", grid=(nt,),
        in_specs=[_rows(tm, D), _rows(tm, D), _rows(tm, D), _whole((1, D)), _whole(wq.shape)],
        out_specs=[_rows(tm, D), _rows(tm, D), _whole((1, D))],
        out_shape=[jax.ShapeDtypeStruct((S, D), F32), jax.ShapeDtypeStruct((S, D), ACT_DTYPE),
                   jax.ShapeDtypeStruct((1, D), F32)],
        scratch_shapes=[pltpu.VMEM((8, D), F32)],
        compiler_params=_params(1),
    )(d, dq, h, g, wq)


def kv_bwd(d, dks, dvs, dcs, fl, h, g, wk, wv, wf, tm):
    S, D = h.shape
    H = wf.shape[1]
    nt = S // tm
    nl = len(dks)
    nc = len(dcs)

    def body(*refs):
        d_ref = refs[0]
        dk_refs = refs[1:1 + nl]
        dv_refs = refs[1 + nl:1 + 2 * nl]
        dc_refs = refs[1 + 2 * nl:1 + 2 * nl + nc]
        (fl_ref, h_ref, g_ref, wk_ref, wv_ref, wf_ref,
         di_ref, dk_ref, dv_ref, dfl_ref, dg_ref, dbf_ref, dg_acc, dbf_acc, carry) = refs[1 + 2 * nl + nc:]
        i = pl.program_id(0)

        @pl.when(i == 0)
        def _():
            dg_acc[...] = jnp.zeros_like(dg_acc)
            dbf_acc[...] = jnp.zeros_like(dbf_acc)
            carry[...] = jnp.zeros_like(carry)

        dkT = dk_refs[0][...]
        dvT = dv_refs[0][...]
        for l in range(1, nl):
            dkT = dkT + dk_refs[l][...]
            dvT = dvT + dv_refs[l][...]
        dk = dkT.T
        dv = dvT.T
        dk_ref[...] = dk.astype(dk_ref.dtype)
        dv_ref[...] = dv.astype(dv_ref.dtype)
        row = lax.broadcasted_iota(jnp.int32, (tm, tm), 0)
        col = lax.broadcasted_iota(jnp.int32, (tm, tm), 1)
        tri = (col >= row).astype(MXU_DTYPE)
        dc = dc_refs[0][...]
        for l in range(1, nc):
            dc = dc + dc_refs[l][...]
        dlogf = _tri_mm(tri, dc) + carry[...]
        carry[...] = dlogf[0:1, :]
        dfl = dlogf * _sigmoid(-fl_ref[...])
        dfl_ref[...] = dfl
        dbf_acc[...] += jnp.sum(dfl, axis=0, keepdims=True)
        x = h_ref[...]
        dn = _mm_nt(dk, wk_ref[...]) + _mm_nt(dv, wv_ref[...]) + _mm_nt(dfl, wf_ref[...])
        r = lax.rsqrt(jnp.mean(x * x, axis=-1, keepdims=True) + EPS)
        dx, dgr = _rms_bwd(x, r, g_ref[...], dn)
        di_ref[...] = d_ref[...] + dx
        dg_acc[...] += _colsum8(dgr)

        @pl.when(i == nt - 1)
        def _():
            dg_ref[...] = jnp.sum(dg_acc[...], axis=0, keepdims=True)
            dbf_ref[...] = dbf_acc[...]

    rev = lambda n: _rows_rev(tm, n, nt)
    return pl.pallas_call(
        body, name="kv_bwd", grid=(nt,),
        in_specs=([rev(D)] + [_cols_rev(D, tm, nt)] * (2 * nl)
                  + [rev(H)] * nc
                  + [rev(H), rev(D), _whole((1, D)), _whole(wk.shape), _whole(wv.shape), _whole(wf.shape)]),
        out_specs=[rev(D), rev(D), rev(D), rev(H), _whole((1, D)), _whole((1, H))],
        out_shape=[jax.ShapeDtypeStruct((S, D), F32), jax.ShapeDtypeStruct((S, D), ACT_DTYPE),
                   jax.ShapeDtypeStruct((S, D), ACT_DTYPE), jax.ShapeDtypeStruct((S, H), F32),
                   jax.ShapeDtypeStruct((1, D), F32), jax.ShapeDtypeStruct((1, H), F32)],
        scratch_shapes=[pltpu.VMEM((8, D), F32), pltpu.VMEM((1, H), F32), pltpu.VMEM((1, H), F32)],
        compiler_params=_params(1),
    )(d, *dks, *dvs, *dcs, fl, h, g, wk, wv, wf)


def conv_bwd(d, h, g, w1, wd, lg, lb, w2, u, z, tm):
    S, D = h.shape
    CW = wd.shape[0]
    nt = S // tm
    assert tm >= HALO and CW - 1 <= HALO

    def body(d_ref, h_ref, g_ref, w1_ref, wd_ref, lg_ref, lb_ref, w2_ref, u_ref, z_ref,
             di_ref, du_ref, db2_ref, dlg_ref, dlb_ref, dbd_ref, dwd_ref, db1_ref, dg_ref,
             ext, win, db2_acc, dlg_acc, dlb_acc, dbd_acc, dwd_acc, db1_acc, dg_acc):
        i = pl.program_id(0)

        @pl.when(i == 0)
        def _():
            ext[tm:tm + HALO, :] = jnp.zeros((HALO, D), F32)
            for acc in (db2_acc, dlg_acc, dlb_acc, dbd_acc, dwd_acc, db1_acc, dg_acc):
                acc[...] = jnp.zeros_like(acc)

        dd = d_ref[...]
        db2_acc[...] += _colsum8(dd)
        dsw = _mm_nt(dd, w2_ref[...])
        zz = z_ref[...]
        zc = zz - jnp.mean(zz, axis=-1, keepdims=True)
        rs = lax.rsqrt(jnp.mean(zc * zc, axis=-1, keepdims=True) + EPS)
        xh = zc * rs
        lgv = lg_ref[...]
        y = xh * lgv + lb_ref[...]
        sg = _sigmoid(y)
        dy = dsw * (sg * (1.0 + y * (1.0 - sg)))
        dlg_acc[...] += _colsum8(dy * xh)
        dlb_acc[...] += _colsum8(dy)
        dxh = dy * lgv
        dz = rs * (dxh - jnp.mean(dxh, axis=-1, keepdims=True) - xh * jnp.mean(dxh * xh, axis=-1, keepdims=True))
        dbd_acc[...] += _colsum8(dz)
        ext[0:tm, :] = dz
        uu = u_ref[...]
        a = uu[:, :D]
        sgg = _sigmoid(uu[:, D:])
        glu = a * sgg
        dglu = jnp.zeros((tm, D), F32)
        for b in range(8):
            amax = (CW - 1 - b) // 8
            win[0:tm + 8 * amax, :] = ext[b:b + tm + 8 * amax, :]
            for a8 in range(amax + 1):
                k = CW - 1 - (8 * a8 + b)
                sh = win[8 * a8:8 * a8 + tm, :]
                dglu = dglu + wd_ref[k:k + 1, :] * sh
                dwd_acc[k] += _colsum8(glu * sh)
        ext[tm:tm + HALO, :] = ext[0:HALO, :]
        da = dglu * sgg
        dgg = dglu * a * sgg * (1.0 - sgg)
        du_ref[:, :D] = da.astype(du_ref.dtype)
        du_ref[:, D:] = dgg.astype(du_ref.dtype)
        db1_acc[:, :D] += _colsum8(da)
        db1_acc[:, D:] += _colsum8(dgg)
        dn = _mm_nt(da, w1_ref[:, :D]) + _mm_nt(dgg, w1_ref[:, D:])
        x = h_ref[...]
        r = lax.rsqrt(jnp.mean(x * x, axis=-1, keepdims=True) + EPS)
        dx, dgr = _rms_bwd(x, r, g_ref[...], dn)
        di_ref[...] = dd + dx
        dg_acc[...] += _colsum8(dgr)

        @pl.when(i == nt - 1)
        def _():
            db2_ref[...] = jnp.sum(db2_acc[...], axis=0, keepdims=True)
            dlg_ref[...] = jnp.sum(dlg_acc[...], axis=0, keepdims=True)
            dlb_ref[...] = jnp.sum(dlb_acc[...], axis=0, keepdims=True)
            dbd_ref[...] = jnp.sum(dbd_acc[...], axis=0, keepdims=True)
            dwd_ref[...] = jnp.sum(dwd_acc[...], axis=1)
            db1_ref[...] = jnp.sum(db1_acc[...], axis=0, keepdims=True)
            dg_ref[...] = jnp.sum(dg_acc[...], axis=0, keepdims=True)

    rev = lambda n: _rows_rev(tm, n, nt)
    vec = jax.ShapeDtypeStruct((1, D), F32)
    return pl.pallas_call(
        body, name="conv_bwd", grid=(nt,),
        in_specs=[rev(D), rev(D), _whole((1, D)), _whole(w1.shape), _whole(wd.shape), _whole((1, D)),
                  _whole((1, D)), _whole(w2.shape), rev(2 * D), rev(D)],
        out_specs=[rev(D), rev(2 * D), _whole((1, D)), _whole((1, D)), _whole((1, D)), _whole((1, D)),
                   _whole((CW, D)), _whole((1, 2 * D)), _whole((1, D))],
        out_shape=[jax.ShapeDtypeStruct((S, D), F32), jax.ShapeDtypeStruct((S, 2 * D), ACT_DTYPE),
                   vec, vec, vec, vec, jax.ShapeDtypeStruct((CW, D), F32),
                   jax.ShapeDtypeStruct((1, 2 * D), F32), vec],
        scratch_shapes=[pltpu.VMEM((tm + HALO, D), F32), pltpu.VMEM((tm + HALO, D), F32),
                        pltpu.VMEM((8, D), F32), pltpu.VMEM((8, D), F32),
                        pltpu.VMEM((8, D), F32), pltpu.VMEM((8, D), F32), pltpu.VMEM((CW, 8, D), F32),
                        pltpu.VMEM((8, 2 * D), F32), pltpu.VMEM((8, D), F32)],
        compiler_params=_params(1),
    )(d, h, g, w1, wd, lg, lb, w2, u, z)


def weight_grad(a, b, ts, name):
    S, M = a.shape
    N = b.shape[1]
    ta = M if M <= 1024 else 1024
    tb = N if N <= 1024 else 1024
    assert M % ta == 0 and N % tb == 0 and S % ts == 0

    def body(a_ref, b_ref, o_ref):
        @pl.when(pl.program_id(2) == 0)
        def _():
            o_ref[...] = jnp.zeros_like(o_ref)

        o_ref[...] += _mm_tn(a_ref[...], b_ref[...])

    return pl.pallas_call(
        body, name=name, grid=(M // ta, N // tb, S // ts),
        in_specs=[pl.BlockSpec((ts, ta), lambda i, j, s: (s, i)), pl.BlockSpec((ts, tb), lambda i, j, s: (s, j))],
        out_specs=pl.BlockSpec((ta, tb), lambda i, j, s: (i, j)),
        out_shape=jax.ShapeDtypeStruct((M, N), F32),
        compiler_params=_params(3),
    )(a, b)


def _position():
    return lax.axis_index("x"), lax.axis_index("y"), lax.axis_index("c")


def all_gather(x, name):
    def body(x_ref, out_ref, send_sems, recv_sems, local_sem):
        x, y, c = _position()
        me, sibling = (x, y, c), (x, y, 1 - c)
        chips = [(1 - x, y), (x, 1 - y), (1 - x, 1 - y)]

        def slot(px, py, pc):
            return out_ref.at[4 * px + 2 * py + pc]

        def copy(k, block, to, src=None):
            return pltpu.make_async_remote_copy(
                src_ref=slot(*block) if src is None else src, dst_ref=slot(*block),
                send_sem=send_sems.at[k], recv_sem=recv_sems.at[k], device_id=to, device_id_type=MESH)

        mine = pltpu.make_async_copy(x_ref, slot(*me), local_sem)
        mine.start()
        first = [copy(0, me, sibling, src=x_ref)]
        first += [copy(1 + j, me, (*chip, c), src=x_ref) for j, chip in enumerate(chips)]
        for cp in first:
            cp.start()
        passed = [copy(4 + j, (*chip, c), sibling) for j, chip in enumerate(chips)]
        for j, chip in enumerate(chips):
            copy(1 + j, (*chip, c), me).wait_recv()
            passed[j].start()
        copy(0, sibling, me).wait_recv()
        for j, chip in enumerate(chips):
            copy(4 + j, (*chip, 1 - c), me).wait_recv()
        for cp in first + passed:
            cp.wait_send()
        mine.wait()

    return pl.pallas_call(
        body, name=name,
        in_specs=[pl.BlockSpec(memory_space=pl.ANY)], out_specs=pl.BlockSpec(memory_space=pl.ANY),
        out_shape=jax.ShapeDtypeStruct((N_DEV,) + x.shape, x.dtype),
        scratch_shapes=[pltpu.SemaphoreType.DMA((7,)), pltpu.SemaphoreType.DMA((7,)), pltpu.SemaphoreType.DMA],
    )(x)


def sibling_exchange(g):
    _, R, C = g.shape

    def body(g_ref, land_ref, send_sems, recv_sems):
        x, y, c = _position()
        copies = [pltpu.make_async_remote_copy(
            src_ref=g_ref.at[2 * j + 1 - c], dst_ref=land_ref.at[j], send_sem=send_sems.at[j],
            recv_sem=recv_sems.at[j], device_id=(x, y, 1 - c), device_id_type=MESH) for j in range(N_CHIP)]
        for cp in copies:
            cp.start()
        for cp in copies:
            cp.wait()

    return pl.pallas_call(
        body, name="grad_sibling_exchange",
        in_specs=[pl.BlockSpec(memory_space=pl.ANY)], out_specs=pl.BlockSpec(memory_space=pl.ANY),
        out_shape=jax.ShapeDtypeStruct((N_CHIP, R, C), g.dtype),
        scratch_shapes=[pltpu.SemaphoreType.DMA((N_CHIP,)), pltpu.SemaphoreType.DMA((N_CHIP,))],
    )(g)


def chip_partial(g, land, core):
    _, R, C = g.shape
    tr = _row_tile(R, PACK_ROW_TILE)

    def body(c_ref, g_ref, l_ref, o_ref):
        o_ref[...] = (g_ref[...] + l_ref[...]).astype(o_ref.dtype)

    grid_spec = pltpu.PrefetchScalarGridSpec(
        num_scalar_prefetch=1, grid=(N_CHIP, R // tr),
        in_specs=[pl.BlockSpec((None, tr, C), lambda j, i, cr: (2 * j + cr[0], i, 0)),
                  pl.BlockSpec((None, tr, C), lambda j, i, cr: (j, i, 0))],
        out_specs=pl.BlockSpec((None, tr, C), lambda j, i, cr: (j, i, 0)))
    return pl.pallas_call(
        body, name="grad_chip_partial", grid_spec=grid_spec,
        out_shape=jax.ShapeDtypeStruct((N_CHIP, R, C), WIRE_DTYPE),
        compiler_params=_params(2),
    )(core, g, land)


def chip_exchange(part):
    def body(p_ref, land_ref, send_sems, recv_sems, local_sem):
        x, y, c = _position()
        mychip = 2 * x + y
        chips = [(1 - x, y), (x, 1 - y), (1 - x, 1 - y)]
        mine = pltpu.make_async_copy(p_ref.at[mychip], land_ref.at[mychip], local_sem)
        mine.start()
        copies = [pltpu.make_async_remote_copy(
            src_ref=p_ref.at[2 * cx + cy], dst_ref=land_ref.at[mychip], send_sem=send_sems.at[k],
            recv_sem=recv_sems.at[k], device_id=(cx, cy, c), device_id_type=MESH)
            for k, (cx, cy) in enumerate(chips)]
        for cp in copies:
            cp.start()
        for k, (cx, cy) in enumerate(chips):
            pltpu.make_async_remote_copy(
                src_ref=p_ref.at[2 * cx + cy], dst_ref=land_ref.at[2 * cx + cy], send_sem=send_sems.at[k],
                recv_sem=recv_sems.at[k], device_id=(cx, cy, c), device_id_type=MESH).wait_recv()
        for cp in copies:
            cp.wait_send()
        mine.wait()

    return pl.pallas_call(
        body, name="grad_chip_exchange",
        in_specs=[pl.BlockSpec(memory_space=pl.ANY)], out_specs=pl.BlockSpec(memory_space=pl.ANY),
        out_shape=jax.ShapeDtypeStruct(part.shape, part.dtype),
        scratch_shapes=[pltpu.SemaphoreType.DMA((3,)), pltpu.SemaphoreType.DMA((3,)), pltpu.SemaphoreType.DMA],
    )(part)


def _adamw(w, g, m, v):
    m = ADAM_B1 * m + (1.0 - ADAM_B1) * g
    v = ADAM_B2 * v + (1.0 - ADAM_B2) * jnp.square(g)
    m_hat = m / (1.0 - ADAM_B1 ** ADAM_STEP)
    v_hat = v / (1.0 - ADAM_B2 ** ADAM_STEP)
    delta = -ADAM_LR * (m_hat / (jnp.sqrt(v_hat) + ADAM_EPS) + ADAM_WD * w)
    return delta, m, v


def adamw_sharded(parts, w, m, v):
    R, C = w.shape
    tr = _row_tile(R, PACK_ROW_TILE)

    def body(p_ref, w_ref, m_ref, v_ref, g_ref, d_ref, nm_ref, nv_ref):
        g = p_ref[0].astype(F32)
        for j in range(1, N_CHIP):
            g = g + p_ref[j].astype(F32)
        g_ref[...] = g
        d_ref[...], nm_ref[...], nv_ref[...] = _adamw(w_ref[...], g, m_ref[...], v_ref[...])

    out = jax.ShapeDtypeStruct((R, C), F32)
    return pl.pallas_call(
        body, name="adamw_sharded", grid=(R // tr,),
        in_specs=[pl.BlockSpec((N_CHIP, tr, C), lambda i: (0, i, 0)), _rows(tr, C), _rows(tr, C), _rows(tr, C)],
        out_specs=[_rows(tr, C)] * 4, out_shape=[out] * 4,
        compiler_params=_params(1),
    )(parts, w, m, v)


def adamw_replicated(gathered, w, m, v):
    R, C = w.shape

    def body(p_ref, w_ref, m_ref, v_ref, g_ref, d_ref, nm_ref, nv_ref):
        g = p_ref[0]
        for j in range(1, N_DEV):
            g = g + p_ref[j]
        g_ref[...] = g
        d_ref[...], nm_ref[...], nv_ref[...] = _adamw(w_ref[...], g, m_ref[...], v_ref[...])

    out = jax.ShapeDtypeStruct((R, C), F32)
    return pl.pallas_call(
        body, name="adamw_replicated", grid=(1,),
        in_specs=[_whole(gathered.shape), _whole((R, C)), _whole((R, C)), _whole((R, C))],
        out_specs=[_whole((R, C))] * 4, out_shape=[out] * 4,
        compiler_params=_params(1),
    )(gathered, w, m, v)


def _piece_rows(n):
    return -(-n // PACK_COLS)


def _as_rows(a, lead):
    flat = a.reshape(a.shape[:lead] + (-1,))
    fill = _piece_rows(flat.shape[-1]) * PACK_COLS - flat.shape[-1]
    if fill:
        flat = jnp.pad(flat, [(0, 0)] * lead + [(0, fill)])
    return flat.reshape(flat.shape[:-1] + (-1, PACK_COLS))


def _pack(arrays, rows_multiple, dtype=None, lead=0):
    pieces = [_as_rows(a if dtype is None else a.astype(dtype), lead) for a in arrays]
    extra = -sum(p.shape[lead] for p in pieces) % rows_multiple
    if extra:
        pieces.append(jnp.zeros(pieces[0].shape[:lead] + (extra, PACK_COLS), pieces[0].dtype))
    return jnp.concatenate(pieces, axis=lead)


def _unpack(packed, shapes, lead=0):
    out, r0 = [], 0
    for shp in shapes:
        n = int(np.prod(shp))
        rows = _piece_rows(n)
        seg = lax.slice_in_dim(packed, r0, r0 + rows, axis=lead).reshape(packed.shape[:lead] + (-1,))
        if rows * PACK_COLS != n:
            seg = seg[..., :n]
        out.append(seg.reshape(packed.shape[:lead] + tuple(shp)))
        r0 += rows
    return out


def _full_from_gathered(gathered, shard_shapes, axes):
    out = []
    for seg, shp, ax in zip(_unpack(gathered, shard_shapes, lead=1), shard_shapes, axes):
        seg = jnp.moveaxis(seg, 0, ax)
        out.append(seg.reshape(tuple(shp[:ax]) + (N_DEV * shp[ax],) + tuple(shp[ax + 1:])))
    return out


def _chunks_from_full(full, ax):
    shp = full.shape
    split = full.reshape(shp[:ax] + (N_DEV, shp[ax] // N_DEV) + shp[ax + 1:])
    return jnp.moveaxis(split, ax, 0).reshape(N_DEV, -1)


def kernel(x, p, mix_norm, conv_w_pw1, conv_b_pw1, conv_w_dw, conv_b_dw, conv_ln_g, conv_ln_b, conv_w_pw2, conv_b_pw2, kv_norm, w_kvf, b_f, attn_w_q, attn_w_o, ffn_norm, ffn_w1, ffn_w2, ple_norm, ple_w_gate, ple_w_proj, final_norm, loss_target, m_mix_norm, m_conv_w_pw1, m_conv_b_pw1, m_conv_w_dw, m_conv_b_dw, m_conv_ln_g, m_conv_ln_b, m_conv_w_pw2, m_conv_b_pw2, m_kv_norm, m_w_kvf, m_b_f, m_attn_w_q, m_attn_w_o, m_ffn_norm, m_ffn_w1, m_ffn_w2, m_ple_norm, m_ple_w_gate, m_ple_w_proj, m_final_norm, v_mix_norm, v_conv_w_pw1, v_conv_b_pw1, v_conv_w_dw, v_conv_b_dw, v_conv_ln_g, v_conv_ln_b, v_conv_w_pw2, v_conv_b_pw2, v_kv_norm, v_w_kvf, v_b_f, v_attn_w_q, v_attn_w_o, v_ffn_norm, v_ffn_w1, v_ffn_w2, v_ple_norm, v_ple_w_gate, v_ple_w_proj, v_final_norm):
    given = dict(locals())
    W = {n: given[n] for n in WEIGHTS}
    M = {n: given["m_" + n] for n in WEIGHTS}
    V = {n: given["v_" + n] for n in WEIGHTS}

    _, S, D = x.shape
    NA = conv_w_pw1.shape[0]
    NB = attn_w_q.shape[0]
    DEPTH = NA + NB
    H = b_f.shape[0]
    dh = D // H
    hg = 128 // dh
    G = D // 128
    scale = dh ** -0.5
    tm = _row_tile(S, 256)
    tq_f = _row_tile(S, FLASH_FWD_TILE[0])
    tkc_f = _row_tile(tq_f, FLASH_FWD_TILE[1])
    tk_b = _row_tile(S, FLASH_BWD_TILE[0])
    tqc_b = _row_tile(tk_b, FLASH_BWD_TILE[1])
    ts = _row_tile(S, 512)
    xs = x[0]
    tgt = loss_target[0]
    ps = p[:, 0]
    row = lambda a: a.reshape(1, -1)

    big_names = list(SHARD_AXIS_BIG)
    small_names = list(SHARD_AXIS_SMALL)
    big = _full_from_gathered(
        all_gather(_pack([W[n] for n in big_names], 16, MXU_DTYPE), "weights_all_gather"),
        [W[n].shape for n in big_names], [SHARD_AXIS_BIG[n] for n in big_names])
    small = _full_from_gathered(
        all_gather(_pack([W[n] for n in small_names], 8), "vectors_all_gather"),
        [W[n].shape for n in small_names], [SHARD_AXIS_SMALL[n] for n in small_names])
    FW = dict(zip(big_names + small_names, big + small))
    wk, wv, wf = FW["w_kvf"][:, :D], FW["w_kvf"][:, D:2 * D], FW["w_kvf"][:, 2 * D:]

    saved = []
    h = xs
    kv = None
    for i in range(DEPTH):
        rec = {"h_in": h}
        if i < NA:
            h, rec["n"], rec["u"], rec["z"], rec["sw"] = conv_fwd(
                h, row(mix_norm[i]), FW["conv_w_pw1"][i], row(FW["conv_b_pw1"][i]), FW["conv_w_dw"][i],
                row(FW["conv_b_dw"][i]), row(FW["conv_ln_g"][i]), row(FW["conv_ln_b"][i]),
                FW["conv_w_pw2"][i], row(FW["conv_b_pw2"][i]), tm)
        else:
            j = i - NA
            if j == 0:
                k_, kT_, vT_, nkv, fl, c = kv_fwd(h, row(kv_norm), wk, wv, wf, row(b_f), tm)
                cg = c.reshape(S, G, hg)
                kv = dict(k=k_, kT=kT_, vT=vT_, n=nkv, fl=fl, h=h, c_col=jnp.transpose(cg, (1, 0, 2)),
                          c_row=jnp.transpose(cg, (1, 2, 0)))
            rec["n"], rec["q"], rec["qT"] = q_fwd(h, row(mix_norm[i]), FW["attn_w_q"][j], scale, tm)
            rec["o"], rec["o32"], rec["lse"] = flash_fwd(rec["qT"], kv["k"], kv["vT"], kv["c_col"], kv["c_row"], dh,
                                                         tq_f, tkc_f)
            h = attn_out_fwd(h, rec["o"], FW["attn_w_o"][j], tm)
        rec["h_ffn"] = h
        h, rec["n_ffn"], rec["a"], rec["s"] = ffn_fwd(h, row(ffn_norm[i]), FW["ffn_w1"][i], FW["ffn_w2"][i], tm)
        rec["h_ple"] = h
        h, rec["n_ple"], rec["gate"] = ple_fwd(h, row(ple_norm[i]), FW["ple_w_gate"][i], ps[i],
                                               FW["ple_w_proj"][i], tm)
        saved.append(rec)

    d, g_final, loss_part = loss_head(h, row(final_norm), tgt, tm)
    GW = {n: [None] * W[n].shape[0] for n in WEIGHTS if W[n].ndim > 1 and n != "w_kvf"}
    dks, dvs, dcs = [], [], []
    for i in reversed(range(DEPTH)):
        rec = saved[i]
        d_out = d
        d, dz, dpp, GW["ple_norm"][i] = ple_bwd(d_out, rec["h_ple"], row(ple_norm[i]), FW["ple_w_gate"][i],
                                                rec["gate"], ps[i], FW["ple_w_proj"][i], tm)
        GW["ple_w_gate"][i] = weight_grad(rec["n_ple"], dz, ts, "grad_ple_w_gate")
        GW["ple_w_proj"][i] = weight_grad(ps[i], dpp, ts, "grad_ple_w_proj")
        d_out = d
        d, da, GW["ffn_norm"][i] = ffn_bwd(d_out, rec["h_ffn"], row(ffn_norm[i]), FW["ffn_w1"][i],
                                           FW["ffn_w2"][i], rec["a"], tm)
        GW["ffn_w2"][i] = weight_grad(rec["s"], d_out, ts, "grad_ffn_w2")
        GW["ffn_w1"][i] = weight_grad(rec["n_ffn"], da, ts, "grad_ffn_w1")
        d_out = d
        if i >= NA:
            j = i - NA
            GW["attn_w_o"][j] = weight_grad(rec["o"], d_out, ts, "grad_attn_w_o")
            do, doT, delta = attn_out_bwd(d_out, FW["attn_w_o"][j], rec["o32"], dh, tm)
            qstat = jnp.concatenate([kv["c_col"], jnp.transpose(rec["lse"], (0, 2, 1)),
                                     jnp.transpose(delta.reshape(S, G, hg), (1, 0, 2))], axis=2)
            dq, dkT, dvT, dck, dcq = flash_bwd(rec["q"], rec["qT"], kv["k"], kv["kT"], kv["vT"], kv["c_row"], do, doT,
                                          qstat, dh, tk_b, tqc_b)
            dks.append(dkT)
            dvs.append(dvT)
            dcs.append(jnp.transpose(dck[:, :hg, :], (2, 0, 1)).reshape(S, H))
            dcs.append(jnp.transpose(dcq, (1, 0, 2)).reshape(S, H))
            d, dqs, GW["mix_norm"][i] = q_bwd(d_out, dq, rec["h_in"], row(mix_norm[i]), FW["attn_w_q"][j], scale, tm)
            GW["attn_w_q"][j] = weight_grad(rec["n"], dqs, ts, "grad_attn_w_q")
            if j == 0:
                d, dk_sum, dv_sum, dfl, g_kv_norm, g_b_f = kv_bwd(d, dks, dvs, dcs, kv["fl"], kv["h"],
                                                                  row(kv_norm), wk, wv, wf, tm)
                g_w_kvf = jnp.concatenate([weight_grad(kv["n"], dk_sum, ts, "grad_w_k"),
                                           weight_grad(kv["n"], dv_sum, ts, "grad_w_v"),
                                           weight_grad(kv["n"], dfl, ts, "grad_w_f")], axis=1)
        else:
            GW["conv_w_pw2"][i] = weight_grad(rec["sw"], d_out, ts, "grad_conv_w_pw2")
            (d, du, GW["conv_b_pw2"][i], GW["conv_ln_g"][i], GW["conv_ln_b"][i], GW["conv_b_dw"][i],
             GW["conv_w_dw"][i], GW["conv_b_pw1"][i], GW["mix_norm"][i]) = conv_bwd(
                d_out, rec["h_in"], row(mix_norm[i]), FW["conv_w_pw1"][i], FW["conv_w_dw"][i],
                row(FW["conv_ln_g"][i]), row(FW["conv_ln_b"][i]), FW["conv_w_pw2"][i], rec["u"], rec["z"], tm)
            GW["conv_w_pw1"][i] = weight_grad(rec["n"], du, ts, "grad_conv_w_pw1")
    grad_x = d[None]

    def stacked(n):
        return jnp.concatenate(GW[n], axis=0) if W[n].ndim == 2 else jnp.stack(GW[n])

    full_grads = {n: stacked(n) for n in GW if n not in REPLICATED}
    full_grads["w_kvf"] = g_w_kvf

    sharded_names = big_names + small_names
    axes = {**SHARD_AXIS_BIG, **SHARD_AXIS_SMALL}
    chunks = _pack([_chunks_from_full(full_grads[n], axes[n]) for n in sharded_names], PACK_ROW_TILE, lead=1)
    core = lax.axis_index("c").astype(jnp.int32).reshape(1)
    parts = chip_exchange(chip_partial(chunks, sibling_exchange(chunks), core))

    pack_sh = lambda src: _pack([src[n] for n in sharded_names], PACK_ROW_TILE)
    outs_sh = adamw_sharded(parts, pack_sh(W), pack_sh(M), pack_sh(V))
    shard_shapes = [W[n].shape for n in sharded_names]
    res = {}
    for kind, packed in zip(("grad", "delta", "new_m", "new_v"), outs_sh):
        for n, a in zip(sharded_names, _unpack(packed, shard_shapes)):
            res[kind, n] = a

    rep_grads = {"mix_norm": jnp.concatenate(GW["mix_norm"], axis=0), "kv_norm": g_kv_norm,
                 "b_f": g_b_f, "ffn_norm": jnp.concatenate(GW["ffn_norm"], axis=0),
                 "ple_norm": jnp.concatenate(GW["ple_norm"], axis=0), "final_norm": g_final}

    def pack_rep(src, extra=None):
        rows_ = [jnp.pad(src[n].reshape(-1, src[n].shape[-1]), ((0, 0), (0, D - src[n].shape[-1])))
                 for n in REPLICATED]
        if extra is not None:
            rows_.append(jnp.pad(extra, ((0, 0), (0, D - extra.shape[-1]))))
        else:
            rows_.append(jnp.zeros((1, D), F32))
        flat = jnp.concatenate(rows_, axis=0)
        return jnp.pad(flat, ((0, -flat.shape[0] % 8), (0, 0)))

    rep_g = all_gather(pack_rep(rep_grads, loss_part), "replicated_all_gather")
    outs_rep = adamw_replicated(rep_g, pack_rep(W), pack_rep(M), pack_rep(V))
    n_rep_rows = sum(int(np.prod(W[n].shape[:-1])) for n in REPLICATED)
    for kind, packed in zip(("grad", "delta", "new_m", "new_v"), outs_rep):
        r0 = 0
        for n in REPLICATED:
            nr = int(np.prod(W[n].shape[:-1]))
            res[kind, n] = packed[r0:r0 + nr, :W[n].shape[-1]].reshape(W[n].shape)
            r0 += nr
    loss = outs_rep[0][n_rep_rows, 0]

    return (loss, grad_x, *[res["grad", n] for n in WEIGHTS], *[res["delta", n] for n in WEIGHTS],
            *[res["new_m", n] for n in WEIGHTS], *[res["new_v", n] for n in WEIGHTS])
```

```python
import numpy as np
import jax
import jax.numpy as jnp
from jax import lax
from jax.experimental import pallas as pl
from jax.experimental.pallas import tpu as pltpu

F32 = jnp.float32
MXU_DTYPE = jnp.bfloat16
ACT_DTYPE = jnp.bfloat16
WIRE_DTYPE = jnp.bfloat16

N_DEV = 8
N_CHIP = 4
EPS = 1e-6
NEG_BIG = -1e30
ADAM_LR = 0.001
ADAM_B1 = 0.9
ADAM_B2 = 0.999
ADAM_EPS = 1e-08
ADAM_WD = 0.01
ADAM_STEP = 10

VMEM_LIMIT_BYTES = 56 * 1024 * 1024
PACK_COLS = 1024
PACK_ROW_TILE = 256
FLASH_FWD_TILE = (1024, 512)
FLASH_BWD_TILE = (1024, 512)
FLASH_TILE = 128
HALO = 32
MESH = pl.DeviceIdType.MESH

SHARD_AXIS_BIG = {"conv_w_pw1": 2, "conv_w_pw2": 1, "w_kvf": 1, "attn_w_q": 1, "attn_w_o": 1,
                  "ffn_w1": 2, "ffn_w2": 1, "ple_w_gate": 1, "ple_w_proj": 2}
SHARD_AXIS_SMALL = {"conv_b_pw1": 1, "conv_w_dw": 2, "conv_b_dw": 1, "conv_ln_g": 1, "conv_ln_b": 1,
                    "conv_b_pw2": 1}
REPLICATED = ["mix_norm", "kv_norm", "b_f", "ffn_norm", "ple_norm", "final_norm"]
WEIGHTS = ["mix_norm", "conv_w_pw1", "conv_b_pw1", "conv_w_dw", "conv_b_dw", "conv_ln_g", "conv_ln_b",
           "conv_w_pw2", "conv_b_pw2", "kv_norm", "w_kvf", "b_f", "attn_w_q", "attn_w_o", "ffn_norm",
           "ffn_w1", "ffn_w2", "ple_norm", "ple_w_gate", "ple_w_proj", "final_norm"]


def _mm(a, b):
    return jnp.dot(a.astype(MXU_DTYPE), b.astype(MXU_DTYPE), preferred_element_type=F32)


def _mm_nt(a, b):
    return lax.dot_general(a.astype(MXU_DTYPE), b.astype(MXU_DTYPE), (((1,), (1,)), ((), ())),
                           preferred_element_type=F32)


def _mm_tn(a, b):
    return lax.dot_general(a.astype(MXU_DTYPE), b.astype(MXU_DTYPE), (((0,), (0,)), ((), ())),
                           preferred_element_type=F32)


def _split3(x):
    hi = x.astype(MXU_DTYPE)
    r1 = x - hi.astype(F32)
    mid = r1.astype(MXU_DTYPE)
    lo = (r1 - mid.astype(F32)).astype(MXU_DTYPE)
    return hi, mid, lo


def _tri_mm(tri, x):
    hi, mid, lo = _split3(x)
    return (jnp.dot(tri, lo, preferred_element_type=F32) + jnp.dot(tri, mid, preferred_element_type=F32)
            + jnp.dot(tri, hi, preferred_element_type=F32))


def _colsum8(x):
    tm, n = x.shape
    return jnp.sum(x.reshape(tm // 8, 8, n), axis=0)


def _rms(x, g):
    r = lax.rsqrt(jnp.mean(x * x, axis=-1, keepdims=True) + EPS)
    return x * r * g, r


def _rms_bwd(x, r, g, dn):
    w = dn * g
    dx = r * w - x * (r * r * r) * jnp.mean(w * x, axis=-1, keepdims=True)
    return dx, dn * x * r


def _sigmoid(x):
    return jax.nn.sigmoid(x)


def _params(n_grid):
    return pltpu.CompilerParams(dimension_semantics=("arbitrary",) * n_grid, vmem_limit_bytes=VMEM_LIMIT_BYTES)


def _rows(tm, n):
    return pl.BlockSpec((tm, n), lambda i: (i, 0))


def _rows_rev(tm, n, nt):
    return pl.BlockSpec((tm, n), lambda i: (nt - 1 - i, 0))


def _cols(n, tm):
    return pl.BlockSpec((n, tm), lambda i: (0, i))


def _cols_rev(n, tm, nt):
    return pl.BlockSpec((n, tm), lambda i: (0, nt - 1 - i))


def _whole(shape):
    nd = len(shape)
    return pl.BlockSpec(shape, lambda i: (0,) * nd)


def _row_tile(s, want):
    tm = min(s, want)
    assert s % tm == 0 and tm % 8 == 0, (s, tm)
    return tm


def conv_fwd(h, g, w1, b1, wd, bd, lg, lb, w2, b2, tm):
    S, D = h.shape
    CW = wd.shape[0]
    off = HALO - (CW - 1)
    assert 0 <= off and tm >= HALO
    nt = S // tm

    def body(h_ref, g_ref, w1_ref, b1_ref, wd_ref, bd_ref, lg_ref, lb_ref, w2_ref, b2_ref,
             ho_ref, n_ref, u_ref, z_ref, sw_ref, ext, win):
        @pl.when(pl.program_id(0) == 0)
        def _():
            ext[0:HALO, :] = jnp.zeros((HALO, D), F32)

        x = h_ref[...]
        n, _ = _rms(x, g_ref[...])
        n_ref[...] = n.astype(n_ref.dtype)
        u = _mm(n, w1_ref[...]) + b1_ref[...]
        u_ref[...] = u
        ext[HALO:HALO + tm, :] = u[:, :D] * _sigmoid(u[:, D:])
        z = jnp.broadcast_to(bd_ref[...], (tm, D))
        for b in range(8):
            amax = (CW - 1 - b) // 8
            win[0:tm + 8 * amax, :] = ext[off + b:off + b + tm + 8 * amax, :]
            for a8 in range(amax + 1):
                z = z + wd_ref[8 * a8 + b:8 * a8 + b + 1, :] * win[8 * a8:8 * a8 + tm, :]
        z_ref[...] = z
        ext[0:HALO, :] = ext[tm:tm + HALO, :]
        mu = jnp.mean(z, axis=-1, keepdims=True)
        zc = z - mu
        y = zc * lax.rsqrt(jnp.mean(zc * zc, axis=-1, keepdims=True) + EPS) * lg_ref[...] + lb_ref[...]
        sw = y * _sigmoid(y)
        sw_ref[...] = sw.astype(sw_ref.dtype)
        ho_ref[...] = x + _mm(sw, w2_ref[...]) + b2_ref[...]

    return pl.pallas_call(
        body, name="conv_fwd", grid=(nt,),
        in_specs=[_rows(tm, D), _whole((1, D)), _whole(w1.shape), _whole((1, 2 * D)), _whole(wd.shape),
                  _whole((1, D)), _whole((1, D)), _whole((1, D)), _whole(w2.shape), _whole((1, D))],
        out_specs=[_rows(tm, D), _rows(tm, D), _rows(tm, 2 * D), _rows(tm, D), _rows(tm, D)],
        out_shape=[jax.ShapeDtypeStruct((S, D), F32), jax.ShapeDtypeStruct((S, D), ACT_DTYPE),
                   jax.ShapeDtypeStruct((S, 2 * D), F32), jax.ShapeDtypeStruct((S, D), F32),
                   jax.ShapeDtypeStruct((S, D), ACT_DTYPE)],
        scratch_shapes=[pltpu.VMEM((HALO + tm, D), F32), pltpu.VMEM((HALO + tm, D), F32)],
        compiler_params=_params(1),
    )(h, g, w1, b1, wd, bd, lg, lb, w2, b2)


def ffn_fwd(h, g, w1, w2, tm):
    S, D = h.shape
    FF = w1.shape[1]

    def body(h_ref, g_ref, w1_ref, w2_ref, ho_ref, n_ref, a_ref, s_ref):
        x = h_ref[...]
        n, _ = _rms(x, g_ref[...])
        n_ref[...] = n.astype(n_ref.dtype)
        a = _mm(n, w1_ref[...])
        a_ref[...] = a
        s = jnp.square(jnp.maximum(a, 0.0))
        s_ref[...] = s.astype(s_ref.dtype)
        ho_ref[...] = x + _mm(s, w2_ref[...])

    return pl.pallas_call(
        body, name="ffn_fwd", grid=(S // tm,),
        in_specs=[_rows(tm, D), _whole((1, D)), _whole(w1.shape), _whole(w2.shape)],
        out_specs=[_rows(tm, D), _rows(tm, D), _rows(tm, FF), _rows(tm, FF)],
        out_shape=[jax.ShapeDtypeStruct((S, D), F32), jax.ShapeDtypeStruct((S, D), ACT_DTYPE),
                   jax.ShapeDtypeStruct((S, FF), F32), jax.ShapeDtypeStruct((S, FF), ACT_DTYPE)],
        compiler_params=_params(1),
    )(h, g, w1, w2)


def ple_fwd(h, g, wg, p, wp, tm):
    S, D = h.shape
    E = p.shape[1]

    def body(h_ref, g_ref, wg_ref, p_ref, wp_ref, ho_ref, n_ref, gate_ref):
        x = h_ref[...]
        n, _ = _rms(x, g_ref[...])
        n_ref[...] = n.astype(n_ref.dtype)
        gate = _sigmoid(_mm(n, wg_ref[...]))
        gate_ref[...] = gate
        ho_ref[...] = x + gate * _mm(p_ref[...], wp_ref[...])

    return pl.pallas_call(
        body, name="ple_fwd", grid=(S // tm,),
        in_specs=[_rows(tm, D), _whole((1, D)), _whole(wg.shape), _rows(tm, E), _whole(wp.shape)],
        out_specs=[_rows(tm, D), _rows(tm, D), _rows(tm, D)],
        out_shape=[jax.ShapeDtypeStruct((S, D), F32), jax.ShapeDtypeStruct((S, D), ACT_DTYPE),
                   jax.ShapeDtypeStruct((S, D), F32)],
        compiler_params=_params(1),
    )(h, g, wg, p, wp)


def kv_fwd(h, g, wk, wv, wf, bf, tm):
    S, D = h.shape
    H = wf.shape[1]

    def body(h_ref, g_ref, wk_ref, wv_ref, wf_ref, bf_ref, k_ref, kT_ref, vT_ref, n_ref, fl_ref, c_ref, carry):
        @pl.when(pl.program_id(0) == 0)
        def _():
            carry[...] = jnp.zeros_like(carry)

        n, _ = _rms(h_ref[...], g_ref[...])
        n_ref[...] = n.astype(n_ref.dtype)
        k = _mm(n, wk_ref[...])
        k_ref[...] = k.astype(k_ref.dtype)
        kT_ref[...] = k.T.astype(kT_ref.dtype)
        vT_ref[...] = _mm(n, wv_ref[...]).T.astype(vT_ref.dtype)
        fl = _mm(n, wf_ref[...]) + bf_ref[...]
        fl_ref[...] = fl
        logf = jnp.minimum(fl, 0.0) - jnp.log1p(jnp.exp(-jnp.abs(fl)))
        row = lax.broadcasted_iota(jnp.int32, (tm, tm), 0)
        col = lax.broadcasted_iota(jnp.int32, (tm, tm), 1)
        tri = (row >= col).astype(MXU_DTYPE)
        c = _tri_mm(tri, logf) + carry[...]
        c_ref[...] = c
        carry[...] = c[tm - 1:tm, :]

    return pl.pallas_call(
        body, name="kv_fwd", grid=(S // tm,),
        in_specs=[_rows(tm, D), _whole((1, D)), _whole(wk.shape), _whole(wv.shape), _whole(wf.shape),
                  _whole((1, H))],
        out_specs=[_rows(tm, D), _cols(D, tm), _cols(D, tm), _rows(tm, D), _rows(tm, H), _rows(tm, H)],
        out_shape=[jax.ShapeDtypeStruct((S, D), ACT_DTYPE), jax.ShapeDtypeStruct((D, S), ACT_DTYPE),
                   jax.ShapeDtypeStruct((D, S), ACT_DTYPE), jax.ShapeDtypeStruct((S, D), ACT_DTYPE),
                   jax.ShapeDtypeStruct((S, H), F32), jax.ShapeDtypeStruct((S, H), F32)],
        scratch_shapes=[pltpu.VMEM((1, H), F32)],
        compiler_params=_params(1),
    )(h, g, wk, wv, wf, bf)


def q_fwd(h, g, wq, scale, tm):
    S, D = h.shape

    def body(h_ref, g_ref, wq_ref, n_ref, q_ref, qT_ref):
        n, _ = _rms(h_ref[...], g_ref[...])
        n_ref[...] = n.astype(n_ref.dtype)
        q = _mm(n, wq_ref[...]) * scale
        q_ref[...] = q.astype(q_ref.dtype)
        qT_ref[...] = q.T.astype(qT_ref.dtype)

    return pl.pallas_call(
        body, name="q_fwd", grid=(S // tm,),
        in_specs=[_rows(tm, D), _whole((1, D)), _whole(wq.shape)],
        out_specs=[_rows(tm, D), _rows(tm, D), _cols(D, tm)],
        out_shape=[jax.ShapeDtypeStruct((S, D), ACT_DTYPE), jax.ShapeDtypeStruct((S, D), ACT_DTYPE),
                   jax.ShapeDtypeStruct((D, S), ACT_DTYPE)],
        compiler_params=_params(1),
    )(h, g, wq)


def attn_out_fwd(h, o, wo, tm):
    S, D = h.shape

    def body(h_ref, o_ref, wo_ref, ho_ref):
        ho_ref[...] = h_ref[...] + _mm(o_ref[...], wo_ref[...])

    return pl.pallas_call(
        body, name="attn_out_fwd", grid=(S // tm,),
        in_specs=[_rows(tm, D), _rows(tm, D), _whole(wo.shape)],
        out_specs=_rows(tm, D),
        out_shape=jax.ShapeDtypeStruct((S, D), F32),
        compiler_params=_params(1),
    )(h, o, wo)


def _causal_mask(key0, qry0, shape, key_axis):
    key = key0 + lax.broadcasted_iota(jnp.int32, shape, key_axis)
    qry = qry0 + lax.broadcasted_iota(jnp.int32, shape, 1 - key_axis)
    return key <= qry


def flash_fwd(qT, k, vT, c_col, c_row, dh, tq, tkc):
    D, S = qT.shape
    hg = 128 // dh
    G = D // 128
    per = tq // tkc
    assert tq % tkc == 0 and S % tq == 0

    def body(qT_ref, k_ref, vT_ref, ccol_ref, crow_ref, o_ref, o32_ref, lse_ref, m_scr, l_scr, acc_scr, mx_scr,
             *scratch):
        i = pl.program_id(1)
        m_scr[...] = jnp.full(m_scr.shape, NEG_BIG, F32)
        l_scr[...] = jnp.zeros(l_scr.shape, F32)
        acc_scr[...] = jnp.zeros(acc_scr.shape, F32)

        def head_shift(j, hh):
            c0 = ccol_ref[pl.ds(j * tkc, 1), hh:hh + 1]
            return c0, crow_ref[hh:hh + 1, :] - c0

        def scores(j, masked, slot):
            keys = pl.ds(pl.multiple_of(j * tkc, tkc), tkc)
            s_scr = scratch[slot]
            for hh in range(hg):
                lanes = slice(hh * dh, (hh + 1) * dh)
                c0, r = head_shift(j, hh)
                s = _mm(k_ref[keys, lanes], qT_ref[lanes, :]) - (ccol_ref[keys, hh:hh + 1] - c0)
                if masked:
                    s = jnp.where(_causal_mask(j * tkc, i * tq, (tkc, tq), 0), s, NEG_BIG)
                s_scr[hh] = s
                mx_scr[slot * hg + hh] = jnp.max(s, axis=0, keepdims=True) + r

        def update(j, slot):
            keys = pl.ds(pl.multiple_of(j * tkc, tkc), tkc)
            s_scr = scratch[slot]
            for hh in range(hg):
                lanes = slice(hh * dh, (hh + 1) * dh)
                _, r = head_shift(j, hh)
                m_old = m_scr[hh]
                m_new = jnp.maximum(m_old, mx_scr[slot * hg + hh])
                alpha = jnp.exp(m_old - m_new)
                p = jnp.exp(s_scr[hh] - (m_new - r))
                l_scr[hh] = alpha * l_scr[hh] + jnp.sum(p, axis=0, keepdims=True)
                acc_scr[lanes, :] = alpha * acc_scr[lanes, :] + _mm(vT_ref[lanes, keys], p)
                m_scr[hh] = m_new

        def chunks(j0, masked):
            for jj in range(per):
                scores(j0 + jj, masked, jj)
            for jj in range(per):
                update(j0 + jj, jj)

        def full_chunks(jb, carry):
            chunks(jb * per, False)
            return carry

        lax.fori_loop(0, i, full_chunks, 0)
        chunks(i * per, True)
        for hh in range(hg):
            lanes = slice(hh * dh, (hh + 1) * dh)
            acc_scr[lanes, :] = acc_scr[lanes, :] / l_scr[hh]
            lse_ref[hh:hh + 1, :] = m_scr[hh] + jnp.log(l_scr[hh])
        o = acc_scr[...].T
        o_ref[...] = o.astype(o_ref.dtype)
        o32_ref[...] = o

    return pl.pallas_call(
        body, name="flash_fwd", grid=(G, S // tq),
        in_specs=[pl.BlockSpec((128, tq), lambda g, i: (g, i)),
                  pl.BlockSpec((S, 128), lambda g, i: (0, g)),
                  pl.BlockSpec((128, S), lambda g, i: (g, 0)),
                  pl.BlockSpec((None, S, hg), lambda g, i: (g, 0, 0)),
                  pl.BlockSpec((None, hg, tq), lambda g, i: (g, 0, i))],
        out_specs=[pl.BlockSpec((tq, 128), lambda g, i: (i, g)),
                   pl.BlockSpec((tq, 128), lambda g, i: (i, g)),
                   pl.BlockSpec((None, hg, tq), lambda g, i: (g, 0, i))],
        out_shape=[jax.ShapeDtypeStruct((S, D), ACT_DTYPE), jax.ShapeDtypeStruct((S, D), F32),
                   jax.ShapeDtypeStruct((G, hg, S), F32)],
        scratch_shapes=([pltpu.VMEM((hg, 1, tq), F32), pltpu.VMEM((hg, 1, tq), F32), pltpu.VMEM((128, tq), F32),
                         pltpu.VMEM((per * hg, 1, tq), F32)]
                        + [pltpu.VMEM((hg, tkc, tq), F32)] * per),
        compiler_params=_params(2),
    )(qT, k, vT, c_col, c_row)


def loss_head(h, g, target, tm):
    S, D = h.shape
    nt = S // tm

    def body(h_ref, g_ref, t_ref, dh_ref, dg_ref, loss_ref, dg_acc, loss_acc):
        i = pl.program_id(0)

        @pl.when(i == 0)
        def _():
            dg_acc[...] = jnp.zeros_like(dg_acc)
            loss_acc[...] = jnp.zeros_like(loss_acc)

        x = h_ref[...]
        gg = g_ref[...]
        y, r = _rms(x, gg)
        e = y - t_ref[...]
        loss_acc[...] += 0.5 * jnp.sum(jnp.mean(e * e, axis=-1, keepdims=True), axis=0, keepdims=True)
        dx, dgr = _rms_bwd(x, r, gg, e / D)
        dh_ref[...] = dx
        dg_acc[...] += _colsum8(dgr)

        @pl.when(i == nt - 1)
        def _():
            dg_ref[...] = jnp.sum(dg_acc[...], axis=0, keepdims=True)
            loss_ref[...] = jnp.broadcast_to(loss_acc[...], loss_ref.shape)

    return pl.pallas_call(
        body, name="loss_head", grid=(nt,),
        in_specs=[_rows(tm, D), _whole((1, D)), _rows(tm, D)],
        out_specs=[_rows(tm, D), _whole((1, D)), _whole((1, 128))],
        out_shape=[jax.ShapeDtypeStruct((S, D), F32), jax.ShapeDtypeStruct((1, D), F32),
                   jax.ShapeDtypeStruct((1, 128), F32)],
        scratch_shapes=[pltpu.VMEM((8, D), F32), pltpu.VMEM((1, 1), F32)],
        compiler_params=_params(1),
    )(h, g, target)


def ple_bwd(d, h, g, wg, gate, p, wp, tm):
    S, D = h.shape
    E = p.shape[1]
    nt = S // tm

    def body(d_ref, h_ref, g_ref, wg_ref, gate_ref, p_ref, wp_ref, di_ref, dz_ref, dpp_ref, dg_ref, dg_acc):
        i = pl.program_id(0)

        @pl.when(i == 0)
        def _():
            dg_acc[...] = jnp.zeros_like(dg_acc)

        dd = d_ref[...]
        x = h_ref[...]
        gg = g_ref[...]
        gt = gate_ref[...]
        pp = _mm(p_ref[...], wp_ref[...])
        dpp_ref[...] = (dd * gt).astype(dpp_ref.dtype)
        dz = dd * pp * gt * (1.0 - gt)
        dz_ref[...] = dz.astype(dz_ref.dtype)
        r = lax.rsqrt(jnp.mean(x * x, axis=-1, keepdims=True) + EPS)
        dx, dgr = _rms_bwd(x, r, gg, _mm_nt(dz, wg_ref[...]))
        di_ref[...] = dd + dx
        dg_acc[...] += _colsum8(dgr)

        @pl.when(i == nt - 1)
        def _():
            dg_ref[...] = jnp.sum(dg_acc[...], axis=0, keepdims=True)

    return pl.pallas_call(
        body, name="ple_bwd", grid=(nt,),
        in_specs=[_rows(tm, D), _rows(tm, D), _whole((1, D)), _whole(wg.shape), _rows(tm, D), _rows(tm, E),
                  _whole(wp.shape)],
        out_specs=[_rows(tm, D), _rows(tm, D), _rows(tm, D), _whole((1, D))],
        out_shape=[jax.ShapeDtypeStruct((S, D), F32), jax.ShapeDtypeStruct((S, D), ACT_DTYPE),
                   jax.ShapeDtypeStruct((S, D), ACT_DTYPE), jax.ShapeDtypeStruct((1, D), F32)],
        scratch_shapes=[pltpu.VMEM((8, D), F32)],
        compiler_params=_params(1),
    )(d, h, g, wg, gate, p, wp)


def ffn_bwd(d, h, g, w1, w2, a, tm):
    S, D = h.shape
    FF = w1.shape[1]
    nt = S // tm

    def body(d_ref, h_ref, g_ref, w1_ref, w2_ref, a_ref, di_ref, da_ref, dg_ref, dg_acc):
        i = pl.program_id(0)

        @pl.when(i == 0)
        def _():
            dg_acc[...] = jnp.zeros_like(dg_acc)

        dd = d_ref[...]
        x = h_ref[...]
        da = _mm_nt(dd, w2_ref[...]) * (2.0 * jnp.maximum(a_ref[...], 0.0))
        da_ref[...] = da.astype(da_ref.dtype)
        r = lax.rsqrt(jnp.mean(x * x, axis=-1, keepdims=True) + EPS)
        dx, dgr = _rms_bwd(x, r, g_ref[...], _mm_nt(da, w1_ref[...]))
        di_ref[...] = dd + dx
        dg_acc[...] += _colsum8(dgr)

        @pl.when(i == nt - 1)
        def _():
            dg_ref[...] = jnp.sum(dg_acc[...], axis=0, keepdims=True)

    return pl.pallas_call(
        body, name="ffn_bwd", grid=(nt,),
        in_specs=[_rows(tm, D), _rows(tm, D), _whole((1, D)), _whole(w1.shape), _whole(w2.shape), _rows(tm, FF)],
        out_specs=[_rows(tm, D), _rows(tm, FF), _whole((1, D))],
        out_shape=[jax.ShapeDtypeStruct((S, D), F32), jax.ShapeDtypeStruct((S, FF), ACT_DTYPE),
                   jax.ShapeDtypeStruct((1, D), F32)],
        scratch_shapes=[pltpu.VMEM((8, D), F32)],
        compiler_params=_params(1),
    )(d, h, g, w1, w2, a)


def attn_out_bwd(d, wo, o32, dh, tm):
    S, D = d.shape
    H = D // dh

    def body(d_ref, wo_ref, o_ref, do_ref, doT_ref, delta_ref):
        do32 = _mm_nt(d_ref[...], wo_ref[...])
        do = do32.astype(do_ref.dtype)
        do_ref[...] = do
        doT_ref[...] = do32.T.astype(doT_ref.dtype)
        lane_head = lax.broadcasted_iota(jnp.int32, (D, H), 0) // dh
        seg = (lane_head == lax.broadcasted_iota(jnp.int32, (D, H), 1)).astype(MXU_DTYPE)
        hi, mid, lo = _split3(do.astype(F32) * o_ref[...])
        delta_ref[...] = (jnp.dot(lo, seg, preferred_element_type=F32) + jnp.dot(mid, seg, preferred_element_type=F32)
                          + jnp.dot(hi, seg, preferred_element_type=F32))

    return pl.pallas_call(
        body, name="attn_out_bwd", grid=(S // tm,),
        in_specs=[_rows(tm, D), _whole(wo.shape), _rows(tm, D)],
        out_specs=[_rows(tm, D), _cols(D, tm), _rows(tm, H)],
        out_shape=[jax.ShapeDtypeStruct((S, D), ACT_DTYPE), jax.ShapeDtypeStruct((D, S), ACT_DTYPE),
                   jax.ShapeDtypeStruct((S, H), F32)],
        compiler_params=_params(1),
    )(d, wo, o32)


def flash_bwd(q, qT, k, kT, vT, c_row, do, doT, qstat, dh, tk, tqc):
    S, D = q.shape
    hg = 128 // dh
    G = D // 128
    per = tk // tqc
    nchunk = S // tqc
    assert hg <= 8 and tk % tqc == 0 and S % tk == 0

    def body(q_ref, qT_ref, k_ref, kT_ref, vT_ref, crow_ref, do_ref, doT_ref, st_ref,
             dq_ref, dkT_ref, dvT_ref, dck_ref, dcq_ref):
        ki = pl.program_id(1)

        @pl.when(ki == 0)
        def _():
            dq_ref[...] = jnp.zeros_like(dq_ref)
            dcq_ref[...] = jnp.zeros_like(dcq_ref)

        dck_ref[...] = jnp.zeros_like(dck_ref)
        dkT_ref[...] = jnp.zeros_like(dkT_ref)
        dvT_ref[...] = jnp.zeros_like(dvT_ref)
        def chunk(jq, masked):
            rows = pl.ds(pl.multiple_of(jq * tqc, tqc), tqc)
            st = st_ref[rows, :]
            for hh in range(hg):
                lanes = slice(hh * dh, (hh + 1) * dh)
                kh = k_ref[:, lanes]
                ck = crow_ref[hh:hh + 1, :]
                c0 = ck[:, 0:1]
                u = (st[:, hh:hh + 1] - c0) - st[:, hg + hh:hg + hh + 1]
                s = (_mm(q_ref[rows, lanes], kT_ref[lanes, :]) - (ck - c0)) + u
                if masked:
                    s = jnp.where(_causal_mask(ki * tk, jq * tqc, (tqc, tk), 1), s, NEG_BIG)
                p = jnp.exp(s)
                dvT_ref[lanes, :] += _mm(doT_ref[lanes, rows], p)
                ds = p * (_mm(do_ref[rows, lanes], vT_ref[lanes, :]) - st[:, 2 * hg + hh:2 * hg + hh + 1])
                dkT_ref[lanes, :] += _mm(qT_ref[lanes, rows], ds)
                dck_ref[hh:hh + 1, :] -= jnp.sum(ds, axis=0, keepdims=True)
                dcq_ref[rows, hh:hh + 1] += jnp.sum(ds, axis=1, keepdims=True)
                dq_ref[rows, lanes] += _mm(ds, kh)

        for jj in range(per):
            chunk(ki * per + jj, True)

        def full_chunk(jq, carry):
            chunk(jq, False)
            return carry

        lax.fori_loop((ki + 1) * per, nchunk, full_chunk, 0)

    return pl.pallas_call(
        body, name="flash_bwd", grid=(G, S // tk),
        in_specs=[pl.BlockSpec((S, 128), lambda g, j: (0, g)),
                  pl.BlockSpec((128, S), lambda g, j: (g, 0)),
                  pl.BlockSpec((tk, 128), lambda g, j: (j, g)),
                  pl.BlockSpec((128, tk), lambda g, j: (g, j)),
                  pl.BlockSpec((128, tk), lambda g, j: (g, j)),
                  pl.BlockSpec((None, hg, tk), lambda g, j: (g, 0, j)),
                  pl.BlockSpec((S, 128), lambda g, j: (0, g)),
                  pl.BlockSpec((128, S), lambda g, j: (g, 0)),
                  pl.BlockSpec((None, S, 3 * hg), lambda g, j: (g, 0, 0))],
        out_specs=[pl.BlockSpec((S, 128), lambda g, j: (0, g)),
                   pl.BlockSpec((128, tk), lambda g, j: (g, j)),
                   pl.BlockSpec((128, tk), lambda g, j: (g, j)),
                   pl.BlockSpec((None, 8, tk), lambda g, j: (g, 0, j)),
                   pl.BlockSpec((None, S, hg), lambda g, j: (g, 0, 0))],
        out_shape=[jax.ShapeDtypeStruct((S, D), F32), jax.ShapeDtypeStruct((D, S), F32),
                   jax.ShapeDtypeStruct((D, S), F32), jax.ShapeDtypeStruct((G, 8, S), F32),
                   jax.ShapeDtypeStruct((G, S, hg), F32)],
        compiler_params=_params(2),
    )(q, qT, k, kT, vT, c_row, do, doT, qstat)


def q_bwd(d, dq, h, g, wq, scale, tm):
    S, D = h.shape
    nt = S // tm

    def body(d_ref, dq_ref, h_ref, g_ref, wq_ref, di_ref, dqs_ref, dg_ref, dg_acc):
        i = pl.program_id(0)

        @pl.when(i == 0)
        def _():
            dg_acc[...] = jnp.zeros_like(dg_acc)

        x = h_ref[...]
        dqs = dq_ref[...] * scale
        dqs_ref[...] = dqs.astype(dqs_ref.dtype)
        r = lax.rsqrt(jnp.mean(x * x, axis=-1, keepdims=True) + EPS)
        dx, dgr = _rms_bwd(x, r, g_ref[...], _mm_nt(dqs, wq_ref[...]))
        di_ref[...] = d_ref[...] + dx
        dg_acc[...] += _colsum8(dgr)

        @pl.when(i == nt - 1)
        def _():
            dg_ref[...] = jnp.sum(dg_acc[...], axis=0, keepdims=True)

    return pl.pallas_call(
        body, name="q_bwd", grid=(nt,),
        in_specs=[_rows(tm, D), _rows(tm, D), _rows(tm, D), _whole((1, D)), _whole(wq.shape)],
        out_specs=[_rows(tm, D), _rows(tm, D), _whole((1, D))],
        out_shape=[jax.ShapeDtypeStruct((S, D), F32), jax.ShapeDtypeStruct((S, D), ACT_DTYPE),
                   jax.ShapeDtypeStruct((1, D), F32)],
        scratch_shapes=[pltpu.VMEM((8, D), F32)],
        compiler_params=_params(1),
    )(d, dq, h, g, wq)


def kv_bwd(d, dks, dvs, dcs, fl, h, g, wk, wv, wf, tm):
    S, D = h.shape
    H = wf.shape[1]
    nt = S // tm
    nl = len(dks)
    nc = len(dcs)

    def body(*refs):
        d_ref = refs[0]
        dk_refs = refs[1:1 + nl]
        dv_refs = refs[1 + nl:1 + 2 * nl]
        dc_refs = refs[1 + 2 * nl:1 + 2 * nl + nc]
        (fl_ref, h_ref, g_ref, wk_ref, wv_ref, wf_ref,
         di_ref, dk_ref, dv_ref, dfl_ref, dg_ref, dbf_ref, dg_acc, dbf_acc, carry) = refs[1 + 2 * nl + nc:]
        i = pl.program_id(0)

        @pl.when(i == 0)
        def _():
            dg_acc[...] = jnp.zeros_like(dg_acc)
            dbf_acc[...] = jnp.zeros_like(dbf_acc)
            carry[...] = jnp.zeros_like(carry)

        dkT = dk_refs[0][...]
        dvT = dv_refs[0][...]
        for l in range(1, nl):
            dkT = dkT + dk_refs[l][...]
            dvT = dvT + dv_refs[l][...]
        dk = dkT.T
        dv = dvT.T
        dk_ref[...] = dk.astype(dk_ref.dtype)
        dv_ref[...] = dv.astype(dv_ref.dtype)
        row = lax.broadcasted_iota(jnp.int32, (tm, tm), 0)
        col = lax.broadcasted_iota(jnp.int32, (tm, tm), 1)
        tri = (col >= row).astype(MXU_DTYPE)
        dc = dc_refs[0][...]
        for l in range(1, nc):
            dc = dc + dc_refs[l][...]
        dlogf = _tri_mm(tri, dc) + carry[...]
        carry[...] = dlogf[0:1, :]
        dfl = dlogf * _sigmoid(-fl_ref[...])
        dfl_ref[...] = dfl
        dbf_acc[...] += jnp.sum(dfl, axis=0, keepdims=True)
        x = h_ref[...]
        dn = _mm_nt(dk, wk_ref[...]) + _mm_nt(dv, wv_ref[...]) + _mm_nt(dfl, wf_ref[...])
        r = lax.rsqrt(jnp.mean(x * x, axis=-1, keepdims=True) + EPS)
        dx, dgr = _rms_bwd(x, r, g_ref[...], dn)
        di_ref[...] = d_ref[...] + dx
        dg_acc[...] += _colsum8(dgr)

        @pl.when(i == nt - 1)
        def _():
            dg_ref[...] = jnp.sum(dg_acc[...], axis=0, keepdims=True)
            dbf_ref[...] = dbf_acc[...]

    rev = lambda n: _rows_rev(tm, n, nt)
    return pl.pallas_call(
        body, name="kv_bwd", grid=(nt,),
        in_specs=([rev(D)] + [_cols_rev(D, tm, nt)] * (2 * nl)
                  + [rev(H)] * nc
                  + [rev(H), rev(D), _whole((1, D)), _whole(wk.shape), _whole(wv.shape), _whole(wf.shape)]),
        out_specs=[rev(D), rev(D), rev(D), rev(H), _whole((1, D)), _whole((1, H))],
        out_shape=[jax.ShapeDtypeStruct((S, D), F32), jax.ShapeDtypeStruct((S, D), ACT_DTYPE),
                   jax.ShapeDtypeStruct((S, D), ACT_DTYPE), jax.ShapeDtypeStruct((S, H), F32),
                   jax.ShapeDtypeStruct((1, D), F32), jax.ShapeDtypeStruct((1, H), F32)],
        scratch_shapes=[pltpu.VMEM((8, D), F32), pltpu.VMEM((1, H), F32), pltpu.VMEM((1, H), F32)],
        compiler_params=_params(1),
    )(d, *dks, *dvs, *dcs, fl, h, g, wk, wv, wf)


def conv_bwd(d, h, g, w1, wd, lg, lb, w2, u, z, tm):
    S, D = h.shape
    CW = wd.shape[0]
    nt = S // tm
    assert tm >= HALO and CW - 1 <= HALO

    def body(d_ref, h_ref, g_ref, w1_ref, wd_ref, lg_ref, lb_ref, w2_ref, u_ref, z_ref,
             di_ref, du_ref, db2_ref, dlg_ref, dlb_ref, dbd_ref, dwd_ref, db1_ref, dg_ref,
             ext, win, db2_acc, dlg_acc, dlb_acc, dbd_acc, dwd_acc, db1_acc, dg_acc):
        i = pl.program_id(0)

        @pl.when(i == 0)
        def _():
            ext[tm:tm + HALO, :] = jnp.zeros((HALO, D), F32)
            for acc in (db2_acc, dlg_acc, dlb_acc, dbd_acc, dwd_acc, db1_acc, dg_acc):
                acc[...] = jnp.zeros_like(acc)

        dd = d_ref[...]
        db2_acc[...] += _colsum8(dd)
        dsw = _mm_nt(dd, w2_ref[...])
        zz = z_ref[...]
        zc = zz - jnp.mean(zz, axis=-1, keepdims=True)
        rs = lax.rsqrt(jnp.mean(zc * zc, axis=-1, keepdims=True) + EPS)
        xh = zc * rs
        lgv = lg_ref[...]
        y = xh * lgv + lb_ref[...]
        sg = _sigmoid(y)
        dy = dsw * (sg * (1.0 + y * (1.0 - sg)))
        dlg_acc[...] += _colsum8(dy * xh)
        dlb_acc[...] += _colsum8(dy)
        dxh = dy * lgv
        dz = rs * (dxh - jnp.mean(dxh, axis=-1, keepdims=True) - xh * jnp.mean(dxh * xh, axis=-1, keepdims=True))
        dbd_acc[...] += _colsum8(dz)
        ext[0:tm, :] = dz
        uu = u_ref[...]
        a = uu[:, :D]
        sgg = _sigmoid(uu[:, D:])
        glu = a * sgg
        dglu = jnp.zeros((tm, D), F32)
        for b in range(8):
            amax = (CW - 1 - b) // 8
            win[0:tm + 8 * amax, :] = ext[b:b + tm + 8 * amax, :]
            for a8 in range(amax + 1):
                k = CW - 1 - (8 * a8 + b)
                sh = win[8 * a8:8 * a8 + tm, :]
                dglu = dglu + wd_ref[k:k + 1, :] * sh
                dwd_acc[k] += _colsum8(glu * sh)
        ext[tm:tm + HALO, :] = ext[0:HALO, :]
        da = dglu * sgg
        dgg = dglu * a * sgg * (1.0 - sgg)
        du_ref[:, :D] = da.astype(du_ref.dtype)
        du_ref[:, D:] = dgg.astype(du_ref.dtype)
        db1_acc[:, :D] += _colsum8(da)
        db1_acc[:, D:] += _colsum8(dgg)
        dn = _mm_nt(da, w1_ref[:, :D]) + _mm_nt(dgg, w1_ref[:, D:])
        x = h_ref[...]
        r = lax.rsqrt(jnp.mean(x * x, axis=-1, keepdims=True) + EPS)
        dx, dgr = _rms_bwd(x, r, g_ref[...], dn)
        di_ref[...] = dd + dx
        dg_acc[...] += _colsum8(dgr)

        @pl.when(i == nt - 1)
        def _():
            db2_ref[...] = jnp.sum(db2_acc[...], axis=0, keepdims=True)
            dlg_ref[...] = jnp.sum(dlg_acc[...], axis=0, keepdims=True)
            dlb_ref[...] = jnp.sum(dlb_acc[...], axis=0, keepdims=True)
            dbd_ref[...] = jnp.sum(dbd_acc[...], axis=0, keepdims=True)
            dwd_ref[...] = jnp.sum(dwd_acc[...], axis=1)
            db1_ref[...] = jnp.sum(db1_acc[...], axis=0, keepdims=True)
            dg_ref[...] = jnp.sum(dg_acc[...], axis=0, keepdims=True)

    rev = lambda n: _rows_rev(tm, n, nt)
    vec = jax.ShapeDtypeStruct((1, D), F32)
    return pl.pallas_call(
        body, name="conv_bwd", grid=(nt,),
        in_specs=[rev(D), rev(D), _whole((1, D)), _whole(w1.shape), _whole(wd.shape), _whole((1, D)),
                  _whole((1, D)), _whole(w2.shape), rev(2 * D), rev(D)],
        out_specs=[rev(D), rev(2 * D), _whole((1, D)), _whole((1, D)), _whole((1, D)), _whole((1, D)),
                   _whole((CW, D)), _whole((1, 2 * D)), _whole((1, D))],
        out_shape=[jax.ShapeDtypeStruct((S, D), F32), jax.ShapeDtypeStruct((S, 2 * D), ACT_DTYPE),
                   vec, vec, vec, vec, jax.ShapeDtypeStruct((CW, D), F32),
                   jax.ShapeDtypeStruct((1, 2 * D), F32), vec],
        scratch_shapes=[pltpu.VMEM((tm + HALO, D), F32), pltpu.VMEM((tm + HALO, D), F32),
                        pltpu.VMEM((8, D), F32), pltpu.VMEM((8, D), F32),
                        pltpu.VMEM((8, D), F32), pltpu.VMEM((8, D), F32), pltpu.VMEM((CW, 8, D), F32),
                        pltpu.VMEM((8, 2 * D), F32), pltpu.VMEM((8, D), F32)],
        compiler_params=_params(1),
    )(d, h, g, w1, wd, lg, lb, w2, u, z)


def weight_grad(a, b, ts, name, column_shards=None):
    S, M = a.shape
    N = b.shape[1]
    ta = M if M <= 1024 else 1024
    tb = (N if N <= 1024 else 1024) if column_shards is None else N // column_shards
    assert M % ta == 0 and N % tb == 0 and S % ts == 0
    if column_shards is None:
        out_spec = pl.BlockSpec((ta, tb), lambda i, j, s: (i, j))
        out_shape = jax.ShapeDtypeStruct((M, N), F32)
    else:
        out_spec = pl.BlockSpec((None, ta, tb), lambda i, j, s: (j, i, 0))
        out_shape = jax.ShapeDtypeStruct((column_shards, M, tb), F32)

    def body(a_ref, b_ref, o_ref):
        @pl.when(pl.program_id(2) == 0)
        def _():
            o_ref[...] = jnp.zeros_like(o_ref)

        o_ref[...] += _mm_tn(a_ref[...], b_ref[...])

    return pl.pallas_call(
        body, name=name, grid=(M // ta, N // tb, S // ts),
        in_specs=[pl.BlockSpec((ts, ta), lambda i, j, s: (s, i)), pl.BlockSpec((ts, tb), lambda i, j, s: (s, j))],
        out_specs=out_spec, out_shape=out_shape,
        compiler_params=_params(3),
    )(a, b)


def _position():
    return lax.axis_index("x"), lax.axis_index("y"), lax.axis_index("c")


def all_gather(x, name):
    def body(x_ref, out_ref, send_sems, recv_sems, local_sem):
        x, y, c = _position()
        me, sibling = (x, y, c), (x, y, 1 - c)
        chips = [(1 - x, y), (x, 1 - y), (1 - x, 1 - y)]

        def slot(px, py, pc):
            return out_ref.at[4 * px + 2 * py + pc]

        def copy(k, block, to, src=None):
            return pltpu.make_async_remote_copy(
                src_ref=slot(*block) if src is None else src, dst_ref=slot(*block),
                send_sem=send_sems.at[k], recv_sem=recv_sems.at[k], device_id=to, device_id_type=MESH)

        mine = pltpu.make_async_copy(x_ref, slot(*me), local_sem)
        mine.start()
        first = [copy(0, me, sibling, src=x_ref)]
        first += [copy(1 + j, me, (*chip, c), src=x_ref) for j, chip in enumerate(chips)]
        for cp in first:
            cp.start()
        passed = [copy(4 + j, (*chip, c), sibling) for j, chip in enumerate(chips)]
        for j, chip in enumerate(chips):
            copy(1 + j, (*chip, c), me).wait_recv()
            passed[j].start()
        copy(0, sibling, me).wait_recv()
        for j, chip in enumerate(chips):
            copy(4 + j, (*chip, 1 - c), me).wait_recv()
        for cp in first + passed:
            cp.wait_send()
        mine.wait()

    return pl.pallas_call(
        body, name=name,
        in_specs=[pl.BlockSpec(memory_space=pl.ANY)], out_specs=pl.BlockSpec(memory_space=pl.ANY),
        out_shape=jax.ShapeDtypeStruct((N_DEV,) + x.shape, x.dtype),
        scratch_shapes=[pltpu.SemaphoreType.DMA((7,)), pltpu.SemaphoreType.DMA((7,)), pltpu.SemaphoreType.DMA],
    )(x)


def sibling_exchange(gs):
    n = len(gs)

    def body(*refs):
        g_refs, land_refs, (send_sems, recv_sems) = refs[:n], refs[n:2 * n], refs[2 * n:]
        x, y, c = _position()
        copies = [pltpu.make_async_remote_copy(
            src_ref=g_refs[a].at[2 * j + 1 - c], dst_ref=land_refs[a].at[j], send_sem=send_sems.at[N_CHIP * a + j],
            recv_sem=recv_sems.at[N_CHIP * a + j], device_id=(x, y, 1 - c), device_id_type=MESH)
            for a in range(n) for j in range(N_CHIP)]
        for cp in copies:
            cp.start()
        for cp in copies:
            cp.wait()

    return pl.pallas_call(
        body, name="grad_sibling_exchange",
        in_specs=[pl.BlockSpec(memory_space=pl.ANY)] * n, out_specs=[pl.BlockSpec(memory_space=pl.ANY)] * n,
        out_shape=[jax.ShapeDtypeStruct((N_CHIP,) + g.shape[1:], g.dtype) for g in gs],
        scratch_shapes=[pltpu.SemaphoreType.DMA((N_CHIP * n,)), pltpu.SemaphoreType.DMA((N_CHIP * n,))],
    )(*gs)


def chip_partial(g, land, core):
    _, R, C = g.shape
    tr = _row_tile(R, PACK_ROW_TILE)

    def body(c_ref, g_ref, l_ref, o_ref):
        o_ref[...] = (g_ref[...] + l_ref[...]).astype(o_ref.dtype)

    grid_spec = pltpu.PrefetchScalarGridSpec(
        num_scalar_prefetch=1, grid=(N_CHIP, R // tr),
        in_specs=[pl.BlockSpec((None, tr, C), lambda j, i, cr: (2 * j + cr[0], i, 0)),
                  pl.BlockSpec((None, tr, C), lambda j, i, cr: (j, i, 0))],
        out_specs=pl.BlockSpec((None, tr, C), lambda j, i, cr: (j, i, 0)))
    return pl.pallas_call(
        body, name="grad_chip_partial", grid_spec=grid_spec,
        out_shape=jax.ShapeDtypeStruct((N_CHIP, R, C), WIRE_DTYPE),
        compiler_params=_params(2),
    )(core, g, land)


def chip_exchange(parts):
    n = len(parts)

    def body(*refs):
        p_refs, land_refs, (send_sems, recv_sems, local_sems) = refs[:n], refs[n:2 * n], refs[2 * n:]
        x, y, c = _position()
        mychip = 2 * x + y
        chips = [(1 - x, y), (x, 1 - y), (1 - x, 1 - y)]

        def remote(a, k, slot):
            cx, cy = chips[k]
            return pltpu.make_async_remote_copy(
                src_ref=p_refs[a].at[2 * cx + cy], dst_ref=land_refs[a].at[slot], send_sem=send_sems.at[3 * a + k],
                recv_sem=recv_sems.at[3 * a + k], device_id=(cx, cy, c), device_id_type=MESH)

        mine = [pltpu.make_async_copy(p_refs[a].at[mychip], land_refs[a].at[mychip], local_sems.at[a])
                for a in range(n)]
        for cp in mine:
            cp.start()
        copies = [remote(a, k, mychip) for a in range(n) for k in range(3)]
        for cp in copies:
            cp.start()
        for a in range(n):
            for k, (cx, cy) in enumerate(chips):
                remote(a, k, 2 * cx + cy).wait_recv()
        for cp in copies:
            cp.wait_send()
        for cp in mine:
            cp.wait()

    return pl.pallas_call(
        body, name="grad_chip_exchange",
        in_specs=[pl.BlockSpec(memory_space=pl.ANY)] * n, out_specs=[pl.BlockSpec(memory_space=pl.ANY)] * n,
        out_shape=[jax.ShapeDtypeStruct(p.shape, p.dtype) for p in parts],
        scratch_shapes=[pltpu.SemaphoreType.DMA((3 * n,)), pltpu.SemaphoreType.DMA((3 * n,)),
                        pltpu.SemaphoreType.DMA((n,))],
    )(*parts)


def _adamw(w, g, m, v):
    m = ADAM_B1 * m + (1.0 - ADAM_B1) * g
    v = ADAM_B2 * v + (1.0 - ADAM_B2) * jnp.square(g)
    m_hat = m / (1.0 - ADAM_B1 ** ADAM_STEP)
    v_hat = v / (1.0 - ADAM_B2 ** ADAM_STEP)
    delta = -ADAM_LR * (m_hat / (jnp.sqrt(v_hat) + ADAM_EPS) + ADAM_WD * w)
    return delta, m, v


def adamw_sharded(parts, w, m, v):
    R, C = w.shape
    tr = _row_tile(R, PACK_ROW_TILE)

    def body(p_ref, w_ref, m_ref, v_ref, g_ref, d_ref, nm_ref, nv_ref):
        g = p_ref[0].astype(F32)
        for j in range(1, N_CHIP):
            g = g + p_ref[j].astype(F32)
        g_ref[...] = g
        d_ref[...], nm_ref[...], nv_ref[...] = _adamw(w_ref[...], g, m_ref[...], v_ref[...])

    out = jax.ShapeDtypeStruct((R, C), F32)
    return pl.pallas_call(
        body, name="adamw_sharded", grid=(R // tr,),
        in_specs=[pl.BlockSpec((N_CHIP, tr, C), lambda i: (0, i, 0)), _rows(tr, C), _rows(tr, C), _rows(tr, C)],
        out_specs=[_rows(tr, C)] * 4, out_shape=[out] * 4,
        compiler_params=_params(1),
    )(parts, w, m, v)


def adamw_replicated(gathered, w, m, v):
    R, C = w.shape

    def body(p_ref, w_ref, m_ref, v_ref, g_ref, d_ref, nm_ref, nv_ref):
        g = p_ref[0]
        for j in range(1, N_DEV):
            g = g + p_ref[j]
        g_ref[...] = g
        d_ref[...], nm_ref[...], nv_ref[...] = _adamw(w_ref[...], g, m_ref[...], v_ref[...])

    out = jax.ShapeDtypeStruct((R, C), F32)
    return pl.pallas_call(
        body, name="adamw_replicated", grid=(1,),
        in_specs=[_whole(gathered.shape), _whole((R, C)), _whole((R, C)), _whole((R, C))],
        out_specs=[_whole((R, C))] * 4, out_shape=[out] * 4,
        compiler_params=_params(1),
    )(gathered, w, m, v)


def _piece_rows(n):
    return -(-n // PACK_COLS)


def _as_rows(a, lead):
    flat = a.reshape(a.shape[:lead] + (-1,))
    fill = _piece_rows(flat.shape[-1]) * PACK_COLS - flat.shape[-1]
    if fill:
        flat = jnp.pad(flat, [(0, 0)] * lead + [(0, fill)])
    return flat.reshape(flat.shape[:-1] + (-1, PACK_COLS))


def _pack(arrays, rows_multiple, dtype=None, lead=0):
    pieces = [_as_rows(a if dtype is None else a.astype(dtype), lead) for a in arrays]
    extra = -sum(p.shape[lead] for p in pieces) % rows_multiple
    if extra:
        pieces.append(jnp.zeros(pieces[0].shape[:lead] + (extra, PACK_COLS), pieces[0].dtype))
    return jnp.concatenate(pieces, axis=lead)


def _unpack(packed, shapes, lead=0):
    out, r0 = [], 0
    for shp in shapes:
        n = int(np.prod(shp))
        rows = _piece_rows(n)
        seg = lax.slice_in_dim(packed, r0, r0 + rows, axis=lead).reshape(packed.shape[:lead] + (-1,))
        if rows * PACK_COLS != n:
            seg = seg[..., :n]
        out.append(seg.reshape(packed.shape[:lead] + tuple(shp)))
        r0 += rows
    return out


def _full_from_gathered(gathered, shard_shapes, axes):
    out = []
    for seg, shp, ax in zip(_unpack(gathered, shard_shapes, lead=1), shard_shapes, axes):
        seg = jnp.moveaxis(seg, 0, ax)
        out.append(seg.reshape(tuple(shp[:ax]) + (N_DEV * shp[ax],) + tuple(shp[ax + 1:])))
    return out


def kernel(x, p, mix_norm, conv_w_pw1, conv_b_pw1, conv_w_dw, conv_b_dw, conv_ln_g, conv_ln_b, conv_w_pw2, conv_b_pw2, kv_norm, w_kvf, b_f, attn_w_q, attn_w_o, ffn_norm, ffn_w1, ffn_w2, ple_norm, ple_w_gate, ple_w_proj, final_norm, loss_target, m_mix_norm, m_conv_w_pw1, m_conv_b_pw1, m_conv_w_dw, m_conv_b_dw, m_conv_ln_g, m_conv_ln_b, m_conv_w_pw2, m_conv_b_pw2, m_kv_norm, m_w_kvf, m_b_f, m_attn_w_q, m_attn_w_o, m_ffn_norm, m_ffn_w1, m_ffn_w2, m_ple_norm, m_ple_w_gate, m_ple_w_proj, m_final_norm, v_mix_norm, v_conv_w_pw1, v_conv_b_pw1, v_conv_w_dw, v_conv_b_dw, v_conv_ln_g, v_conv_ln_b, v_conv_w_pw2, v_conv_b_pw2, v_kv_norm, v_w_kvf, v_b_f, v_attn_w_q, v_attn_w_o, v_ffn_norm, v_ffn_w1, v_ffn_w2, v_ple_norm, v_ple_w_gate, v_ple_w_proj, v_final_norm):
    given = dict(locals())
    W = {n: given[n] for n in WEIGHTS}
    M = {n: given["m_" + n] for n in WEIGHTS}
    V = {n: given["v_" + n] for n in WEIGHTS}

    _, S, D = x.shape
    NA = conv_w_pw1.shape[0]
    NB = attn_w_q.shape[0]
    DEPTH = NA + NB
    H = b_f.shape[0]
    dh = D // H
    hg = 128 // dh
    G = D // 128
    scale = dh ** -0.5
    tm = _row_tile(S, 256)
    tq_f = _row_tile(S, FLASH_FWD_TILE[0])
    tkc_f = _row_tile(tq_f, FLASH_FWD_TILE[1])
    tk_b = _row_tile(S, FLASH_BWD_TILE[0])
    tqc_b = _row_tile(tk_b, FLASH_BWD_TILE[1])
    ts = _row_tile(S, 512)
    xs = x[0]
    tgt = loss_target[0]
    ps = p[:, 0]
    row = lambda a: a.reshape(1, -1)

    big_names = list(SHARD_AXIS_BIG)
    small_names = list(SHARD_AXIS_SMALL)
    big = _full_from_gathered(
        all_gather(_pack([W[n] for n in big_names], 16, MXU_DTYPE), "weights_all_gather"),
        [W[n].shape for n in big_names], [SHARD_AXIS_BIG[n] for n in big_names])
    small = _full_from_gathered(
        all_gather(_pack([W[n] for n in small_names], 8), "vectors_all_gather"),
        [W[n].shape for n in small_names], [SHARD_AXIS_SMALL[n] for n in small_names])
    FW = dict(zip(big_names + small_names, big + small))
    wk, wv, wf = FW["w_kvf"][:, :D], FW["w_kvf"][:, D:2 * D], FW["w_kvf"][:, 2 * D:]

    saved = []
    h = xs
    kv = None
    for i in range(DEPTH):
        rec = {"h_in": h}
        if i < NA:
            h, rec["n"], rec["u"], rec["z"], rec["sw"] = conv_fwd(
                h, row(mix_norm[i]), FW["conv_w_pw1"][i], row(FW["conv_b_pw1"][i]), FW["conv_w_dw"][i],
                row(FW["conv_b_dw"][i]), row(FW["conv_ln_g"][i]), row(FW["conv_ln_b"][i]),
                FW["conv_w_pw2"][i], row(FW["conv_b_pw2"][i]), tm)
        else:
            j = i - NA
            if j == 0:
                k_, kT_, vT_, nkv, fl, c = kv_fwd(h, row(kv_norm), wk, wv, wf, row(b_f), tm)
                cg = c.reshape(S, G, hg)
                kv = dict(k=k_, kT=kT_, vT=vT_, n=nkv, fl=fl, h=h, c_col=jnp.transpose(cg, (1, 0, 2)),
                          c_row=jnp.transpose(cg, (1, 2, 0)))
            rec["n"], rec["q"], rec["qT"] = q_fwd(h, row(mix_norm[i]), FW["attn_w_q"][j], scale, tm)
            rec["o"], rec["o32"], rec["lse"] = flash_fwd(rec["qT"], kv["k"], kv["vT"], kv["c_col"], kv["c_row"], dh,
                                                         tq_f, tkc_f)
            h = attn_out_fwd(h, rec["o"], FW["attn_w_o"][j], tm)
        rec["h_ffn"] = h
        h, rec["n_ffn"], rec["a"], rec["s"] = ffn_fwd(h, row(ffn_norm[i]), FW["ffn_w1"][i], FW["ffn_w2"][i], tm)
        rec["h_ple"] = h
        h, rec["n_ple"], rec["gate"] = ple_fwd(h, row(ple_norm[i]), FW["ple_w_gate"][i], ps[i],
                                               FW["ple_w_proj"][i], tm)
        saved.append(rec)

    d, g_final, loss_part = loss_head(h, row(final_norm), tgt, tm)
    GW = {n: [None] * W[n].shape[0] for n in WEIGHTS if W[n].ndim > 1 and n != "w_kvf"}
    dks, dvs, dcs = [], [], []
    for i in reversed(range(DEPTH)):
        rec = saved[i]
        d_out = d
        d, dz, dpp, GW["ple_norm"][i] = ple_bwd(d_out, rec["h_ple"], row(ple_norm[i]), FW["ple_w_gate"][i],
                                                rec["gate"], ps[i], FW["ple_w_proj"][i], tm)
        GW["ple_w_gate"][i] = weight_grad(rec["n_ple"], dz, ts, "grad_ple_w_gate")
        GW["ple_w_proj"][i] = weight_grad(ps[i], dpp, ts, "grad_ple_w_proj", column_shards=N_DEV)
        d_out = d
        d, da, GW["ffn_norm"][i] = ffn_bwd(d_out, rec["h_ffn"], row(ffn_norm[i]), FW["ffn_w1"][i],
                                           FW["ffn_w2"][i], rec["a"], tm)
        GW["ffn_w2"][i] = weight_grad(rec["s"], d_out, ts, "grad_ffn_w2")
        GW["ffn_w1"][i] = weight_grad(rec["n_ffn"], da, ts, "grad_ffn_w1", column_shards=N_DEV)
        d_out = d
        if i >= NA:
            j = i - NA
            GW["attn_w_o"][j] = weight_grad(rec["o"], d_out, ts, "grad_attn_w_o")
            do, doT, delta = attn_out_bwd(d_out, FW["attn_w_o"][j], rec["o32"], dh, tm)
            qstat = jnp.concatenate([kv["c_col"], jnp.transpose(rec["lse"], (0, 2, 1)),
                                     jnp.transpose(delta.reshape(S, G, hg), (1, 0, 2))], axis=2)
            dq, dkT, dvT, dck, dcq = flash_bwd(rec["q"], rec["qT"], kv["k"], kv["kT"], kv["vT"], kv["c_row"], do, doT,
                                          qstat, dh, tk_b, tqc_b)
            dks.append(dkT)
            dvs.append(dvT)
            dcs.append(jnp.transpose(dck[:, :hg, :], (2, 0, 1)).reshape(S, H))
            dcs.append(jnp.transpose(dcq, (1, 0, 2)).reshape(S, H))
            d, dqs, GW["mix_norm"][i] = q_bwd(d_out, dq, rec["h_in"], row(mix_norm[i]), FW["attn_w_q"][j], scale, tm)
            GW["attn_w_q"][j] = weight_grad(rec["n"], dqs, ts, "grad_attn_w_q")
            if j == 0:
                d, dk_sum, dv_sum, dfl, g_kv_norm, g_b_f = kv_bwd(d, dks, dvs, dcs, kv["fl"], kv["h"],
                                                                  row(kv_norm), wk, wv, wf, tm)
                g_w_kvf = jnp.concatenate([weight_grad(kv["n"], dk_sum, ts, "grad_w_k"),
                                           weight_grad(kv["n"], dv_sum, ts, "grad_w_v"),
                                           weight_grad(kv["n"], dfl, ts, "grad_w_f")], axis=1)
        else:
            GW["conv_w_pw2"][i] = weight_grad(rec["sw"], d_out, ts, "grad_conv_w_pw2")
            (d, du, GW["conv_b_pw2"][i], GW["conv_ln_g"][i], GW["conv_ln_b"][i], GW["conv_b_dw"][i],
             GW["conv_w_dw"][i], GW["conv_b_pw1"][i], GW["mix_norm"][i]) = conv_bwd(
                d_out, rec["h_in"], row(mix_norm[i]), FW["conv_w_pw1"][i], FW["conv_w_dw"][i],
                row(FW["conv_ln_g"][i]), row(FW["conv_ln_b"][i]), FW["conv_w_pw2"][i], rec["u"], rec["z"], tm)
            GW["conv_w_pw1"][i] = weight_grad(rec["n"], du, ts, "grad_conv_w_pw1", column_shards=N_DEV)
    grad_x = d[None]

    sharded_names = big_names + small_names
    GW["w_kvf"] = [g_w_kvf]

    def device_major(n, g):
        width = W[n].shape[-1]
        if g.ndim == 3:
            return g
        if g.shape[-1] == width:
            return g.reshape(N_DEV, -1, width)
        return jnp.transpose(g.reshape(-1, N_DEV, width), (1, 0, 2))

    widths = sorted({W[n].shape[-1] for n in sharded_names}, reverse=True)
    groups = [[n for n in sharded_names if W[n].shape[-1] == width] for width in widths]

    def stack_rows(pieces, axis):
        rows = sum(p.shape[axis] for p in pieces)
        fill = -rows % (PACK_ROW_TILE if rows > PACK_ROW_TILE else 8)
        if fill:
            shape = list(pieces[0].shape)
            shape[axis] = fill
            pieces = pieces + [jnp.zeros(shape, pieces[0].dtype)]
        return jnp.concatenate(pieces, axis=axis)

    chunks = [stack_rows([device_major(n, g) for n in names for g in GW[n]], 1) for names in groups]
    core = lax.axis_index("c").astype(jnp.int32).reshape(1)
    landed = sibling_exchange(chunks)
    parts = chip_exchange([chip_partial(g, l, core) for g, l in zip(chunks, landed)])

    res = {}
    for names, width, part in zip(groups, widths, parts):
        group_rows = lambda src: stack_rows([src[n].reshape(-1, width) for n in names], 0)
        outs = adamw_sharded(part, group_rows(W), group_rows(M), group_rows(V))
        for kind, packed in zip(("grad", "delta", "new_m", "new_v"), outs):
            r0 = 0
            for n in names:
                nr = W[n].size // width
                res[kind, n] = packed[r0:r0 + nr].reshape(W[n].shape)
                r0 += nr

    rep_grads = {"mix_norm": jnp.concatenate(GW["mix_norm"], axis=0), "kv_norm": g_kv_norm,
                 "b_f": g_b_f, "ffn_norm": jnp.concatenate(GW["ffn_norm"], axis=0),
                 "ple_norm": jnp.concatenate(GW["ple_norm"], axis=0), "final_norm": g_final}

    def pack_rep(src, extra=None):
        rows_ = [jnp.pad(src[n].reshape(-1, src[n].shape[-1]), ((0, 0), (0, D - src[n].shape[-1])))
                 for n in REPLICATED]
        if extra is not None:
            rows_.append(jnp.pad(extra, ((0, 0), (0, D - extra.shape[-1]))))
        else:
            rows_.append(jnp.zeros((1, D), F32))
        flat = jnp.concatenate(rows_, axis=0)
        return jnp.pad(flat, ((0, -flat.shape[0] % 8), (0, 0)))

    rep_g = all_gather(pack_rep(rep_grads, loss_part), "replicated_all_gather")
    outs_rep = adamw_replicated(rep_g, pack_rep(W), pack_rep(M), pack_rep(V))
    n_rep_rows = sum(int(np.prod(W[n].shape[:-1])) for n in REPLICATED)
    for kind, packed in zip(("grad", "delta", "new_m", "new_v"), outs_rep):
        r0 = 0
        for n in REPLICATED:
            nr = int(np.prod(W[n].shape[:-1]))
            res[kind, n] = packed[r0:r0 + nr, :W[n].shape[-1]].reshape(W[n].shape)
            r0 += nr
    loss = outs_rep[0][n_rep_rows, 0]

    return (loss, grad_x, *[res["grad", n] for n in WEIGHTS], *[res["delta", n] for n in WEIGHTS],
            *[res["new_m", n] for n in WEIGHTS], *[res["new_v", n] for n in WEIGHTS])
```

```python
import numpy as np
import jax
import jax.numpy as jnp
from jax import lax
from jax.experimental import pallas as pl
from jax.experimental.pallas import tpu as pltpu

F32 = jnp.float32
MXU_DTYPE = jnp.bfloat16
ACT_DTYPE = jnp.bfloat16
WIRE_DTYPE = jnp.bfloat16

N_DEV = 8
N_CHIP = 4
EPS = 1e-6
NEG_BIG = -1e30
ADAM_LR = 0.001
ADAM_B1 = 0.9
ADAM_B2 = 0.999
ADAM_EPS = 1e-08
ADAM_WD = 0.01
ADAM_STEP = 10

VMEM_LIMIT_BYTES = 56 * 1024 * 1024
PACK_COLS = 1024
PACK_ROW_TILE = 256
FLASH_FWD_TILE = (1024, 512)
FLASH_BWD_TILE = (1024, 512)
FLASH_TILE = 128
HALO = 32
MESH = pl.DeviceIdType.MESH

SHARD_AXIS_BIG = {"conv_w_pw1": 2, "conv_w_pw2": 1, "w_kvf": 1, "attn_w_q": 1, "attn_w_o": 1,
                  "ffn_w1": 2, "ffn_w2": 1, "ple_w_gate": 1, "ple_w_proj": 2}
SHARD_AXIS_SMALL = {"conv_b_pw1": 1, "conv_w_dw": 2, "conv_b_dw": 1, "conv_ln_g": 1, "conv_ln_b": 1,
                    "conv_b_pw2": 1}
REPLICATED = ["mix_norm", "kv_norm", "b_f", "ffn_norm", "ple_norm", "final_norm"]
WEIGHTS = ["mix_norm", "conv_w_pw1", "conv_b_pw1", "conv_w_dw", "conv_b_dw", "conv_ln_g", "conv_ln_b",
           "conv_w_pw2", "conv_b_pw2", "kv_norm", "w_kvf", "b_f", "attn_w_q", "attn_w_o", "ffn_norm",
           "ffn_w1", "ffn_w2", "ple_norm", "ple_w_gate", "ple_w_proj", "final_norm"]


def _mm(a, b):
    return jnp.dot(a.astype(MXU_DTYPE), b.astype(MXU_DTYPE), preferred_element_type=F32)


def _mm_nt(a, b):
    return lax.dot_general(a.astype(MXU_DTYPE), b.astype(MXU_DTYPE), (((1,), (1,)), ((), ())),
                           preferred_element_type=F32)


def _mm_tn(a, b):
    return lax.dot_general(a.astype(MXU_DTYPE), b.astype(MXU_DTYPE), (((0,), (0,)), ((), ())),
                           preferred_element_type=F32)


def _split3(x):
    hi = x.astype(MXU_DTYPE)
    r1 = x - hi.astype(F32)
    mid = r1.astype(MXU_DTYPE)
    lo = (r1 - mid.astype(F32)).astype(MXU_DTYPE)
    return hi, mid, lo


def _tri_mm(tri, x):
    hi, mid, lo = _split3(x)
    return (jnp.dot(tri, lo, preferred_element_type=F32) + jnp.dot(tri, mid, preferred_element_type=F32)
            + jnp.dot(tri, hi, preferred_element_type=F32))


def _colsum8(x):
    tm, n = x.shape
    return jnp.sum(x.reshape(tm // 8, 8, n), axis=0)


def _rms(x, g):
    r = lax.rsqrt(jnp.mean(x * x, axis=-1, keepdims=True) + EPS)
    return x * r * g, r


def _rms_bwd(x, r, g, dn):
    w = dn * g
    dx = r * w - x * (r * r * r) * jnp.mean(w * x, axis=-1, keepdims=True)
    return dx, dn * x * r


def _sigmoid(x):
    return jax.nn.sigmoid(x)


def _params(n_grid):
    return pltpu.CompilerParams(dimension_semantics=("arbitrary",) * n_grid, vmem_limit_bytes=VMEM_LIMIT_BYTES)


def _rows(tm, n):
    return pl.BlockSpec((tm, n), lambda i: (i, 0))


def _rows_rev(tm, n, nt):
    return pl.BlockSpec((tm, n), lambda i: (nt - 1 - i, 0))


def _cols(n, tm):
    return pl.BlockSpec((n, tm), lambda i: (0, i))


def _cols_rev(n, tm, nt):
    return pl.BlockSpec((n, tm), lambda i: (0, nt - 1 - i))


def _whole(shape):
    nd = len(shape)
    return pl.BlockSpec(shape, lambda i: (0,) * nd)


def _row_tile(s, want):
    tm = min(s, want)
    assert s % tm == 0 and tm % 8 == 0, (s, tm)
    return tm


def conv_fwd(h, g, w1, b1, wd, bd, lg, lb, w2, b2, tm):
    S, D = h.shape
    CW = wd.shape[0]
    off = HALO - (CW - 1)
    assert 0 <= off and tm >= HALO
    nt = S // tm

    def body(h_ref, g_ref, w1_ref, b1_ref, wd_ref, bd_ref, lg_ref, lb_ref, w2_ref, b2_ref,
             ho_ref, n_ref, u_ref, z_ref, sw_ref, ext, win):
        @pl.when(pl.program_id(0) == 0)
        def _():
            ext[0:HALO, :] = jnp.zeros((HALO, D), F32)

        x = h_ref[...]
        n, _ = _rms(x, g_ref[...])
        n_ref[...] = n.astype(n_ref.dtype)
        u = _mm(n, w1_ref[...]) + b1_ref[...]
        u_ref[...] = u
        ext[HALO:HALO + tm, :] = u[:, :D] * _sigmoid(u[:, D:])
        z = jnp.broadcast_to(bd_ref[...], (tm, D))
        for b in range(8):
            amax = (CW - 1 - b) // 8
            win[0:tm + 8 * amax, :] = ext[off + b:off + b + tm + 8 * amax, :]
            for a8 in range(amax + 1):
                z = z + wd_ref[8 * a8 + b:8 * a8 + b + 1, :] * win[8 * a8:8 * a8 + tm, :]
        z_ref[...] = z
        ext[0:HALO, :] = ext[tm:tm + HALO, :]
        mu = jnp.mean(z, axis=-1, keepdims=True)
        zc = z - mu
        y = zc * lax.rsqrt(jnp.mean(zc * zc, axis=-1, keepdims=True) + EPS) * lg_ref[...] + lb_ref[...]
        sw = y * _sigmoid(y)
        sw_ref[...] = sw.astype(sw_ref.dtype)
        ho_ref[...] = x + _mm(sw, w2_ref[...]) + b2_ref[...]

    return pl.pallas_call(
        body, name="conv_fwd", grid=(nt,),
        in_specs=[_rows(tm, D), _whole((1, D)), _whole(w1.shape), _whole((1, 2 * D)), _whole(wd.shape),
                  _whole((1, D)), _whole((1, D)), _whole((1, D)), _whole(w2.shape), _whole((1, D))],
        out_specs=[_rows(tm, D), _rows(tm, D), _rows(tm, 2 * D), _rows(tm, D), _rows(tm, D)],
        out_shape=[jax.ShapeDtypeStruct((S, D), F32), jax.ShapeDtypeStruct((S, D), ACT_DTYPE),
                   jax.ShapeDtypeStruct((S, 2 * D), F32), jax.ShapeDtypeStruct((S, D), F32),
                   jax.ShapeDtypeStruct((S, D), ACT_DTYPE)],
        scratch_shapes=[pltpu.VMEM((HALO + tm, D), F32), pltpu.VMEM((HALO + tm, D), F32)],
        compiler_params=_params(1),
    )(h, g, w1, b1, wd, bd, lg, lb, w2, b2)


def ffn_fwd(h, g, w1, w2, tm):
    S, D = h.shape
    FF = w1.shape[1]

    def body(h_ref, g_ref, w1_ref, w2_ref, ho_ref, n_ref, a_ref, s_ref):
        x = h_ref[...]
        n, _ = _rms(x, g_ref[...])
        n_ref[...] = n.astype(n_ref.dtype)
        a = _mm(n, w1_ref[...])
        a_ref[...] = a
        s = jnp.square(jnp.maximum(a, 0.0))
        s_ref[...] = s.astype(s_ref.dtype)
        ho_ref[...] = x + _mm(s, w2_ref[...])

    return pl.pallas_call(
        body, name="ffn_fwd", grid=(S // tm,),
        in_specs=[_rows(tm, D), _whole((1, D)), _whole(w1.shape), _whole(w2.shape)],
        out_specs=[_rows(tm, D), _rows(tm, D), _rows(tm, FF), _rows(tm, FF)],
        out_shape=[jax.ShapeDtypeStruct((S, D), F32), jax.ShapeDtypeStruct((S, D), ACT_DTYPE),
                   jax.ShapeDtypeStruct((S, FF), F32), jax.ShapeDtypeStruct((S, FF), ACT_DTYPE)],
        compiler_params=_params(1),
    )(h, g, w1, w2)


def ple_fwd(h, g, wg, p, wp, tm):
    S, D = h.shape
    E = p.shape[1]

    def body(h_ref, g_ref, wg_ref, p_ref, wp_ref, ho_ref, n_ref, gate_ref):
        x = h_ref[...]
        n, _ = _rms(x, g_ref[...])
        n_ref[...] = n.astype(n_ref.dtype)
        gate = _sigmoid(_mm(n, wg_ref[...]))
        gate_ref[...] = gate
        ho_ref[...] = x + gate * _mm(p_ref[...], wp_ref[...])

    return pl.pallas_call(
        body, name="ple_fwd", grid=(S // tm,),
        in_specs=[_rows(tm, D), _whole((1, D)), _whole(wg.shape), _rows(tm, E), _whole(wp.shape)],
        out_specs=[_rows(tm, D), _rows(tm, D), _rows(tm, D)],
        out_shape=[jax.ShapeDtypeStruct((S, D), F32), jax.ShapeDtypeStruct((S, D), ACT_DTYPE),
                   jax.ShapeDtypeStruct((S, D), F32)],
        compiler_params=_params(1),
    )(h, g, wg, p, wp)


def kv_fwd(h, g, wk, wv, wf, bf, tm):
    S, D = h.shape
    H = wf.shape[1]

    def body(h_ref, g_ref, wk_ref, wv_ref, wf_ref, bf_ref, k_ref, kT_ref, vT_ref, n_ref, fl_ref, c_ref, carry):
        @pl.when(pl.program_id(0) == 0)
        def _():
            carry[...] = jnp.zeros_like(carry)

        n, _ = _rms(h_ref[...], g_ref[...])
        n_ref[...] = n.astype(n_ref.dtype)
        k = _mm(n, wk_ref[...])
        k_ref[...] = k.astype(k_ref.dtype)
        kT_ref[...] = k.T.astype(kT_ref.dtype)
        vT_ref[...] = _mm(n, wv_ref[...]).T.astype(vT_ref.dtype)
        fl = _mm(n, wf_ref[...]) + bf_ref[...]
        fl_ref[...] = fl
        logf = jnp.minimum(fl, 0.0) - jnp.log1p(jnp.exp(-jnp.abs(fl)))
        row = lax.broadcasted_iota(jnp.int32, (tm, tm), 0)
        col = lax.broadcasted_iota(jnp.int32, (tm, tm), 1)
        tri = (row >= col).astype(MXU_DTYPE)
        c = _tri_mm(tri, logf) + carry[...]
        c_ref[...] = c
        carry[...] = c[tm - 1:tm, :]

    return pl.pallas_call(
        body, name="kv_fwd", grid=(S // tm,),
        in_specs=[_rows(tm, D), _whole((1, D)), _whole(wk.shape), _whole(wv.shape), _whole(wf.shape),
                  _whole((1, H))],
        out_specs=[_rows(tm, D), _cols(D, tm), _cols(D, tm), _rows(tm, D), _rows(tm, H), _rows(tm, H)],
        out_shape=[jax.ShapeDtypeStruct((S, D), ACT_DTYPE), jax.ShapeDtypeStruct((D, S), ACT_DTYPE),
                   jax.ShapeDtypeStruct((D, S), ACT_DTYPE), jax.ShapeDtypeStruct((S, D), ACT_DTYPE),
                   jax.ShapeDtypeStruct((S, H), F32), jax.ShapeDtypeStruct((S, H), F32)],
        scratch_shapes=[pltpu.VMEM((1, H), F32)],
        compiler_params=_params(1),
    )(h, g, wk, wv, wf, bf)


def q_fwd(h, g, wq, scale, tm):
    S, D = h.shape

    def body(h_ref, g_ref, wq_ref, n_ref, q_ref, qT_ref):
        n, _ = _rms(h_ref[...], g_ref[...])
        n_ref[...] = n.astype(n_ref.dtype)
        q = _mm(n, wq_ref[...]) * scale
        q_ref[...] = q.astype(q_ref.dtype)
        qT_ref[...] = q.T.astype(qT_ref.dtype)

    return pl.pallas_call(
        body, name="q_fwd", grid=(S // tm,),
        in_specs=[_rows(tm, D), _whole((1, D)), _whole(wq.shape)],
        out_specs=[_rows(tm, D), _rows(tm, D), _cols(D, tm)],
        out_shape=[jax.ShapeDtypeStruct((S, D), ACT_DTYPE), jax.ShapeDtypeStruct((S, D), ACT_DTYPE),
                   jax.ShapeDtypeStruct((D, S), ACT_DTYPE)],
        compiler_params=_params(1),
    )(h, g, wq)


def attn_out_fwd(h, o, wo, tm):
    S, D = h.shape

    def body(h_ref, o_ref, wo_ref, ho_ref):
        ho_ref[...] = h_ref[...] + _mm(o_ref[...], wo_ref[...])

    return pl.pallas_call(
        body, name="attn_out_fwd", grid=(S // tm,),
        in_specs=[_rows(tm, D), _rows(tm, D), _whole(wo.shape)],
        out_specs=_rows(tm, D),
        out_shape=jax.ShapeDtypeStruct((S, D), F32),
        compiler_params=_params(1),
    )(h, o, wo)


def _causal_mask(key0, qry0, shape, key_axis):
    key = key0 + lax.broadcasted_iota(jnp.int32, shape, key_axis)
    qry = qry0 + lax.broadcasted_iota(jnp.int32, shape, 1 - key_axis)
    return key <= qry


def flash_fwd(qT, k, vT, c_col, c_row, dh, tq, tkc):
    D, S = qT.shape
    hg = 128 // dh
    G = D // 128
    per = tq // tkc
    assert tq % tkc == 0 and S % tq == 0

    def body(qT_ref, k_ref, vT_ref, ccol_ref, crow_ref, o_ref, o32_ref, lse_ref, m_scr, l_scr, acc_scr, mx_scr,
             *scratch):
        i = pl.program_id(1)
        m_scr[...] = jnp.full(m_scr.shape, NEG_BIG, F32)
        l_scr[...] = jnp.zeros(l_scr.shape, F32)
        acc_scr[...] = jnp.zeros(acc_scr.shape, F32)

        def head_shift(j, hh):
            c0 = ccol_ref[pl.ds(j * tkc, 1), hh:hh + 1]
            return c0, crow_ref[hh:hh + 1, :] - c0

        def scores(j, masked, slot):
            keys = pl.ds(pl.multiple_of(j * tkc, tkc), tkc)
            s_scr = scratch[slot]
            for hh in range(hg):
                lanes = slice(hh * dh, (hh + 1) * dh)
                c0, r = head_shift(j, hh)
                s = _mm(k_ref[keys, lanes], qT_ref[lanes, :]) - (ccol_ref[keys, hh:hh + 1] - c0)
                if masked:
                    s = jnp.where(_causal_mask(j * tkc, i * tq, (tkc, tq), 0), s, NEG_BIG)
                s_scr[hh] = s
                mx_scr[slot * hg + hh] = jnp.max(s, axis=0, keepdims=True) + r

        def update(j, slot):
            keys = pl.ds(pl.multiple_of(j * tkc, tkc), tkc)
            s_scr = scratch[slot]
            for hh in range(hg):
                lanes = slice(hh * dh, (hh + 1) * dh)
                _, r = head_shift(j, hh)
                m_old = m_scr[hh]
                m_new = jnp.maximum(m_old, mx_scr[slot * hg + hh])
                alpha = jnp.exp(m_old - m_new)
                p = jnp.exp(s_scr[hh] - (m_new - r))
                l_scr[hh] = alpha * l_scr[hh] + jnp.sum(p, axis=0, keepdims=True)
                acc_scr[lanes, :] = alpha * acc_scr[lanes, :] + _mm(vT_ref[lanes, keys], p)
                m_scr[hh] = m_new

        def chunks(j0, masked):
            for jj in range(per):
                scores(j0 + jj, masked, jj)
            for jj in range(per):
                update(j0 + jj, jj)

        def full_chunks(jb, carry):
            chunks(jb * per, False)
            return carry

        lax.fori_loop(0, i, full_chunks, 0)
        chunks(i * per, True)
        for hh in range(hg):
            lanes = slice(hh * dh, (hh + 1) * dh)
            acc_scr[lanes, :] = acc_scr[lanes, :] / l_scr[hh]
            lse_ref[hh:hh + 1, :] = m_scr[hh] + jnp.log(l_scr[hh])
        o = acc_scr[...].T
        o_ref[...] = o.astype(o_ref.dtype)
        o32_ref[...] = o

    return pl.pallas_call(
        body, name="flash_fwd", grid=(G, S // tq),
        in_specs=[pl.BlockSpec((128, tq), lambda g, i: (g, i)),
                  pl.BlockSpec((S, 128), lambda g, i: (0, g)),
                  pl.BlockSpec((128, S), lambda g, i: (g, 0)),
                  pl.BlockSpec((None, S, hg), lambda g, i: (g, 0, 0)),
                  pl.BlockSpec((None, hg, tq), lambda g, i: (g, 0, i))],
        out_specs=[pl.BlockSpec((tq, 128), lambda g, i: (i, g)),
                   pl.BlockSpec((tq, 128), lambda g, i: (i, g)),
                   pl.BlockSpec((None, hg, tq), lambda g, i: (g, 0, i))],
        out_shape=[jax.ShapeDtypeStruct((S, D), ACT_DTYPE), jax.ShapeDtypeStruct((S, D), F32),
                   jax.ShapeDtypeStruct((G, hg, S), F32)],
        scratch_shapes=([pltpu.VMEM((hg, 1, tq), F32), pltpu.VMEM((hg, 1, tq), F32), pltpu.VMEM((128, tq), F32),
                         pltpu.VMEM((per * hg, 1, tq), F32)]
                        + [pltpu.VMEM((hg, tkc, tq), F32)] * per),
        compiler_params=_params(2),
    )(qT, k, vT, c_col, c_row)


def loss_head(h, g, target, tm):
    S, D = h.shape
    nt = S // tm

    def body(h_ref, g_ref, t_ref, dh_ref, dg_ref, loss_ref, dg_acc, loss_acc):
        i = pl.program_id(0)

        @pl.when(i == 0)
        def _():
            dg_acc[...] = jnp.zeros_like(dg_acc)
            loss_acc[...] = jnp.zeros_like(loss_acc)

        x = h_ref[...]
        gg = g_ref[...]
        y, r = _rms(x, gg)
        e = y - t_ref[...]
        loss_acc[...] += 0.5 * jnp.sum(jnp.mean(e * e, axis=-1, keepdims=True), axis=0, keepdims=True)
        dx, dgr = _rms_bwd(x, r, gg, e / D)
        dh_ref[...] = dx
        dg_acc[...] += _colsum8(dgr)

        @pl.when(i == nt - 1)
        def _():
            dg_ref[...] = jnp.sum(dg_acc[...], axis=0, keepdims=True)
            loss_ref[...] = jnp.broadcast_to(loss_acc[...], loss_ref.shape)

    return pl.pallas_call(
        body, name="loss_head", grid=(nt,),
        in_specs=[_rows(tm, D), _whole((1, D)), _rows(tm, D)],
        out_specs=[_rows(tm, D), _whole((1, D)), _whole((1, 128))],
        out_shape=[jax.ShapeDtypeStruct((S, D), F32), jax.ShapeDtypeStruct((1, D), F32),
                   jax.ShapeDtypeStruct((1, 128), F32)],
        scratch_shapes=[pltpu.VMEM((8, D), F32), pltpu.VMEM((1, 1), F32)],
        compiler_params=_params(1),
    )(h, g, target)


def ple_bwd(d, h, g, wg, gate, p, wp, tm):
    S, D = h.shape
    E = p.shape[1]
    nt = S // tm

    def body(d_ref, h_ref, g_ref, wg_ref, gate_ref, p_ref, wp_ref, di_ref, dz_ref, dpp_ref, dg_ref, dg_acc):
        i = pl.program_id(0)

        @pl.when(i == 0)
        def _():
            dg_acc[...] = jnp.zeros_like(dg_acc)

        dd = d_ref[...]
        x = h_ref[...]
        gg = g_ref[...]
        gt = gate_ref[...]
        pp = _mm(p_ref[...], wp_ref[...])
        dpp_ref[...] = (dd * gt).astype(dpp_ref.dtype)
        dz = dd * pp * gt * (1.0 - gt)
        dz_ref[...] = dz.astype(dz_ref.dtype)
        r = lax.rsqrt(jnp.mean(x * x, axis=-1, keepdims=True) + EPS)
        dx, dgr = _rms_bwd(x, r, gg, _mm_nt(dz, wg_ref[...]))
        di_ref[...] = dd + dx
        dg_acc[...] += _colsum8(dgr)

        @pl.when(i == nt - 1)
        def _():
            dg_ref[...] = jnp.sum(dg_acc[...], axis=0, keepdims=True)

    return pl.pallas_call(
        body, name="ple_bwd", grid=(nt,),
        in_specs=[_rows(tm, D), _rows(tm, D), _whole((1, D)), _whole(wg.shape), _rows(tm, D), _rows(tm, E),
                  _whole(wp.shape)],
        out_specs=[_rows(tm, D), _rows(tm, D), _rows(tm, D), _whole((1, D))],
        out_shape=[jax.ShapeDtypeStruct((S, D), F32), jax.ShapeDtypeStruct((S, D), ACT_DTYPE),
                   jax.ShapeDtypeStruct((S, D), ACT_DTYPE), jax.ShapeDtypeStruct((1, D), F32)],
        scratch_shapes=[pltpu.VMEM((8, D), F32)],
        compiler_params=_params(1),
    )(d, h, g, wg, gate, p, wp)


def ffn_bwd(d, h, g, w1, w2, a, tm):
    S, D = h.shape
    FF = w1.shape[1]
    nt = S // tm

    def body(d_ref, h_ref, g_ref, w1_ref, w2_ref, a_ref, di_ref, da_ref, dg_ref, dg_acc):
        i = pl.program_id(0)

        @pl.when(i == 0)
        def _():
            dg_acc[...] = jnp.zeros_like(dg_acc)

        dd = d_ref[...]
        x = h_ref[...]
        da = _mm_nt(dd, w2_ref[...]) * (2.0 * jnp.maximum(a_ref[...], 0.0))
        da_ref[...] = da.astype(da_ref.dtype)
        r = lax.rsqrt(jnp.mean(x * x, axis=-1, keepdims=True) + EPS)
        dx, dgr = _rms_bwd(x, r, g_ref[...], _mm_nt(da, w1_ref[...]))
        di_ref[...] = dd + dx
        dg_acc[...] += _colsum8(dgr)

        @pl.when(i == nt - 1)
        def _():
            dg_ref[...] = jnp.sum(dg_acc[...], axis=0, keepdims=True)

    return pl.pallas_call(
        body, name="ffn_bwd", grid=(nt,),
        in_specs=[_rows(tm, D), _rows(tm, D), _whole((1, D)), _whole(w1.shape), _whole(w2.shape), _rows(tm, FF)],
        out_specs=[_rows(tm, D), _rows(tm, FF), _whole((1, D))],
        out_shape=[jax.ShapeDtypeStruct((S, D), F32), jax.ShapeDtypeStruct((S, FF), ACT_DTYPE),
                   jax.ShapeDtypeStruct((1, D), F32)],
        scratch_shapes=[pltpu.VMEM((8, D), F32)],
        compiler_params=_params(1),
    )(d, h, g, w1, w2, a)


def attn_out_bwd(d, wo, o32, dh, tm):
    S, D = d.shape
    H = D // dh

    def body(d_ref, wo_ref, o_ref, do_ref, doT_ref, delta_ref):
        do32 = _mm_nt(d_ref[...], wo_ref[...])
        do = do32.astype(do_ref.dtype)
        do_ref[...] = do
        doT_ref[...] = do32.T.astype(doT_ref.dtype)
        lane_head = lax.broadcasted_iota(jnp.int32, (D, H), 0) // dh
        seg = (lane_head == lax.broadcasted_iota(jnp.int32, (D, H), 1)).astype(MXU_DTYPE)
        hi, mid, lo = _split3(do.astype(F32) * o_ref[...])
        delta_ref[...] = (jnp.dot(lo, seg, preferred_element_type=F32) + jnp.dot(mid, seg, preferred_element_type=F32)
                          + jnp.dot(hi, seg, preferred_element_type=F32))

    return pl.pallas_call(
        body, name="attn_out_bwd", grid=(S // tm,),
        in_specs=[_rows(tm, D), _whole(wo.shape), _rows(tm, D)],
        out_specs=[_rows(tm, D), _cols(D, tm), _rows(tm, H)],
        out_shape=[jax.ShapeDtypeStruct((S, D), ACT_DTYPE), jax.ShapeDtypeStruct((D, S), ACT_DTYPE),
                   jax.ShapeDtypeStruct((S, H), F32)],
        compiler_params=_params(1),
    )(d, wo, o32)


def flash_bwd(q, qT, k, kT, vT, c_row, do, doT, qstat, dh, tk, tqc):
    S, D = q.shape
    hg = 128 // dh
    G = D // 128
    per = tk // tqc
    nchunk = S // tqc
    assert hg <= 8 and tk % tqc == 0 and S % tk == 0

    def body(q_ref, qT_ref, k_ref, kT_ref, vT_ref, crow_ref, do_ref, doT_ref, st_ref,
             dq_ref, dkT_ref, dvT_ref, dck_ref, dcq_ref):
        ki = pl.program_id(1)

        @pl.when(ki == 0)
        def _():
            dq_ref[...] = jnp.zeros_like(dq_ref)
            dcq_ref[...] = jnp.zeros_like(dcq_ref)

        dck_ref[...] = jnp.zeros_like(dck_ref)
        dkT_ref[...] = jnp.zeros_like(dkT_ref)
        dvT_ref[...] = jnp.zeros_like(dvT_ref)
        def chunk(jq, masked):
            rows = pl.ds(pl.multiple_of(jq * tqc, tqc), tqc)
            st = st_ref[rows, :]
            for hh in range(hg):
                lanes = slice(hh * dh, (hh + 1) * dh)
                kh = k_ref[:, lanes]
                ck = crow_ref[hh:hh + 1, :]
                c0 = ck[:, 0:1]
                u = (st[:, hh:hh + 1] - c0) - st[:, hg + hh:hg + hh + 1]
                s = (_mm(q_ref[rows, lanes], kT_ref[lanes, :]) - (ck - c0)) + u
                if masked:
                    s = jnp.where(_causal_mask(ki * tk, jq * tqc, (tqc, tk), 1), s, NEG_BIG)
                p = jnp.exp(s)
                dvT_ref[lanes, :] += _mm(doT_ref[lanes, rows], p)
                ds = p * (_mm(do_ref[rows, lanes], vT_ref[lanes, :]) - st[:, 2 * hg + hh:2 * hg + hh + 1])
                dkT_ref[lanes, :] += _mm(qT_ref[lanes, rows], ds)
                dck_ref[hh:hh + 1, :] -= jnp.sum(ds, axis=0, keepdims=True)
                dcq_ref[rows, hh:hh + 1] += jnp.sum(ds, axis=1, keepdims=True)
                dq_ref[rows, lanes] += _mm(ds, kh)

        for jj in range(per):
            chunk(ki * per + jj, True)

        def full_chunk(jq, carry):
            chunk(jq, False)
            return carry

        lax.fori_loop((ki + 1) * per, nchunk, full_chunk, 0)

    return pl.pallas_call(
        body, name="flash_bwd", grid=(G, S // tk),
        in_specs=[pl.BlockSpec((S, 128), lambda g, j: (0, g)),
                  pl.BlockSpec((128, S), lambda g, j: (g, 0)),
                  pl.BlockSpec((tk, 128), lambda g, j: (j, g)),
                  pl.BlockSpec((128, tk), lambda g, j: (g, j)),
                  pl.BlockSpec((128, tk), lambda g, j: (g, j)),
                  pl.BlockSpec((None, hg, tk), lambda g, j: (g, 0, j)),
                  pl.BlockSpec((S, 128), lambda g, j: (0, g)),
                  pl.BlockSpec((128, S), lambda g, j: (g, 0)),
                  pl.BlockSpec((None, S, 3 * hg), lambda g, j: (g, 0, 0))],
        out_specs=[pl.BlockSpec((S, 128), lambda g, j: (0, g)),
                   pl.BlockSpec((128, tk), lambda g, j: (g, j)),
                   pl.BlockSpec((128, tk), lambda g, j: (g, j)),
                   pl.BlockSpec((None, 8, tk), lambda g, j: (g, 0, j)),
                   pl.BlockSpec((None, S, hg), lambda g, j: (g, 0, 0))],
        out_shape=[jax.ShapeDtypeStruct((S, D), F32), jax.ShapeDtypeStruct((D, S), F32),
                   jax.ShapeDtypeStruct((D, S), F32), jax.ShapeDtypeStruct((G, 8, S), F32),
                   jax.ShapeDtypeStruct((G, S, hg), F32)],
        compiler_params=_params(2),
    )(q, qT, k, kT, vT, c_row, do, doT, qstat)


def q_bwd(d, dq, h, g, wq, scale, tm):
    S, D = h.shape
    nt = S // tm

    def body(d_ref, dq_ref, h_ref, g_ref, wq_ref, di_ref, dqs_ref, dg_ref, dg_acc):
        i = pl.program_id(0)

        @pl.when(i == 0)
        def _():
            dg_acc[...] = jnp.zeros_like(dg_acc)

        x = h_ref[...]
        dqs = dq_ref[...] * scale
        dqs_ref[...] = dqs.astype(dqs_ref.dtype)
        r = lax.rsqrt(jnp.mean(x * x, axis=-1, keepdims=True) + EPS)
        dx, dgr = _rms_bwd(x, r, g_ref[...], _mm_nt(dqs, wq_ref[...]))
        di_ref[...] = d_ref[...] + dx
        dg_acc[...] += _colsum8(dgr)

        @pl.when(i == nt - 1)
        def _():
            dg_ref[...] = jnp.sum(dg_acc[...], axis=0, keepdims=True)

    return pl.pallas_call(
        body, name="q_bwd", grid=(nt,),
        in_specs=[_rows(tm, D), _rows(tm, D), _rows(tm, D), _whole((1, D)), _whole(wq.shape)],
        out_specs=[_rows(tm, D), _rows(tm, D), _whole((1, D))],
        out_shape=[jax.ShapeDtypeStruct((S, D), F32), jax.ShapeDtypeStruct((S, D), ACT_DTYPE),
                   jax.ShapeDtypeStruct((1, D), F32)],
        scratch_shapes=[pltpu.VMEM((8, D), F32)],
        compiler_params=_params(1),
    )(d, dq, h, g, wq)


def kv_bwd(d, dks, dvs, dcs, fl, h, g, wk, wv, wf, tm):
    S, D = h.shape
    H = wf.shape[1]
    nt = S // tm
    nl = len(dks)
    nc = len(dcs)

    def body(*refs):
        d_ref = refs[0]
        dk_refs = refs[1:1 + nl]
        dv_refs = refs[1 + nl:1 + 2 * nl]
        dc_refs = refs[1 + 2 * nl:1 + 2 * nl + nc]
        (fl_ref, h_ref, g_ref, wk_ref, wv_ref, wf_ref,
         di_ref, dk_ref, dv_ref, dfl_ref, dg_ref, dbf_ref, dg_acc, dbf_acc, carry) = refs[1 + 2 * nl + nc:]
        i = pl.program_id(0)

        @pl.when(i == 0)
        def _():
            dg_acc[...] = jnp.zeros_like(dg_acc)
            dbf_acc[...] = jnp.zeros_like(dbf_acc)
            carry[...] = jnp.zeros_like(carry)

        dkT = dk_refs[0][...]
        dvT = dv_refs[0][...]
        for l in range(1, nl):
            dkT = dkT + dk_refs[l][...]
            dvT = dvT + dv_refs[l][...]
        dk = dkT.T
        dv = dvT.T
        dk_ref[...] = dk.astype(dk_ref.dtype)
        dv_ref[...] = dv.astype(dv_ref.dtype)
        row = lax.broadcasted_iota(jnp.int32, (tm, tm), 0)
        col = lax.broadcasted_iota(jnp.int32, (tm, tm), 1)
        tri = (col >= row).astype(MXU_DTYPE)
        dc = dc_refs[0][...]
        for l in range(1, nc):
            dc = dc + dc_refs[l][...]
        dlogf = _tri_mm(tri, dc) + carry[...]
        carry[...] = dlogf[0:1, :]
        dfl = dlogf * _sigmoid(-fl_ref[...])
        dfl_ref[...] = dfl
        dbf_acc[...] += jnp.sum(dfl, axis=0, keepdims=True)
        x = h_ref[...]
        dn = _mm_nt(dk, wk_ref[...]) + _mm_nt(dv, wv_ref[...]) + _mm_nt(dfl, wf_ref[...])
        r = lax.rsqrt(jnp.mean(x * x, axis=-1, keepdims=True) + EPS)
        dx, dgr = _rms_bwd(x, r, g_ref[...], dn)
        di_ref[...] = d_ref[...] + dx
        dg_acc[...] += _colsum8(dgr)

        @pl.when(i == nt - 1)
        def _():
            dg_ref[...] = jnp.sum(dg_acc[...], axis=0, keepdims=True)
            dbf_ref[...] = dbf_acc[...]

    rev = lambda n: _rows_rev(tm, n, nt)
    return pl.pallas_call(
        body, name="kv_bwd", grid=(nt,),
        in_specs=([rev(D)] + [_cols_rev(D, tm, nt)] * (2 * nl)
                  + [rev(H)] * nc
                  + [rev(H), rev(D), _whole((1, D)), _whole(wk.shape), _whole(wv.shape), _whole(wf.shape)]),
        out_specs=[rev(D), rev(D), rev(D), rev(H), _whole((1, D)), _whole((1, H))],
        out_shape=[jax.ShapeDtypeStruct((S, D), F32), jax.ShapeDtypeStruct((S, D), ACT_DTYPE),
                   jax.ShapeDtypeStruct((S, D), ACT_DTYPE), jax.ShapeDtypeStruct((S, H), F32),
                   jax.ShapeDtypeStruct((1, D), F32), jax.ShapeDtypeStruct((1, H), F32)],
        scratch_shapes=[pltpu.VMEM((8, D), F32), pltpu.VMEM((1, H), F32), pltpu.VMEM((1, H), F32)],
        compiler_params=_params(1),
    )(d, *dks, *dvs, *dcs, fl, h, g, wk, wv, wf)


def conv_bwd(d, h, g, w1, wd, lg, lb, w2, u, z, tm):
    S, D = h.shape
    CW = wd.shape[0]
    nt = S // tm
    assert tm >= HALO and CW - 1 <= HALO

    def body(d_ref, h_ref, g_ref, w1_ref, wd_ref, lg_ref, lb_ref, w2_ref, u_ref, z_ref,
             di_ref, du_ref, db2_ref, dlg_ref, dlb_ref, dbd_ref, dwd_ref, db1_ref, dg_ref,
             ext, win, db2_acc, dlg_acc, dlb_acc, dbd_acc, dwd_acc, db1_acc, dg_acc):
        i = pl.program_id(0)

        @pl.when(i == 0)
        def _():
            ext[tm:tm + HALO, :] = jnp.zeros((HALO, D), F32)
            for acc in (db2_acc, dlg_acc, dlb_acc, dbd_acc, dwd_acc, db1_acc, dg_acc):
                acc[...] = jnp.zeros_like(acc)

        dd = d_ref[...]
        db2_acc[...] += _colsum8(dd)
        dsw = _mm_nt(dd, w2_ref[...])
        zz = z_ref[...]
        zc = zz - jnp.mean(zz, axis=-1, keepdims=True)
        rs = lax.rsqrt(jnp.mean(zc * zc, axis=-1, keepdims=True) + EPS)
        xh = zc * rs
        lgv = lg_ref[...]
        y = xh * lgv + lb_ref[...]
        sg = _sigmoid(y)
        dy = dsw * (sg * (1.0 + y * (1.0 - sg)))
        dlg_acc[...] += _colsum8(dy * xh)
        dlb_acc[...] += _colsum8(dy)
        dxh = dy * lgv
        dz = rs * (dxh - jnp.mean(dxh, axis=-1, keepdims=True) - xh * jnp.mean(dxh * xh, axis=-1, keepdims=True))
        dbd_acc[...] += _colsum8(dz)
        ext[0:tm, :] = dz
        uu = u_ref[...]
        a = uu[:, :D]
        sgg = _sigmoid(uu[:, D:])
        glu = a * sgg
        dglu = jnp.zeros((tm, D), F32)
        for b in range(8):
            amax = (CW - 1 - b) // 8
            win[0:tm + 8 * amax, :] = ext[b:b + tm + 8 * amax, :]
            for a8 in range(amax + 1):
                k = CW - 1 - (8 * a8 + b)
                sh = win[8 * a8:8 * a8 + tm, :]
                dglu = dglu + wd_ref[k:k + 1, :] * sh
                dwd_acc[k] += _colsum8(glu * sh)
        ext[tm:tm + HALO, :] = ext[0:HALO, :]
        da = dglu * sgg
        dgg = dglu * a * sgg * (1.0 - sgg)
        du_ref[:, :D] = da.astype(du_ref.dtype)
        du_ref[:, D:] = dgg.astype(du_ref.dtype)
        db1_acc[:, :D] += _colsum8(da)
        db1_acc[:, D:] += _colsum8(dgg)
        dn = _mm_nt(da, w1_ref[:, :D]) + _mm_nt(dgg, w1_ref[:, D:])
        x = h_ref[...]
        r = lax.rsqrt(jnp.mean(x * x, axis=-1, keepdims=True) + EPS)
        dx, dgr = _rms_bwd(x, r, g_ref[...], dn)
        di_ref[...] = dd + dx
        dg_acc[...] += _colsum8(dgr)

        @pl.when(i == nt - 1)
        def _():
            db2_ref[...] = jnp.sum(db2_acc[...], axis=0, keepdims=True)
            dlg_ref[...] = jnp.sum(dlg_acc[...], axis=0, keepdims=True)
            dlb_ref[...] = jnp.sum(dlb_acc[...], axis=0, keepdims=True)
            dbd_ref[...] = jnp.sum(dbd_acc[...], axis=0, keepdims=True)
            dwd_ref[...] = jnp.sum(dwd_acc[...], axis=1)
            db1_ref[...] = jnp.sum(db1_acc[...], axis=0, keepdims=True)
            dg_ref[...] = jnp.sum(dg_acc[...], axis=0, keepdims=True)

    rev = lambda n: _rows_rev(tm, n, nt)
    vec = jax.ShapeDtypeStruct((1, D), F32)
    return pl.pallas_call(
        body, name="conv_bwd", grid=(nt,),
        in_specs=[rev(D), rev(D), _whole((1, D)), _whole(w1.shape), _whole(wd.shape), _whole((1, D)),
                  _whole((1, D)), _whole(w2.shape), rev(2 * D), rev(D)],
        out_specs=[rev(D), rev(2 * D), _whole((1, D)), _whole((1, D)), _whole((1, D)), _whole((1, D)),
                   _whole((CW, D)), _whole((1, 2 * D)), _whole((1, D))],
        out_shape=[jax.ShapeDtypeStruct((S, D), F32), jax.ShapeDtypeStruct((S, 2 * D), ACT_DTYPE),
                   vec, vec, vec, vec, jax.ShapeDtypeStruct((CW, D), F32),
                   jax.ShapeDtypeStruct((1, 2 * D), F32), vec],
        scratch_shapes=[pltpu.VMEM((tm + HALO, D), F32), pltpu.VMEM((tm + HALO, D), F32),
                        pltpu.VMEM((8, D), F32), pltpu.VMEM((8, D), F32),
                        pltpu.VMEM((8, D), F32), pltpu.VMEM((8, D), F32), pltpu.VMEM((CW, 8, D), F32),
                        pltpu.VMEM((8, 2 * D), F32), pltpu.VMEM((8, D), F32)],
        compiler_params=_params(1),
    )(d, h, g, w1, wd, lg, lb, w2, u, z)


def weight_grad(a, b, ts, name, column_shards=None):
    S, M = a.shape
    N = b.shape[1]
    ta = M if M <= 1024 else 1024
    tb = N if N <= 1024 else 1024
    assert M % ta == 0 and N % tb == 0 and S % ts == 0
    if column_shards is None:
        width, per_tile = tb, 1
        out_spec = pl.BlockSpec((ta, tb), lambda i, j, s: (i, j))
        out_shape = jax.ShapeDtypeStruct((M, N), F32)
    else:
        width = N // column_shards
        per_tile = tb // width
        assert tb % width == 0
        out_spec = pl.BlockSpec((per_tile, ta, width), lambda i, j, s: (j, i, 0))
        out_shape = jax.ShapeDtypeStruct((column_shards, M, width), F32)

    def body(a_ref, b_ref, o_ref):
        @pl.when(pl.program_id(2) == 0)
        def _():
            o_ref[...] = jnp.zeros_like(o_ref)

        res = _mm_tn(a_ref[...], b_ref[...])
        if column_shards is None:
            o_ref[...] += res
        else:
            for d in range(per_tile):
                o_ref[d] += res[:, d * width:(d + 1) * width]

    return pl.pallas_call(
        body, name=name, grid=(M // ta, N // tb, S // ts),
        in_specs=[pl.BlockSpec((ts, ta), lambda i, j, s: (s, i)), pl.BlockSpec((ts, tb), lambda i, j, s: (s, j))],
        out_specs=out_spec, out_shape=out_shape,
        compiler_params=_params(3),
    )(a, b)


def _position():
    return lax.axis_index("x"), lax.axis_index("y"), lax.axis_index("c")


def all_gather(x, name):
    def body(x_ref, out_ref, send_sems, recv_sems, local_sem):
        x, y, c = _position()
        me, sibling = (x, y, c), (x, y, 1 - c)
        chips = [(1 - x, y), (x, 1 - y), (1 - x, 1 - y)]

        def slot(px, py, pc):
            return out_ref.at[4 * px + 2 * py + pc]

        def copy(k, block, to, src=None):
            return pltpu.make_async_remote_copy(
                src_ref=slot(*block) if src is None else src, dst_ref=slot(*block),
                send_sem=send_sems.at[k], recv_sem=recv_sems.at[k], device_id=to, device_id_type=MESH)

        mine = pltpu.make_async_copy(x_ref, slot(*me), local_sem)
        mine.start()
        first = [copy(0, me, sibling, src=x_ref)]
        first += [copy(1 + j, me, (*chip, c), src=x_ref) for j, chip in enumerate(chips)]
        for cp in first:
            cp.start()
        passed = [copy(4 + j, (*chip, c), sibling) for j, chip in enumerate(chips)]
        for j, chip in enumerate(chips):
            copy(1 + j, (*chip, c), me).wait_recv()
            passed[j].start()
        copy(0, sibling, me).wait_recv()
        for j, chip in enumerate(chips):
            copy(4 + j, (*chip, 1 - c), me).wait_recv()
        for cp in first + passed:
            cp.wait_send()
        mine.wait()

    return pl.pallas_call(
        body, name=name,
        in_specs=[pl.BlockSpec(memory_space=pl.ANY)], out_specs=pl.BlockSpec(memory_space=pl.ANY),
        out_shape=jax.ShapeDtypeStruct((N_DEV,) + x.shape, x.dtype),
        scratch_shapes=[pltpu.SemaphoreType.DMA((7,)), pltpu.SemaphoreType.DMA((7,)), pltpu.SemaphoreType.DMA],
    )(x)


def sibling_exchange(gs):
    n = len(gs)

    def body(*refs):
        g_refs, land_refs, (send_sems, recv_sems) = refs[:n], refs[n:2 * n], refs[2 * n:]
        x, y, c = _position()
        copies = [pltpu.make_async_remote_copy(
            src_ref=g_refs[a].at[2 * j + 1 - c], dst_ref=land_refs[a].at[j], send_sem=send_sems.at[N_CHIP * a + j],
            recv_sem=recv_sems.at[N_CHIP * a + j], device_id=(x, y, 1 - c), device_id_type=MESH)
            for a in range(n) for j in range(N_CHIP)]
        for cp in copies:
            cp.start()
        for cp in copies:
            cp.wait()

    return pl.pallas_call(
        body, name="grad_sibling_exchange",
        in_specs=[pl.BlockSpec(memory_space=pl.ANY)] * n, out_specs=[pl.BlockSpec(memory_space=pl.ANY)] * n,
        out_shape=[jax.ShapeDtypeStruct((N_CHIP,) + g.shape[1:], g.dtype) for g in gs],
        scratch_shapes=[pltpu.SemaphoreType.DMA((N_CHIP * n,)), pltpu.SemaphoreType.DMA((N_CHIP * n,))],
    )(*gs)


def chip_partial(g, land, core):
    _, R, C = g.shape
    tr = _row_tile(R, PACK_ROW_TILE)

    def body(c_ref, g_ref, l_ref, o_ref):
        o_ref[...] = (g_ref[...] + l_ref[...]).astype(o_ref.dtype)

    grid_spec = pltpu.PrefetchScalarGridSpec(
        num_scalar_prefetch=1, grid=(N_CHIP, R // tr),
        in_specs=[pl.BlockSpec((None, tr, C), lambda j, i, cr: (2 * j + cr[0], i, 0)),
                  pl.BlockSpec((None, tr, C), lambda j, i, cr: (j, i, 0))],
        out_specs=pl.BlockSpec((None, tr, C), lambda j, i, cr: (j, i, 0)))
    return pl.pallas_call(
        body, name="grad_chip_partial", grid_spec=grid_spec,
        out_shape=jax.ShapeDtypeStruct((N_CHIP, R, C), WIRE_DTYPE),
        compiler_params=_params(2),
    )(core, g, land)


def chip_exchange(parts):
    n = len(parts)

    def body(*refs):
        p_refs, land_refs, (send_sems, recv_sems, local_sems) = refs[:n], refs[n:2 * n], refs[2 * n:]
        x, y, c = _position()
        mychip = 2 * x + y
        chips = [(1 - x, y), (x, 1 - y), (1 - x, 1 - y)]

        def remote(a, k, slot):
            cx, cy = chips[k]
            return pltpu.make_async_remote_copy(
                src_ref=p_refs[a].at[2 * cx + cy], dst_ref=land_refs[a].at[slot], send_sem=send_sems.at[3 * a + k],
                recv_sem=recv_sems.at[3 * a + k], device_id=(cx, cy, c), device_id_type=MESH)

        mine = [pltpu.make_async_copy(p_refs[a].at[mychip], land_refs[a].at[mychip], local_sems.at[a])
                for a in range(n)]
        for cp in mine:
            cp.start()
        copies = [remote(a, k, mychip) for a in range(n) for k in range(3)]
        for cp in copies:
            cp.start()
        for a in range(n):
            for k, (cx, cy) in enumerate(chips):
                remote(a, k, 2 * cx + cy).wait_recv()
        for cp in copies:
            cp.wait_send()
        for cp in mine:
            cp.wait()

    return pl.pallas_call(
        body, name="grad_chip_exchange",
        in_specs=[pl.BlockSpec(memory_space=pl.ANY)] * n, out_specs=[pl.BlockSpec(memory_space=pl.ANY)] * n,
        out_shape=[jax.ShapeDtypeStruct(p.shape, p.dtype) for p in parts],
        scratch_shapes=[pltpu.SemaphoreType.DMA((3 * n,)), pltpu.SemaphoreType.DMA((3 * n,)),
                        pltpu.SemaphoreType.DMA((n,))],
    )(*parts)


def _adamw(w, g, m, v):
    m = ADAM_B1 * m + (1.0 - ADAM_B1) * g
    v = ADAM_B2 * v + (1.0 - ADAM_B2) * jnp.square(g)
    m_hat = m / (1.0 - ADAM_B1 ** ADAM_STEP)
    v_hat = v / (1.0 - ADAM_B2 ** ADAM_STEP)
    delta = -ADAM_LR * (m_hat / (jnp.sqrt(v_hat) + ADAM_EPS) + ADAM_WD * w)
    return delta, m, v


def adamw_sharded(parts, w, m, v):
    R, C = w.shape
    tr = _row_tile(R, PACK_ROW_TILE)

    def body(p_ref, w_ref, m_ref, v_ref, g_ref, d_ref, nm_ref, nv_ref):
        g = p_ref[0].astype(F32)
        for j in range(1, N_CHIP):
            g = g + p_ref[j].astype(F32)
        g_ref[...] = g
        d_ref[...], nm_ref[...], nv_ref[...] = _adamw(w_ref[...], g, m_ref[...], v_ref[...])

    out = jax.ShapeDtypeStruct((R, C), F32)
    return pl.pallas_call(
        body, name="adamw_sharded", grid=(R // tr,),
        in_specs=[pl.BlockSpec((N_CHIP, tr, C), lambda i: (0, i, 0)), _rows(tr, C), _rows(tr, C), _rows(tr, C)],
        out_specs=[_rows(tr, C)] * 4, out_shape=[out] * 4,
        compiler_params=_params(1),
    )(parts, w, m, v)


def adamw_replicated(gathered, w, m, v):
    R, C = w.shape

    def body(p_ref, w_ref, m_ref, v_ref, g_ref, d_ref, nm_ref, nv_ref):
        g = p_ref[0]
        for j in range(1, N_DEV):
            g = g + p_ref[j]
        g_ref[...] = g
        d_ref[...], nm_ref[...], nv_ref[...] = _adamw(w_ref[...], g, m_ref[...], v_ref[...])

    out = jax.ShapeDtypeStruct((R, C), F32)
    return pl.pallas_call(
        body, name="adamw_replicated", grid=(1,),
        in_specs=[_whole(gathered.shape), _whole((R, C)), _whole((R, C)), _whole((R, C))],
        out_specs=[_whole((R, C))] * 4, out_shape=[out] * 4,
        compiler_params=_params(1),
    )(gathered, w, m, v)


def _piece_rows(n):
    return -(-n // PACK_COLS)


def _as_rows(a, lead):
    flat = a.reshape(a.shape[:lead] + (-1,))
    fill = _piece_rows(flat.shape[-1]) * PACK_COLS - flat.shape[-1]
    if fill:
        flat = jnp.pad(flat, [(0, 0)] * lead + [(0, fill)])
    return flat.reshape(flat.shape[:-1] + (-1, PACK_COLS))


def _pack(arrays, rows_multiple, dtype=None, lead=0):
    pieces = [_as_rows(a if dtype is None else a.astype(dtype), lead) for a in arrays]
    extra = -sum(p.shape[lead] for p in pieces) % rows_multiple
    if extra:
        pieces.append(jnp.zeros(pieces[0].shape[:lead] + (extra, PACK_COLS), pieces[0].dtype))
    return jnp.concatenate(pieces, axis=lead)


def _unpack(packed, shapes, lead=0):
    out, r0 = [], 0
    for shp in shapes:
        n = int(np.prod(shp))
        rows = _piece_rows(n)
        seg = lax.slice_in_dim(packed, r0, r0 + rows, axis=lead).reshape(packed.shape[:lead] + (-1,))
        if rows * PACK_COLS != n:
            seg = seg[..., :n]
        out.append(seg.reshape(packed.shape[:lead] + tuple(shp)))
        r0 += rows
    return out


def _full_from_gathered(gathered, shard_shapes, axes):
    out = []
    for seg, shp, ax in zip(_unpack(gathered, shard_shapes, lead=1), shard_shapes, axes):
        seg = jnp.moveaxis(seg, 0, ax)
        out.append(seg.reshape(tuple(shp[:ax]) + (N_DEV * shp[ax],) + tuple(shp[ax + 1:])))
    return out


def kernel(x, p, mix_norm, conv_w_pw1, conv_b_pw1, conv_w_dw, conv_b_dw, conv_ln_g, conv_ln_b, conv_w_pw2, conv_b_pw2, kv_norm, w_kvf, b_f, attn_w_q, attn_w_o, ffn_norm, ffn_w1, ffn_w2, ple_norm, ple_w_gate, ple_w_proj, final_norm, loss_target, m_mix_norm, m_conv_w_pw1, m_conv_b_pw1, m_conv_w_dw, m_conv_b_dw, m_conv_ln_g, m_conv_ln_b, m_conv_w_pw2, m_conv_b_pw2, m_kv_norm, m_w_kvf, m_b_f, m_attn_w_q, m_attn_w_o, m_ffn_norm, m_ffn_w1, m_ffn_w2, m_ple_norm, m_ple_w_gate, m_ple_w_proj, m_final_norm, v_mix_norm, v_conv_w_pw1, v_conv_b_pw1, v_conv_w_dw, v_conv_b_dw, v_conv_ln_g, v_conv_ln_b, v_conv_w_pw2, v_conv_b_pw2, v_kv_norm, v_w_kvf, v_b_f, v_attn_w_q, v_attn_w_o, v_ffn_norm, v_ffn_w1, v_ffn_w2, v_ple_norm, v_ple_w_gate, v_ple_w_proj, v_final_norm):
    given = dict(locals())
    W = {n: given[n] for n in WEIGHTS}
    M = {n: given["m_" + n] for n in WEIGHTS}
    V = {n: given["v_" + n] for n in WEIGHTS}

    _, S, D = x.shape
    NA = conv_w_pw1.shape[0]
    NB = attn_w_q.shape[0]
    DEPTH = NA + NB
    H = b_f.shape[0]
    dh = D // H
    hg = 128 // dh
    G = D // 128
    scale = dh ** -0.5
    tm = _row_tile(S, 256)
    tq_f = _row_tile(S, FLASH_FWD_TILE[0])
    tkc_f = _row_tile(tq_f, FLASH_FWD_TILE[1])
    tk_b = _row_tile(S, FLASH_BWD_TILE[0])
    tqc_b = _row_tile(tk_b, FLASH_BWD_TILE[1])
    ts = _row_tile(S, 512)
    xs = x[0]
    tgt = loss_target[0]
    ps = p[:, 0]
    row = lambda a: a.reshape(1, -1)

    big_names = list(SHARD_AXIS_BIG)
    small_names = list(SHARD_AXIS_SMALL)
    big = _full_from_gathered(
        all_gather(_pack([W[n] for n in big_names], 16, MXU_DTYPE), "weights_all_gather"),
        [W[n].shape for n in big_names], [SHARD_AXIS_BIG[n] for n in big_names])
    small = _full_from_gathered(
        all_gather(_pack([W[n] for n in small_names], 8), "vectors_all_gather"),
        [W[n].shape for n in small_names], [SHARD_AXIS_SMALL[n] for n in small_names])
    FW = dict(zip(big_names + small_names, big + small))
    wk, wv, wf = FW["w_kvf"][:, :D], FW["w_kvf"][:, D:2 * D], FW["w_kvf"][:, 2 * D:]

    saved = []
    h = xs
    kv = None
    for i in range(DEPTH):
        rec = {"h_in": h}
        if i < NA:
            h, rec["n"], rec["u"], rec["z"], rec["sw"] = conv_fwd(
                h, row(mix_norm[i]), FW["conv_w_pw1"][i], row(FW["conv_b_pw1"][i]), FW["conv_w_dw"][i],
                row(FW["conv_b_dw"][i]), row(FW["conv_ln_g"][i]), row(FW["conv_ln_b"][i]),
                FW["conv_w_pw2"][i], row(FW["conv_b_pw2"][i]), tm)
        else:
            j = i - NA
            if j == 0:
                k_, kT_, vT_, nkv, fl, c = kv_fwd(h, row(kv_norm), wk, wv, wf, row(b_f), tm)
                cg = c.reshape(S, G, hg)
                kv = dict(k=k_, kT=kT_, vT=vT_, n=nkv, fl=fl, h=h, c_col=jnp.transpose(cg, (1, 0, 2)),
                          c_row=jnp.transpose(cg, (1, 2, 0)))
            rec["n"], rec["q"], rec["qT"] = q_fwd(h, row(mix_norm[i]), FW["attn_w_q"][j], scale, tm)
            rec["o"], rec["o32"], rec["lse"] = flash_fwd(rec["qT"], kv["k"], kv["vT"], kv["c_col"], kv["c_row"], dh,
                                                         tq_f, tkc_f)
            h = attn_out_fwd(h, rec["o"], FW["attn_w_o"][j], tm)
        rec["h_ffn"] = h
        h, rec["n_ffn"], rec["a"], rec["s"] = ffn_fwd(h, row(ffn_norm[i]), FW["ffn_w1"][i], FW["ffn_w2"][i], tm)
        rec["h_ple"] = h
        h, rec["n_ple"], rec["gate"] = ple_fwd(h, row(ple_norm[i]), FW["ple_w_gate"][i], ps[i],
                                               FW["ple_w_proj"][i], tm)
        saved.append(rec)

    d, g_final, loss_part = loss_head(h, row(final_norm), tgt, tm)
    GW = {n: [None] * W[n].shape[0] for n in WEIGHTS if W[n].ndim > 1 and n != "w_kvf"}
    dks, dvs, dcs = [], [], []
    for i in reversed(range(DEPTH)):
        rec = saved[i]
        d_out = d
        d, dz, dpp, GW["ple_norm"][i] = ple_bwd(d_out, rec["h_ple"], row(ple_norm[i]), FW["ple_w_gate"][i],
                                                rec["gate"], ps[i], FW["ple_w_proj"][i], tm)
        GW["ple_w_gate"][i] = weight_grad(rec["n_ple"], dz, ts, "grad_ple_w_gate")
        GW["ple_w_proj"][i] = weight_grad(ps[i], dpp, ts, "grad_ple_w_proj", column_shards=N_DEV)
        d_out = d
        d, da, GW["ffn_norm"][i] = ffn_bwd(d_out, rec["h_ffn"], row(ffn_norm[i]), FW["ffn_w1"][i],
                                           FW["ffn_w2"][i], rec["a"], tm)
        GW["ffn_w2"][i] = weight_grad(rec["s"], d_out, ts, "grad_ffn_w2")
        GW["ffn_w1"][i] = weight_grad(rec["n_ffn"], da, ts, "grad_ffn_w1", column_shards=N_DEV)
        d_out = d
        if i >= NA:
            j = i - NA
            GW["attn_w_o"][j] = weight_grad(rec["o"], d_out, ts, "grad_attn_w_o")
            do, doT, delta = attn_out_bwd(d_out, FW["attn_w_o"][j], rec["o32"], dh, tm)
            qstat = jnp.concatenate([kv["c_col"], jnp.transpose(rec["lse"], (0, 2, 1)),
                                     jnp.transpose(delta.reshape(S, G, hg), (1, 0, 2))], axis=2)
            dq, dkT, dvT, dck, dcq = flash_bwd(rec["q"], rec["qT"], kv["k"], kv["kT"], kv["vT"], kv["c_row"], do, doT,
                                          qstat, dh, tk_b, tqc_b)
            dks.append(dkT)
            dvs.append(dvT)
            dcs.append(jnp.transpose(dck[:, :hg, :], (2, 0, 1)).reshape(S, H))
            dcs.append(jnp.transpose(dcq, (1, 0, 2)).reshape(S, H))
            d, dqs, GW["mix_norm"][i] = q_bwd(d_out, dq, rec["h_in"], row(mix_norm[i]), FW["attn_w_q"][j], scale, tm)
            GW["attn_w_q"][j] = weight_grad(rec["n"], dqs, ts, "grad_attn_w_q")
            if j == 0:
                d, dk_sum, dv_sum, dfl, g_kv_norm, g_b_f = kv_bwd(d, dks, dvs, dcs, kv["fl"], kv["h"],
                                                                  row(kv_norm), wk, wv, wf, tm)
                g_w_kvf = jnp.concatenate([weight_grad(kv["n"], dk_sum, ts, "grad_w_k"),
                                           weight_grad(kv["n"], dv_sum, ts, "grad_w_v"),
                                           weight_grad(kv["n"], dfl, ts, "grad_w_f")], axis=1)
        else:
            GW["conv_w_pw2"][i] = weight_grad(rec["sw"], d_out, ts, "grad_conv_w_pw2")
            (d, du, GW["conv_b_pw2"][i], GW["conv_ln_g"][i], GW["conv_ln_b"][i], GW["conv_b_dw"][i],
             GW["conv_w_dw"][i], GW["conv_b_pw1"][i], GW["mix_norm"][i]) = conv_bwd(
                d_out, rec["h_in"], row(mix_norm[i]), FW["conv_w_pw1"][i], FW["conv_w_dw"][i],
                row(FW["conv_ln_g"][i]), row(FW["conv_ln_b"][i]), FW["conv_w_pw2"][i], rec["u"], rec["z"], tm)
            GW["conv_w_pw1"][i] = weight_grad(rec["n"], du, ts, "grad_conv_w_pw1", column_shards=N_DEV)
    grad_x = d[None]

    sharded_names = big_names + small_names
    GW["w_kvf"] = [g_w_kvf]

    def device_major(n, g):
        width = W[n].shape[-1]
        if g.ndim == 3:
            return g
        if g.shape[-1] == width:
            return g.reshape(N_DEV, -1, width)
        return jnp.transpose(g.reshape(-1, N_DEV, width), (1, 0, 2))

    widths = sorted({W[n].shape[-1] for n in sharded_names}, reverse=True)
    groups = [[n for n in sharded_names if W[n].shape[-1] == width] for width in widths]

    def stack_rows(pieces, axis):
        rows = sum(p.shape[axis] for p in pieces)
        fill = -rows % (PACK_ROW_TILE if rows > PACK_ROW_TILE else 8)
        if fill:
            shape = list(pieces[0].shape)
            shape[axis] = fill
            pieces = pieces + [jnp.zeros(shape, pieces[0].dtype)]
        return jnp.concatenate(pieces, axis=axis)

    chunks = [stack_rows([device_major(n, g) for n in names for g in GW[n]], 1) for names in groups]
    core = lax.axis_index("c").astype(jnp.int32).reshape(1)
    landed = sibling_exchange(chunks)
    parts = chip_exchange([chip_partial(g, l, core) for g, l in zip(chunks, landed)])

    res = {}
    for names, width, part in zip(groups, widths, parts):
        group_rows = lambda src: stack_rows([src[n].reshape(-1, width) for n in names], 0)
        outs = adamw_sharded(part, group_rows(W), group_rows(M), group_rows(V))
        for kind, packed in zip(("grad", "delta", "new_m", "new_v"), outs):
            r0 = 0
            for n in names:
                nr = W[n].size // width
                res[kind, n] = packed[r0:r0 + nr].reshape(W[n].shape)
                r0 += nr

    rep_grads = {"mix_norm": jnp.concatenate(GW["mix_norm"], axis=0), "kv_norm": g_kv_norm,
                 "b_f": g_b_f, "ffn_norm": jnp.concatenate(GW["ffn_norm"], axis=0),
                 "ple_norm": jnp.concatenate(GW["ple_norm"], axis=0), "final_norm": g_final}

    def pack_rep(src, extra=None):
        rows_ = [jnp.pad(src[n].reshape(-1, src[n].shape[-1]), ((0, 0), (0, D - src[n].shape[-1])))
                 for n in REPLICATED]
        if extra is not None:
            rows_.append(jnp.pad(extra, ((0, 0), (0, D - extra.shape[-1]))))
        else:
            rows_.append(jnp.zeros((1, D), F32))
        flat = jnp.concatenate(rows_, axis=0)
        return jnp.pad(flat, ((0, -flat.shape[0] % 8), (0, 0)))

    rep_g = all_gather(pack_rep(rep_grads, loss_part), "replicated_all_gather")
    outs_rep = adamw_replicated(rep_g, pack_rep(W), pack_rep(M), pack_rep(V))
    n_rep_rows = sum(int(np.prod(W[n].shape[:-1])) for n in REPLICATED)
    for kind, packed in zip(("grad", "delta", "new_m", "new_v"), outs_rep):
        r0 = 0
        for n in REPLICATED:
            nr = int(np.prod(W[n].shape[:-1]))
            res[kind, n] = packed[r0:r0 + nr, :W[n].shape[-1]].reshape(W[n].shape)
            r0 += nr
    loss = outs_rep[0][n_rep_rows, 0]

    return (loss, grad_x, *[res["grad", n] for n in WEIGHTS], *[res["delta", n] for n in WEIGHTS],
            *[res["new_m", n] for n in WEIGHTS], *[res["new_v", n] for n in WEIGHTS])
```

```python
import numpy as np
import jax
import jax.numpy as jnp
from jax import lax
from jax.experimental import pallas as pl
from jax.experimental.pallas import tpu as pltpu

F32 = jnp.float32
MXU_DTYPE = jnp.bfloat16
ACT_DTYPE = jnp.bfloat16
WIRE_DTYPE = jnp.bfloat16

N_DEV = 8
N_CHIP = 4
EPS = 1e-6
NEG_BIG = -1e30
ADAM_LR = 0.001
ADAM_B1 = 0.9
ADAM_B2 = 0.999
ADAM_EPS = 1e-08
ADAM_WD = 0.01
ADAM_STEP = 10

VMEM_LIMIT_BYTES = 56 * 1024 * 1024
PACK_COLS = 1024
PACK_ROW_TILE = 256
FLASH_FWD_TILE = (1024, 512)
FLASH_BWD_TILE = (1024, 512)
FLASH_TILE = 128
HALO = 32
MESH = pl.DeviceIdType.MESH

SHARD_AXIS_BIG = {"conv_w_pw1": 2, "conv_w_pw2": 1, "w_kvf": 1, "attn_w_q": 1, "attn_w_o": 1,
                  "ffn_w1": 2, "ffn_w2": 1, "ple_w_gate": 1, "ple_w_proj": 2}
SHARD_AXIS_SMALL = {"conv_b_pw1": 1, "conv_w_dw": 2, "conv_b_dw": 1, "conv_ln_g": 1, "conv_ln_b": 1,
                    "conv_b_pw2": 1}
REPLICATED = ["mix_norm", "kv_norm", "b_f", "ffn_norm", "ple_norm", "final_norm"]
WEIGHTS = ["mix_norm", "conv_w_pw1", "conv_b_pw1", "conv_w_dw", "conv_b_dw", "conv_ln_g", "conv_ln_b",
           "conv_w_pw2", "conv_b_pw2", "kv_norm", "w_kvf", "b_f", "attn_w_q", "attn_w_o", "ffn_norm",
           "ffn_w1", "ffn_w2", "ple_norm", "ple_w_gate", "ple_w_proj", "final_norm"]


def _mm(a, b):
    return jnp.dot(a.astype(MXU_DTYPE), b.astype(MXU_DTYPE), preferred_element_type=F32)


def _mm_nt(a, b):
    return lax.dot_general(a.astype(MXU_DTYPE), b.astype(MXU_DTYPE), (((1,), (1,)), ((), ())),
                           preferred_element_type=F32)


def _mm_tn(a, b):
    return lax.dot_general(a.astype(MXU_DTYPE), b.astype(MXU_DTYPE), (((0,), (0,)), ((), ())),
                           preferred_element_type=F32)


def _split3(x):
    hi = x.astype(MXU_DTYPE)
    r1 = x - hi.astype(F32)
    mid = r1.astype(MXU_DTYPE)
    lo = (r1 - mid.astype(F32)).astype(MXU_DTYPE)
    return hi, mid, lo


def _tri_mm(tri, x):
    hi, mid, lo = _split3(x)
    return (jnp.dot(tri, lo, preferred_element_type=F32) + jnp.dot(tri, mid, preferred_element_type=F32)
            + jnp.dot(tri, hi, preferred_element_type=F32))


def _colsum8(x):
    tm, n = x.shape
    return jnp.sum(x.reshape(tm // 8, 8, n), axis=0)


def _rms(x, g):
    r = lax.rsqrt(jnp.mean(x * x, axis=-1, keepdims=True) + EPS)
    return x * r * g, r


def _rms_bwd(x, r, g, dn):
    w = dn * g
    dx = r * w - x * (r * r * r) * jnp.mean(w * x, axis=-1, keepdims=True)
    return dx, dn * x * r


def _sigmoid(x):
    return jax.nn.sigmoid(x)


def _params(n_grid):
    return pltpu.CompilerParams(dimension_semantics=("arbitrary",) * n_grid, vmem_limit_bytes=VMEM_LIMIT_BYTES)


def _rows(tm, n):
    return pl.BlockSpec((tm, n), lambda i: (i, 0))


def _rows_rev(tm, n, nt):
    return pl.BlockSpec((tm, n), lambda i: (nt - 1 - i, 0))


def _cols(n, tm):
    return pl.BlockSpec((n, tm), lambda i: (0, i))


def _cols_rev(n, tm, nt):
    return pl.BlockSpec((n, tm), lambda i: (0, nt - 1 - i))


def _whole(shape):
    nd = len(shape)
    return pl.BlockSpec(shape, lambda i: (0,) * nd)


def _row_tile(s, want):
    tm = min(s, want)
    assert s % tm == 0 and tm % 8 == 0, (s, tm)
    return tm


def conv_fwd(h, g, w1, b1, wd, bd, lg, lb, w2, b2, tm):
    S, D = h.shape
    CW = wd.shape[0]
    off = HALO - (CW - 1)
    assert 0 <= off and tm >= HALO
    nt = S // tm

    def body(h_ref, g_ref, w1_ref, b1_ref, wd_ref, bd_ref, lg_ref, lb_ref, w2_ref, b2_ref,
             ho_ref, n_ref, u_ref, z_ref, sw_ref, ext, win):
        @pl.when(pl.program_id(0) == 0)
        def _():
            ext[0:HALO, :] = jnp.zeros((HALO, D), F32)

        x = h_ref[...]
        n, _ = _rms(x, g_ref[...])
        n_ref[...] = n.astype(n_ref.dtype)
        u = _mm(n, w1_ref[...]) + b1_ref[...]
        u_ref[...] = u
        ext[HALO:HALO + tm, :] = u[:, :D] * _sigmoid(u[:, D:])
        z = jnp.broadcast_to(bd_ref[...], (tm, D))
        for b in range(8):
            amax = (CW - 1 - b) // 8
            win[0:tm + 8 * amax, :] = ext[off + b:off + b + tm + 8 * amax, :]
            for a8 in range(amax + 1):
                z = z + wd_ref[8 * a8 + b:8 * a8 + b + 1, :] * win[8 * a8:8 * a8 + tm, :]
        z_ref[...] = z
        ext[0:HALO, :] = ext[tm:tm + HALO, :]
        mu = jnp.mean(z, axis=-1, keepdims=True)
        zc = z - mu
        y = zc * lax.rsqrt(jnp.mean(zc * zc, axis=-1, keepdims=True) + EPS) * lg_ref[...] + lb_ref[...]
        sw = y * _sigmoid(y)
        sw_ref[...] = sw.astype(sw_ref.dtype)
        ho_ref[...] = x + _mm(sw, w2_ref[...]) + b2_ref[...]

    return pl.pallas_call(
        body, name="conv_fwd", grid=(nt,),
        in_specs=[_rows(tm, D), _whole((1, D)), _whole(w1.shape), _whole((1, 2 * D)), _whole(wd.shape),
                  _whole((1, D)), _whole((1, D)), _whole((1, D)), _whole(w2.shape), _whole((1, D))],
        out_specs=[_rows(tm, D), _rows(tm, D), _rows(tm, 2 * D), _rows(tm, D), _rows(tm, D)],
        out_shape=[jax.ShapeDtypeStruct((S, D), F32), jax.ShapeDtypeStruct((S, D), ACT_DTYPE),
                   jax.ShapeDtypeStruct((S, 2 * D), F32), jax.ShapeDtypeStruct((S, D), F32),
                   jax.ShapeDtypeStruct((S, D), ACT_DTYPE)],
        scratch_shapes=[pltpu.VMEM((HALO + tm, D), F32), pltpu.VMEM((HALO + tm, D), F32)],
        compiler_params=_params(1),
    )(h, g, w1, b1, wd, bd, lg, lb, w2, b2)


def ffn_fwd(h, g, w1, w2, tm):
    S, D = h.shape
    FF = w1.shape[1]

    def body(h_ref, g_ref, w1_ref, w2_ref, ho_ref, n_ref, a_ref, s_ref):
        x = h_ref[...]
        n, _ = _rms(x, g_ref[...])
        n_ref[...] = n.astype(n_ref.dtype)
        a = _mm(n, w1_ref[...])
        a_ref[...] = a
        s = jnp.square(jnp.maximum(a, 0.0))
        s_ref[...] = s.astype(s_ref.dtype)
        ho_ref[...] = x + _mm(s, w2_ref[...])

    return pl.pallas_call(
        body, name="ffn_fwd", grid=(S // tm,),
        in_specs=[_rows(tm, D), _whole((1, D)), _whole(w1.shape), _whole(w2.shape)],
        out_specs=[_rows(tm, D), _rows(tm, D), _rows(tm, FF), _rows(tm, FF)],
        out_shape=[jax.ShapeDtypeStruct((S, D), F32), jax.ShapeDtypeStruct((S, D), ACT_DTYPE),
                   jax.ShapeDtypeStruct((S, FF), F32), jax.ShapeDtypeStruct((S, FF), ACT_DTYPE)],
        compiler_params=_params(1),
    )(h, g, w1, w2)


def ple_fwd(h, g, wg, p, wp, tm):
    S, D = h.shape
    E = p.shape[1]

    def body(h_ref, g_ref, wg_ref, p_ref, wp_ref, ho_ref, n_ref, gate_ref):
        x = h_ref[...]
        n, _ = _rms(x, g_ref[...])
        n_ref[...] = n.astype(n_ref.dtype)
        gate = _sigmoid(_mm(n, wg_ref[...]))
        gate_ref[...] = gate
        ho_ref[...] = x + gate * _mm(p_ref[...], wp_ref[...])

    return pl.pallas_call(
        body, name="ple_fwd", grid=(S // tm,),
        in_specs=[_rows(tm, D), _whole((1, D)), _whole(wg.shape), _rows(tm, E), _whole(wp.shape)],
        out_specs=[_rows(tm, D), _rows(tm, D), _rows(tm, D)],
        out_shape=[jax.ShapeDtypeStruct((S, D), F32), jax.ShapeDtypeStruct((S, D), ACT_DTYPE),
                   jax.ShapeDtypeStruct((S, D), F32)],
        compiler_params=_params(1),
    )(h, g, wg, p, wp)


def kv_fwd(h, g, wk, wv, wf, bf, tm):
    S, D = h.shape
    H = wf.shape[1]

    def body(h_ref, g_ref, wk_ref, wv_ref, wf_ref, bf_ref, k_ref, kT_ref, vT_ref, n_ref, fl_ref, c_ref, carry):
        @pl.when(pl.program_id(0) == 0)
        def _():
            carry[...] = jnp.zeros_like(carry)

        n, _ = _rms(h_ref[...], g_ref[...])
        n_ref[...] = n.astype(n_ref.dtype)
        k = _mm(n, wk_ref[...])
        k_ref[...] = k.astype(k_ref.dtype)
        kT_ref[...] = k.T.astype(kT_ref.dtype)
        vT_ref[...] = _mm(n, wv_ref[...]).T.astype(vT_ref.dtype)
        fl = _mm(n, wf_ref[...]) + bf_ref[...]
        fl_ref[...] = fl
        logf = jnp.minimum(fl, 0.0) - jnp.log1p(jnp.exp(-jnp.abs(fl)))
        row = lax.broadcasted_iota(jnp.int32, (tm, tm), 0)
        col = lax.broadcasted_iota(jnp.int32, (tm, tm), 1)
        tri = (row >= col).astype(MXU_DTYPE)
        c = _tri_mm(tri, logf) + carry[...]
        c_ref[...] = c
        carry[...] = c[tm - 1:tm, :]

    return pl.pallas_call(
        body, name="kv_fwd", grid=(S // tm,),
        in_specs=[_rows(tm, D), _whole((1, D)), _whole(wk.shape), _whole(wv.shape), _whole(wf.shape),
                  _whole((1, H))],
        out_specs=[_rows(tm, D), _cols(D, tm), _cols(D, tm), _rows(tm, D), _rows(tm, H), _rows(tm, H)],
        out_shape=[jax.ShapeDtypeStruct((S, D), ACT_DTYPE), jax.ShapeDtypeStruct((D, S), ACT_DTYPE),
                   jax.ShapeDtypeStruct((D, S), ACT_DTYPE), jax.ShapeDtypeStruct((S, D), ACT_DTYPE),
                   jax.ShapeDtypeStruct((S, H), F32), jax.ShapeDtypeStruct((S, H), F32)],
        scratch_shapes=[pltpu.VMEM((1, H), F32)],
        compiler_params=_params(1),
    )(h, g, wk, wv, wf, bf)


def q_fwd(h, g, wq, scale, tm):
    S, D = h.shape

    def body(h_ref, g_ref, wq_ref, n_ref, q_ref, qT_ref):
        n, _ = _rms(h_ref[...], g_ref[...])
        n_ref[...] = n.astype(n_ref.dtype)
        q = _mm(n, wq_ref[...]) * scale
        q_ref[...] = q.astype(q_ref.dtype)
        qT_ref[...] = q.T.astype(qT_ref.dtype)

    return pl.pallas_call(
        body, name="q_fwd", grid=(S // tm,),
        in_specs=[_rows(tm, D), _whole((1, D)), _whole(wq.shape)],
        out_specs=[_rows(tm, D), _rows(tm, D), _cols(D, tm)],
        out_shape=[jax.ShapeDtypeStruct((S, D), ACT_DTYPE), jax.ShapeDtypeStruct((S, D), ACT_DTYPE),
                   jax.ShapeDtypeStruct((D, S), ACT_DTYPE)],
        compiler_params=_params(1),
    )(h, g, wq)


def attn_out_fwd(h, o, wo, tm):
    S, D = h.shape

    def body(h_ref, o_ref, wo_ref, ho_ref):
        ho_ref[...] = h_ref[...] + _mm(o_ref[...], wo_ref[...])

    return pl.pallas_call(
        body, name="attn_out_fwd", grid=(S // tm,),
        in_specs=[_rows(tm, D), _rows(tm, D), _whole(wo.shape)],
        out_specs=_rows(tm, D),
        out_shape=jax.ShapeDtypeStruct((S, D), F32),
        compiler_params=_params(1),
    )(h, o, wo)


def _causal_mask(key0, qry0, shape, key_axis):
    key = key0 + lax.broadcasted_iota(jnp.int32, shape, key_axis)
    qry = qry0 + lax.broadcasted_iota(jnp.int32, shape, 1 - key_axis)
    return key <= qry


def flash_fwd(qT, k, vT, c_col, c_row, dh, tq, tkc):
    D, S = qT.shape
    hg = 128 // dh
    G = D // 128
    per = tq // tkc
    assert tq % tkc == 0 and S % tq == 0

    def body(qT_ref, k_ref, vT_ref, ccol_ref, crow_ref, o_ref, o32_ref, lse_ref, m_scr, l_scr, acc_scr, mx_scr,
             *scratch):
        i = pl.program_id(1)
        m_scr[...] = jnp.full(m_scr.shape, NEG_BIG, F32)
        l_scr[...] = jnp.zeros(l_scr.shape, F32)
        acc_scr[...] = jnp.zeros(acc_scr.shape, F32)

        def head_shift(j, hh):
            c0 = ccol_ref[pl.ds(j * tkc, 1), hh:hh + 1]
            return c0, crow_ref[hh:hh + 1, :] - c0

        def scores(j, masked, slot):
            keys = pl.ds(pl.multiple_of(j * tkc, tkc), tkc)
            s_scr = scratch[slot]
            for hh in range(hg):
                lanes = slice(hh * dh, (hh + 1) * dh)
                c0, r = head_shift(j, hh)
                s = _mm(k_ref[keys, lanes], qT_ref[lanes, :]) - (ccol_ref[keys, hh:hh + 1] - c0)
                if masked:
                    s = jnp.where(_causal_mask(j * tkc, i * tq, (tkc, tq), 0), s, NEG_BIG)
                s_scr[hh] = s
                mx_scr[slot * hg + hh] = jnp.max(s, axis=0, keepdims=True) + r

        def update(j, slot):
            keys = pl.ds(pl.multiple_of(j * tkc, tkc), tkc)
            s_scr = scratch[slot]
            for hh in range(hg):
                lanes = slice(hh * dh, (hh + 1) * dh)
                _, r = head_shift(j, hh)
                m_old = m_scr[hh]
                m_new = jnp.maximum(m_old, mx_scr[slot * hg + hh])
                alpha = jnp.exp(m_old - m_new)
                p = jnp.exp(s_scr[hh] - (m_new - r))
                l_scr[hh] = alpha * l_scr[hh] + jnp.sum(p, axis=0, keepdims=True)
                acc_scr[lanes, :] = alpha * acc_scr[lanes, :] + _mm(vT_ref[lanes, keys], p)
                m_scr[hh] = m_new

        def chunks(j0, masked):
            for jj in range(per):
                scores(j0 + jj, masked, jj)
            for jj in range(per):
                update(j0 + jj, jj)

        def full_chunks(jb, carry):
            chunks(jb * per, False)
            return carry

        lax.fori_loop(0, i, full_chunks, 0)
        chunks(i * per, True)
        for hh in range(hg):
            lanes = slice(hh * dh, (hh + 1) * dh)
            acc_scr[lanes, :] = acc_scr[lanes, :] / l_scr[hh]
            lse_ref[hh:hh + 1, :] = m_scr[hh] + jnp.log(l_scr[hh])
        o = acc_scr[...].T
        o_ref[...] = o.astype(o_ref.dtype)
        o32_ref[...] = o

    return pl.pallas_call(
        body, name="flash_fwd", grid=(G, S // tq),
        in_specs=[pl.BlockSpec((128, tq), lambda g, i: (g, i)),
                  pl.BlockSpec((S, 128), lambda g, i: (0, g)),
                  pl.BlockSpec((128, S), lambda g, i: (g, 0)),
                  pl.BlockSpec((None, S, hg), lambda g, i: (g, 0, 0)),
                  pl.BlockSpec((None, hg, tq), lambda g, i: (g, 0, i))],
        out_specs=[pl.BlockSpec((tq, 128), lambda g, i: (i, g)),
                   pl.BlockSpec((tq, 128), lambda g, i: (i, g)),
                   pl.BlockSpec((None, hg, tq), lambda g, i: (g, 0, i))],
        out_shape=[jax.ShapeDtypeStruct((S, D), ACT_DTYPE), jax.ShapeDtypeStruct((S, D), F32),
                   jax.ShapeDtypeStruct((G, hg, S), F32)],
        scratch_shapes=([pltpu.VMEM((hg, 1, tq), F32), pltpu.VMEM((hg, 1, tq), F32), pltpu.VMEM((128, tq), F32),
                         pltpu.VMEM((per * hg, 1, tq), F32)]
                        + [pltpu.VMEM((hg, tkc, tq), F32)] * per),
        compiler_params=_params(2),
    )(qT, k, vT, c_col, c_row)


def loss_head(h, g, target, tm):
    S, D = h.shape
    nt = S // tm

    def body(h_ref, g_ref, t_ref, dh_ref, dg_ref, loss_ref, dg_acc, loss_acc):
        i = pl.program_id(0)

        @pl.when(i == 0)
        def _():
            dg_acc[...] = jnp.zeros_like(dg_acc)
            loss_acc[...] = jnp.zeros_like(loss_acc)

        x = h_ref[...]
        gg = g_ref[...]
        y, r = _rms(x, gg)
        e = y - t_ref[...]
        loss_acc[...] += 0.5 * jnp.sum(jnp.mean(e * e, axis=-1, keepdims=True), axis=0, keepdims=True)
        dx, dgr = _rms_bwd(x, r, gg, e / D)
        dh_ref[...] = dx
        dg_acc[...] += _colsum8(dgr)

        @pl.when(i == nt - 1)
        def _():
            dg_ref[...] = jnp.sum(dg_acc[...], axis=0, keepdims=True)
            loss_ref[...] = jnp.broadcast_to(loss_acc[...], loss_ref.shape)

    return pl.pallas_call(
        body, name="loss_head", grid=(nt,),
        in_specs=[_rows(tm, D), _whole((1, D)), _rows(tm, D)],
        out_specs=[_rows(tm, D), _whole((1, D)), _whole((1, 128))],
        out_shape=[jax.ShapeDtypeStruct((S, D), F32), jax.ShapeDtypeStruct((1, D), F32),
                   jax.ShapeDtypeStruct((1, 128), F32)],
        scratch_shapes=[pltpu.VMEM((8, D), F32), pltpu.VMEM((1, 1), F32)],
        compiler_params=_params(1),
    )(h, g, target)


def ple_bwd(d, h, g, wg, gate, p, wp, tm):
    S, D = h.shape
    E = p.shape[1]
    nt = S // tm

    def body(d_ref, h_ref, g_ref, wg_ref, gate_ref, p_ref, wp_ref, di_ref, dz_ref, dpp_ref, dg_ref, dg_acc):
        i = pl.program_id(0)

        @pl.when(i == 0)
        def _():
            dg_acc[...] = jnp.zeros_like(dg_acc)

        dd = d_ref[...]
        x = h_ref[...]
        gg = g_ref[...]
        gt = gate_ref[...]
        pp = _mm(p_ref[...], wp_ref[...])
        dpp_ref[...] = (dd * gt).astype(dpp_ref.dtype)
        dz = dd * pp * gt * (1.0 - gt)
        dz_ref[...] = dz.astype(dz_ref.dtype)
        r = lax.rsqrt(jnp.mean(x * x, axis=-1, keepdims=True) + EPS)
        dx, dgr = _rms_bwd(x, r, gg, _mm_nt(dz, wg_ref[...]))
        di_ref[...] = dd + dx
        dg_acc[...] += _colsum8(dgr)

        @pl.when(i == nt - 1)
        def _():
            dg_ref[...] = jnp.sum(dg_acc[...], axis=0, keepdims=True)

    return pl.pallas_call(
        body, name="ple_bwd", grid=(nt,),
        in_specs=[_rows(tm, D), _rows(tm, D), _whole((1, D)), _whole(wg.shape), _rows(tm, D), _rows(tm, E),
                  _whole(wp.shape)],
        out_specs=[_rows(tm, D), _rows(tm, D), _rows(tm, D), _whole((1, D))],
        out_shape=[jax.ShapeDtypeStruct((S, D), F32), jax.ShapeDtypeStruct((S, D), ACT_DTYPE),
                   jax.ShapeDtypeStruct((S, D), ACT_DTYPE), jax.ShapeDtypeStruct((1, D), F32)],
        scratch_shapes=[pltpu.VMEM((8, D), F32)],
        compiler_params=_params(1),
    )(d, h, g, wg, gate, p, wp)


def ffn_bwd(d, h, g, w1, w2, a, tm):
    S, D = h.shape
    FF = w1.shape[1]
    nt = S // tm

    def body(d_ref, h_ref, g_ref, w1_ref, w2_ref, a_ref, di_ref, da_ref, dg_ref, dg_acc):
        i = pl.program_id(0)

        @pl.when(i == 0)
        def _():
            dg_acc[...] = jnp.zeros_like(dg_acc)

        dd = d_ref[...]
        x = h_ref[...]
        da = _mm_nt(dd, w2_ref[...]) * (2.0 * jnp.maximum(a_ref[...], 0.0))
        da_ref[...] = da.astype(da_ref.dtype)
        r = lax.rsqrt(jnp.mean(x * x, axis=-1, keepdims=True) + EPS)
        dx, dgr = _rms_bwd(x, r, g_ref[...], _mm_nt(da, w1_ref[...]))
        di_ref[...] = dd + dx
        dg_acc[...] += _colsum8(dgr)

        @pl.when(i == nt - 1)
        def _():
            dg_ref[...] = jnp.sum(dg_acc[...], axis=0, keepdims=True)

    return pl.pallas_call(
        body, name="ffn_bwd", grid=(nt,),
        in_specs=[_rows(tm, D), _rows(tm, D), _whole((1, D)), _whole(w1.shape), _whole(w2.shape), _rows(tm, FF)],
        out_specs=[_rows(tm, D), _rows(tm, FF), _whole((1, D))],
        out_shape=[jax.ShapeDtypeStruct((S, D), F32), jax.ShapeDtypeStruct((S, FF), ACT_DTYPE),
                   jax.ShapeDtypeStruct((1, D), F32)],
        scratch_shapes=[pltpu.VMEM((8, D), F32)],
        compiler_params=_params(1),
    )(d, h, g, w1, w2, a)


def attn_out_bwd(d, wo, o32, dh, tm):
    S, D = d.shape
    H = D // dh

    def body(d_ref, wo_ref, o_ref, do_ref, doT_ref, delta_ref):
        do32 = _mm_nt(d_ref[...], wo_ref[...])
        do = do32.astype(do_ref.dtype)
        do_ref[...] = do
        doT_ref[...] = do32.T.astype(doT_ref.dtype)
        lane_head = lax.broadcasted_iota(jnp.int32, (D, H), 0) // dh
        seg = (lane_head == lax.broadcasted_iota(jnp.int32, (D, H), 1)).astype(MXU_DTYPE)
        hi, mid, lo = _split3(do.astype(F32) * o_ref[...])
        delta_ref[...] = (jnp.dot(lo, seg, preferred_element_type=F32) + jnp.dot(mid, seg, preferred_element_type=F32)
                          + jnp.dot(hi, seg, preferred_element_type=F32))

    return pl.pallas_call(
        body, name="attn_out_bwd", grid=(S // tm,),
        in_specs=[_rows(tm, D), _whole(wo.shape), _rows(tm, D)],
        out_specs=[_rows(tm, D), _cols(D, tm), _rows(tm, H)],
        out_shape=[jax.ShapeDtypeStruct((S, D), ACT_DTYPE), jax.ShapeDtypeStruct((D, S), ACT_DTYPE),
                   jax.ShapeDtypeStruct((S, H), F32)],
        compiler_params=_params(1),
    )(d, wo, o32)


def flash_bwd(q, qT, k, kT, vT, c_row, do, doT, qstat, dh, tk, tqc):
    S, D = q.shape
    hg = 128 // dh
    G = D // 128
    per = tk // tqc
    nchunk = S // tqc
    assert hg <= 8 and tk % tqc == 0 and S % tk == 0

    def body(q_ref, qT_ref, k_ref, kT_ref, vT_ref, crow_ref, do_ref, doT_ref, st_ref,
             dq_ref, dkT_ref, dvT_ref, dck_ref, dcq_ref):
        ki = pl.program_id(1)

        @pl.when(ki == 0)
        def _():
            dq_ref[...] = jnp.zeros_like(dq_ref)
            dcq_ref[...] = jnp.zeros_like(dcq_ref)

        dck_ref[...] = jnp.zeros_like(dck_ref)
        dkT_ref[...] = jnp.zeros_like(dkT_ref)
        dvT_ref[...] = jnp.zeros_like(dvT_ref)
        def chunk(jq, masked):
            rows = pl.ds(pl.multiple_of(jq * tqc, tqc), tqc)
            st = st_ref[rows, :]
            for hh in range(hg):
                lanes = slice(hh * dh, (hh + 1) * dh)
                kh = k_ref[:, lanes]
                ck = crow_ref[hh:hh + 1, :]
                c0 = ck[:, 0:1]
                u = (st[:, hh:hh + 1] - c0) - st[:, hg + hh:hg + hh + 1]
                s = (_mm(q_ref[rows, lanes], kT_ref[lanes, :]) - (ck - c0)) + u
                if masked:
                    s = jnp.where(_causal_mask(ki * tk, jq * tqc, (tqc, tk), 1), s, NEG_BIG)
                p = jnp.exp(s)
                dvT_ref[lanes, :] += _mm(doT_ref[lanes, rows], p)
                ds = p * (_mm(do_ref[rows, lanes], vT_ref[lanes, :]) - st[:, 2 * hg + hh:2 * hg + hh + 1])
                dkT_ref[lanes, :] += _mm(qT_ref[lanes, rows], ds)
                dck_ref[hh:hh + 1, :] -= jnp.sum(ds, axis=0, keepdims=True)
                dcq_ref[rows, hh:hh + 1] += jnp.sum(ds, axis=1, keepdims=True)
                dq_ref[rows, lanes] += _mm(ds, kh)

        for jj in range(per):
            chunk(ki * per + jj, True)

        def full_chunk(jq, carry):
            chunk(jq, False)
            return carry

        lax.fori_loop((ki + 1) * per, nchunk, full_chunk, 0)

    return pl.pallas_call(
        body, name="flash_bwd", grid=(G, S // tk),
        in_specs=[pl.BlockSpec((S, 128), lambda g, j: (0, g)),
                  pl.BlockSpec((128, S), lambda g, j: (g, 0)),
                  pl.BlockSpec((tk, 128), lambda g, j: (j, g)),
                  pl.BlockSpec((128, tk), lambda g, j: (g, j)),
                  pl.BlockSpec((128, tk), lambda g, j: (g, j)),
                  pl.BlockSpec((None, hg, tk), lambda g, j: (g, 0, j)),
                  pl.BlockSpec((S, 128), lambda g, j: (0, g)),
                  pl.BlockSpec((128, S), lambda g, j: (g, 0)),
                  pl.BlockSpec((None, S, 3 * hg), lambda g, j: (g, 0, 0))],
        out_specs=[pl.BlockSpec((S, 128), lambda g, j: (0, g)),
                   pl.BlockSpec((128, tk), lambda g, j: (g, j)),
                   pl.BlockSpec((128, tk), lambda g, j: (g, j)),
                   pl.BlockSpec((None, 8, tk), lambda g, j: (g, 0, j)),
                   pl.BlockSpec((None, S, hg), lambda g, j: (g, 0, 0))],
        out_shape=[jax.ShapeDtypeStruct((S, D), F32), jax.ShapeDtypeStruct((D, S), F32),
                   jax.ShapeDtypeStruct((D, S), F32), jax.ShapeDtypeStruct((G, 8, S), F32),
                   jax.ShapeDtypeStruct((G, S, hg), F32)],
        compiler_params=_params(2),
    )(q, qT, k, kT, vT, c_row, do, doT, qstat)


def q_bwd(d, dq, h, g, wq, scale, tm):
    S, D = h.shape
    nt = S // tm

    def body(d_ref, dq_ref, h_ref, g_ref, wq_ref, di_ref, dqs_ref, dg_ref, dg_acc):
        i = pl.program_id(0)

        @pl.when(i == 0)
        def _():
            dg_acc[...] = jnp.zeros_like(dg_acc)

        x = h_ref[...]
        dqs = dq_ref[...] * scale
        dqs_ref[...] = dqs.astype(dqs_ref.dtype)
        r = lax.rsqrt(jnp.mean(x * x, axis=-1, keepdims=True) + EPS)
        dx, dgr = _rms_bwd(x, r, g_ref[...], _mm_nt(dqs, wq_ref[...]))
        di_ref[...] = d_ref[...] + dx
        dg_acc[...] += _colsum8(dgr)

        @pl.when(i == nt - 1)
        def _():
            dg_ref[...] = jnp.sum(dg_acc[...], axis=0, keepdims=True)

    return pl.pallas_call(
        body, name="q_bwd", grid=(nt,),
        in_specs=[_rows(tm, D), _rows(tm, D), _rows(tm, D), _whole((1, D)), _whole(wq.shape)],
        out_specs=[_rows(tm, D), _rows(tm, D), _whole((1, D))],
        out_shape=[jax.ShapeDtypeStruct((S, D), F32), jax.ShapeDtypeStruct((S, D), ACT_DTYPE),
                   jax.ShapeDtypeStruct((1, D), F32)],
        scratch_shapes=[pltpu.VMEM((8, D), F32)],
        compiler_params=_params(1),
    )(d, dq, h, g, wq)


def kv_bwd(d, dks, dvs, dcs, fl, h, g, wk, wv, wf, tm):
    S, D = h.shape
    H = wf.shape[1]
    nt = S // tm
    nl = len(dks)
    nc = len(dcs)

    def body(*refs):
        d_ref = refs[0]
        dk_refs = refs[1:1 + nl]
        dv_refs = refs[1 + nl:1 + 2 * nl]
        dc_refs = refs[1 + 2 * nl:1 + 2 * nl + nc]
        (fl_ref, h_ref, g_ref, wk_ref, wv_ref, wf_ref,
         di_ref, dk_ref, dv_ref, dfl_ref, dg_ref, dbf_ref, dg_acc, dbf_acc, carry) = refs[1 + 2 * nl + nc:]
        i = pl.program_id(0)

        @pl.when(i == 0)
        def _():
            dg_acc[...] = jnp.zeros_like(dg_acc)
            dbf_acc[...] = jnp.zeros_like(dbf_acc)
            carry[...] = jnp.zeros_like(carry)

        dkT = dk_refs[0][...]
        dvT = dv_refs[0][...]
        for l in range(1, nl):
            dkT = dkT + dk_refs[l][...]
            dvT = dvT + dv_refs[l][...]
        dk = dkT.T
        dv = dvT.T
        dk_ref[...] = dk.astype(dk_ref.dtype)
        dv_ref[...] = dv.astype(dv_ref.dtype)
        row = lax.broadcasted_iota(jnp.int32, (tm, tm), 0)
        col = lax.broadcasted_iota(jnp.int32, (tm, tm), 1)
        tri = (col >= row).astype(MXU_DTYPE)
        dc = dc_refs[0][...]
        for l in range(1, nc):
            dc = dc + dc_refs[l][...]
        dlogf = _tri_mm(tri, dc) + carry[...]
        carry[...] = dlogf[0:1, :]
        dfl = dlogf * _sigmoid(-fl_ref[...])
        dfl_ref[...] = dfl
        dbf_acc[...] += jnp.sum(dfl, axis=0, keepdims=True)
        x = h_ref[...]
        dn = _mm_nt(dk, wk_ref[...]) + _mm_nt(dv, wv_ref[...]) + _mm_nt(dfl, wf_ref[...])
        r = lax.rsqrt(jnp.mean(x * x, axis=-1, keepdims=True) + EPS)
        dx, dgr = _rms_bwd(x, r, g_ref[...], dn)
        di_ref[...] = d_ref[...] + dx
        dg_acc[...] += _colsum8(dgr)

        @pl.when(i == nt - 1)
        def _():
            dg_ref[...] = jnp.sum(dg_acc[...], axis=0, keepdims=True)
            dbf_ref[...] = dbf_acc[...]

    rev = lambda n: _rows_rev(tm, n, nt)
    return pl.pallas_call(
        body, name="kv_bwd", grid=(nt,),
        in_specs=([rev(D)] + [_cols_rev(D, tm, nt)] * (2 * nl)
                  + [rev(H)] * nc
                  + [rev(H), rev(D), _whole((1, D)), _whole(wk.shape), _whole(wv.shape), _whole(wf.shape)]),
        out_specs=[rev(D), rev(D), rev(D), rev(H), _whole((1, D)), _whole((1, H))],
        out_shape=[jax.ShapeDtypeStruct((S, D), F32), jax.ShapeDtypeStruct((S, D), ACT_DTYPE),
                   jax.ShapeDtypeStruct((S, D), ACT_DTYPE), jax.ShapeDtypeStruct((S, H), F32),
                   jax.ShapeDtypeStruct((1, D), F32), jax.ShapeDtypeStruct((1, H), F32)],
        scratch_shapes=[pltpu.VMEM((8, D), F32), pltpu.VMEM((1, H), F32), pltpu.VMEM((1, H), F32)],
        compiler_params=_params(1),
    )(d, *dks, *dvs, *dcs, fl, h, g, wk, wv, wf)


def conv_bwd(d, h, g, w1, wd, lg, lb, w2, u, z, tm):
    S, D = h.shape
    CW = wd.shape[0]
    nt = S // tm
    assert tm >= HALO and CW - 1 <= HALO

    def body(d_ref, h_ref, g_ref, w1_ref, wd_ref, lg_ref, lb_ref, w2_ref, u_ref, z_ref,
             di_ref, du_ref, db2_ref, dlg_ref, dlb_ref, dbd_ref, dwd_ref, db1_ref, dg_ref,
             ext, win, db2_acc, dlg_acc, dlb_acc, dbd_acc, dwd_acc, db1_acc, dg_acc):
        i = pl.program_id(0)

        @pl.when(i == 0)
        def _():
            ext[tm:tm + HALO, :] = jnp.zeros((HALO, D), F32)
            for acc in (db2_acc, dlg_acc, dlb_acc, dbd_acc, dwd_acc, db1_acc, dg_acc):
                acc[...] = jnp.zeros_like(acc)

        dd = d_ref[...]
        db2_acc[...] += _colsum8(dd)
        dsw = _mm_nt(dd, w2_ref[...])
        zz = z_ref[...]
        zc = zz - jnp.mean(zz, axis=-1, keepdims=True)
        rs = lax.rsqrt(jnp.mean(zc * zc, axis=-1, keepdims=True) + EPS)
        xh = zc * rs
        lgv = lg_ref[...]
        y = xh * lgv + lb_ref[...]
        sg = _sigmoid(y)
        dy = dsw * (sg * (1.0 + y * (1.0 - sg)))
        dlg_acc[...] += _colsum8(dy * xh)
        dlb_acc[...] += _colsum8(dy)
        dxh = dy * lgv
        dz = rs * (dxh - jnp.mean(dxh, axis=-1, keepdims=True) - xh * jnp.mean(dxh * xh, axis=-1, keepdims=True))
        dbd_acc[...] += _colsum8(dz)
        ext[0:tm, :] = dz
        uu = u_ref[...]
        a = uu[:, :D]
        sgg = _sigmoid(uu[:, D:])
        glu = a * sgg
        dglu = jnp.zeros((tm, D), F32)
        for b in range(8):
            amax = (CW - 1 - b) // 8
            win[0:tm + 8 * amax, :] = ext[b:b + tm + 8 * amax, :]
            for a8 in range(amax + 1):
                k = CW - 1 - (8 * a8 + b)
                sh = win[8 * a8:8 * a8 + tm, :]
                dglu = dglu + wd_ref[k:k + 1, :] * sh
                dwd_acc[k] += _colsum8(glu * sh)
        ext[tm:tm + HALO, :] = ext[0:HALO, :]
        da = dglu * sgg
        dgg = dglu * a * sgg * (1.0 - sgg)
        du_ref[:, :D] = da.astype(du_ref.dtype)
        du_ref[:, D:] = dgg.astype(du_ref.dtype)
        db1_acc[:, :D] += _colsum8(da)
        db1_acc[:, D:] += _colsum8(dgg)
        dn = _mm_nt(da, w1_ref[:, :D]) + _mm_nt(dgg, w1_ref[:, D:])
        x = h_ref[...]
        r = lax.rsqrt(jnp.mean(x * x, axis=-1, keepdims=True) + EPS)
        dx, dgr = _rms_bwd(x, r, g_ref[...], dn)
        di_ref[...] = dd + dx
        dg_acc[...] += _colsum8(dgr)

        @pl.when(i == nt - 1)
        def _():
            db2_ref[...] = jnp.sum(db2_acc[...], axis=0, keepdims=True)
            dlg_ref[...] = jnp.sum(dlg_acc[...], axis=0, keepdims=True)
            dlb_ref[...] = jnp.sum(dlb_acc[...], axis=0, keepdims=True)
            dbd_ref[...] = jnp.sum(dbd_acc[...], axis=0, keepdims=True)
            dwd_ref[...] = jnp.sum(dwd_acc[...], axis=1)
            db1_ref[...] = jnp.sum(db1_acc[...], axis=0, keepdims=True)
            dg_ref[...] = jnp.sum(dg_acc[...], axis=0, keepdims=True)

    rev = lambda n: _rows_rev(tm, n, nt)
    vec = jax.ShapeDtypeStruct((1, D), F32)
    return pl.pallas_call(
        body, name="conv_bwd", grid=(nt,),
        in_specs=[rev(D), rev(D), _whole((1, D)), _whole(w1.shape), _whole(wd.shape), _whole((1, D)),
                  _whole((1, D)), _whole(w2.shape), rev(2 * D), rev(D)],
        out_specs=[rev(D), rev(2 * D), _whole((1, D)), _whole((1, D)), _whole((1, D)), _whole((1, D)),
                   _whole((CW, D)), _whole((1, 2 * D)), _whole((1, D))],
        out_shape=[jax.ShapeDtypeStruct((S, D), F32), jax.ShapeDtypeStruct((S, 2 * D), ACT_DTYPE),
                   vec, vec, vec, vec, jax.ShapeDtypeStruct((CW, D), F32),
                   jax.ShapeDtypeStruct((1, 2 * D), F32), vec],
        scratch_shapes=[pltpu.VMEM((tm + HALO, D), F32), pltpu.VMEM((tm + HALO, D), F32),
                        pltpu.VMEM((8, D), F32), pltpu.VMEM((8, D), F32),
                        pltpu.VMEM((8, D), F32), pltpu.VMEM((8, D), F32), pltpu.VMEM((CW, 8, D), F32),
                        pltpu.VMEM((8, 2 * D), F32), pltpu.VMEM((8, D), F32)],
        compiler_params=_params(1),
    )(d, h, g, w1, wd, lg, lb, w2, u, z)


def weight_grad(a, b, ts, name, column_shards=None):
    S, M = a.shape
    N = b.shape[1]
    ta = M if M <= 1024 else 1024
    tb = N if N <= 1024 else 1024
    assert M % ta == 0 and N % tb == 0 and S % ts == 0
    if column_shards is None:
        width, per_tile = tb, 1
        out_spec = pl.BlockSpec((ta, tb), lambda i, j, s: (i, j))
        out_shape = jax.ShapeDtypeStruct((M, N), F32)
    else:
        width = N // column_shards
        per_tile = tb // width
        assert tb % width == 0
        out_spec = pl.BlockSpec((per_tile, ta, width), lambda i, j, s: (j, i, 0))
        out_shape = jax.ShapeDtypeStruct((column_shards, M, width), F32)

    def body(a_ref, b_ref, o_ref):
        @pl.when(pl.program_id(2) == 0)
        def _():
            o_ref[...] = jnp.zeros_like(o_ref)

        res = _mm_tn(a_ref[...], b_ref[...])
        if column_shards is None:
            o_ref[...] += res
        else:
            for d in range(per_tile):
                o_ref[d] += res[:, d * width:(d + 1) * width]

    return pl.pallas_call(
        body, name=name, grid=(M // ta, N // tb, S // ts),
        in_specs=[pl.BlockSpec((ts, ta), lambda i, j, s: (s, i)), pl.BlockSpec((ts, tb), lambda i, j, s: (s, j))],
        out_specs=out_spec, out_shape=out_shape,
        compiler_params=_params(3),
    )(a, b)


def _position():
    return lax.axis_index("x"), lax.axis_index("y"), lax.axis_index("c")


def all_gather(x, name):
    def body(x_ref, out_ref, send_sems, recv_sems, local_sem):
        x, y, c = _position()
        me, sibling = (x, y, c), (x, y, 1 - c)
        chips = [(1 - x, y), (x, 1 - y), (1 - x, 1 - y)]

        def slot(px, py, pc):
            return out_ref.at[4 * px + 2 * py + pc]

        def copy(k, block, to, src=None):
            return pltpu.make_async_remote_copy(
                src_ref=slot(*block) if src is None else src, dst_ref=slot(*block),
                send_sem=send_sems.at[k], recv_sem=recv_sems.at[k], device_id=to, device_id_type=MESH)

        mine = pltpu.make_async_copy(x_ref, slot(*me), local_sem)
        mine.start()
        first = [copy(0, me, sibling, src=x_ref)]
        first += [copy(1 + j, me, (*chip, c), src=x_ref) for j, chip in enumerate(chips)]
        for cp in first:
            cp.start()
        passed = [copy(4 + j, (*chip, c), sibling) for j, chip in enumerate(chips)]
        for j, chip in enumerate(chips):
            copy(1 + j, (*chip, c), me).wait_recv()
            passed[j].start()
        copy(0, sibling, me).wait_recv()
        for j, chip in enumerate(chips):
            copy(4 + j, (*chip, 1 - c), me).wait_recv()
        for cp in first + passed:
            cp.wait_send()
        mine.wait()

    return pl.pallas_call(
        body, name=name,
        in_specs=[pl.BlockSpec(memory_space=pl.ANY)], out_specs=pl.BlockSpec(memory_space=pl.ANY),
        out_shape=jax.ShapeDtypeStruct((N_DEV,) + x.shape, x.dtype),
        scratch_shapes=[pltpu.SemaphoreType.DMA((7,)), pltpu.SemaphoreType.DMA((7,)), pltpu.SemaphoreType.DMA],
    )(x)


def sibling_exchange(gs):
    n = len(gs)

    def body(*refs):
        g_refs, land_refs, (send_sems, recv_sems) = refs[:n], refs[n:2 * n], refs[2 * n:]
        x, y, c = _position()
        copies = [pltpu.make_async_remote_copy(
            src_ref=g_refs[a].at[2 * j + 1 - c], dst_ref=land_refs[a].at[j], send_sem=send_sems.at[N_CHIP * a + j],
            recv_sem=recv_sems.at[N_CHIP * a + j], device_id=(x, y, 1 - c), device_id_type=MESH)
            for a in range(n) for j in range(N_CHIP)]
        for cp in copies:
            cp.start()
        for cp in copies:
            cp.wait()

    return pl.pallas_call(
        body, name="grad_sibling_exchange",
        in_specs=[pl.BlockSpec(memory_space=pl.ANY)] * n, out_specs=[pl.BlockSpec(memory_space=pl.ANY)] * n,
        out_shape=[jax.ShapeDtypeStruct((N_CHIP,) + g.shape[1:], g.dtype) for g in gs],
        scratch_shapes=[pltpu.SemaphoreType.DMA((N_CHIP * n,)), pltpu.SemaphoreType.DMA((N_CHIP * n,))],
    )(*gs)


def chip_partial(g, land, core):
    _, R, C = g.shape
    tr = _row_tile(R, PACK_ROW_TILE)

    def body(c_ref, g_ref, l_ref, o_ref):
        o_ref[...] = (g_ref[...] + l_ref[...]).astype(o_ref.dtype)

    grid_spec = pltpu.PrefetchScalarGridSpec(
        num_scalar_prefetch=1, grid=(N_CHIP, R // tr),
        in_specs=[pl.BlockSpec((None, tr, C), lambda j, i, cr: (2 * j + cr[0], i, 0)),
                  pl.BlockSpec((None, tr, C), lambda j, i, cr: (j, i, 0))],
        out_specs=pl.BlockSpec((None, tr, C), lambda j, i, cr: (j, i, 0)))
    return pl.pallas_call(
        body, name="grad_chip_partial", grid_spec=grid_spec,
        out_shape=jax.ShapeDtypeStruct((N_CHIP, R, C), WIRE_DTYPE),
        compiler_params=_params(2),
    )(core, g, land)


def chip_exchange(parts):
    n = len(parts)

    def body(*refs):
        p_refs, land_refs, (send_sems, recv_sems, local_sems) = refs[:n], refs[n:2 * n], refs[2 * n:]
        x, y, c = _position()
        mychip = 2 * x + y
        chips = [(1 - x, y), (x, 1 - y), (1 - x, 1 - y)]

        def remote(a, k, slot):
            cx, cy = chips[k]
            return pltpu.make_async_remote_copy(
                src_ref=p_refs[a].at[2 * cx + cy], dst_ref=land_refs[a].at[slot], send_sem=send_sems.at[3 * a + k],
                recv_sem=recv_sems.at[3 * a + k], device_id=(cx, cy, c), device_id_type=MESH)

        mine = [pltpu.make_async_copy(p_refs[a].at[mychip], land_refs[a].at[mychip], local_sems.at[a])
                for a in range(n)]
        for cp in mine:
            cp.start()
        copies = [remote(a, k, mychip) for a in range(n) for k in range(3)]
        for cp in copies:
            cp.start()
        for a in range(n):
            for k, (cx, cy) in enumerate(chips):
                remote(a, k, 2 * cx + cy).wait_recv()
        for cp in copies:
            cp.wait_send()
        for cp in mine:
            cp.wait()

    return pl.pallas_call(
        body, name="grad_chip_exchange",
        in_specs=[pl.BlockSpec(memory_space=pl.ANY)] * n, out_specs=[pl.BlockSpec(memory_space=pl.ANY)] * n,
        out_shape=[jax.ShapeDtypeStruct(p.shape, p.dtype) for p in parts],
        scratch_shapes=[pltpu.SemaphoreType.DMA((3 * n,)), pltpu.SemaphoreType.DMA((3 * n,)),
                        pltpu.SemaphoreType.DMA((n,))],
    )(*parts)


def _adamw(w, g, m, v):
    m = ADAM_B1 * m + (1.0 - ADAM_B1) * g
    v = ADAM_B2 * v + (1.0 - ADAM_B2) * jnp.square(g)
    m_hat = m / (1.0 - ADAM_B1 ** ADAM_STEP)
    v_hat = v / (1.0 - ADAM_B2 ** ADAM_STEP)
    delta = -ADAM_LR * (m_hat / (jnp.sqrt(v_hat) + ADAM_EPS) + ADAM_WD * w)
    return delta, m, v


def adamw_sharded(parts, w, m, v):
    R, C = w.shape
    tr = _row_tile(R, PACK_ROW_TILE)

    def body(p_ref, w_ref, m_ref, v_ref, g_ref, d_ref, nm_ref, nv_ref):
        g = p_ref[0].astype(F32)
        for j in range(1, N_CHIP):
            g = g + p_ref[j].astype(F32)
        g_ref[...] = g
        d_ref[...], nm_ref[...], nv_ref[...] = _adamw(w_ref[...], g, m_ref[...], v_ref[...])

    out = jax.ShapeDtypeStruct((R, C), F32)
    return pl.pallas_call(
        body, name="adamw_sharded", grid=(R // tr,),
        in_specs=[pl.BlockSpec((N_CHIP, tr, C), lambda i: (0, i, 0)), _rows(tr, C), _rows(tr, C), _rows(tr, C)],
        out_specs=[_rows(tr, C)] * 4, out_shape=[out] * 4,
        compiler_params=_params(1),
    )(parts, w, m, v)


def adamw_replicated(gathered, w, m, v):
    R, C = w.shape

    def body(p_ref, w_ref, m_ref, v_ref, g_ref, d_ref, nm_ref, nv_ref):
        g = p_ref[0]
        for j in range(1, N_DEV):
            g = g + p_ref[j]
        g_ref[...] = g
        d_ref[...], nm_ref[...], nv_ref[...] = _adamw(w_ref[...], g, m_ref[...], v_ref[...])

    out = jax.ShapeDtypeStruct((R, C), F32)
    return pl.pallas_call(
        body, name="adamw_replicated", grid=(1,),
        in_specs=[_whole(gathered.shape), _whole((R, C)), _whole((R, C)), _whole((R, C))],
        out_specs=[_whole((R, C))] * 4, out_shape=[out] * 4,
        compiler_params=_params(1),
    )(gathered, w, m, v)


def _piece_rows(n):
    return -(-n // PACK_COLS)


def _as_rows(a, lead):
    flat = a.reshape(a.shape[:lead] + (-1,))
    fill = _piece_rows(flat.shape[-1]) * PACK_COLS - flat.shape[-1]
    if fill:
        flat = jnp.pad(flat, [(0, 0)] * lead + [(0, fill)])
    return flat.reshape(flat.shape[:-1] + (-1, PACK_COLS))


def _pack(arrays, rows_multiple, dtype=None, lead=0):
    pieces = [_as_rows(a if dtype is None else a.astype(dtype), lead) for a in arrays]
    extra = -sum(p.shape[lead] for p in pieces) % rows_multiple
    if extra:
        pieces.append(jnp.zeros(pieces[0].shape[:lead] + (extra, PACK_COLS), pieces[0].dtype))
    return jnp.concatenate(pieces, axis=lead)


def _unpack(packed, shapes, lead=0):
    out, r0 = [], 0
    for shp in shapes:
        n = int(np.prod(shp))
        rows = _piece_rows(n)
        seg = lax.slice_in_dim(packed, r0, r0 + rows, axis=lead).reshape(packed.shape[:lead] + (-1,))
        if rows * PACK_COLS != n:
            seg = seg[..., :n]
        out.append(seg.reshape(packed.shape[:lead] + tuple(shp)))
        r0 += rows
    return out


def _full_from_gathered(gathered, shard_shapes, axes):
    out = []
    for seg, shp, ax in zip(_unpack(gathered, shard_shapes, lead=1), shard_shapes, axes):
        seg = jnp.moveaxis(seg, 0, ax)
        out.append(seg.reshape(tuple(shp[:ax]) + (N_DEV * shp[ax],) + tuple(shp[ax + 1:])))
    return out


def kernel(x, p, mix_norm, conv_w_pw1, conv_b_pw1, conv_w_dw, conv_b_dw, conv_ln_g, conv_ln_b, conv_w_pw2, conv_b_pw2, kv_norm, w_kvf, b_f, attn_w_q, attn_w_o, ffn_norm, ffn_w1, ffn_w2, ple_norm, ple_w_gate, ple_w_proj, final_norm, loss_target, m_mix_norm, m_conv_w_pw1, m_conv_b_pw1, m_conv_w_dw, m_conv_b_dw, m_conv_ln_g, m_conv_ln_b, m_conv_w_pw2, m_conv_b_pw2, m_kv_norm, m_w_kvf, m_b_f, m_attn_w_q, m_attn_w_o, m_ffn_norm, m_ffn_w1, m_ffn_w2, m_ple_norm, m_ple_w_gate, m_ple_w_proj, m_final_norm, v_mix_norm, v_conv_w_pw1, v_conv_b_pw1, v_conv_w_dw, v_conv_b_dw, v_conv_ln_g, v_conv_ln_b, v_conv_w_pw2, v_conv_b_pw2, v_kv_norm, v_w_kvf, v_b_f, v_attn_w_q, v_attn_w_o, v_ffn_norm, v_ffn_w1, v_ffn_w2, v_ple_norm, v_ple_w_gate, v_ple_w_proj, v_final_norm):
    given = dict(locals())
    W = {n: given[n] for n in WEIGHTS}
    M = {n: given["m_" + n] for n in WEIGHTS}
    V = {n: given["v_" + n] for n in WEIGHTS}

    _, S, D = x.shape
    NA = conv_w_pw1.shape[0]
    NB = attn_w_q.shape[0]
    DEPTH = NA + NB
    H = b_f.shape[0]
    dh = D // H
    hg = 128 // dh
    G = D // 128
    scale = dh ** -0.5
    tm = _row_tile(S, 256)
    tq_f = _row_tile(S, FLASH_FWD_TILE[0])
    tkc_f = _row_tile(tq_f, FLASH_FWD_TILE[1])
    tk_b = _row_tile(S, FLASH_BWD_TILE[0])
    tqc_b = _row_tile(tk_b, FLASH_BWD_TILE[1])
    ts = _row_tile(S, 1024)
    xs = x[0]
    tgt = loss_target[0]
    ps = p[:, 0]
    row = lambda a: a.reshape(1, -1)

    big_names = list(SHARD_AXIS_BIG)
    small_names = list(SHARD_AXIS_SMALL)
    big = _full_from_gathered(
        all_gather(_pack([W[n] for n in big_names], 16, MXU_DTYPE), "weights_all_gather"),
        [W[n].shape for n in big_names], [SHARD_AXIS_BIG[n] for n in big_names])
    small = _full_from_gathered(
        all_gather(_pack([W[n] for n in small_names], 8), "vectors_all_gather"),
        [W[n].shape for n in small_names], [SHARD_AXIS_SMALL[n] for n in small_names])
    FW = dict(zip(big_names + small_names, big + small))
    wk, wv, wf = FW["w_kvf"][:, :D], FW["w_kvf"][:, D:2 * D], FW["w_kvf"][:, 2 * D:]

    saved = []
    h = xs
    kv = None
    for i in range(DEPTH):
        rec = {"h_in": h}
        if i < NA:
            h, rec["n"], rec["u"], rec["z"], rec["sw"] = conv_fwd(
                h, row(mix_norm[i]), FW["conv_w_pw1"][i], row(FW["conv_b_pw1"][i]), FW["conv_w_dw"][i],
                row(FW["conv_b_dw"][i]), row(FW["conv_ln_g"][i]), row(FW["conv_ln_b"][i]),
                FW["conv_w_pw2"][i], row(FW["conv_b_pw2"][i]), tm)
        else:
            j = i - NA
            if j == 0:
                k_, kT_, vT_, nkv, fl, c = kv_fwd(h, row(kv_norm), wk, wv, wf, row(b_f), tm)
                cg = c.reshape(S, G, hg)
                kv = dict(k=k_, kT=kT_, vT=vT_, n=nkv, fl=fl, h=h, c_col=jnp.transpose(cg, (1, 0, 2)),
                          c_row=jnp.transpose(cg, (1, 2, 0)))
            rec["n"], rec["q"], rec["qT"] = q_fwd(h, row(mix_norm[i]), FW["attn_w_q"][j], scale, tm)
            rec["o"], rec["o32"], rec["lse"] = flash_fwd(rec["qT"], kv["k"], kv["vT"], kv["c_col"], kv["c_row"], dh,
                                                         tq_f, tkc_f)
            h = attn_out_fwd(h, rec["o"], FW["attn_w_o"][j], tm)
        rec["h_ffn"] = h
        h, rec["n_ffn"], rec["a"], rec["s"] = ffn_fwd(h, row(ffn_norm[i]), FW["ffn_w1"][i], FW["ffn_w2"][i], tm)
        rec["h_ple"] = h
        h, rec["n_ple"], rec["gate"] = ple_fwd(h, row(ple_norm[i]), FW["ple_w_gate"][i], ps[i],
                                               FW["ple_w_proj"][i], tm)
        saved.append(rec)

    d, g_final, loss_part = loss_head(h, row(final_norm), tgt, tm)
    GW = {n: [None] * W[n].shape[0] for n in WEIGHTS if W[n].ndim > 1 and n != "w_kvf"}
    dks, dvs, dcs = [], [], []
    for i in reversed(range(DEPTH)):
        rec = saved[i]
        d_out = d
        d, dz, dpp, GW["ple_norm"][i] = ple_bwd(d_out, rec["h_ple"], row(ple_norm[i]), FW["ple_w_gate"][i],
                                                rec["gate"], ps[i], FW["ple_w_proj"][i], tm)
        GW["ple_w_gate"][i] = weight_grad(rec["n_ple"], dz, ts, "grad_ple_w_gate")
        GW["ple_w_proj"][i] = weight_grad(ps[i], dpp, ts, "grad_ple_w_proj", column_shards=N_DEV)
        d_out = d
        d, da, GW["ffn_norm"][i] = ffn_bwd(d_out, rec["h_ffn"], row(ffn_norm[i]), FW["ffn_w1"][i],
                                           FW["ffn_w2"][i], rec["a"], tm)
        GW["ffn_w2"][i] = weight_grad(rec["s"], d_out, ts, "grad_ffn_w2")
        GW["ffn_w1"][i] = weight_grad(rec["n_ffn"], da, ts, "grad_ffn_w1", column_shards=N_DEV)
        d_out = d
        if i >= NA:
            j = i - NA
            GW["attn_w_o"][j] = weight_grad(rec["o"], d_out, ts, "grad_attn_w_o")
            do, doT, delta = attn_out_bwd(d_out, FW["attn_w_o"][j], rec["o32"], dh, tm)
            qstat = jnp.concatenate([kv["c_col"], jnp.transpose(rec["lse"], (0, 2, 1)),
                                     jnp.transpose(delta.reshape(S, G, hg), (1, 0, 2))], axis=2)
            dq, dkT, dvT, dck, dcq = flash_bwd(rec["q"], rec["qT"], kv["k"], kv["kT"], kv["vT"], kv["c_row"], do, doT,
                                          qstat, dh, tk_b, tqc_b)
            dks.append(dkT)
            dvs.append(dvT)
            dcs.append(jnp.transpose(dck[:, :hg, :], (2, 0, 1)).reshape(S, H))
            dcs.append(jnp.transpose(dcq, (1, 0, 2)).reshape(S, H))
            d, dqs, GW["mix_norm"][i] = q_bwd(d_out, dq, rec["h_in"], row(mix_norm[i]), FW["attn_w_q"][j], scale, tm)
            GW["attn_w_q"][j] = weight_grad(rec["n"], dqs, ts, "grad_attn_w_q")
            if j == 0:
                d, dk_sum, dv_sum, dfl, g_kv_norm, g_b_f = kv_bwd(d, dks, dvs, dcs, kv["fl"], kv["h"],
                                                                  row(kv_norm), wk, wv, wf, tm)
                g_w_kvf = jnp.concatenate([weight_grad(kv["n"], dk_sum, ts, "grad_w_k"),
                                           weight_grad(kv["n"], dv_sum, ts, "grad_w_v"),
                                           weight_grad(kv["n"], dfl, ts, "grad_w_f")], axis=1)
        else:
            GW["conv_w_pw2"][i] = weight_grad(rec["sw"], d_out, ts, "grad_conv_w_pw2")
            (d, du, GW["conv_b_pw2"][i], GW["conv_ln_g"][i], GW["conv_ln_b"][i], GW["conv_b_dw"][i],
             GW["conv_w_dw"][i], GW["conv_b_pw1"][i], GW["mix_norm"][i]) = conv_bwd(
                d_out, rec["h_in"], row(mix_norm[i]), FW["conv_w_pw1"][i], FW["conv_w_dw"][i],
                row(FW["conv_ln_g"][i]), row(FW["conv_ln_b"][i]), FW["conv_w_pw2"][i], rec["u"], rec["z"], tm)
            GW["conv_w_pw1"][i] = weight_grad(rec["n"], du, ts, "grad_conv_w_pw1", column_shards=N_DEV)
    grad_x = d[None]

    sharded_names = big_names + small_names
    GW["w_kvf"] = [g_w_kvf]

    def device_major(n, g):
        width = W[n].shape[-1]
        if g.ndim == 3:
            return g
        if g.shape[-1] == width:
            return g.reshape(N_DEV, -1, width)
        return jnp.transpose(g.reshape(-1, N_DEV, width), (1, 0, 2))

    widths = sorted({W[n].shape[-1] for n in sharded_names}, reverse=True)
    groups = [[n for n in sharded_names if W[n].shape[-1] == width] for width in widths]

    def stack_rows(pieces, axis):
        rows = sum(p.shape[axis] for p in pieces)
        fill = -rows % (PACK_ROW_TILE if rows > PACK_ROW_TILE else 8)
        if fill:
            shape = list(pieces[0].shape)
            shape[axis] = fill
            pieces = pieces + [jnp.zeros(shape, pieces[0].dtype)]
        return jnp.concatenate(pieces, axis=axis)

    chunks = [stack_rows([device_major(n, g) for n in names for g in GW[n]], 1) for names in groups]
    core = lax.axis_index("c").astype(jnp.int32).reshape(1)
    landed = sibling_exchange(chunks)
    parts = chip_exchange([chip_partial(g, l, core) for g, l in zip(chunks, landed)])

    res = {}
    for names, width, part in zip(groups, widths, parts):
        group_rows = lambda src: stack_rows([src[n].reshape(-1, width) for n in names], 0)
        outs = adamw_sharded(part, group_rows(W), group_rows(M), group_rows(V))
        for kind, packed in zip(("grad", "delta", "new_m", "new_v"), outs):
            r0 = 0
            for n in names:
                nr = W[n].size // width
                res[kind, n] = packed[r0:r0 + nr].reshape(W[n].shape)
                r0 += nr

    rep_grads = {"mix_norm": jnp.concatenate(GW["mix_norm"], axis=0), "kv_norm": g_kv_norm,
                 "b_f": g_b_f, "ffn_norm": jnp.concatenate(GW["ffn_norm"], axis=0),
                 "ple_norm": jnp.concatenate(GW["ple_norm"], axis=0), "final_norm": g_final}

    def pack_rep(src, extra=None):
        rows_ = [jnp.pad(src[n].reshape(-1, src[n].shape[-1]), ((0, 0), (0, D - src[n].shape[-1])))
                 for n in REPLICATED]
        if extra is not None:
            rows_.append(jnp.pad(extra, ((0, 0), (0, D - extra.shape[-1]))))
        else:
            rows_.append(jnp.zeros((1, D), F32))
        flat = jnp.concatenate(rows_, axis=0)
        return jnp.pad(flat, ((0, -flat.shape[0] % 8), (0, 0)))

    rep_g = all_gather(pack_rep(rep_grads, loss_part), "replicated_all_gather")
    outs_rep = adamw_replicated(rep_g, pack_rep(W), pack_rep(M), pack_rep(V))
    n_rep_rows = sum(int(np.prod(W[n].shape[:-1])) for n in REPLICATED)
    for kind, packed in zip(("grad", "delta", "new_m", "new_v"), outs_rep):
        r0 = 0
        for n in REPLICATED:
            nr = int(np.prod(W[n].shape[:-1]))
            res[kind, n] = packed[r0:r0 + nr, :W[n].shape[-1]].reshape(W[n].shape)
            r0 += nr
    loss = outs_rep[0][n_rep_rows, 0]

    return (loss, grad_x, *[res["grad", n] for n in WEIGHTS], *[res["delta", n] for n in WEIGHTS],
            *[res["new_m", n] for n in WEIGHTS], *[res["new_v", n] for n in WEIGHTS])
```

```python
import numpy as np
import jax
import jax.numpy as jnp
from jax import lax
from jax.experimental import pallas as pl
from jax.experimental.pallas import tpu as pltpu

F32 = jnp.float32
MXU_DTYPE = jnp.bfloat16
ACT_DTYPE = jnp.bfloat16
WIRE_DTYPE = jnp.bfloat16

N_DEV = 8
N_CHIP = 4
EPS = 1e-6
NEG_BIG = -1e30
ADAM_LR = 0.001
ADAM_B1 = 0.9
ADAM_B2 = 0.999
ADAM_EPS = 1e-08
ADAM_WD = 0.01
ADAM_STEP = 10

VMEM_LIMIT_BYTES = 56 * 1024 * 1024
PACK_COLS = 1024
PACK_ROW_TILE = 256
FLASH_FWD_TILE = (1024, 512)
FLASH_BWD_TILE = (1024, 512)
FLASH_TILE = 128
HALO = 32
MESH = pl.DeviceIdType.MESH

SHARD_AXIS_BIG = {"conv_w_pw1": 2, "conv_w_pw2": 1, "w_kvf": 1, "attn_w_q": 1, "attn_w_o": 1,
                  "ffn_w1": 2, "ffn_w2": 1, "ple_w_gate": 1, "ple_w_proj": 2}
SHARD_AXIS_SMALL = {"conv_b_pw1": 1, "conv_w_dw": 2, "conv_b_dw": 1, "conv_ln_g": 1, "conv_ln_b": 1,
                    "conv_b_pw2": 1}
REPLICATED = ["mix_norm", "kv_norm", "b_f", "ffn_norm", "ple_norm", "final_norm"]
WEIGHTS = ["mix_norm", "conv_w_pw1", "conv_b_pw1", "conv_w_dw", "conv_b_dw", "conv_ln_g", "conv_ln_b",
           "conv_w_pw2", "conv_b_pw2", "kv_norm", "w_kvf", "b_f", "attn_w_q", "attn_w_o", "ffn_norm",
           "ffn_w1", "ffn_w2", "ple_norm", "ple_w_gate", "ple_w_proj", "final_norm"]


def _mm(a, b):
    return jnp.dot(a.astype(MXU_DTYPE), b.astype(MXU_DTYPE), preferred_element_type=F32)


def _mm_nt(a, b):
    return lax.dot_general(a.astype(MXU_DTYPE), b.astype(MXU_DTYPE), (((1,), (1,)), ((), ())),
                           preferred_element_type=F32)


def _mm_tn(a, b):
    return lax.dot_general(a.astype(MXU_DTYPE), b.astype(MXU_DTYPE), (((0,), (0,)), ((), ())),
                           preferred_element_type=F32)


def _split3(x):
    hi = x.astype(MXU_DTYPE)
    r1 = x - hi.astype(F32)
    mid = r1.astype(MXU_DTYPE)
    lo = (r1 - mid.astype(F32)).astype(MXU_DTYPE)
    return hi, mid, lo


def _tri_mm(tri, x):
    hi, mid, lo = _split3(x)
    return (jnp.dot(tri, lo, preferred_element_type=F32) + jnp.dot(tri, mid, preferred_element_type=F32)
            + jnp.dot(tri, hi, preferred_element_type=F32))


def _colsum8(x):
    tm, n = x.shape
    return jnp.sum(x.reshape(tm // 8, 8, n), axis=0)


def _rms(x, g):
    r = lax.rsqrt(jnp.mean(x * x, axis=-1, keepdims=True) + EPS)
    return x * r * g, r


def _rms_bwd(x, r, g, dn):
    w = dn * g
    dx = r * w - x * (r * r * r) * jnp.mean(w * x, axis=-1, keepdims=True)
    return dx, dn * x * r


def _sigmoid(x):
    return jax.nn.sigmoid(x)


def _params(n_grid):
    return pltpu.CompilerParams(dimension_semantics=("arbitrary",) * n_grid, vmem_limit_bytes=VMEM_LIMIT_BYTES)


def _rows(tm, n):
    return pl.BlockSpec((tm, n), lambda i: (i, 0))


def _rows_rev(tm, n, nt):
    return pl.BlockSpec((tm, n), lambda i: (nt - 1 - i, 0))


def _cols(n, tm):
    return pl.BlockSpec((n, tm), lambda i: (0, i))


def _cols_rev(n, tm, nt):
    return pl.BlockSpec((n, tm), lambda i: (0, nt - 1 - i))


def _whole(shape):
    nd = len(shape)
    return pl.BlockSpec(shape, lambda i: (0,) * nd)


def _row_tile(s, want):
    tm = min(s, want)
    assert s % tm == 0 and tm % 8 == 0, (s, tm)
    return tm


def conv_fwd(h, g, w1, b1, wd, bd, lg, lb, w2, b2, tm):
    S, D = h.shape
    CW = wd.shape[0]
    off = HALO - (CW - 1)
    assert 0 <= off and tm >= HALO
    nt = S // tm

    def body(h_ref, g_ref, w1_ref, b1_ref, wd_ref, bd_ref, lg_ref, lb_ref, w2_ref, b2_ref,
             ho_ref, n_ref, u_ref, z_ref, sw_ref, ext, win):
        @pl.when(pl.program_id(0) == 0)
        def _():
            ext[0:HALO, :] = jnp.zeros((HALO, D), F32)

        x = h_ref[...]
        n, _ = _rms(x, g_ref[...])
        n_ref[...] = n.astype(n_ref.dtype)
        u = _mm(n, w1_ref[...]) + b1_ref[...]
        u_ref[...] = u
        ext[HALO:HALO + tm, :] = u[:, :D] * _sigmoid(u[:, D:])
        z = jnp.broadcast_to(bd_ref[...], (tm, D))
        for b in range(8):
            amax = (CW - 1 - b) // 8
            win[0:tm + 8 * amax, :] = ext[off + b:off + b + tm + 8 * amax, :]
            for a8 in range(amax + 1):
                z = z + wd_ref[8 * a8 + b:8 * a8 + b + 1, :] * win[8 * a8:8 * a8 + tm, :]
        z_ref[...] = z
        ext[0:HALO, :] = ext[tm:tm + HALO, :]
        mu = jnp.mean(z, axis=-1, keepdims=True)
        zc = z - mu
        y = zc * lax.rsqrt(jnp.mean(zc * zc, axis=-1, keepdims=True) + EPS) * lg_ref[...] + lb_ref[...]
        sw = y * _sigmoid(y)
        sw_ref[...] = sw.astype(sw_ref.dtype)
        ho_ref[...] = x + _mm(sw, w2_ref[...]) + b2_ref[...]

    return pl.pallas_call(
        body, name="conv_fwd", grid=(nt,),
        in_specs=[_rows(tm, D), _whole((1, D)), _whole(w1.shape), _whole((1, 2 * D)), _whole(wd.shape),
                  _whole((1, D)), _whole((1, D)), _whole((1, D)), _whole(w2.shape), _whole((1, D))],
        out_specs=[_rows(tm, D), _rows(tm, D), _rows(tm, 2 * D), _rows(tm, D), _rows(tm, D)],
        out_shape=[jax.ShapeDtypeStruct((S, D), F32), jax.ShapeDtypeStruct((S, D), ACT_DTYPE),
                   jax.ShapeDtypeStruct((S, 2 * D), F32), jax.ShapeDtypeStruct((S, D), F32),
                   jax.ShapeDtypeStruct((S, D), ACT_DTYPE)],
        scratch_shapes=[pltpu.VMEM((HALO + tm, D), F32), pltpu.VMEM((HALO + tm, D), F32)],
        compiler_params=_params(1),
    )(h, g, w1, b1, wd, bd, lg, lb, w2, b2)


def ffn_fwd(h, g, w1, w2, tm):
    S, D = h.shape
    FF = w1.shape[1]

    def body(h_ref, g_ref, w1_ref, w2_ref, ho_ref, n_ref, a_ref, s_ref):
        x = h_ref[...]
        n, _ = _rms(x, g_ref[...])
        n_ref[...] = n.astype(n_ref.dtype)
        a = _mm(n, w1_ref[...])
        a_ref[...] = a
        s = jnp.square(jnp.maximum(a, 0.0))
        s_ref[...] = s.astype(s_ref.dtype)
        ho_ref[...] = x + _mm(s, w2_ref[...])

    return pl.pallas_call(
        body, name="ffn_fwd", grid=(S // tm,),
        in_specs=[_rows(tm, D), _whole((1, D)), _whole(w1.shape), _whole(w2.shape)],
        out_specs=[_rows(tm, D), _rows(tm, D), _rows(tm, FF), _rows(tm, FF)],
        out_shape=[jax.ShapeDtypeStruct((S, D), F32), jax.ShapeDtypeStruct((S, D), ACT_DTYPE),
                   jax.ShapeDtypeStruct((S, FF), F32), jax.ShapeDtypeStruct((S, FF), ACT_DTYPE)],
        compiler_params=_params(1),
    )(h, g, w1, w2)


def ple_fwd(h, g, wg, p, wp, tm):
    S, D = h.shape
    E = p.shape[1]

    def body(h_ref, g_ref, wg_ref, p_ref, wp_ref, ho_ref, n_ref, gate_ref):
        x = h_ref[...]
        n, _ = _rms(x, g_ref[...])
        n_ref[...] = n.astype(n_ref.dtype)
        gate = _sigmoid(_mm(n, wg_ref[...]))
        gate_ref[...] = gate
        ho_ref[...] = x + gate * _mm(p_ref[...], wp_ref[...])

    return pl.pallas_call(
        body, name="ple_fwd", grid=(S // tm,),
        in_specs=[_rows(tm, D), _whole((1, D)), _whole(wg.shape), _rows(tm, E), _whole(wp.shape)],
        out_specs=[_rows(tm, D), _rows(tm, D), _rows(tm, D)],
        out_shape=[jax.ShapeDtypeStruct((S, D), F32), jax.ShapeDtypeStruct((S, D), ACT_DTYPE),
                   jax.ShapeDtypeStruct((S, D), F32)],
        compiler_params=_params(1),
    )(h, g, wg, p, wp)


def kv_fwd(h, g, wk, wv, wf, bf, tm):
    S, D = h.shape
    H = wf.shape[1]

    def body(h_ref, g_ref, wk_ref, wv_ref, wf_ref, bf_ref, k_ref, kT_ref, vT_ref, n_ref, fl_ref, c_ref, carry):
        @pl.when(pl.program_id(0) == 0)
        def _():
            carry[...] = jnp.zeros_like(carry)

        n, _ = _rms(h_ref[...], g_ref[...])
        n_ref[...] = n.astype(n_ref.dtype)
        k = _mm(n, wk_ref[...])
        k_ref[...] = k.astype(k_ref.dtype)
        kT_ref[...] = k.T.astype(kT_ref.dtype)
        vT_ref[...] = _mm(n, wv_ref[...]).T.astype(vT_ref.dtype)
        fl = _mm(n, wf_ref[...]) + bf_ref[...]
        fl_ref[...] = fl
        logf = jnp.minimum(fl, 0.0) - jnp.log1p(jnp.exp(-jnp.abs(fl)))
        row = lax.broadcasted_iota(jnp.int32, (tm, tm), 0)
        col = lax.broadcasted_iota(jnp.int32, (tm, tm), 1)
        tri = (row >= col).astype(MXU_DTYPE)
        c = _tri_mm(tri, logf) + carry[...]
        c_ref[...] = c
        carry[...] = c[tm - 1:tm, :]

    return pl.pallas_call(
        body, name="kv_fwd", grid=(S // tm,),
        in_specs=[_rows(tm, D), _whole((1, D)), _whole(wk.shape), _whole(wv.shape), _whole(wf.shape),
                  _whole((1, H))],
        out_specs=[_rows(tm, D), _cols(D, tm), _cols(D, tm), _rows(tm, D), _rows(tm, H), _rows(tm, H)],
        out_shape=[jax.ShapeDtypeStruct((S, D), ACT_DTYPE), jax.ShapeDtypeStruct((D, S), ACT_DTYPE),
                   jax.ShapeDtypeStruct((D, S), ACT_DTYPE), jax.ShapeDtypeStruct((S, D), ACT_DTYPE),
                   jax.ShapeDtypeStruct((S, H), F32), jax.ShapeDtypeStruct((S, H), F32)],
        scratch_shapes=[pltpu.VMEM((1, H), F32)],
        compiler_params=_params(1),
    )(h, g, wk, wv, wf, bf)


def q_fwd(h, g, wq, scale, tm):
    S, D = h.shape

    def body(h_ref, g_ref, wq_ref, n_ref, q_ref, qT_ref):
        n, _ = _rms(h_ref[...], g_ref[...])
        n_ref[...] = n.astype(n_ref.dtype)
        q = _mm(n, wq_ref[...]) * scale
        q_ref[...] = q.astype(q_ref.dtype)
        qT_ref[...] = q.T.astype(qT_ref.dtype)

    return pl.pallas_call(
        body, name="q_fwd", grid=(S // tm,),
        in_specs=[_rows(tm, D), _whole((1, D)), _whole(wq.shape)],
        out_specs=[_rows(tm, D), _rows(tm, D), _cols(D, tm)],
        out_shape=[jax.ShapeDtypeStruct((S, D), ACT_DTYPE), jax.ShapeDtypeStruct((S, D), ACT_DTYPE),
                   jax.ShapeDtypeStruct((D, S), ACT_DTYPE)],
        compiler_params=_params(1),
    )(h, g, wq)


def attn_out_fwd(h, o, wo, tm):
    S, D = h.shape

    def body(h_ref, o_ref, wo_ref, ho_ref):
        ho_ref[...] = h_ref[...] + _mm(o_ref[...], wo_ref[...])

    return pl.pallas_call(
        body, name="attn_out_fwd", grid=(S // tm,),
        in_specs=[_rows(tm, D), _rows(tm, D), _whole(wo.shape)],
        out_specs=_rows(tm, D),
        out_shape=jax.ShapeDtypeStruct((S, D), F32),
        compiler_params=_params(1),
    )(h, o, wo)


def _causal_mask(key0, qry0, shape, key_axis):
    key = key0 + lax.broadcasted_iota(jnp.int32, shape, key_axis)
    qry = qry0 + lax.broadcasted_iota(jnp.int32, shape, 1 - key_axis)
    return key <= qry


def flash_fwd(qT, k, vT, c_col, c_row, dh, tq, tkc):
    D, S = qT.shape
    hg = 128 // dh
    G = D // 128
    per = tq // tkc
    assert tq % tkc == 0 and S % tq == 0

    def body(qT_ref, k_ref, vT_ref, ccol_ref, crow_ref, o_ref, o32_ref, lse_ref, m_scr, l_scr, acc_scr, mx_scr,
             *scratch):
        i = pl.program_id(1)
        m_scr[...] = jnp.full(m_scr.shape, NEG_BIG, F32)
        l_scr[...] = jnp.zeros(l_scr.shape, F32)
        acc_scr[...] = jnp.zeros(acc_scr.shape, F32)

        def head_shift(j, hh):
            c0 = ccol_ref[pl.ds(j * tkc, 1), hh:hh + 1]
            return c0, crow_ref[hh:hh + 1, :] - c0

        def scores(j, masked, slot):
            keys = pl.ds(pl.multiple_of(j * tkc, tkc), tkc)
            s_scr = scratch[slot]
            for hh in range(hg):
                lanes = slice(hh * dh, (hh + 1) * dh)
                c0, r = head_shift(j, hh)
                s = _mm(k_ref[keys, lanes], qT_ref[lanes, :]) - (ccol_ref[keys, hh:hh + 1] - c0)
                if masked:
                    s = jnp.where(_causal_mask(j * tkc, i * tq, (tkc, tq), 0), s, NEG_BIG)
                s_scr[hh] = s
                mx_scr[slot * hg + hh] = jnp.max(s, axis=0, keepdims=True) + r

        def update(j, slot):
            keys = pl.ds(pl.multiple_of(j * tkc, tkc), tkc)
            s_scr = scratch[slot]
            for hh in range(hg):
                lanes = slice(hh * dh, (hh + 1) * dh)
                _, r = head_shift(j, hh)
                m_old = m_scr[hh]
                m_new = jnp.maximum(m_old, mx_scr[slot * hg + hh])
                alpha = jnp.exp(m_old - m_new)
                p = jnp.exp(s_scr[hh] - (m_new - r))
                l_scr[hh] = alpha * l_scr[hh] + jnp.sum(p, axis=0, keepdims=True)
                acc_scr[lanes, :] = alpha * acc_scr[lanes, :] + _mm(vT_ref[lanes, keys], p)
                m_scr[hh] = m_new

        def chunks(j0, masked):
            for jj in range(per):
                scores(j0 + jj, masked, jj)
            for jj in range(per):
                update(j0 + jj, jj)

        def full_chunks(jb, carry):
            chunks(jb * per, False)
            return carry

        lax.fori_loop(0, i, full_chunks, 0)
        chunks(i * per, True)
        for hh in range(hg):
            lanes = slice(hh * dh, (hh + 1) * dh)
            acc_scr[lanes, :] = acc_scr[lanes, :] / l_scr[hh]
            lse_ref[hh:hh + 1, :] = m_scr[hh] + jnp.log(l_scr[hh])
        o = acc_scr[...].T
        o_ref[...] = o.astype(o_ref.dtype)
        o32_ref[...] = o

    return pl.pallas_call(
        body, name="flash_fwd", grid=(G, S // tq),
        in_specs=[pl.BlockSpec((128, tq), lambda g, i: (g, i)),
                  pl.BlockSpec((S, 128), lambda g, i: (0, g)),
                  pl.BlockSpec((128, S), lambda g, i: (g, 0)),
                  pl.BlockSpec((None, S, hg), lambda g, i: (g, 0, 0)),
                  pl.BlockSpec((None, hg, tq), lambda g, i: (g, 0, i))],
        out_specs=[pl.BlockSpec((tq, 128), lambda g, i: (i, g)),
                   pl.BlockSpec((tq, 128), lambda g, i: (i, g)),
                   pl.BlockSpec((None, hg, tq), lambda g, i: (g, 0, i))],
        out_shape=[jax.ShapeDtypeStruct((S, D), ACT_DTYPE), jax.ShapeDtypeStruct((S, D), F32),
                   jax.ShapeDtypeStruct((G, hg, S), F32)],
        scratch_shapes=([pltpu.VMEM((hg, 1, tq), F32), pltpu.VMEM((hg, 1, tq), F32), pltpu.VMEM((128, tq), F32),
                         pltpu.VMEM((per * hg, 1, tq), F32)]
                        + [pltpu.VMEM((hg, tkc, tq), F32)] * per),
        compiler_params=_params(2),
    )(qT, k, vT, c_col, c_row)


def loss_head(h, g, target, tm):
    S, D = h.shape
    nt = S // tm

    def body(h_ref, g_ref, t_ref, dh_ref, dg_ref, loss_ref, dg_acc, loss_acc):
        i = pl.program_id(0)

        @pl.when(i == 0)
        def _():
            dg_acc[...] = jnp.zeros_like(dg_acc)
            loss_acc[...] = jnp.zeros_like(loss_acc)

        x = h_ref[...]
        gg = g_ref[...]
        y, r = _rms(x, gg)
        e = y - t_ref[...]
        loss_acc[...] += 0.5 * jnp.sum(jnp.mean(e * e, axis=-1, keepdims=True), axis=0, keepdims=True)
        dx, dgr = _rms_bwd(x, r, gg, e / D)
        dh_ref[...] = dx
        dg_acc[...] += _colsum8(dgr)

        @pl.when(i == nt - 1)
        def _():
            dg_ref[...] = jnp.sum(dg_acc[...], axis=0, keepdims=True)
            loss_ref[...] = jnp.broadcast_to(loss_acc[...], loss_ref.shape)

    return pl.pallas_call(
        body, name="loss_head", grid=(nt,),
        in_specs=[_rows(tm, D), _whole((1, D)), _rows(tm, D)],
        out_specs=[_rows(tm, D), _whole((1, D)), _whole((1, 128))],
        out_shape=[jax.ShapeDtypeStruct((S, D), F32), jax.ShapeDtypeStruct((1, D), F32),
                   jax.ShapeDtypeStruct((1, 128), F32)],
        scratch_shapes=[pltpu.VMEM((8, D), F32), pltpu.VMEM((1, 1), F32)],
        compiler_params=_params(1),
    )(h, g, target)


def ple_bwd(d, h, g, wg, gate, p, wp, tm):
    S, D = h.shape
    E = p.shape[1]
    nt = S // tm

    def body(d_ref, h_ref, g_ref, wg_ref, gate_ref, p_ref, wp_ref, di_ref, dz_ref, dpp_ref, dg_ref, dg_acc):
        i = pl.program_id(0)

        @pl.when(i == 0)
        def _():
            dg_acc[...] = jnp.zeros_like(dg_acc)

        dd = d_ref[...]
        x = h_ref[...]
        gg = g_ref[...]
        gt = gate_ref[...]
        pp = _mm(p_ref[...], wp_ref[...])
        dpp_ref[...] = (dd * gt).astype(dpp_ref.dtype)
        dz = dd * pp * gt * (1.0 - gt)
        dz_ref[...] = dz.astype(dz_ref.dtype)
        r = lax.rsqrt(jnp.mean(x * x, axis=-1, keepdims=True) + EPS)
        dx, dgr = _rms_bwd(x, r, gg, _mm_nt(dz, wg_ref[...]))
        di_ref[...] = dd + dx
        dg_acc[...] += _colsum8(dgr)

        @pl.when(i == nt - 1)
        def _():
            dg_ref[...] = jnp.sum(dg_acc[...], axis=0, keepdims=True)

    return pl.pallas_call(
        body, name="ple_bwd", grid=(nt,),
        in_specs=[_rows(tm, D), _rows(tm, D), _whole((1, D)), _whole(wg.shape), _rows(tm, D), _rows(tm, E),
                  _whole(wp.shape)],
        out_specs=[_rows(tm, D), _rows(tm, D), _rows(tm, D), _whole((1, D))],
        out_shape=[jax.ShapeDtypeStruct((S, D), F32), jax.ShapeDtypeStruct((S, D), ACT_DTYPE),
                   jax.ShapeDtypeStruct((S, D), ACT_DTYPE), jax.ShapeDtypeStruct((1, D), F32)],
        scratch_shapes=[pltpu.VMEM((8, D), F32)],
        compiler_params=_params(1),
    )(d, h, g, wg, gate, p, wp)


def ffn_bwd(d, h, g, w1, w2, a, tm):
    S, D = h.shape
    FF = w1.shape[1]
    nt = S // tm

    def body(d_ref, h_ref, g_ref, w1_ref, w2_ref, a_ref, di_ref, da_ref, dg_ref, dg_acc):
        i = pl.program_id(0)

        @pl.when(i == 0)
        def _():
            dg_acc[...] = jnp.zeros_like(dg_acc)

        dd = d_ref[...]
        x = h_ref[...]
        da = _mm_nt(dd, w2_ref[...]) * (2.0 * jnp.maximum(a_ref[...], 0.0))
        da_ref[...] = da.astype(da_ref.dtype)
        r = lax.rsqrt(jnp.mean(x * x, axis=-1, keepdims=True) + EPS)
        dx, dgr = _rms_bwd(x, r, g_ref[...], _mm_nt(da, w1_ref[...]))
        di_ref[...] = dd + dx
        dg_acc[...] += _colsum8(dgr)

        @pl.when(i == nt - 1)
        def _():
            dg_ref[...] = jnp.sum(dg_acc[...], axis=0, keepdims=True)

    return pl.pallas_call(
        body, name="ffn_bwd", grid=(nt,),
        in_specs=[_rows(tm, D), _rows(tm, D), _whole((1, D)), _whole(w1.shape), _whole(w2.shape), _rows(tm, FF)],
        out_specs=[_rows(tm, D), _rows(tm, FF), _whole((1, D))],
        out_shape=[jax.ShapeDtypeStruct((S, D), F32), jax.ShapeDtypeStruct((S, FF), ACT_DTYPE),
                   jax.ShapeDtypeStruct((1, D), F32)],
        scratch_shapes=[pltpu.VMEM((8, D), F32)],
        compiler_params=_params(1),
    )(d, h, g, w1, w2, a)


def attn_out_bwd(d, wo, o32, dh, tm):
    S, D = d.shape
    H = D // dh

    def body(d_ref, wo_ref, o_ref, do_ref, doT_ref, delta_ref):
        do32 = _mm_nt(d_ref[...], wo_ref[...])
        do = do32.astype(do_ref.dtype)
        do_ref[...] = do
        doT_ref[...] = do32.T.astype(doT_ref.dtype)
        lane_head = lax.broadcasted_iota(jnp.int32, (D, H), 0) // dh
        seg = (lane_head == lax.broadcasted_iota(jnp.int32, (D, H), 1)).astype(MXU_DTYPE)
        hi, mid, lo = _split3(do.astype(F32) * o_ref[...])
        delta_ref[...] = (jnp.dot(lo, seg, preferred_element_type=F32) + jnp.dot(mid, seg, preferred_element_type=F32)
                          + jnp.dot(hi, seg, preferred_element_type=F32))

    return pl.pallas_call(
        body, name="attn_out_bwd", grid=(S // tm,),
        in_specs=[_rows(tm, D), _whole(wo.shape), _rows(tm, D)],
        out_specs=[_rows(tm, D), _cols(D, tm), _rows(tm, H)],
        out_shape=[jax.ShapeDtypeStruct((S, D), ACT_DTYPE), jax.ShapeDtypeStruct((D, S), ACT_DTYPE),
                   jax.ShapeDtypeStruct((S, H), F32)],
        compiler_params=_params(1),
    )(d, wo, o32)


def flash_bwd(q, qT, k, kT, vT, c_row, do, doT, qstat, dh, tk, tqc):
    S, D = q.shape
    hg = 128 // dh
    G = D // 128
    per = tk // tqc
    nchunk = S // tqc
    assert hg <= 8 and tk % tqc == 0 and S % tk == 0

    def body(q_ref, qT_ref, k_ref, kT_ref, vT_ref, crow_ref, do_ref, doT_ref, st_ref,
             dq_ref, dkT_ref, dvT_ref, dck_ref, dcq_ref):
        ki = pl.program_id(1)

        @pl.when(ki == 0)
        def _():
            dq_ref[...] = jnp.zeros_like(dq_ref)
            dcq_ref[...] = jnp.zeros_like(dcq_ref)

        dck_ref[...] = jnp.zeros_like(dck_ref)
        dkT_ref[...] = jnp.zeros_like(dkT_ref)
        dvT_ref[...] = jnp.zeros_like(dvT_ref)
        def chunk(jq, masked):
            rows = pl.ds(pl.multiple_of(jq * tqc, tqc), tqc)
            st = st_ref[rows, :]
            for hh in range(hg):
                lanes = slice(hh * dh, (hh + 1) * dh)
                kh = k_ref[:, lanes]
                ck = crow_ref[hh:hh + 1, :]
                c0 = ck[:, 0:1]
                u = (st[:, hh:hh + 1] - c0) - st[:, hg + hh:hg + hh + 1]
                s = (_mm(q_ref[rows, lanes], kT_ref[lanes, :]) - (ck - c0)) + u
                if masked:
                    s = jnp.where(_causal_mask(ki * tk, jq * tqc, (tqc, tk), 1), s, NEG_BIG)
                p = jnp.exp(s)
                dvT_ref[lanes, :] += _mm(doT_ref[lanes, rows], p)
                ds = p * (_mm(do_ref[rows, lanes], vT_ref[lanes, :]) - st[:, 2 * hg + hh:2 * hg + hh + 1])
                dkT_ref[lanes, :] += _mm(qT_ref[lanes, rows], ds)
                dck_ref[hh:hh + 1, :] -= jnp.sum(ds, axis=0, keepdims=True)
                dcq_ref[rows, hh:hh + 1] += jnp.sum(ds, axis=1, keepdims=True)
                dq_ref[rows, lanes] += _mm(ds, kh)

        for jj in range(per):
            chunk(ki * per + jj, True)

        def full_chunk(jq, carry):
            chunk(jq, False)
            return carry

        lax.fori_loop((ki + 1) * per, nchunk, full_chunk, 0)

    return pl.pallas_call(
        body, name="flash_bwd", grid=(G, S // tk),
        in_specs=[pl.BlockSpec((S, 128), lambda g, j: (0, g)),
                  pl.BlockSpec((128, S), lambda g, j: (g, 0)),
                  pl.BlockSpec((tk, 128), lambda g, j: (j, g)),
                  pl.BlockSpec((128, tk), lambda g, j: (g, j)),
                  pl.BlockSpec((128, tk), lambda g, j: (g, j)),
                  pl.BlockSpec((None, hg, tk), lambda g, j: (g, 0, j)),
                  pl.BlockSpec((S, 128), lambda g, j: (0, g)),
                  pl.BlockSpec((128, S), lambda g, j: (g, 0)),
                  pl.BlockSpec((None, S, 3 * hg), lambda g, j: (g, 0, 0))],
        out_specs=[pl.BlockSpec((S, 128), lambda g, j: (0, g)),
                   pl.BlockSpec((128, tk), lambda g, j: (g, j)),
                   pl.BlockSpec((128, tk), lambda g, j: (g, j)),
                   pl.BlockSpec((None, 8, tk), lambda g, j: (g, 0, j)),
                   pl.BlockSpec((None, S, hg), lambda g, j: (g, 0, 0))],
        out_shape=[jax.ShapeDtypeStruct((S, D), F32), jax.ShapeDtypeStruct((D, S), F32),
                   jax.ShapeDtypeStruct((D, S), F32), jax.ShapeDtypeStruct((G, 8, S), F32),
                   jax.ShapeDtypeStruct((G, S, hg), F32)],
        compiler_params=_params(2),
    )(q, qT, k, kT, vT, c_row, do, doT, qstat)


def q_bwd(d, dq, h, g, wq, scale, tm):
    S, D = h.shape
    nt = S // tm

    def body(d_ref, dq_ref, h_ref, g_ref, wq_ref, di_ref, dqs_ref, dg_ref, dg_acc):
        i = pl.program_id(0)

        @pl.when(i == 0)
        def _():
            dg_acc[...] = jnp.zeros_like(dg_acc)

        x = h_ref[...]
        dqs = dq_ref[...] * scale
        dqs_ref[...] = dqs.astype(dqs_ref.dtype)
        r = lax.rsqrt(jnp.mean(x * x, axis=-1, keepdims=True) + EPS)
        dx, dgr = _rms_bwd(x, r, g_ref[...], _mm_nt(dqs, wq_ref[...]))
        di_ref[...] = d_ref[...] + dx
        dg_acc[...] += _colsum8(dgr)

        @pl.when(i == nt - 1)
        def _():
            dg_ref[...] = jnp.sum(dg_acc[...], axis=0, keepdims=True)

    return pl.pallas_call(
        body, name="q_bwd", grid=(nt,),
        in_specs=[_rows(tm, D), _rows(tm, D), _rows(tm, D), _whole((1, D)), _whole(wq.shape)],
        out_specs=[_rows(tm, D), _rows(tm, D), _whole((1, D))],
        out_shape=[jax.ShapeDtypeStruct((S, D), F32), jax.ShapeDtypeStruct((S, D), ACT_DTYPE),
                   jax.ShapeDtypeStruct((1, D), F32)],
        scratch_shapes=[pltpu.VMEM((8, D), F32)],
        compiler_params=_params(1),
    )(d, dq, h, g, wq)


def kv_bwd(d, dks, dvs, dcs, fl, h, g, wk, wv, wf, tm):
    S, D = h.shape
    H = wf.shape[1]
    nt = S // tm
    nl = len(dks)
    nc = len(dcs)

    def body(*refs):
        d_ref = refs[0]
        dk_refs = refs[1:1 + nl]
        dv_refs = refs[1 + nl:1 + 2 * nl]
        dc_refs = refs[1 + 2 * nl:1 + 2 * nl + nc]
        (fl_ref, h_ref, g_ref, wk_ref, wv_ref, wf_ref,
         di_ref, dk_ref, dv_ref, dfl_ref, dg_ref, dbf_ref, dg_acc, dbf_acc, carry) = refs[1 + 2 * nl + nc:]
        i = pl.program_id(0)

        @pl.when(i == 0)
        def _():
            dg_acc[...] = jnp.zeros_like(dg_acc)
            dbf_acc[...] = jnp.zeros_like(dbf_acc)
            carry[...] = jnp.zeros_like(carry)

        dkT = dk_refs[0][...]
        dvT = dv_refs[0][...]
        for l in range(1, nl):
            dkT = dkT + dk_refs[l][...]
            dvT = dvT + dv_refs[l][...]
        dk = dkT.T
        dv = dvT.T
        dk_ref[...] = dk.astype(dk_ref.dtype)
        dv_ref[...] = dv.astype(dv_ref.dtype)
        row = lax.broadcasted_iota(jnp.int32, (tm, tm), 0)
        col = lax.broadcasted_iota(jnp.int32, (tm, tm), 1)
        tri = (col >= row).astype(MXU_DTYPE)
        dc = dc_refs[0][...]
        for l in range(1, nc):
            dc = dc + dc_refs[l][...]
        dlogf = _tri_mm(tri, dc) + carry[...]
        carry[...] = dlogf[0:1, :]
        dfl = dlogf * _sigmoid(-fl_ref[...])
        dfl_ref[...] = dfl
        dbf_acc[...] += jnp.sum(dfl, axis=0, keepdims=True)
        x = h_ref[...]
        dn = _mm_nt(dk, wk_ref[...]) + _mm_nt(dv, wv_ref[...]) + _mm_nt(dfl, wf_ref[...])
        r = lax.rsqrt(jnp.mean(x * x, axis=-1, keepdims=True) + EPS)
        dx, dgr = _rms_bwd(x, r, g_ref[...], dn)
        di_ref[...] = d_ref[...] + dx
        dg_acc[...] += _colsum8(dgr)

        @pl.when(i == nt - 1)
        def _():
            dg_ref[...] = jnp.sum(dg_acc[...], axis=0, keepdims=True)
            dbf_ref[...] = dbf_acc[...]

    rev = lambda n: _rows_rev(tm, n, nt)
    return pl.pallas_call(
        body, name="kv_bwd", grid=(nt,),
        in_specs=([rev(D)] + [_cols_rev(D, tm, nt)] * (2 * nl)
                  + [rev(H)] * nc
                  + [rev(H), rev(D), _whole((1, D)), _whole(wk.shape), _whole(wv.shape), _whole(wf.shape)]),
        out_specs=[rev(D), rev(D), rev(D), rev(H), _whole((1, D)), _whole((1, H))],
        out_shape=[jax.ShapeDtypeStruct((S, D), F32), jax.ShapeDtypeStruct((S, D), ACT_DTYPE),
                   jax.ShapeDtypeStruct((S, D), ACT_DTYPE), jax.ShapeDtypeStruct((S, H), F32),
                   jax.ShapeDtypeStruct((1, D), F32), jax.ShapeDtypeStruct((1, H), F32)],
        scratch_shapes=[pltpu.VMEM((8, D), F32), pltpu.VMEM((1, H), F32), pltpu.VMEM((1, H), F32)],
        compiler_params=_params(1),
    )(d, *dks, *dvs, *dcs, fl, h, g, wk, wv, wf)


def conv_bwd(d, h, g, w1, wd, lg, lb, w2, u, z, tm):
    S, D = h.shape
    CW = wd.shape[0]
    nt = S // tm
    assert tm >= HALO and CW - 1 <= HALO

    def body(d_ref, h_ref, g_ref, w1_ref, wd_ref, lg_ref, lb_ref, w2_ref, u_ref, z_ref,
             di_ref, du_ref, db2_ref, dlg_ref, dlb_ref, dbd_ref, dwd_ref, db1_ref, dg_ref,
             ext, win, db2_acc, dlg_acc, dlb_acc, dbd_acc, dwd_acc, db1_acc, dg_acc):
        i = pl.program_id(0)

        @pl.when(i == 0)
        def _():
            ext[tm:tm + HALO, :] = jnp.zeros((HALO, D), F32)
            for acc in (db2_acc, dlg_acc, dlb_acc, dbd_acc, dwd_acc, db1_acc, dg_acc):
                acc[...] = jnp.zeros_like(acc)

        dd = d_ref[...]
        db2_acc[...] += _colsum8(dd)
        dsw = _mm_nt(dd, w2_ref[...])
        zz = z_ref[...]
        zc = zz - jnp.mean(zz, axis=-1, keepdims=True)
        rs = lax.rsqrt(jnp.mean(zc * zc, axis=-1, keepdims=True) + EPS)
        xh = zc * rs
        lgv = lg_ref[...]
        y = xh * lgv + lb_ref[...]
        sg = _sigmoid(y)
        dy = dsw * (sg * (1.0 + y * (1.0 - sg)))
        dlg_acc[...] += _colsum8(dy * xh)
        dlb_acc[...] += _colsum8(dy)
        dxh = dy * lgv
        dz = rs * (dxh - jnp.mean(dxh, axis=-1, keepdims=True) - xh * jnp.mean(dxh * xh, axis=-1, keepdims=True))
        dbd_acc[...] += _colsum8(dz)
        ext[0:tm, :] = dz
        uu = u_ref[...]
        a = uu[:, :D]
        sgg = _sigmoid(uu[:, D:])
        glu = a * sgg
        dglu = jnp.zeros((tm, D), F32)
        for b in range(8):
            amax = (CW - 1 - b) // 8
            win[0:tm + 8 * amax, :] = ext[b:b + tm + 8 * amax, :]
            for a8 in range(amax + 1):
                k = CW - 1 - (8 * a8 + b)
                sh = win[8 * a8:8 * a8 + tm, :]
                dglu = dglu + wd_ref[k:k + 1, :] * sh
                dwd_acc[k] += _colsum8(glu * sh)
        ext[tm:tm + HALO, :] = ext[0:HALO, :]
        da = dglu * sgg
        dgg = dglu * a * sgg * (1.0 - sgg)
        du_ref[:, :D] = da.astype(du_ref.dtype)
        du_ref[:, D:] = dgg.astype(du_ref.dtype)
        db1_acc[:, :D] += _colsum8(da)
        db1_acc[:, D:] += _colsum8(dgg)
        dn = _mm_nt(da, w1_ref[:, :D]) + _mm_nt(dgg, w1_ref[:, D:])
        x = h_ref[...]
        r = lax.rsqrt(jnp.mean(x * x, axis=-1, keepdims=True) + EPS)
        dx, dgr = _rms_bwd(x, r, g_ref[...], dn)
        di_ref[...] = dd + dx
        dg_acc[...] += _colsum8(dgr)

        @pl.when(i == nt - 1)
        def _():
            db2_ref[...] = jnp.sum(db2_acc[...], axis=0, keepdims=True)
            dlg_ref[...] = jnp.sum(dlg_acc[...], axis=0, keepdims=True)
            dlb_ref[...] = jnp.sum(dlb_acc[...], axis=0, keepdims=True)
            dbd_ref[...] = jnp.sum(dbd_acc[...], axis=0, keepdims=True)
            dwd_ref[...] = jnp.sum(dwd_acc[...], axis=1)
            db1_ref[...] = jnp.sum(db1_acc[...], axis=0, keepdims=True)
            dg_ref[...] = jnp.sum(dg_acc[...], axis=0, keepdims=True)

    rev = lambda n: _rows_rev(tm, n, nt)
    vec = jax.ShapeDtypeStruct((1, D), F32)
    return pl.pallas_call(
        body, name="conv_bwd", grid=(nt,),
        in_specs=[rev(D), rev(D), _whole((1, D)), _whole(w1.shape), _whole(wd.shape), _whole((1, D)),
                  _whole((1, D)), _whole(w2.shape), rev(2 * D), rev(D)],
        out_specs=[rev(D), rev(2 * D), _whole((1, D)), _whole((1, D)), _whole((1, D)), _whole((1, D)),
                   _whole((CW, D)), _whole((1, 2 * D)), _whole((1, D))],
        out_shape=[jax.ShapeDtypeStruct((S, D), F32), jax.ShapeDtypeStruct((S, 2 * D), ACT_DTYPE),
                   vec, vec, vec, vec, jax.ShapeDtypeStruct((CW, D), F32),
                   jax.ShapeDtypeStruct((1, 2 * D), F32), vec],
        scratch_shapes=[pltpu.VMEM((tm + HALO, D), F32), pltpu.VMEM((tm + HALO, D), F32),
                        pltpu.VMEM((8, D), F32), pltpu.VMEM((8, D), F32),
                        pltpu.VMEM((8, D), F32), pltpu.VMEM((8, D), F32), pltpu.VMEM((CW, 8, D), F32),
                        pltpu.VMEM((8, 2 * D), F32), pltpu.VMEM((8, D), F32)],
        compiler_params=_params(1),
    )(d, h, g, w1, wd, lg, lb, w2, u, z)


def weight_grad(a, b, ts, name, column_shards=None):
    S, M = a.shape
    N = b.shape[1]
    ta = M if M <= 1024 else 1024
    tb = N if N <= 1024 else 1024
    assert M % ta == 0 and N % tb == 0 and S % ts == 0
    if column_shards is None:
        width, per_tile = tb, 1
        out_spec = pl.BlockSpec((ta, tb), lambda i, j, s: (i, j))
        out_shape = jax.ShapeDtypeStruct((M, N), F32)
    else:
        width = N // column_shards
        per_tile = tb // width
        assert tb % width == 0
        out_spec = pl.BlockSpec((per_tile, ta, width), lambda i, j, s: (j, i, 0))
        out_shape = jax.ShapeDtypeStruct((column_shards, M, width), F32)

    def body(a_ref, b_ref, o_ref):
        @pl.when(pl.program_id(2) == 0)
        def _():
            o_ref[...] = jnp.zeros_like(o_ref)

        res = _mm_tn(a_ref[...], b_ref[...])
        if column_shards is None:
            o_ref[...] += res
        else:
            for d in range(per_tile):
                o_ref[d] += res[:, d * width:(d + 1) * width]

    return pl.pallas_call(
        body, name=name, grid=(M // ta, N // tb, S // ts),
        in_specs=[pl.BlockSpec((ts, ta), lambda i, j, s: (s, i)), pl.BlockSpec((ts, tb), lambda i, j, s: (s, j))],
        out_specs=out_spec, out_shape=out_shape,
        compiler_params=_params(3),
    )(a, b)


def _position():
    return lax.axis_index("x"), lax.axis_index("y"), lax.axis_index("c")


def all_gather(x, name):
    def body(x_ref, out_ref, send_sems, recv_sems, local_sem):
        x, y, c = _position()
        me, sibling = (x, y, c), (x, y, 1 - c)
        chips = [(1 - x, y), (x, 1 - y), (1 - x, 1 - y)]

        def slot(px, py, pc):
            return out_ref.at[4 * px + 2 * py + pc]

        def copy(k, block, to, src=None):
            return pltpu.make_async_remote_copy(
                src_ref=slot(*block) if src is None else src, dst_ref=slot(*block),
                send_sem=send_sems.at[k], recv_sem=recv_sems.at[k], device_id=to, device_id_type=MESH)

        mine = pltpu.make_async_copy(x_ref, slot(*me), local_sem)
        mine.start()
        first = [copy(0, me, sibling, src=x_ref)]
        first += [copy(1 + j, me, (*chip, c), src=x_ref) for j, chip in enumerate(chips)]
        for cp in first:
            cp.start()
        passed = [copy(4 + j, (*chip, c), sibling) for j, chip in enumerate(chips)]
        for j, chip in enumerate(chips):
            copy(1 + j, (*chip, c), me).wait_recv()
            passed[j].start()
        copy(0, sibling, me).wait_recv()
        for j, chip in enumerate(chips):
            copy(4 + j, (*chip, 1 - c), me).wait_recv()
        for cp in first + passed:
            cp.wait_send()
        mine.wait()

    return pl.pallas_call(
        body, name=name,
        in_specs=[pl.BlockSpec(memory_space=pl.ANY)], out_specs=pl.BlockSpec(memory_space=pl.ANY),
        out_shape=jax.ShapeDtypeStruct((N_DEV,) + x.shape, x.dtype),
        scratch_shapes=[pltpu.SemaphoreType.DMA((7,)), pltpu.SemaphoreType.DMA((7,)), pltpu.SemaphoreType.DMA],
    )(x)


def sibling_exchange(gs):
    n = len(gs)

    def body(*refs):
        g_refs, land_refs, (send_sems, recv_sems) = refs[:n], refs[n:2 * n], refs[2 * n:]
        x, y, c = _position()
        copies = [pltpu.make_async_remote_copy(
            src_ref=g_refs[a].at[2 * j + 1 - c], dst_ref=land_refs[a].at[j], send_sem=send_sems.at[N_CHIP * a + j],
            recv_sem=recv_sems.at[N_CHIP * a + j], device_id=(x, y, 1 - c), device_id_type=MESH)
            for a in range(n) for j in range(N_CHIP)]
        for cp in copies:
            cp.start()
        for cp in copies:
            cp.wait()

    return pl.pallas_call(
        body, name="grad_sibling_exchange",
        in_specs=[pl.BlockSpec(memory_space=pl.ANY)] * n, out_specs=[pl.BlockSpec(memory_space=pl.ANY)] * n,
        out_shape=[jax.ShapeDtypeStruct((N_CHIP,) + g.shape[1:], g.dtype) for g in gs],
        scratch_shapes=[pltpu.SemaphoreType.DMA((N_CHIP * n,)), pltpu.SemaphoreType.DMA((N_CHIP * n,))],
    )(*gs)


def chip_partial(g, land, core):
    _, R, C = g.shape
    tr = _row_tile(R, PACK_ROW_TILE)

    def body(c_ref, g_ref, l_ref, o_ref):
        o_ref[...] = (g_ref[...] + l_ref[...]).astype(o_ref.dtype)

    grid_spec = pltpu.PrefetchScalarGridSpec(
        num_scalar_prefetch=1, grid=(N_CHIP, R // tr),
        in_specs=[pl.BlockSpec((None, tr, C), lambda j, i, cr: (2 * j + cr[0], i, 0)),
                  pl.BlockSpec((None, tr, C), lambda j, i, cr: (j, i, 0))],
        out_specs=pl.BlockSpec((None, tr, C), lambda j, i, cr: (j, i, 0)))
    return pl.pallas_call(
        body, name="grad_chip_partial", grid_spec=grid_spec,
        out_shape=jax.ShapeDtypeStruct((N_CHIP, R, C), WIRE_DTYPE),
        compiler_params=_params(2),
    )(core, g, land)


def chip_exchange(parts):
    n = len(parts)

    def body(*refs):
        p_refs, land_refs, (send_sems, recv_sems, local_sems) = refs[:n], refs[n:2 * n], refs[2 * n:]
        x, y, c = _position()
        mychip = 2 * x + y
        chips = [(1 - x, y), (x, 1 - y), (1 - x, 1 - y)]

        def remote(a, k, slot):
            cx, cy = chips[k]
            return pltpu.make_async_remote_copy(
                src_ref=p_refs[a].at[2 * cx + cy], dst_ref=land_refs[a].at[slot], send_sem=send_sems.at[3 * a + k],
                recv_sem=recv_sems.at[3 * a + k], device_id=(cx, cy, c), device_id_type=MESH)

        mine = [pltpu.make_async_copy(p_refs[a].at[mychip], land_refs[a].at[mychip], local_sems.at[a])
                for a in range(n)]
        for cp in mine:
            cp.start()
        copies = [remote(a, k, mychip) for a in range(n) for k in range(3)]
        for cp in copies:
            cp.start()
        for a in range(n):
            for k, (cx, cy) in enumerate(chips):
                remote(a, k, 2 * cx + cy).wait_recv()
        for cp in copies:
            cp.wait_send()
        for cp in mine:
            cp.wait()

    return pl.pallas_call(
        body, name="grad_chip_exchange",
        in_specs=[pl.BlockSpec(memory_space=pl.ANY)] * n, out_specs=[pl.BlockSpec(memory_space=pl.ANY)] * n,
        out_shape=[jax.ShapeDtypeStruct(p.shape, p.dtype) for p in parts],
        scratch_shapes=[pltpu.SemaphoreType.DMA((3 * n,)), pltpu.SemaphoreType.DMA((3 * n,)),
                        pltpu.SemaphoreType.DMA((n,))],
    )(*parts)


def _adamw(w, g, m, v):
    m = ADAM_B1 * m + (1.0 - ADAM_B1) * g
    v = ADAM_B2 * v + (1.0 - ADAM_B2) * jnp.square(g)
    m_hat = m / (1.0 - ADAM_B1 ** ADAM_STEP)
    v_hat = v / (1.0 - ADAM_B2 ** ADAM_STEP)
    delta = -ADAM_LR * (m_hat / (jnp.sqrt(v_hat) + ADAM_EPS) + ADAM_WD * w)
    return delta, m, v


def adamw_sharded(parts, w, m, v):
    R, C = w.shape
    tr = _row_tile(R, PACK_ROW_TILE)

    def body(p_ref, w_ref, m_ref, v_ref, g_ref, d_ref, nm_ref, nv_ref):
        g = p_ref[0].astype(F32)
        for j in range(1, N_CHIP):
            g = g + p_ref[j].astype(F32)
        g_ref[...] = g
        d_ref[...], nm_ref[...], nv_ref[...] = _adamw(w_ref[...], g, m_ref[...], v_ref[...])

    out = jax.ShapeDtypeStruct((R, C), F32)
    return pl.pallas_call(
        body, name="adamw_sharded", grid=(R // tr,),
        in_specs=[pl.BlockSpec((N_CHIP, tr, C), lambda i: (0, i, 0)), _rows(tr, C), _rows(tr, C), _rows(tr, C)],
        out_specs=[_rows(tr, C)] * 4, out_shape=[out] * 4,
        compiler_params=_params(1),
    )(parts, w, m, v)


def adamw_replicated(gathered, w, m, v):
    R, C = w.shape

    def body(p_ref, w_ref, m_ref, v_ref, g_ref, d_ref, nm_ref, nv_ref):
        g = p_ref[0]
        for j in range(1, N_DEV):
            g = g + p_ref[j]
        g_ref[...] = g
        d_ref[...], nm_ref[...], nv_ref[...] = _adamw(w_ref[...], g, m_ref[...], v_ref[...])

    out = jax.ShapeDtypeStruct((R, C), F32)
    return pl.pallas_call(
        body, name="adamw_replicated", grid=(1,),
        in_specs=[_whole(gathered.shape), _whole((R, C)), _whole((R, C)), _whole((R, C))],
        out_specs=[_whole((R, C))] * 4, out_shape=[out] * 4,
        compiler_params=_params(1),
    )(gathered, w, m, v)


def _piece_rows(n):
    return -(-n // PACK_COLS)


def _as_rows(a, lead):
    flat = a.reshape(a.shape[:lead] + (-1,))
    fill = _piece_rows(flat.shape[-1]) * PACK_COLS - flat.shape[-1]
    if fill:
        flat = jnp.pad(flat, [(0, 0)] * lead + [(0, fill)])
    return flat.reshape(flat.shape[:-1] + (-1, PACK_COLS))


def _pack(arrays, rows_multiple, dtype=None, lead=0):
    pieces = [_as_rows(a if dtype is None else a.astype(dtype), lead) for a in arrays]
    extra = -sum(p.shape[lead] for p in pieces) % rows_multiple
    if extra:
        pieces.append(jnp.zeros(pieces[0].shape[:lead] + (extra, PACK_COLS), pieces[0].dtype))
    return jnp.concatenate(pieces, axis=lead)


def _unpack(packed, shapes, lead=0):
    out, r0 = [], 0
    for shp in shapes:
        n = int(np.prod(shp))
        rows = _piece_rows(n)
        seg = lax.slice_in_dim(packed, r0, r0 + rows, axis=lead).reshape(packed.shape[:lead] + (-1,))
        if rows * PACK_COLS != n:
            seg = seg[..., :n]
        out.append(seg.reshape(packed.shape[:lead] + tuple(shp)))
        r0 += rows
    return out


def _full_from_gathered(gathered, shard_shapes, axes):
    out = []
    for seg, shp, ax in zip(_unpack(gathered, shard_shapes, lead=1), shard_shapes, axes):
        seg = jnp.moveaxis(seg, 0, ax)
        out.append(seg.reshape(tuple(shp[:ax]) + (N_DEV * shp[ax],) + tuple(shp[ax + 1:])))
    return out


def kernel(x, p, mix_norm, conv_w_pw1, conv_b_pw1, conv_w_dw, conv_b_dw, conv_ln_g, conv_ln_b, conv_w_pw2, conv_b_pw2, kv_norm, w_kvf, b_f, attn_w_q, attn_w_o, ffn_norm, ffn_w1, ffn_w2, ple_norm, ple_w_gate, ple_w_proj, final_norm, loss_target, m_mix_norm, m_conv_w_pw1, m_conv_b_pw1, m_conv_w_dw, m_conv_b_dw, m_conv_ln_g, m_conv_ln_b, m_conv_w_pw2, m_conv_b_pw2, m_kv_norm, m_w_kvf, m_b_f, m_attn_w_q, m_attn_w_o, m_ffn_norm, m_ffn_w1, m_ffn_w2, m_ple_norm, m_ple_w_gate, m_ple_w_proj, m_final_norm, v_mix_norm, v_conv_w_pw1, v_conv_b_pw1, v_conv_w_dw, v_conv_b_dw, v_conv_ln_g, v_conv_ln_b, v_conv_w_pw2, v_conv_b_pw2, v_kv_norm, v_w_kvf, v_b_f, v_attn_w_q, v_attn_w_o, v_ffn_norm, v_ffn_w1, v_ffn_w2, v_ple_norm, v_ple_w_gate, v_ple_w_proj, v_final_norm):
    given = dict(locals())
    W = {n: given[n] for n in WEIGHTS}
    M = {n: given["m_" + n] for n in WEIGHTS}
    V = {n: given["v_" + n] for n in WEIGHTS}

    _, S, D = x.shape
    NA = conv_w_pw1.shape[0]
    NB = attn_w_q.shape[0]
    DEPTH = NA + NB
    H = b_f.shape[0]
    dh = D // H
    hg = 128 // dh
    G = D // 128
    scale = dh ** -0.5
    tm = _row_tile(S, 256)
    tq_f = _row_tile(S, FLASH_FWD_TILE[0])
    tkc_f = _row_tile(tq_f, FLASH_FWD_TILE[1])
    tk_b = _row_tile(S, FLASH_BWD_TILE[0])
    tqc_b = _row_tile(tk_b, FLASH_BWD_TILE[1])
    ts = _row_tile(S, 2048)
    xs = x[0]
    tgt = loss_target[0]
    ps = p[:, 0]
    row = lambda a: a.reshape(1, -1)

    big_names = list(SHARD_AXIS_BIG)
    small_names = list(SHARD_AXIS_SMALL)
    big = _full_from_gathered(
        all_gather(_pack([W[n] for n in big_names], 16, MXU_DTYPE), "weights_all_gather"),
        [W[n].shape for n in big_names], [SHARD_AXIS_BIG[n] for n in big_names])
    small = _full_from_gathered(
        all_gather(_pack([W[n] for n in small_names], 8), "vectors_all_gather"),
        [W[n].shape for n in small_names], [SHARD_AXIS_SMALL[n] for n in small_names])
    FW = dict(zip(big_names + small_names, big + small))
    wk, wv, wf = FW["w_kvf"][:, :D], FW["w_kvf"][:, D:2 * D], FW["w_kvf"][:, 2 * D:]

    saved = []
    h = xs
    kv = None
    for i in range(DEPTH):
        rec = {"h_in": h}
        if i < NA:
            h, rec["n"], rec["u"], rec["z"], rec["sw"] = conv_fwd(
                h, row(mix_norm[i]), FW["conv_w_pw1"][i], row(FW["conv_b_pw1"][i]), FW["conv_w_dw"][i],
                row(FW["conv_b_dw"][i]), row(FW["conv_ln_g"][i]), row(FW["conv_ln_b"][i]),
                FW["conv_w_pw2"][i], row(FW["conv_b_pw2"][i]), tm)
        else:
            j = i - NA
            if j == 0:
                k_, kT_, vT_, nkv, fl, c = kv_fwd(h, row(kv_norm), wk, wv, wf, row(b_f), tm)
                cg = c.reshape(S, G, hg)
                kv = dict(k=k_, kT=kT_, vT=vT_, n=nkv, fl=fl, h=h, c_col=jnp.transpose(cg, (1, 0, 2)),
                          c_row=jnp.transpose(cg, (1, 2, 0)))
            rec["n"], rec["q"], rec["qT"] = q_fwd(h, row(mix_norm[i]), FW["attn_w_q"][j], scale, tm)
            rec["o"], rec["o32"], rec["lse"] = flash_fwd(rec["qT"], kv["k"], kv["vT"], kv["c_col"], kv["c_row"], dh,
                                                         tq_f, tkc_f)
            h = attn_out_fwd(h, rec["o"], FW["attn_w_o"][j], tm)
        rec["h_ffn"] = h
        h, rec["n_ffn"], rec["a"], rec["s"] = ffn_fwd(h, row(ffn_norm[i]), FW["ffn_w1"][i], FW["ffn_w2"][i], tm)
        rec["h_ple"] = h
        h, rec["n_ple"], rec["gate"] = ple_fwd(h, row(ple_norm[i]), FW["ple_w_gate"][i], ps[i],
                                               FW["ple_w_proj"][i], tm)
        saved.append(rec)

    d, g_final, loss_part = loss_head(h, row(final_norm), tgt, tm)
    GW = {n: [None] * W[n].shape[0] for n in WEIGHTS if W[n].ndim > 1 and n != "w_kvf"}
    dks, dvs, dcs = [], [], []
    for i in reversed(range(DEPTH)):
        rec = saved[i]
        d_out = d
        d, dz, dpp, GW["ple_norm"][i] = ple_bwd(d_out, rec["h_ple"], row(ple_norm[i]), FW["ple_w_gate"][i],
                                                rec["gate"], ps[i], FW["ple_w_proj"][i], tm)
        GW["ple_w_gate"][i] = weight_grad(rec["n_ple"], dz, ts, "grad_ple_w_gate")
        GW["ple_w_proj"][i] = weight_grad(ps[i], dpp, ts, "grad_ple_w_proj", column_shards=N_DEV)
        d_out = d
        d, da, GW["ffn_norm"][i] = ffn_bwd(d_out, rec["h_ffn"], row(ffn_norm[i]), FW["ffn_w1"][i],
                                           FW["ffn_w2"][i], rec["a"], tm)
        GW["ffn_w2"][i] = weight_grad(rec["s"], d_out, ts, "grad_ffn_w2")
        GW["ffn_w1"][i] = weight_grad(rec["n_ffn"], da, ts, "grad_ffn_w1", column_shards=N_DEV)
        d_out = d
        if i >= NA:
            j = i - NA
            GW["attn_w_o"][j] = weight_grad(rec["o"], d_out, ts, "grad_attn_w_o")
            do, doT, delta = attn_out_bwd(d_out, FW["attn_w_o"][j], rec["o32"], dh, tm)
            qstat = jnp.concatenate([kv["c_col"], jnp.transpose(rec["lse"], (0, 2, 1)),
                                     jnp.transpose(delta.reshape(S, G, hg), (1, 0, 2))], axis=2)
            dq, dkT, dvT, dck, dcq = flash_bwd(rec["q"], rec["qT"], kv["k"], kv["kT"], kv["vT"], kv["c_row"], do, doT,
                                          qstat, dh, tk_b, tqc_b)
            dks.append(dkT)
            dvs.append(dvT)
            dcs.append(jnp.transpose(dck[:, :hg, :], (2, 0, 1)).reshape(S, H))
            dcs.append(jnp.transpose(dcq, (1, 0, 2)).reshape(S, H))
            d, dqs, GW["mix_norm"][i] = q_bwd(d_out, dq, rec["h_in"], row(mix_norm[i]), FW["attn_w_q"][j], scale, tm)
            GW["attn_w_q"][j] = weight_grad(rec["n"], dqs, ts, "grad_attn_w_q")
            if j == 0:
                d, dk_sum, dv_sum, dfl, g_kv_norm, g_b_f = kv_bwd(d, dks, dvs, dcs, kv["fl"], kv["h"],
                                                                  row(kv_norm), wk, wv, wf, tm)
                g_w_kvf = jnp.concatenate([weight_grad(kv["n"], dk_sum, ts, "grad_w_k"),
                                           weight_grad(kv["n"], dv_sum, ts, "grad_w_v"),
                                           weight_grad(kv["n"], dfl, ts, "grad_w_f")], axis=1)
        else:
            GW["conv_w_pw2"][i] = weight_grad(rec["sw"], d_out, ts, "grad_conv_w_pw2")
            (d, du, GW["conv_b_pw2"][i], GW["conv_ln_g"][i], GW["conv_ln_b"][i], GW["conv_b_dw"][i],
             GW["conv_w_dw"][i], GW["conv_b_pw1"][i], GW["mix_norm"][i]) = conv_bwd(
                d_out, rec["h_in"], row(mix_norm[i]), FW["conv_w_pw1"][i], FW["conv_w_dw"][i],
                row(FW["conv_ln_g"][i]), row(FW["conv_ln_b"][i]), FW["conv_w_pw2"][i], rec["u"], rec["z"], tm)
            GW["conv_w_pw1"][i] = weight_grad(rec["n"], du, ts, "grad_conv_w_pw1", column_shards=N_DEV)
    grad_x = d[None]

    sharded_names = big_names + small_names
    GW["w_kvf"] = [g_w_kvf]

    def device_major(n, g):
        width = W[n].shape[-1]
        if g.ndim == 3:
            return g
        if g.shape[-1] == width:
            return g.reshape(N_DEV, -1, width)
        return jnp.transpose(g.reshape(-1, N_DEV, width), (1, 0, 2))

    widths = sorted({W[n].shape[-1] for n in sharded_names}, reverse=True)
    groups = [[n for n in sharded_names if W[n].shape[-1] == width] for width in widths]

    def stack_rows(pieces, axis):
        rows = sum(p.shape[axis] for p in pieces)
        fill = -rows % (PACK_ROW_TILE if rows > PACK_ROW_TILE else 8)
        if fill:
            shape = list(pieces[0].shape)
            shape[axis] = fill
            pieces = pieces + [jnp.zeros(shape, pieces[0].dtype)]
        return jnp.concatenate(pieces, axis=axis)

    chunks = [stack_rows([device_major(n, g) for n in names for g in GW[n]], 1) for names in groups]
    core = lax.axis_index("c").astype(jnp.int32).reshape(1)
    landed = sibling_exchange(chunks)
    parts = chip_exchange([chip_partial(g, l, core) for g, l in zip(chunks, landed)])

    res = {}
    for names, width, part in zip(groups, widths, parts):
        group_rows = lambda src: stack_rows([src[n].reshape(-1, width) for n in names], 0)
        outs = adamw_sharded(part, group_rows(W), group_rows(M), group_rows(V))
        for kind, packed in zip(("grad", "delta", "new_m", "new_v"), outs):
            r0 = 0
            for n in names:
                nr = W[n].size // width
                res[kind, n] = packed[r0:r0 + nr].reshape(W[n].shape)
                r0 += nr

    rep_grads = {"mix_norm": jnp.concatenate(GW["mix_norm"], axis=0), "kv_norm": g_kv_norm,
                 "b_f": g_b_f, "ffn_norm": jnp.concatenate(GW["ffn_norm"], axis=0),
                 "ple_norm": jnp.concatenate(GW["ple_norm"], axis=0), "final_norm": g_final}

    def pack_rep(src, extra=None):
        rows_ = [jnp.pad(src[n].reshape(-1, src[n].shape[-1]), ((0, 0), (0, D - src[n].shape[-1])))
                 for n in REPLICATED]
        if extra is not None:
            rows_.append(jnp.pad(extra, ((0, 0), (0, D - extra.shape[-1]))))
        else:
            rows_.append(jnp.zeros((1, D), F32))
        flat = jnp.concatenate(rows_, axis=0)
        return jnp.pad(flat, ((0, -flat.shape[0] % 8), (0, 0)))

    rep_g = all_gather(pack_rep(rep_grads, loss_part), "replicated_all_gather")
    outs_rep = adamw_replicated(rep_g, pack_rep(W), pack_rep(M), pack_rep(V))
    n_rep_rows = sum(int(np.prod(W[n].shape[:-1])) for n in REPLICATED)
    for kind, packed in zip(("grad", "delta", "new_m", "new_v"), outs_rep):
        r0 = 0
        for n in REPLICATED:
            nr = int(np.prod(W[n].shape[:-1]))
            res[kind, n] = packed[r0:r0 + nr, :W[n].shape[-1]].reshape(W[n].shape)
            r0 += nr
    loss = outs_rep[0][n_rep_rows, 0]

    return (loss, grad_x, *[res["grad", n] for n in WEIGHTS], *[res["delta", n] for n in WEIGHTS],
            *[res["new_m", n] for n in WEIGHTS], *[res["new_v", n] for n in WEIGHTS])
```

```python
import numpy as np
import jax
import jax.numpy as jnp
from jax import lax
from jax.experimental import pallas as pl
from jax.experimental.pallas import tpu as pltpu

F32 = jnp.float32
MXU_DTYPE = jnp.bfloat16
ACT_DTYPE = jnp.bfloat16
WIRE_DTYPE = jnp.bfloat16

N_DEV = 8
N_CHIP = 4
EPS = 1e-6
NEG_BIG = -1e30
ADAM_LR = 0.001
ADAM_B1 = 0.9
ADAM_B2 = 0.999
ADAM_EPS = 1e-08
ADAM_WD = 0.01
ADAM_STEP = 10

VMEM_LIMIT_BYTES = 56 * 1024 * 1024
PACK_COLS = 1024
PACK_ROW_TILE = 256
FLASH_FWD_TILE = (1024, 512)
FLASH_BWD_TILE = (1024, 512)
FLASH_TILE = 128
HALO = 32
MESH = pl.DeviceIdType.MESH

SHARD_AXIS_BIG = {"conv_w_pw1": 2, "conv_w_pw2": 1, "w_kvf": 1, "attn_w_q": 1, "attn_w_o": 1,
                  "ffn_w1": 2, "ffn_w2": 1, "ple_w_gate": 1, "ple_w_proj": 2}
SHARD_AXIS_SMALL = {"conv_b_pw1": 1, "conv_w_dw": 2, "conv_b_dw": 1, "conv_ln_g": 1, "conv_ln_b": 1,
                    "conv_b_pw2": 1}
REPLICATED = ["mix_norm", "kv_norm", "b_f", "ffn_norm", "ple_norm", "final_norm"]
WEIGHTS = ["mix_norm", "conv_w_pw1", "conv_b_pw1", "conv_w_dw", "conv_b_dw", "conv_ln_g", "conv_ln_b",
           "conv_w_pw2", "conv_b_pw2", "kv_norm", "w_kvf", "b_f", "attn_w_q", "attn_w_o", "ffn_norm",
           "ffn_w1", "ffn_w2", "ple_norm", "ple_w_gate", "ple_w_proj", "final_norm"]


def _mm(a, b):
    return jnp.dot(a.astype(MXU_DTYPE), b.astype(MXU_DTYPE), preferred_element_type=F32)


def _mm_nt(a, b):
    return lax.dot_general(a.astype(MXU_DTYPE), b.astype(MXU_DTYPE), (((1,), (1,)), ((), ())),
                           preferred_element_type=F32)


def _mm_tn(a, b):
    return lax.dot_general(a.astype(MXU_DTYPE), b.astype(MXU_DTYPE), (((0,), (0,)), ((), ())),
                           preferred_element_type=F32)


def _split3(x):
    hi = x.astype(MXU_DTYPE)
    r1 = x - hi.astype(F32)
    mid = r1.astype(MXU_DTYPE)
    lo = (r1 - mid.astype(F32)).astype(MXU_DTYPE)
    return hi, mid, lo


def _tri_mm(tri, x):
    hi, mid, lo = _split3(x)
    return (jnp.dot(tri, lo, preferred_element_type=F32) + jnp.dot(tri, mid, preferred_element_type=F32)
            + jnp.dot(tri, hi, preferred_element_type=F32))


def _colsum8(x):
    tm, n = x.shape
    return jnp.sum(x.reshape(tm // 8, 8, n), axis=0)


def _rms(x, g):
    r = lax.rsqrt(jnp.mean(x * x, axis=-1, keepdims=True) + EPS)
    return x * r * g, r


def _rms_bwd(x, r, g, dn):
    w = dn * g
    dx = r * w - x * (r * r * r) * jnp.mean(w * x, axis=-1, keepdims=True)
    return dx, dn * x * r


def _sigmoid(x):
    return jax.nn.sigmoid(x)


def _params(n_grid):
    return pltpu.CompilerParams(dimension_semantics=("arbitrary",) * n_grid, vmem_limit_bytes=VMEM_LIMIT_BYTES)


def _rows(tm, n):
    return pl.BlockSpec((tm, n), lambda i: (i, 0))


def _rows_rev(tm, n, nt):
    return pl.BlockSpec((tm, n), lambda i: (nt - 1 - i, 0))


def _cols(n, tm):
    return pl.BlockSpec((n, tm), lambda i: (0, i))


def _cols_rev(n, tm, nt):
    return pl.BlockSpec((n, tm), lambda i: (0, nt - 1 - i))


def _whole(shape):
    nd = len(shape)
    return pl.BlockSpec(shape, lambda i: (0,) * nd)


def _row_tile(s, want):
    tm = min(s, want)
    assert s % tm == 0 and tm % 8 == 0, (s, tm)
    return tm


def conv_fwd(h, g, w1, b1, wd, bd, lg, lb, w2, b2, tm):
    S, D = h.shape
    CW = wd.shape[0]
    off = HALO - (CW - 1)
    assert 0 <= off and tm >= HALO
    nt = S // tm

    def body(h_ref, g_ref, w1_ref, b1_ref, wd_ref, bd_ref, lg_ref, lb_ref, w2_ref, b2_ref,
             ho_ref, n_ref, u_ref, z_ref, sw_ref, ext, win):
        @pl.when(pl.program_id(0) == 0)
        def _():
            ext[0:HALO, :] = jnp.zeros((HALO, D), F32)

        x = h_ref[...]
        n, _ = _rms(x, g_ref[...])
        n_ref[...] = n.astype(n_ref.dtype)
        u = _mm(n, w1_ref[...]) + b1_ref[...]
        u_ref[...] = u
        ext[HALO:HALO + tm, :] = u[:, :D] * _sigmoid(u[:, D:])
        z = jnp.broadcast_to(bd_ref[...], (tm, D))
        for b in range(8):
            amax = (CW - 1 - b) // 8
            win[0:tm + 8 * amax, :] = ext[off + b:off + b + tm + 8 * amax, :]
            for a8 in range(amax + 1):
                z = z + wd_ref[8 * a8 + b:8 * a8 + b + 1, :] * win[8 * a8:8 * a8 + tm, :]
        z_ref[...] = z
        ext[0:HALO, :] = ext[tm:tm + HALO, :]
        mu = jnp.mean(z, axis=-1, keepdims=True)
        zc = z - mu
        y = zc * lax.rsqrt(jnp.mean(zc * zc, axis=-1, keepdims=True) + EPS) * lg_ref[...] + lb_ref[...]
        sw = y * _sigmoid(y)
        sw_ref[...] = sw.astype(sw_ref.dtype)
        ho_ref[...] = x + _mm(sw, w2_ref[...]) + b2_ref[...]

    return pl.pallas_call(
        body, name="conv_fwd", grid=(nt,),
        in_specs=[_rows(tm, D), _whole((1, D)), _whole(w1.shape), _whole((1, 2 * D)), _whole(wd.shape),
                  _whole((1, D)), _whole((1, D)), _whole((1, D)), _whole(w2.shape), _whole((1, D))],
        out_specs=[_rows(tm, D), _rows(tm, D), _rows(tm, 2 * D), _rows(tm, D), _rows(tm, D)],
        out_shape=[jax.ShapeDtypeStruct((S, D), F32), jax.ShapeDtypeStruct((S, D), ACT_DTYPE),
                   jax.ShapeDtypeStruct((S, 2 * D), F32), jax.ShapeDtypeStruct((S, D), F32),
                   jax.ShapeDtypeStruct((S, D), ACT_DTYPE)],
        scratch_shapes=[pltpu.VMEM((HALO + tm, D), F32), pltpu.VMEM((HALO + tm, D), F32)],
        compiler_params=_params(1),
    )(h, g, w1, b1, wd, bd, lg, lb, w2, b2)


def ffn_fwd(h, g, w1, w2, tm):
    S, D = h.shape
    FF = w1.shape[1]

    def body(h_ref, g_ref, w1_ref, w2_ref, ho_ref, n_ref, a_ref, s_ref):
        x = h_ref[...]
        n, _ = _rms(x, g_ref[...])
        n_ref[...] = n.astype(n_ref.dtype)
        a = _mm(n, w1_ref[...])
        a_ref[...] = a
        s = jnp.square(jnp.maximum(a, 0.0))
        s_ref[...] = s.astype(s_ref.dtype)
        ho_ref[...] = x + _mm(s, w2_ref[...])

    return pl.pallas_call(
        body, name="ffn_fwd", grid=(S // tm,),
        in_specs=[_rows(tm, D), _whole((1, D)), _whole(w1.shape), _whole(w2.shape)],
        out_specs=[_rows(tm, D), _rows(tm, D), _rows(tm, FF), _rows(tm, FF)],
        out_shape=[jax.ShapeDtypeStruct((S, D), F32), jax.ShapeDtypeStruct((S, D), ACT_DTYPE),
                   jax.ShapeDtypeStruct((S, FF), F32), jax.ShapeDtypeStruct((S, FF), ACT_DTYPE)],
        compiler_params=_params(1),
    )(h, g, w1, w2)


def ple_fwd(h, g, wg, p, wp, tm):
    S, D = h.shape
    E = p.shape[1]

    def body(h_ref, g_ref, wg_ref, p_ref, wp_ref, ho_ref, n_ref, gate_ref):
        x = h_ref[...]
        n, _ = _rms(x, g_ref[...])
        n_ref[...] = n.astype(n_ref.dtype)
        gate = _sigmoid(_mm(n, wg_ref[...]))
        gate_ref[...] = gate
        ho_ref[...] = x + gate * _mm(p_ref[...], wp_ref[...])

    return pl.pallas_call(
        body, name="ple_fwd", grid=(S // tm,),
        in_specs=[_rows(tm, D), _whole((1, D)), _whole(wg.shape), _rows(tm, E), _whole(wp.shape)],
        out_specs=[_rows(tm, D), _rows(tm, D), _rows(tm, D)],
        out_shape=[jax.ShapeDtypeStruct((S, D), F32), jax.ShapeDtypeStruct((S, D), ACT_DTYPE),
                   jax.ShapeDtypeStruct((S, D), F32)],
        compiler_params=_params(1),
    )(h, g, wg, p, wp)


def kv_fwd(h, g, wk, wv, wf, bf, tm):
    S, D = h.shape
    H = wf.shape[1]

    def body(h_ref, g_ref, wk_ref, wv_ref, wf_ref, bf_ref, k_ref, kT_ref, vT_ref, n_ref, fl_ref, c_ref, carry):
        @pl.when(pl.program_id(0) == 0)
        def _():
            carry[...] = jnp.zeros_like(carry)

        n, _ = _rms(h_ref[...], g_ref[...])
        n_ref[...] = n.astype(n_ref.dtype)
        k = _mm(n, wk_ref[...])
        k_ref[...] = k.astype(k_ref.dtype)
        kT_ref[...] = k.T.astype(kT_ref.dtype)
        vT_ref[...] = _mm(n, wv_ref[...]).T.astype(vT_ref.dtype)
        fl = _mm(n, wf_ref[...]) + bf_ref[...]
        fl_ref[...] = fl
        logf = jnp.minimum(fl, 0.0) - jnp.log1p(jnp.exp(-jnp.abs(fl)))
        row = lax.broadcasted_iota(jnp.int32, (tm, tm), 0)
        col = lax.broadcasted_iota(jnp.int32, (tm, tm), 1)
        tri = (row >= col).astype(MXU_DTYPE)
        c = _tri_mm(tri, logf) + carry[...]
        c_ref[...] = c
        carry[...] = c[tm - 1:tm, :]

    return pl.pallas_call(
        body, name="kv_fwd", grid=(S // tm,),
        in_specs=[_rows(tm, D), _whole((1, D)), _whole(wk.shape), _whole(wv.shape), _whole(wf.shape),
                  _whole((1, H))],
        out_specs=[_rows(tm, D), _cols(D, tm), _cols(D, tm), _rows(tm, D), _rows(tm, H), _rows(tm, H)],
        out_shape=[jax.ShapeDtypeStruct((S, D), ACT_DTYPE), jax.ShapeDtypeStruct((D, S), ACT_DTYPE),
                   jax.ShapeDtypeStruct((D, S), ACT_DTYPE), jax.ShapeDtypeStruct((S, D), ACT_DTYPE),
                   jax.ShapeDtypeStruct((S, H), F32), jax.ShapeDtypeStruct((S, H), F32)],
        scratch_shapes=[pltpu.VMEM((1, H), F32)],
        compiler_params=_params(1),
    )(h, g, wk, wv, wf, bf)


def q_fwd(h, g, wq, scale, tm):
    S, D = h.shape

    def body(h_ref, g_ref, wq_ref, n_ref, q_ref, qT_ref):
        n, _ = _rms(h_ref[...], g_ref[...])
        n_ref[...] = n.astype(n_ref.dtype)
        q = _mm(n, wq_ref[...]) * scale
        q_ref[...] = q.astype(q_ref.dtype)
        qT_ref[...] = q.T.astype(qT_ref.dtype)

    return pl.pallas_call(
        body, name="q_fwd", grid=(S // tm,),
        in_specs=[_rows(tm, D), _whole((1, D)), _whole(wq.shape)],
        out_specs=[_rows(tm, D), _rows(tm, D), _cols(D, tm)],
        out_shape=[jax.ShapeDtypeStruct((S, D), ACT_DTYPE), jax.ShapeDtypeStruct((S, D), ACT_DTYPE),
                   jax.ShapeDtypeStruct((D, S), ACT_DTYPE)],
        compiler_params=_params(1),
    )(h, g, wq)


def attn_out_fwd(h, o, wo, tm):
    S, D = h.shape

    def body(h_ref, o_ref, wo_ref, ho_ref):
        ho_ref[...] = h_ref[...] + _mm(o_ref[...], wo_ref[...])

    return pl.pallas_call(
        body, name="attn_out_fwd", grid=(S // tm,),
        in_specs=[_rows(tm, D), _rows(tm, D), _whole(wo.shape)],
        out_specs=_rows(tm, D),
        out_shape=jax.ShapeDtypeStruct((S, D), F32),
        compiler_params=_params(1),
    )(h, o, wo)


def _causal_mask(key0, qry0, shape, key_axis):
    key = key0 + lax.broadcasted_iota(jnp.int32, shape, key_axis)
    qry = qry0 + lax.broadcasted_iota(jnp.int32, shape, 1 - key_axis)
    return key <= qry


def flash_fwd(qT, k, vT, c_col, c_row, dh, tq, tkc):
    D, S = qT.shape
    hg = 128 // dh
    G = D // 128
    per = tq // tkc
    assert tq % tkc == 0 and S % tq == 0

    def body(qT_ref, k_ref, vT_ref, ccol_ref, crow_ref, o_ref, o32_ref, lse_ref, m_scr, l_scr, acc_scr, mx_scr,
             *scratch):
        i = pl.program_id(1)
        m_scr[...] = jnp.full(m_scr.shape, NEG_BIG, F32)
        l_scr[...] = jnp.zeros(l_scr.shape, F32)
        acc_scr[...] = jnp.zeros(acc_scr.shape, F32)

        def head_shift(j, hh):
            c0 = ccol_ref[pl.ds(j * tkc, 1), hh:hh + 1]
            return c0, crow_ref[hh:hh + 1, :] - c0

        def scores(j, masked, slot):
            keys = pl.ds(pl.multiple_of(j * tkc, tkc), tkc)
            s_scr = scratch[slot]
            for hh in range(hg):
                lanes = slice(hh * dh, (hh + 1) * dh)
                c0, r = head_shift(j, hh)
                s = _mm(k_ref[keys, lanes], qT_ref[lanes, :]) - (ccol_ref[keys, hh:hh + 1] - c0)
                if masked:
                    s = jnp.where(_causal_mask(j * tkc, i * tq, (tkc, tq), 0), s, NEG_BIG)
                s_scr[hh] = s
                mx_scr[slot * hg + hh] = jnp.max(s, axis=0, keepdims=True) + r

        def update(j, slot):
            keys = pl.ds(pl.multiple_of(j * tkc, tkc), tkc)
            s_scr = scratch[slot]
            for hh in range(hg):
                lanes = slice(hh * dh, (hh + 1) * dh)
                _, r = head_shift(j, hh)
                m_old = m_scr[hh]
                m_new = jnp.maximum(m_old, mx_scr[slot * hg + hh])
                alpha = jnp.exp(m_old - m_new)
                p = jnp.exp(s_scr[hh] - (m_new - r))
                l_scr[hh] = alpha * l_scr[hh] + jnp.sum(p, axis=0, keepdims=True)
                acc_scr[lanes, :] = alpha * acc_scr[lanes, :] + _mm(vT_ref[lanes, keys], p)
                m_scr[hh] = m_new

        def chunks(j0, masked):
            for jj in range(per):
                scores(j0 + jj, masked, jj)
            for jj in range(per):
                update(j0 + jj, jj)

        def full_chunks(jb, carry):
            chunks(jb * per, False)
            return carry

        lax.fori_loop(0, i, full_chunks, 0)
        chunks(i * per, True)
        for hh in range(hg):
            lanes = slice(hh * dh, (hh + 1) * dh)
            acc_scr[lanes, :] = acc_scr[lanes, :] / l_scr[hh]
            lse_ref[hh:hh + 1, :] = m_scr[hh] + jnp.log(l_scr[hh])
        o = acc_scr[...].T
        o_ref[...] = o.astype(o_ref.dtype)
        o32_ref[...] = o

    return pl.pallas_call(
        body, name="flash_fwd", grid=(G, S // tq),
        in_specs=[pl.BlockSpec((128, tq), lambda g, i: (g, i)),
                  pl.BlockSpec((S, 128), lambda g, i: (0, g)),
                  pl.BlockSpec((128, S), lambda g, i: (g, 0)),
                  pl.BlockSpec((None, S, hg), lambda g, i: (g, 0, 0)),
                  pl.BlockSpec((None, hg, tq), lambda g, i: (g, 0, i))],
        out_specs=[pl.BlockSpec((tq, 128), lambda g, i: (i, g)),
                   pl.BlockSpec((tq, 128), lambda g, i: (i, g)),
                   pl.BlockSpec((None, hg, tq), lambda g, i: (g, 0, i))],
        out_shape=[jax.ShapeDtypeStruct((S, D), ACT_DTYPE), jax.ShapeDtypeStruct((S, D), F32),
                   jax.ShapeDtypeStruct((G, hg, S), F32)],
        scratch_shapes=([pltpu.VMEM((hg, 1, tq), F32), pltpu.VMEM((hg, 1, tq), F32), pltpu.VMEM((128, tq), F32),
                         pltpu.VMEM((per * hg, 1, tq), F32)]
                        + [pltpu.VMEM((hg, tkc, tq), F32)] * per),
        compiler_params=_params(2),
    )(qT, k, vT, c_col, c_row)


def loss_head(h, g, target, tm):
    S, D = h.shape
    nt = S // tm

    def body(h_ref, g_ref, t_ref, dh_ref, dg_ref, loss_ref, dg_acc, loss_acc):
        i = pl.program_id(0)

        @pl.when(i == 0)
        def _():
            dg_acc[...] = jnp.zeros_like(dg_acc)
            loss_acc[...] = jnp.zeros_like(loss_acc)

        x = h_ref[...]
        gg = g_ref[...]
        y, r = _rms(x, gg)
        e = y - t_ref[...]
        loss_acc[...] += 0.5 * jnp.sum(jnp.mean(e * e, axis=-1, keepdims=True), axis=0, keepdims=True)
        dx, dgr = _rms_bwd(x, r, gg, e / D)
        dh_ref[...] = dx
        dg_acc[...] += _colsum8(dgr)

        @pl.when(i == nt - 1)
        def _():
            dg_ref[...] = jnp.sum(dg_acc[...], axis=0, keepdims=True)
            loss_ref[...] = jnp.broadcast_to(loss_acc[...], loss_ref.shape)

    return pl.pallas_call(
        body, name="loss_head", grid=(nt,),
        in_specs=[_rows(tm, D), _whole((1, D)), _rows(tm, D)],
        out_specs=[_rows(tm, D), _whole((1, D)), _whole((1, 128))],
        out_shape=[jax.ShapeDtypeStruct((S, D), F32), jax.ShapeDtypeStruct((1, D), F32),
                   jax.ShapeDtypeStruct((1, 128), F32)],
        scratch_shapes=[pltpu.VMEM((8, D), F32), pltpu.VMEM((1, 1), F32)],
        compiler_params=_params(1),
    )(h, g, target)


def ple_bwd(d, h, g, wg, gate, p, wp, tm):
    S, D = h.shape
    E = p.shape[1]
    nt = S // tm

    def body(d_ref, h_ref, g_ref, wg_ref, gate_ref, p_ref, wp_ref, di_ref, dz_ref, dpp_ref, dg_ref, dg_acc):
        i = pl.program_id(0)

        @pl.when(i == 0)
        def _():
            dg_acc[...] = jnp.zeros_like(dg_acc)

        dd = d_ref[...]
        x = h_ref[...]
        gg = g_ref[...]
        gt = gate_ref[...]
        pp = _mm(p_ref[...], wp_ref[...])
        dpp_ref[...] = (dd * gt).astype(dpp_ref.dtype)
        dz = dd * pp * gt * (1.0 - gt)
        dz_ref[...] = dz.astype(dz_ref.dtype)
        r = lax.rsqrt(jnp.mean(x * x, axis=-1, keepdims=True) + EPS)
        dx, dgr = _rms_bwd(x, r, gg, _mm_nt(dz, wg_ref[...]))
        di_ref[...] = dd + dx
        dg_acc[...] += _colsum8(dgr)

        @pl.when(i == nt - 1)
        def _():
            dg_ref[...] = jnp.sum(dg_acc[...], axis=0, keepdims=True)

    return pl.pallas_call(
        body, name="ple_bwd", grid=(nt,),
        in_specs=[_rows(tm, D), _rows(tm, D), _whole((1, D)), _whole(wg.shape), _rows(tm, D), _rows(tm, E),
                  _whole(wp.shape)],
        out_specs=[_rows(tm, D), _rows(tm, D), _rows(tm, D), _whole((1, D))],
        out_shape=[jax.ShapeDtypeStruct((S, D), F32), jax.ShapeDtypeStruct((S, D), ACT_DTYPE),
                   jax.ShapeDtypeStruct((S, D), ACT_DTYPE), jax.ShapeDtypeStruct((1, D), F32)],
        scratch_shapes=[pltpu.VMEM((8, D), F32)],
        compiler_params=_params(1),
    )(d, h, g, wg, gate, p, wp)


def ffn_bwd(d, h, g, w1, w2, a, tm):
    S, D = h.shape
    FF = w1.shape[1]
    nt = S // tm

    def body(d_ref, h_ref, g_ref, w1_ref, w2_ref, a_ref, di_ref, da_ref, dg_ref, dg_acc):
        i = pl.program_id(0)

        @pl.when(i == 0)
        def _():
            dg_acc[...] = jnp.zeros_like(dg_acc)

        dd = d_ref[...]
        x = h_ref[...]
        da = _mm_nt(dd, w2_ref[...]) * (2.0 * jnp.maximum(a_ref[...], 0.0))
        da_ref[...] = da.astype(da_ref.dtype)
        r = lax.rsqrt(jnp.mean(x * x, axis=-1, keepdims=True) + EPS)
        dx, dgr = _rms_bwd(x, r, g_ref[...], _mm_nt(da, w1_ref[...]))
        di_ref[...] = dd + dx
        dg_acc[...] += _colsum8(dgr)

        @pl.when(i == nt - 1)
        def _():
            dg_ref[...] = jnp.sum(dg_acc[...], axis=0, keepdims=True)

    return pl.pallas_call(
        body, name="ffn_bwd", grid=(nt,),
        in_specs=[_rows(tm, D), _rows(tm, D), _whole((1, D)), _whole(w1.shape), _whole(w2.shape), _rows(tm, FF)],
        out_specs=[_rows(tm, D), _rows(tm, FF), _whole((1, D))],
        out_shape=[jax.ShapeDtypeStruct((S, D), F32), jax.ShapeDtypeStruct((S, FF), ACT_DTYPE),
                   jax.ShapeDtypeStruct((1, D), F32)],
        scratch_shapes=[pltpu.VMEM((8, D), F32)],
        compiler_params=_params(1),
    )(d, h, g, w1, w2, a)


def attn_out_bwd(d, wo, o32, dh, tm):
    S, D = d.shape
    H = D // dh

    def body(d_ref, wo_ref, o_ref, do_ref, doT_ref, delta_ref):
        do32 = _mm_nt(d_ref[...], wo_ref[...])
        do = do32.astype(do_ref.dtype)
        do_ref[...] = do
        doT_ref[...] = do32.T.astype(doT_ref.dtype)
        lane_head = lax.broadcasted_iota(jnp.int32, (D, H), 0) // dh
        seg = (lane_head == lax.broadcasted_iota(jnp.int32, (D, H), 1)).astype(MXU_DTYPE)
        hi, mid, lo = _split3(do.astype(F32) * o_ref[...])
        delta_ref[...] = (jnp.dot(lo, seg, preferred_element_type=F32) + jnp.dot(mid, seg, preferred_element_type=F32)
                          + jnp.dot(hi, seg, preferred_element_type=F32))

    return pl.pallas_call(
        body, name="attn_out_bwd", grid=(S // tm,),
        in_specs=[_rows(tm, D), _whole(wo.shape), _rows(tm, D)],
        out_specs=[_rows(tm, D), _cols(D, tm), _rows(tm, H)],
        out_shape=[jax.ShapeDtypeStruct((S, D), ACT_DTYPE), jax.ShapeDtypeStruct((D, S), ACT_DTYPE),
                   jax.ShapeDtypeStruct((S, H), F32)],
        compiler_params=_params(1),
    )(d, wo, o32)


def flash_bwd(q, qT, k, kT, vT, c_row, do, doT, qstat, dh, tk, tqc):
    S, D = q.shape
    hg = 128 // dh
    G = D // 128
    per = tk // tqc
    nchunk = S // tqc
    assert hg <= 8 and tk % tqc == 0 and S % tk == 0

    def body(q_ref, qT_ref, k_ref, kT_ref, vT_ref, crow_ref, do_ref, doT_ref, st_ref,
             dq_ref, dkT_ref, dvT_ref, dck_ref, dcq_ref):
        ki = pl.program_id(1)

        @pl.when(ki == 0)
        def _():
            dq_ref[...] = jnp.zeros_like(dq_ref)
            dcq_ref[...] = jnp.zeros_like(dcq_ref)

        dck_ref[...] = jnp.zeros_like(dck_ref)
        dkT_ref[...] = jnp.zeros_like(dkT_ref)
        dvT_ref[...] = jnp.zeros_like(dvT_ref)
        def chunk(jq, masked, nk):
            rows = pl.ds(pl.multiple_of(jq * tqc, tqc), tqc)
            st = st_ref[rows, :]
            for hh in range(hg):
                lanes = slice(hh * dh, (hh + 1) * dh)
                ck = crow_ref[hh:hh + 1, 0:nk]
                c0 = ck[:, 0:1]
                u = (st[:, hh:hh + 1] - c0) - st[:, hg + hh:hg + hh + 1]
                s = (_mm(q_ref[rows, lanes], kT_ref[lanes, 0:nk]) - (ck - c0)) + u
                if masked:
                    s = jnp.where(_causal_mask(ki * tk, jq * tqc, (tqc, nk), 1), s, NEG_BIG)
                p = jnp.exp(s)
                dvT_ref[lanes, 0:nk] += _mm(doT_ref[lanes, rows], p)
                ds = p * (_mm(do_ref[rows, lanes], vT_ref[lanes, 0:nk]) - st[:, 2 * hg + hh:2 * hg + hh + 1])
                dkT_ref[lanes, 0:nk] += _mm(qT_ref[lanes, rows], ds)
                dck_ref[hh:hh + 1, 0:nk] -= jnp.sum(ds, axis=0, keepdims=True)
                dcq_ref[rows, hh:hh + 1] += jnp.sum(ds, axis=1, keepdims=True)
                dq_ref[rows, lanes] += _mm(ds, k_ref[0:nk, lanes])

        for jj in range(per):
            chunk(ki * per + jj, True, (jj + 1) * tqc)

        def full_chunk(jq, carry):
            chunk(jq, False, tk)
            return carry

        lax.fori_loop((ki + 1) * per, nchunk, full_chunk, 0)

    return pl.pallas_call(
        body, name="flash_bwd", grid=(G, S // tk),
        in_specs=[pl.BlockSpec((S, 128), lambda g, j: (0, g)),
                  pl.BlockSpec((128, S), lambda g, j: (g, 0)),
                  pl.BlockSpec((tk, 128), lambda g, j: (j, g)),
                  pl.BlockSpec((128, tk), lambda g, j: (g, j)),
                  pl.BlockSpec((128, tk), lambda g, j: (g, j)),
                  pl.BlockSpec((None, hg, tk), lambda g, j: (g, 0, j)),
                  pl.BlockSpec((S, 128), lambda g, j: (0, g)),
                  pl.BlockSpec((128, S), lambda g, j: (g, 0)),
                  pl.BlockSpec((None, S, 3 * hg), lambda g, j: (g, 0, 0))],
        out_specs=[pl.BlockSpec((S, 128), lambda g, j: (0, g)),
                   pl.BlockSpec((128, tk), lambda g, j: (g, j)),
                   pl.BlockSpec((128, tk), lambda g, j: (g, j)),
                   pl.BlockSpec((None, 8, tk), lambda g, j: (g, 0, j)),
                   pl.BlockSpec((None, S, hg), lambda g, j: (g, 0, 0))],
        out_shape=[jax.ShapeDtypeStruct((S, D), F32), jax.ShapeDtypeStruct((D, S), F32),
                   jax.ShapeDtypeStruct((D, S), F32), jax.ShapeDtypeStruct((G, 8, S), F32),
                   jax.ShapeDtypeStruct((G, S, hg), F32)],
        compiler_params=_params(2),
    )(q, qT, k, kT, vT, c_row, do, doT, qstat)


def q_bwd(d, dq, h, g, wq, scale, tm):
    S, D = h.shape
    nt = S // tm

    def body(d_ref, dq_ref, h_ref, g_ref, wq_ref, di_ref, dqs_ref, dg_ref, dg_acc):
        i = pl.program_id(0)

        @pl.when(i == 0)
        def _():
            dg_acc[...] = jnp.zeros_like(dg_acc)

        x = h_ref[...]
        dqs = dq_ref[...] * scale
        dqs_ref[...] = dqs.astype(dqs_ref.dtype)
        r = lax.rsqrt(jnp.mean(x * x, axis=-1, keepdims=True) + EPS)
        dx, dgr = _rms_bwd(x, r, g_ref[...], _mm_nt(dqs, wq_ref[...]))
        di_ref[...] = d_ref[...] + dx
        dg_acc[...] += _colsum8(dgr)

        @pl.when(i == nt - 1)
        def _():
            dg_ref[...] = jnp.sum(dg_acc[...], axis=0, keepdims=True)

    return pl.pallas_call(
        body, name="q_bwd", grid=(nt,),
        in_specs=[_rows(tm, D), _rows(tm, D), _rows(tm, D), _whole((1, D)), _whole(wq.shape)],
        out_specs=[_rows(tm, D), _rows(tm, D), _whole((1, D))],
        out_shape=[jax.ShapeDtypeStruct((S, D), F32), jax.ShapeDtypeStruct((S, D), ACT_DTYPE),
                   jax.ShapeDtypeStruct((1, D), F32)],
        scratch_shapes=[pltpu.VMEM((8, D), F32)],
        compiler_params=_params(1),
    )(d, dq, h, g, wq)


def kv_bwd(d, dks, dvs, dcs, fl, h, g, wk, wv, wf, tm):
    S, D = h.shape
    H = wf.shape[1]
    nt = S // tm
    nl = len(dks)
    nc = len(dcs)

    def body(*refs):
        d_ref = refs[0]
        dk_refs = refs[1:1 + nl]
        dv_refs = refs[1 + nl:1 + 2 * nl]
        dc_refs = refs[1 + 2 * nl:1 + 2 * nl + nc]
        (fl_ref, h_ref, g_ref, wk_ref, wv_ref, wf_ref,
         di_ref, dk_ref, dv_ref, dfl_ref, dg_ref, dbf_ref, dg_acc, dbf_acc, carry) = refs[1 + 2 * nl + nc:]
        i = pl.program_id(0)

        @pl.when(i == 0)
        def _():
            dg_acc[...] = jnp.zeros_like(dg_acc)
            dbf_acc[...] = jnp.zeros_like(dbf_acc)
            carry[...] = jnp.zeros_like(carry)

        dkT = dk_refs[0][...]
        dvT = dv_refs[0][...]
        for l in range(1, nl):
            dkT = dkT + dk_refs[l][...]
            dvT = dvT + dv_refs[l][...]
        dk = dkT.T
        dv = dvT.T
        dk_ref[...] = dk.astype(dk_ref.dtype)
        dv_ref[...] = dv.astype(dv_ref.dtype)
        row = lax.broadcasted_iota(jnp.int32, (tm, tm), 0)
        col = lax.broadcasted_iota(jnp.int32, (tm, tm), 1)
        tri = (col >= row).astype(MXU_DTYPE)
        dc = dc_refs[0][...]
        for l in range(1, nc):
            dc = dc + dc_refs[l][...]
        dlogf = _tri_mm(tri, dc) + carry[...]
        carry[...] = dlogf[0:1, :]
        dfl = dlogf * _sigmoid(-fl_ref[...])
        dfl_ref[...] = dfl
        dbf_acc[...] += jnp.sum(dfl, axis=0, keepdims=True)
        x = h_ref[...]
        dn = _mm_nt(dk, wk_ref[...]) + _mm_nt(dv, wv_ref[...]) + _mm_nt(dfl, wf_ref[...])
        r = lax.rsqrt(jnp.mean(x * x, axis=-1, keepdims=True) + EPS)
        dx, dgr = _rms_bwd(x, r, g_ref[...], dn)
        di_ref[...] = d_ref[...] + dx
        dg_acc[...] += _colsum8(dgr)

        @pl.when(i == nt - 1)
        def _():
            dg_ref[...] = jnp.sum(dg_acc[...], axis=0, keepdims=True)
            dbf_ref[...] = dbf_acc[...]

    rev = lambda n: _rows_rev(tm, n, nt)
    return pl.pallas_call(
        body, name="kv_bwd", grid=(nt,),
        in_specs=([rev(D)] + [_cols_rev(D, tm, nt)] * (2 * nl)
                  + [rev(H)] * nc
                  + [rev(H), rev(D), _whole((1, D)), _whole(wk.shape), _whole(wv.shape), _whole(wf.shape)]),
        out_specs=[rev(D), rev(D), rev(D), rev(H), _whole((1, D)), _whole((1, H))],
        out_shape=[jax.ShapeDtypeStruct((S, D), F32), jax.ShapeDtypeStruct((S, D), ACT_DTYPE),
                   jax.ShapeDtypeStruct((S, D), ACT_DTYPE), jax.ShapeDtypeStruct((S, H), F32),
                   jax.ShapeDtypeStruct((1, D), F32), jax.ShapeDtypeStruct((1, H), F32)],
        scratch_shapes=[pltpu.VMEM((8, D), F32), pltpu.VMEM((1, H), F32), pltpu.VMEM((1, H), F32)],
        compiler_params=_params(1),
    )(d, *dks, *dvs, *dcs, fl, h, g, wk, wv, wf)


def conv_bwd(d, h, g, w1, wd, lg, lb, w2, u, z, tm):
    S, D = h.shape
    CW = wd.shape[0]
    nt = S // tm
    assert tm >= HALO and CW - 1 <= HALO

    def body(d_ref, h_ref, g_ref, w1_ref, wd_ref, lg_ref, lb_ref, w2_ref, u_ref, z_ref,
             di_ref, du_ref, db2_ref, dlg_ref, dlb_ref, dbd_ref, dwd_ref, db1_ref, dg_ref,
             ext, win, db2_acc, dlg_acc, dlb_acc, dbd_acc, dwd_acc, db1_acc, dg_acc):
        i = pl.program_id(0)

        @pl.when(i == 0)
        def _():
            ext[tm:tm + HALO, :] = jnp.zeros((HALO, D), F32)
            for acc in (db2_acc, dlg_acc, dlb_acc, dbd_acc, dwd_acc, db1_acc, dg_acc):
                acc[...] = jnp.zeros_like(acc)

        dd = d_ref[...]
        db2_acc[...] += _colsum8(dd)
        dsw = _mm_nt(dd, w2_ref[...])
        zz = z_ref[...]
        zc = zz - jnp.mean(zz, axis=-1, keepdims=True)
        rs = lax.rsqrt(jnp.mean(zc * zc, axis=-1, keepdims=True) + EPS)
        xh = zc * rs
        lgv = lg_ref[...]
        y = xh * lgv + lb_ref[...]
        sg = _sigmoid(y)
        dy = dsw * (sg * (1.0 + y * (1.0 - sg)))
        dlg_acc[...] += _colsum8(dy * xh)
        dlb_acc[...] += _colsum8(dy)
        dxh = dy * lgv
        dz = rs * (dxh - jnp.mean(dxh, axis=-1, keepdims=True) - xh * jnp.mean(dxh * xh, axis=-1, keepdims=True))
        dbd_acc[...] += _colsum8(dz)
        ext[0:tm, :] = dz
        uu = u_ref[...]
        a = uu[:, :D]
        sgg = _sigmoid(uu[:, D:])
        glu = a * sgg
        dglu = jnp.zeros((tm, D), F32)
        for b in range(8):
            amax = (CW - 1 - b) // 8
            win[0:tm + 8 * amax, :] = ext[b:b + tm + 8 * amax, :]
            for a8 in range(amax + 1):
                k = CW - 1 - (8 * a8 + b)
                sh = win[8 * a8:8 * a8 + tm, :]
                dglu = dglu + wd_ref[k:k + 1, :] * sh
                dwd_acc[k] += _colsum8(glu * sh)
        ext[tm:tm + HALO, :] = ext[0:HALO, :]
        da = dglu * sgg
        dgg = dglu * a * sgg * (1.0 - sgg)
        du_ref[:, :D] = da.astype(du_ref.dtype)
        du_ref[:, D:] = dgg.astype(du_ref.dtype)
        db1_acc[:, :D] += _colsum8(da)
        db1_acc[:, D:] += _colsum8(dgg)
        dn = _mm_nt(da, w1_ref[:, :D]) + _mm_nt(dgg, w1_ref[:, D:])
        x = h_ref[...]
        r = lax.rsqrt(jnp.mean(x * x, axis=-1, keepdims=True) + EPS)
        dx, dgr = _rms_bwd(x, r, g_ref[...], dn)
        di_ref[...] = dd + dx
        dg_acc[...] += _colsum8(dgr)

        @pl.when(i == nt - 1)
        def _():
            db2_ref[...] = jnp.sum(db2_acc[...], axis=0, keepdims=True)
            dlg_ref[...] = jnp.sum(dlg_acc[...], axis=0, keepdims=True)
            dlb_ref[...] = jnp.sum(dlb_acc[...], axis=0, keepdims=True)
            dbd_ref[...] = jnp.sum(dbd_acc[...], axis=0, keepdims=True)
            dwd_ref[...] = jnp.sum(dwd_acc[...], axis=1)
            db1_ref[...] = jnp.sum(db1_acc[...], axis=0, keepdims=True)
            dg_ref[...] = jnp.sum(dg_acc[...], axis=0, keepdims=True)

    rev = lambda n: _rows_rev(tm, n, nt)
    vec = jax.ShapeDtypeStruct((1, D), F32)
    return pl.pallas_call(
        body, name="conv_bwd", grid=(nt,),
        in_specs=[rev(D), rev(D), _whole((1, D)), _whole(w1.shape), _whole(wd.shape), _whole((1, D)),
                  _whole((1, D)), _whole(w2.shape), rev(2 * D), rev(D)],
        out_specs=[rev(D), rev(2 * D), _whole((1, D)), _whole((1, D)), _whole((1, D)), _whole((1, D)),
                   _whole((CW, D)), _whole((1, 2 * D)), _whole((1, D))],
        out_shape=[jax.ShapeDtypeStruct((S, D), F32), jax.ShapeDtypeStruct((S, 2 * D), ACT_DTYPE),
                   vec, vec, vec, vec, jax.ShapeDtypeStruct((CW, D), F32),
                   jax.ShapeDtypeStruct((1, 2 * D), F32), vec],
        scratch_shapes=[pltpu.VMEM((tm + HALO, D), F32), pltpu.VMEM((tm + HALO, D), F32),
                        pltpu.VMEM((8, D), F32), pltpu.VMEM((8, D), F32),
                        pltpu.VMEM((8, D), F32), pltpu.VMEM((8, D), F32), pltpu.VMEM((CW, 8, D), F32),
                        pltpu.VMEM((8, 2 * D), F32), pltpu.VMEM((8, D), F32)],
        compiler_params=_params(1),
    )(d, h, g, w1, wd, lg, lb, w2, u, z)


def weight_grad(a, b, ts, name, column_shards=None):
    S, M = a.shape
    N = b.shape[1]
    ta = M if M <= 1024 else 1024
    tb = N if N <= 1024 else 1024
    assert M % ta == 0 and N % tb == 0 and S % ts == 0
    if column_shards is None:
        width, per_tile = tb, 1
        out_spec = pl.BlockSpec((ta, tb), lambda i, j, s: (i, j))
        out_shape = jax.ShapeDtypeStruct((M, N), F32)
    else:
        width = N // column_shards
        per_tile = tb // width
        assert tb % width == 0
        out_spec = pl.BlockSpec((per_tile, ta, width), lambda i, j, s: (j, i, 0))
        out_shape = jax.ShapeDtypeStruct((column_shards, M, width), F32)

    def body(a_ref, b_ref, o_ref):
        @pl.when(pl.program_id(2) == 0)
        def _():
            o_ref[...] = jnp.zeros_like(o_ref)

        res = _mm_tn(a_ref[...], b_ref[...])
        if column_shards is None:
            o_ref[...] += res
        else:
            for d in range(per_tile):
                o_ref[d] += res[:, d * width:(d + 1) * width]

    return pl.pallas_call(
        body, name=name, grid=(M // ta, N // tb, S // ts),
        in_specs=[pl.BlockSpec((ts, ta), lambda i, j, s: (s, i)), pl.BlockSpec((ts, tb), lambda i, j, s: (s, j))],
        out_specs=out_spec, out_shape=out_shape,
        compiler_params=_params(3),
    )(a, b)


def _position():
    return lax.axis_index("x"), lax.axis_index("y"), lax.axis_index("c")


def all_gather(x, name):
    def body(x_ref, out_ref, send_sems, recv_sems, local_sem):
        x, y, c = _position()
        me, sibling = (x, y, c), (x, y, 1 - c)
        chips = [(1 - x, y), (x, 1 - y), (1 - x, 1 - y)]

        def slot(px, py, pc):
            return out_ref.at[4 * px + 2 * py + pc]

        def copy(k, block, to, src=None):
            return pltpu.make_async_remote_copy(
                src_ref=slot(*block) if src is None else src, dst_ref=slot(*block),
                send_sem=send_sems.at[k], recv_sem=recv_sems.at[k], device_id=to, device_id_type=MESH)

        mine = pltpu.make_async_copy(x_ref, slot(*me), local_sem)
        mine.start()
        first = [copy(0, me, sibling, src=x_ref)]
        first += [copy(1 + j, me, (*chip, c), src=x_ref) for j, chip in enumerate(chips)]
        for cp in first:
            cp.start()
        passed = [copy(4 + j, (*chip, c), sibling) for j, chip in enumerate(chips)]
        for j, chip in enumerate(chips):
            copy(1 + j, (*chip, c), me).wait_recv()
            passed[j].start()
        copy(0, sibling, me).wait_recv()
        for j, chip in enumerate(chips):
            copy(4 + j, (*chip, 1 - c), me).wait_recv()
        for cp in first + passed:
            cp.wait_send()
        mine.wait()

    return pl.pallas_call(
        body, name=name,
        in_specs=[pl.BlockSpec(memory_space=pl.ANY)], out_specs=pl.BlockSpec(memory_space=pl.ANY),
        out_shape=jax.ShapeDtypeStruct((N_DEV,) + x.shape, x.dtype),
        scratch_shapes=[pltpu.SemaphoreType.DMA((7,)), pltpu.SemaphoreType.DMA((7,)), pltpu.SemaphoreType.DMA],
    )(x)


def sibling_exchange(gs):
    n = len(gs)

    def body(*refs):
        g_refs, land_refs, (send_sems, recv_sems) = refs[:n], refs[n:2 * n], refs[2 * n:]
        x, y, c = _position()
        copies = [pltpu.make_async_remote_copy(
            src_ref=g_refs[a].at[2 * j + 1 - c], dst_ref=land_refs[a].at[j], send_sem=send_sems.at[N_CHIP * a + j],
            recv_sem=recv_sems.at[N_CHIP * a + j], device_id=(x, y, 1 - c), device_id_type=MESH)
            for a in range(n) for j in range(N_CHIP)]
        for cp in copies:
            cp.start()
        for cp in copies:
            cp.wait()

    return pl.pallas_call(
        body, name="grad_sibling_exchange",
        in_specs=[pl.BlockSpec(memory_space=pl.ANY)] * n, out_specs=[pl.BlockSpec(memory_space=pl.ANY)] * n,
        out_shape=[jax.ShapeDtypeStruct((N_CHIP,) + g.shape[1:], g.dtype) for g in gs],
        scratch_shapes=[pltpu.SemaphoreType.DMA((N_CHIP * n,)), pltpu.SemaphoreType.DMA((N_CHIP * n,))],
    )(*gs)


def chip_partial(g, land, core):
    _, R, C = g.shape
    tr = _row_tile(R, PACK_ROW_TILE)

    def body(c_ref, g_ref, l_ref, o_ref):
        o_ref[...] = (g_ref[...] + l_ref[...]).astype(o_ref.dtype)

    grid_spec = pltpu.PrefetchScalarGridSpec(
        num_scalar_prefetch=1, grid=(N_CHIP, R // tr),
        in_specs=[pl.BlockSpec((None, tr, C), lambda j, i, cr: (2 * j + cr[0], i, 0)),
                  pl.BlockSpec((None, tr, C), lambda j, i, cr: (j, i, 0))],
        out_specs=pl.BlockSpec((None, tr, C), lambda j, i, cr: (j, i, 0)))
    return pl.pallas_call(
        body, name="grad_chip_partial", grid_spec=grid_spec,
        out_shape=jax.ShapeDtypeStruct((N_CHIP, R, C), WIRE_DTYPE),
        compiler_params=_params(2),
    )(core, g, land)


def chip_exchange(parts):
    n = len(parts)

    def body(*refs):
        p_refs, land_refs, (send_sems, recv_sems, local_sems) = refs[:n], refs[n:2 * n], refs[2 * n:]
        x, y, c = _position()
        mychip = 2 * x + y
        chips = [(1 - x, y), (x, 1 - y), (1 - x, 1 - y)]

        def remote(a, k, slot):
            cx, cy = chips[k]
            return pltpu.make_async_remote_copy(
                src_ref=p_refs[a].at[2 * cx + cy], dst_ref=land_refs[a].at[slot], send_sem=send_sems.at[3 * a + k],
                recv_sem=recv_sems.at[3 * a + k], device_id=(cx, cy, c), device_id_type=MESH)

        mine = [pltpu.make_async_copy(p_refs[a].at[mychip], land_refs[a].at[mychip], local_sems.at[a])
                for a in range(n)]
        for cp in mine:
            cp.start()
        copies = [remote(a, k, mychip) for a in range(n) for k in range(3)]
        for cp in copies:
            cp.start()
        for a in range(n):
            for k, (cx, cy) in enumerate(chips):
                remote(a, k, 2 * cx + cy).wait_recv()
        for cp in copies:
            cp.wait_send()
        for cp in mine:
            cp.wait()

    return pl.pallas_call(
        body, name="grad_chip_exchange",
        in_specs=[pl.BlockSpec(memory_space=pl.ANY)] * n, out_specs=[pl.BlockSpec(memory_space=pl.ANY)] * n,
        out_shape=[jax.ShapeDtypeStruct(p.shape, p.dtype) for p in parts],
        scratch_shapes=[pltpu.SemaphoreType.DMA((3 * n,)), pltpu.SemaphoreType.DMA((3 * n,)),
                        pltpu.SemaphoreType.DMA((n,))],
    )(*parts)


def _adamw(w, g, m, v):
    m = ADAM_B1 * m + (1.0 - ADAM_B1) * g
    v = ADAM_B2 * v + (1.0 - ADAM_B2) * jnp.square(g)
    m_hat = m / (1.0 - ADAM_B1 ** ADAM_STEP)
    v_hat = v / (1.0 - ADAM_B2 ** ADAM_STEP)
    delta = -ADAM_LR * (m_hat / (jnp.sqrt(v_hat) + ADAM_EPS) + ADAM_WD * w)
    return delta, m, v


def adamw_sharded(parts, w, m, v):
    R, C = w.shape
    tr = _row_tile(R, PACK_ROW_TILE)

    def body(p_ref, w_ref, m_ref, v_ref, g_ref, d_ref, nm_ref, nv_ref):
        g = p_ref[0].astype(F32)
        for j in range(1, N_CHIP):
            g = g + p_ref[j].astype(F32)
        g_ref[...] = g
        d_ref[...], nm_ref[...], nv_ref[...] = _adamw(w_ref[...], g, m_ref[...], v_ref[...])

    out = jax.ShapeDtypeStruct((R, C), F32)
    return pl.pallas_call(
        body, name="adamw_sharded", grid=(R // tr,),
        in_specs=[pl.BlockSpec((N_CHIP, tr, C), lambda i: (0, i, 0)), _rows(tr, C), _rows(tr, C), _rows(tr, C)],
        out_specs=[_rows(tr, C)] * 4, out_shape=[out] * 4,
        compiler_params=_params(1),
    )(parts, w, m, v)


def adamw_replicated(gathered, w, m, v):
    R, C = w.shape

    def body(p_ref, w_ref, m_ref, v_ref, g_ref, d_ref, nm_ref, nv_ref):
        g = p_ref[0]
        for j in range(1, N_DEV):
            g = g + p_ref[j]
        g_ref[...] = g
        d_ref[...], nm_ref[...], nv_ref[...] = _adamw(w_ref[...], g, m_ref[...], v_ref[...])

    out = jax.ShapeDtypeStruct((R, C), F32)
    return pl.pallas_call(
        body, name="adamw_replicated", grid=(1,),
        in_specs=[_whole(gathered.shape), _whole((R, C)), _whole((R, C)), _whole((R, C))],
        out_specs=[_whole((R, C))] * 4, out_shape=[out] * 4,
        compiler_params=_params(1),
    )(gathered, w, m, v)


def _piece_rows(n):
    return -(-n // PACK_COLS)


def _as_rows(a, lead):
    flat = a.reshape(a.shape[:lead] + (-1,))
    fill = _piece_rows(flat.shape[-1]) * PACK_COLS - flat.shape[-1]
    if fill:
        flat = jnp.pad(flat, [(0, 0)] * lead + [(0, fill)])
    return flat.reshape(flat.shape[:-1] + (-1, PACK_COLS))


def _pack(arrays, rows_multiple, dtype=None, lead=0):
    pieces = [_as_rows(a if dtype is None else a.astype(dtype), lead) for a in arrays]
    extra = -sum(p.shape[lead] for p in pieces) % rows_multiple
    if extra:
        pieces.append(jnp.zeros(pieces[0].shape[:lead] + (extra, PACK_COLS), pieces[0].dtype))
    return jnp.concatenate(pieces, axis=lead)


def _unpack(packed, shapes, lead=0):
    out, r0 = [], 0
    for shp in shapes:
        n = int(np.prod(shp))
        rows = _piece_rows(n)
        seg = lax.slice_in_dim(packed, r0, r0 + rows, axis=lead).reshape(packed.shape[:lead] + (-1,))
        if rows * PACK_COLS != n:
            seg = seg[..., :n]
        out.append(seg.reshape(packed.shape[:lead] + tuple(shp)))
        r0 += rows
    return out


def _full_from_gathered(gathered, shard_shapes, axes):
    out = []
    for seg, shp, ax in zip(_unpack(gathered, shard_shapes, lead=1), shard_shapes, axes):
        seg = jnp.moveaxis(seg, 0, ax)
        out.append(seg.reshape(tuple(shp[:ax]) + (N_DEV * shp[ax],) + tuple(shp[ax + 1:])))
    return out


def kernel(x, p, mix_norm, conv_w_pw1, conv_b_pw1, conv_w_dw, conv_b_dw, conv_ln_g, conv_ln_b, conv_w_pw2, conv_b_pw2, kv_norm, w_kvf, b_f, attn_w_q, attn_w_o, ffn_norm, ffn_w1, ffn_w2, ple_norm, ple_w_gate, ple_w_proj, final_norm, loss_target, m_mix_norm, m_conv_w_pw1, m_conv_b_pw1, m_conv_w_dw, m_conv_b_dw, m_conv_ln_g, m_conv_ln_b, m_conv_w_pw2, m_conv_b_pw2, m_kv_norm, m_w_kvf, m_b_f, m_attn_w_q, m_attn_w_o, m_ffn_norm, m_ffn_w1, m_ffn_w2, m_ple_norm, m_ple_w_gate, m_ple_w_proj, m_final_norm, v_mix_norm, v_conv_w_pw1, v_conv_b_pw1, v_conv_w_dw, v_conv_b_dw, v_conv_ln_g, v_conv_ln_b, v_conv_w_pw2, v_conv_b_pw2, v_kv_norm, v_w_kvf, v_b_f, v_attn_w_q, v_attn_w_o, v_ffn_norm, v_ffn_w1, v_ffn_w2, v_ple_norm, v_ple_w_gate, v_ple_w_proj, v_final_norm):
    given = dict(locals())
    W = {n: given[n] for n in WEIGHTS}
    M = {n: given["m_" + n] for n in WEIGHTS}
    V = {n: given["v_" + n] for n in WEIGHTS}

    _, S, D = x.shape
    NA = conv_w_pw1.shape[0]
    NB = attn_w_q.shape[0]
    DEPTH = NA + NB
    H = b_f.shape[0]
    dh = D // H
    hg = 128 // dh
    G = D // 128
    scale = dh ** -0.5
    tm = _row_tile(S, 256)
    tq_f = _row_tile(S, FLASH_FWD_TILE[0])
    tkc_f = _row_tile(tq_f, FLASH_FWD_TILE[1])
    tk_b = _row_tile(S, FLASH_BWD_TILE[0])
    tqc_b = _row_tile(tk_b, FLASH_BWD_TILE[1])
    ts = _row_tile(S, 2048)
    xs = x[0]
    tgt = loss_target[0]
    ps = p[:, 0]
    row = lambda a: a.reshape(1, -1)

    big_names = list(SHARD_AXIS_BIG)
    small_names = list(SHARD_AXIS_SMALL)
    big = _full_from_gathered(
        all_gather(_pack([W[n] for n in big_names], 16, MXU_DTYPE), "weights_all_gather"),
        [W[n].shape for n in big_names], [SHARD_AXIS_BIG[n] for n in big_names])
    small = _full_from_gathered(
        all_gather(_pack([W[n] for n in small_names], 8), "vectors_all_gather"),
        [W[n].shape for n in small_names], [SHARD_AXIS_SMALL[n] for n in small_names])
    FW = dict(zip(big_names + small_names, big + small))
    wk, wv, wf = FW["w_kvf"][:, :D], FW["w_kvf"][:, D:2 * D], FW["w_kvf"][:, 2 * D:]

    saved = []
    h = xs
    kv = None
    for i in range(DEPTH):
        rec = {"h_in": h}
        if i < NA:
            h, rec["n"], rec["u"], rec["z"], rec["sw"] = conv_fwd(
                h, row(mix_norm[i]), FW["conv_w_pw1"][i], row(FW["conv_b_pw1"][i]), FW["conv_w_dw"][i],
                row(FW["conv_b_dw"][i]), row(FW["conv_ln_g"][i]), row(FW["conv_ln_b"][i]),
                FW["conv_w_pw2"][i], row(FW["conv_b_pw2"][i]), tm)
        else:
            j = i - NA
            if j == 0:
                k_, kT_, vT_, nkv, fl, c = kv_fwd(h, row(kv_norm), wk, wv, wf, row(b_f), tm)
                cg = c.reshape(S, G, hg)
                kv = dict(k=k_, kT=kT_, vT=vT_, n=nkv, fl=fl, h=h, c_col=jnp.transpose(cg, (1, 0, 2)),
                          c_row=jnp.transpose(cg, (1, 2, 0)))
            rec["n"], rec["q"], rec["qT"] = q_fwd(h, row(mix_norm[i]), FW["attn_w_q"][j], scale, tm)
            rec["o"], rec["o32"], rec["lse"] = flash_fwd(rec["qT"], kv["k"], kv["vT"], kv["c_col"], kv["c_row"], dh,
                                                         tq_f, tkc_f)
            h = attn_out_fwd(h, rec["o"], FW["attn_w_o"][j], tm)
        rec["h_ffn"] = h
        h, rec["n_ffn"], rec["a"], rec["s"] = ffn_fwd(h, row(ffn_norm[i]), FW["ffn_w1"][i], FW["ffn_w2"][i], tm)
        rec["h_ple"] = h
        h, rec["n_ple"], rec["gate"] = ple_fwd(h, row(ple_norm[i]), FW["ple_w_gate"][i], ps[i],
                                               FW["ple_w_proj"][i], tm)
        saved.append(rec)

    d, g_final, loss_part = loss_head(h, row(final_norm), tgt, tm)
    GW = {n: [None] * W[n].shape[0] for n in WEIGHTS if W[n].ndim > 1 and n != "w_kvf"}
    dks, dvs, dcs = [], [], []
    for i in reversed(range(DEPTH)):
        rec = saved[i]
        d_out = d
        d, dz, dpp, GW["ple_norm"][i] = ple_bwd(d_out, rec["h_ple"], row(ple_norm[i]), FW["ple_w_gate"][i],
                                                rec["gate"], ps[i], FW["ple_w_proj"][i], tm)
        GW["ple_w_gate"][i] = weight_grad(rec["n_ple"], dz, ts, "grad_ple_w_gate")
        GW["ple_w_proj"][i] = weight_grad(ps[i], dpp, ts, "grad_ple_w_proj", column_shards=N_DEV)
        d_out = d
        d, da, GW["ffn_norm"][i] = ffn_bwd(d_out, rec["h_ffn"], row(ffn_norm[i]), FW["ffn_w1"][i],
                                           FW["ffn_w2"][i], rec["a"], tm)
        GW["ffn_w2"][i] = weight_grad(rec["s"], d_out, ts, "grad_ffn_w2")
        GW["ffn_w1"][i] = weight_grad(rec["n_ffn"], da, ts, "grad_ffn_w1", column_shards=N_DEV)
        d_out = d
        if i >= NA:
            j = i - NA
            GW["attn_w_o"][j] = weight_grad(rec["o"], d_out, ts, "grad_attn_w_o")
            do, doT, delta = attn_out_bwd(d_out, FW["attn_w_o"][j], rec["o32"], dh, tm)
            qstat = jnp.concatenate([kv["c_col"], jnp.transpose(rec["lse"], (0, 2, 1)),
                                     jnp.transpose(delta.reshape(S, G, hg), (1, 0, 2))], axis=2)
            dq, dkT, dvT, dck, dcq = flash_bwd(rec["q"], rec["qT"], kv["k"], kv["kT"], kv["vT"], kv["c_row"], do, doT,
                                          qstat, dh, tk_b, tqc_b)
            dks.append(dkT)
            dvs.append(dvT)
            dcs.append(jnp.transpose(dck[:, :hg, :], (2, 0, 1)).reshape(S, H))
            dcs.append(jnp.transpose(dcq, (1, 0, 2)).reshape(S, H))
            d, dqs, GW["mix_norm"][i] = q_bwd(d_out, dq, rec["h_in"], row(mix_norm[i]), FW["attn_w_q"][j], scale, tm)
            GW["attn_w_q"][j] = weight_grad(rec["n"], dqs, ts, "grad_attn_w_q")
            if j == 0:
                d, dk_sum, dv_sum, dfl, g_kv_norm, g_b_f = kv_bwd(d, dks, dvs, dcs, kv["fl"], kv["h"],
                                                                  row(kv_norm), wk, wv, wf, tm)
                g_w_kvf = jnp.concatenate([weight_grad(kv["n"], dk_sum, ts, "grad_w_k"),
                                           weight_grad(kv["n"], dv_sum, ts, "grad_w_v"),
                                           weight_grad(kv["n"], dfl, ts, "grad_w_f")], axis=1)
        else:
            GW["conv_w_pw2"][i] = weight_grad(rec["sw"], d_out, ts, "grad_conv_w_pw2")
            (d, du, GW["conv_b_pw2"][i], GW["conv_ln_g"][i], GW["conv_ln_b"][i], GW["conv_b_dw"][i],
             GW["conv_w_dw"][i], GW["conv_b_pw1"][i], GW["mix_norm"][i]) = conv_bwd(
                d_out, rec["h_in"], row(mix_norm[i]), FW["conv_w_pw1"][i], FW["conv_w_dw"][i],
                row(FW["conv_ln_g"][i]), row(FW["conv_ln_b"][i]), FW["conv_w_pw2"][i], rec["u"], rec["z"], tm)
            GW["conv_w_pw1"][i] = weight_grad(rec["n"], du, ts, "grad_conv_w_pw1", column_shards=N_DEV)
    grad_x = d[None]

    sharded_names = big_names + small_names
    GW["w_kvf"] = [g_w_kvf]

    def device_major(n, g):
        width = W[n].shape[-1]
        if g.ndim == 3:
            return g
        if g.shape[-1] == width:
            return g.reshape(N_DEV, -1, width)
        return jnp.transpose(g.reshape(-1, N_DEV, width), (1, 0, 2))

    widths = sorted({W[n].shape[-1] for n in sharded_names}, reverse=True)
    groups = [[n for n in sharded_names if W[n].shape[-1] == width] for width in widths]

    def stack_rows(pieces, axis):
        rows = sum(p.shape[axis] for p in pieces)
        fill = -rows % (PACK_ROW_TILE if rows > PACK_ROW_TILE else 8)
        if fill:
            shape = list(pieces[0].shape)
            shape[axis] = fill
            pieces = pieces + [jnp.zeros(shape, pieces[0].dtype)]
        return jnp.concatenate(pieces, axis=axis)

    chunks = [stack_rows([device_major(n, g) for n in names for g in GW[n]], 1) for names in groups]
    core = lax.axis_index("c").astype(jnp.int32).reshape(1)
    landed = sibling_exchange(chunks)
    parts = chip_exchange([chip_partial(g, l, core) for g, l in zip(chunks, landed)])

    res = {}
    for names, width, part in zip(groups, widths, parts):
        group_rows = lambda src: stack_rows([src[n].reshape(-1, width) for n in names], 0)
        outs = adamw_sharded(part, group_rows(W), group_rows(M), group_rows(V))
        for kind, packed in zip(("grad", "delta", "new_m", "new_v"), outs):
            r0 = 0
            for n in names:
                nr = W[n].size // width
                res[kind, n] = packed[r0:r0 + nr].reshape(W[n].shape)
                r0 += nr

    rep_grads = {"mix_norm": jnp.concatenate(GW["mix_norm"], axis=0), "kv_norm": g_kv_norm,
                 "b_f": g_b_f, "ffn_norm": jnp.concatenate(GW["ffn_norm"], axis=0),
                 "ple_norm": jnp.concatenate(GW["ple_norm"], axis=0), "final_norm": g_final}

    def pack_rep(src, extra=None):
        rows_ = [jnp.pad(src[n].reshape(-1, src[n].shape[-1]), ((0, 0), (0, D - src[n].shape[-1])))
                 for n in REPLICATED]
        if extra is not None:
            rows_.append(jnp.pad(extra, ((0, 0), (0, D - extra.shape[-1]))))
        else:
            rows_.append(jnp.zeros((1, D), F32))
        flat = jnp.concatenate(rows_, axis=0)
        return jnp.pad(flat, ((0, -flat.shape[0] % 8), (0, 0)))

    rep_g = all_gather(pack_rep(rep_grads, loss_part), "replicated_all_gather")
    outs_rep = adamw_replicated(rep_g, pack_rep(W), pack_rep(M), pack_rep(V))
    n_rep_rows = sum(int(np.prod(W[n].shape[:-1])) for n in REPLICATED)
    for kind, packed in zip(("grad", "delta", "new_m", "new_v"), outs_rep):
        r0 = 0
        for n in REPLICATED:
            nr = int(np.prod(W[n].shape[:-1]))
            res[kind, n] = packed[r0:r0 + nr, :W[n].shape[-1]].reshape(W[n].shape)
            r0 += nr
    loss = outs_rep[0][n_rep_rows, 0]

    return (loss, grad_x, *[res["grad", n] for n in WEIGHTS], *[res["delta", n] for n in WEIGHTS],
            *[res["new_m", n] for n in WEIGHTS], *[res["new_v", n] for n in WEIGHTS])
```

```python
import numpy as np
import jax
import jax.numpy as jnp
from jax import lax
from jax.experimental import pallas as pl
from jax.experimental.pallas import tpu as pltpu

F32 = jnp.float32
MXU_DTYPE = jnp.bfloat16
ACT_DTYPE = jnp.bfloat16
WIRE_DTYPE = jnp.bfloat16

N_DEV = 8
N_CHIP = 4
EPS = 1e-6
NEG_BIG = -1e30
ADAM_LR = 0.001
ADAM_B1 = 0.9
ADAM_B2 = 0.999
ADAM_EPS = 1e-08
ADAM_WD = 0.01
ADAM_STEP = 10

VMEM_LIMIT_BYTES = 56 * 1024 * 1024
PACK_COLS = 1024
PACK_ROW_TILE = 256
FLASH_FWD_TILE = (1024, 512)
FLASH_BWD_TILE = (1024, 512)
FLASH_TILE = 128
HALO = 32
MESH = pl.DeviceIdType.MESH

SHARD_AXIS_BIG = {"conv_w_pw1": 2, "conv_w_pw2": 1, "w_kvf": 1, "attn_w_q": 1, "attn_w_o": 1,
                  "ffn_w1": 2, "ffn_w2": 1, "ple_w_gate": 1, "ple_w_proj": 2}
SHARD_AXIS_SMALL = {"conv_b_pw1": 1, "conv_w_dw": 2, "conv_b_dw": 1, "conv_ln_g": 1, "conv_ln_b": 1,
                    "conv_b_pw2": 1}
REPLICATED = ["mix_norm", "kv_norm", "b_f", "ffn_norm", "ple_norm", "final_norm"]
WEIGHTS = ["mix_norm", "conv_w_pw1", "conv_b_pw1", "conv_w_dw", "conv_b_dw", "conv_ln_g", "conv_ln_b",
           "conv_w_pw2", "conv_b_pw2", "kv_norm", "w_kvf", "b_f", "attn_w_q", "attn_w_o", "ffn_norm",
           "ffn_w1", "ffn_w2", "ple_norm", "ple_w_gate", "ple_w_proj", "final_norm"]


def _mm(a, b):
    return jnp.dot(a.astype(MXU_DTYPE), b.astype(MXU_DTYPE), preferred_element_type=F32)


def _mm_nt(a, b):
    return lax.dot_general(a.astype(MXU_DTYPE), b.astype(MXU_DTYPE), (((1,), (1,)), ((), ())),
                           preferred_element_type=F32)


def _mm_tn(a, b):
    return lax.dot_general(a.astype(MXU_DTYPE), b.astype(MXU_DTYPE), (((0,), (0,)), ((), ())),
                           preferred_element_type=F32)


def _split3(x):
    hi = x.astype(MXU_DTYPE)
    r1 = x - hi.astype(F32)
    mid = r1.astype(MXU_DTYPE)
    lo = (r1 - mid.astype(F32)).astype(MXU_DTYPE)
    return hi, mid, lo


def _tri_mm(tri, x):
    hi, mid, lo = _split3(x)
    return (jnp.dot(tri, lo, preferred_element_type=F32) + jnp.dot(tri, mid, preferred_element_type=F32)
            + jnp.dot(tri, hi, preferred_element_type=F32))


def _colsum8(x):
    tm, n = x.shape
    return jnp.sum(x.reshape(tm // 8, 8, n), axis=0)


def _rms(x, g):
    r = lax.rsqrt(jnp.mean(x * x, axis=-1, keepdims=True) + EPS)
    return x * r * g, r


def _rms_bwd(x, r, g, dn):
    w = dn * g
    dx = r * w - x * (r * r * r) * jnp.mean(w * x, axis=-1, keepdims=True)
    return dx, dn * x * r


def _sigmoid(x):
    return jax.nn.sigmoid(x)


def _params(n_grid):
    return pltpu.CompilerParams(dimension_semantics=("arbitrary",) * n_grid, vmem_limit_bytes=VMEM_LIMIT_BYTES)


def _rows(tm, n):
    return pl.BlockSpec((tm, n), lambda i: (i, 0))


def _rows_rev(tm, n, nt):
    return pl.BlockSpec((tm, n), lambda i: (nt - 1 - i, 0))


def _cols(n, tm):
    return pl.BlockSpec((n, tm), lambda i: (0, i))


def _cols_rev(n, tm, nt):
    return pl.BlockSpec((n, tm), lambda i: (0, nt - 1 - i))


def _whole(shape):
    nd = len(shape)
    return pl.BlockSpec(shape, lambda i: (0,) * nd)


def _row_tile(s, want):
    tm = min(s, want)
    assert s % tm == 0 and tm % 8 == 0, (s, tm)
    return tm


def conv_fwd(h, g, w1, b1, wd, bd, lg, lb, w2, b2, tm):
    S, D = h.shape
    CW = wd.shape[0]
    off = HALO - (CW - 1)
    assert 0 <= off and tm >= HALO
    nt = S // tm

    def body(h_ref, g_ref, w1_ref, b1_ref, wd_ref, bd_ref, lg_ref, lb_ref, w2_ref, b2_ref,
             ho_ref, n_ref, u_ref, z_ref, sw_ref, ext, win):
        @pl.when(pl.program_id(0) == 0)
        def _():
            ext[0:HALO, :] = jnp.zeros((HALO, D), F32)

        x = h_ref[...]
        n, _ = _rms(x, g_ref[...])
        n_ref[...] = n.astype(n_ref.dtype)
        u = _mm(n, w1_ref[...]) + b1_ref[...]
        u_ref[...] = u
        ext[HALO:HALO + tm, :] = u[:, :D] * _sigmoid(u[:, D:])
        z = jnp.broadcast_to(bd_ref[...], (tm, D))
        for b in range(8):
            amax = (CW - 1 - b) // 8
            win[0:tm + 8 * amax, :] = ext[off + b:off + b + tm + 8 * amax, :]
            for a8 in range(amax + 1):
                z = z + wd_ref[8 * a8 + b:8 * a8 + b + 1, :] * win[8 * a8:8 * a8 + tm, :]
        z_ref[...] = z
        ext[0:HALO, :] = ext[tm:tm + HALO, :]
        mu = jnp.mean(z, axis=-1, keepdims=True)
        zc = z - mu
        y = zc * lax.rsqrt(jnp.mean(zc * zc, axis=-1, keepdims=True) + EPS) * lg_ref[...] + lb_ref[...]
        sw = y * _sigmoid(y)
        sw_ref[...] = sw.astype(sw_ref.dtype)
        ho_ref[...] = x + _mm(sw, w2_ref[...]) + b2_ref[...]

    return pl.pallas_call(
        body, name="conv_fwd", grid=(nt,),
        in_specs=[_rows(tm, D), _whole((1, D)), _whole(w1.shape), _whole((1, 2 * D)), _whole(wd.shape),
                  _whole((1, D)), _whole((1, D)), _whole((1, D)), _whole(w2.shape), _whole((1, D))],
        out_specs=[_rows(tm, D), _rows(tm, D), _rows(tm, 2 * D), _rows(tm, D), _rows(tm, D)],
        out_shape=[jax.ShapeDtypeStruct((S, D), F32), jax.ShapeDtypeStruct((S, D), ACT_DTYPE),
                   jax.ShapeDtypeStruct((S, 2 * D), F32), jax.ShapeDtypeStruct((S, D), F32),
                   jax.ShapeDtypeStruct((S, D), ACT_DTYPE)],
        scratch_shapes=[pltpu.VMEM((HALO + tm, D), F32), pltpu.VMEM((HALO + tm, D), F32)],
        compiler_params=_params(1),
    )(h, g, w1, b1, wd, bd, lg, lb, w2, b2)


def ffn_fwd(h, g, w1, w2, tm):
    S, D = h.shape
    FF = w1.shape[1]

    def body(h_ref, g_ref, w1_ref, w2_ref, ho_ref, n_ref, a_ref, s_ref):
        x = h_ref[...]
        n, _ = _rms(x, g_ref[...])
        n_ref[...] = n.astype(n_ref.dtype)
        a = _mm(n, w1_ref[...])
        a_ref[...] = a
        s = jnp.square(jnp.maximum(a, 0.0))
        s_ref[...] = s.astype(s_ref.dtype)
        ho_ref[...] = x + _mm(s, w2_ref[...])

    return pl.pallas_call(
        body, name="ffn_fwd", grid=(S // tm,),
        in_specs=[_rows(tm, D), _whole((1, D)), _whole(w1.shape), _whole(w2.shape)],
        out_specs=[_rows(tm, D), _rows(tm, D), _rows(tm, FF), _rows(tm, FF)],
        out_shape=[jax.ShapeDtypeStruct((S, D), F32), jax.ShapeDtypeStruct((S, D), ACT_DTYPE),
                   jax.ShapeDtypeStruct((S, FF), F32), jax.ShapeDtypeStruct((S, FF), ACT_DTYPE)],
        compiler_params=_params(1),
    )(h, g, w1, w2)


def ple_fwd(h, g, wg, p, wp, tm):
    S, D = h.shape
    E = p.shape[1]

    def body(h_ref, g_ref, wg_ref, p_ref, wp_ref, ho_ref, n_ref, gate_ref):
        x = h_ref[...]
        n, _ = _rms(x, g_ref[...])
        n_ref[...] = n.astype(n_ref.dtype)
        gate = _sigmoid(_mm(n, wg_ref[...]))
        gate_ref[...] = gate
        ho_ref[...] = x + gate * _mm(p_ref[...], wp_ref[...])

    return pl.pallas_call(
        body, name="ple_fwd", grid=(S // tm,),
        in_specs=[_rows(tm, D), _whole((1, D)), _whole(wg.shape), _rows(tm, E), _whole(wp.shape)],
        out_specs=[_rows(tm, D), _rows(tm, D), _rows(tm, D)],
        out_shape=[jax.ShapeDtypeStruct((S, D), F32), jax.ShapeDtypeStruct((S, D), ACT_DTYPE),
                   jax.ShapeDtypeStruct((S, D), F32)],
        compiler_params=_params(1),
    )(h, g, wg, p, wp)


def kv_fwd(h, g, wk, wv, wf, bf, tm):
    S, D = h.shape
    H = wf.shape[1]

    def body(h_ref, g_ref, wk_ref, wv_ref, wf_ref, bf_ref, k_ref, kT_ref, vT_ref, n_ref, fl_ref, c_ref, carry):
        @pl.when(pl.program_id(0) == 0)
        def _():
            carry[...] = jnp.zeros_like(carry)

        n, _ = _rms(h_ref[...], g_ref[...])
        n_ref[...] = n.astype(n_ref.dtype)
        k = _mm(n, wk_ref[...])
        k_ref[...] = k.astype(k_ref.dtype)
        kT_ref[...] = k.T.astype(kT_ref.dtype)
        vT_ref[...] = _mm(n, wv_ref[...]).T.astype(vT_ref.dtype)
        fl = _mm(n, wf_ref[...]) + bf_ref[...]
        fl_ref[...] = fl
        logf = jnp.minimum(fl, 0.0) - jnp.log1p(jnp.exp(-jnp.abs(fl)))
        row = lax.broadcasted_iota(jnp.int32, (tm, tm), 0)
        col = lax.broadcasted_iota(jnp.int32, (tm, tm), 1)
        tri = (row >= col).astype(MXU_DTYPE)
        c = _tri_mm(tri, logf) + carry[...]
        c_ref[...] = c
        carry[...] = c[tm - 1:tm, :]

    return pl.pallas_call(
        body, name="kv_fwd", grid=(S // tm,),
        in_specs=[_rows(tm, D), _whole((1, D)), _whole(wk.shape), _whole(wv.shape), _whole(wf.shape),
                  _whole((1, H))],
        out_specs=[_rows(tm, D), _cols(D, tm), _cols(D, tm), _rows(tm, D), _rows(tm, H), _rows(tm, H)],
        out_shape=[jax.ShapeDtypeStruct((S, D), ACT_DTYPE), jax.ShapeDtypeStruct((D, S), ACT_DTYPE),
                   jax.ShapeDtypeStruct((D, S), ACT_DTYPE), jax.ShapeDtypeStruct((S, D), ACT_DTYPE),
                   jax.ShapeDtypeStruct((S, H), F32), jax.ShapeDtypeStruct((S, H), F32)],
        scratch_shapes=[pltpu.VMEM((1, H), F32)],
        compiler_params=_params(1),
    )(h, g, wk, wv, wf, bf)


def q_fwd(h, g, wq, scale, tm):
    S, D = h.shape

    def body(h_ref, g_ref, wq_ref, n_ref, q_ref, qT_ref):
        n, _ = _rms(h_ref[...], g_ref[...])
        n_ref[...] = n.astype(n_ref.dtype)
        q = _mm(n, wq_ref[...]) * scale
        q_ref[...] = q.astype(q_ref.dtype)
        qT_ref[...] = q.T.astype(qT_ref.dtype)

    return pl.pallas_call(
        body, name="q_fwd", grid=(S // tm,),
        in_specs=[_rows(tm, D), _whole((1, D)), _whole(wq.shape)],
        out_specs=[_rows(tm, D), _rows(tm, D), _cols(D, tm)],
        out_shape=[jax.ShapeDtypeStruct((S, D), ACT_DTYPE), jax.ShapeDtypeStruct((S, D), ACT_DTYPE),
                   jax.ShapeDtypeStruct((D, S), ACT_DTYPE)],
        compiler_params=_params(1),
    )(h, g, wq)


def attn_out_fwd(h, o, wo, tm):
    S, D = h.shape

    def body(h_ref, o_ref, wo_ref, ho_ref):
        ho_ref[...] = h_ref[...] + _mm(o_ref[...], wo_ref[...])

    return pl.pallas_call(
        body, name="attn_out_fwd", grid=(S // tm,),
        in_specs=[_rows(tm, D), _rows(tm, D), _whole(wo.shape)],
        out_specs=_rows(tm, D),
        out_shape=jax.ShapeDtypeStruct((S, D), F32),
        compiler_params=_params(1),
    )(h, o, wo)


def _causal_mask(key0, qry0, shape, key_axis):
    key = key0 + lax.broadcasted_iota(jnp.int32, shape, key_axis)
    qry = qry0 + lax.broadcasted_iota(jnp.int32, shape, 1 - key_axis)
    return key <= qry


def flash_fwd(qT, k, vT, c_col, c_row, dh, tq, tkc):
    D, S = qT.shape
    hg = 128 // dh
    G = D // 128
    per = tq // tkc
    assert tq % tkc == 0 and S % tq == 0

    def body(qT_ref, k_ref, vT_ref, ccol_ref, crow_ref, o_ref, o32_ref, lse_ref, m_scr, l_scr, acc_scr, mx_scr,
             *scratch):
        i = pl.program_id(1)
        m_scr[...] = jnp.full(m_scr.shape, NEG_BIG, F32)
        l_scr[...] = jnp.zeros(l_scr.shape, F32)
        acc_scr[...] = jnp.zeros(acc_scr.shape, F32)

        def head_shift(j, hh):
            c0 = ccol_ref[pl.ds(j * tkc, 1), hh:hh + 1]
            return c0, crow_ref[hh:hh + 1, :] - c0

        def scores(j, masked, slot, q0):
            keys = pl.ds(pl.multiple_of(j * tkc, tkc), tkc)
            s_scr = scratch[slot]
            for hh in range(hg):
                lanes = slice(hh * dh, (hh + 1) * dh)
                c0, r = head_shift(j, hh)
                s = _mm(k_ref[keys, lanes], qT_ref[lanes, q0:tq]) - (ccol_ref[keys, hh:hh + 1] - c0)
                if masked:
                    s = jnp.where(_causal_mask(j * tkc, i * tq + q0, (tkc, tq - q0), 0), s, NEG_BIG)
                s_scr[hh, :, q0:tq] = s
                mx_scr[slot * hg + hh, :, q0:tq] = jnp.max(s, axis=0, keepdims=True) + r[:, q0:tq]

        def update(j, slot, q0):
            keys = pl.ds(pl.multiple_of(j * tkc, tkc), tkc)
            s_scr = scratch[slot]
            for hh in range(hg):
                lanes = slice(hh * dh, (hh + 1) * dh)
                _, r = head_shift(j, hh)
                m_old = m_scr[hh, :, q0:tq]
                m_new = jnp.maximum(m_old, mx_scr[slot * hg + hh, :, q0:tq])
                alpha = jnp.exp(m_old - m_new)
                p = jnp.exp(s_scr[hh, :, q0:tq] - (m_new - r[:, q0:tq]))
                l_scr[hh, :, q0:tq] = alpha * l_scr[hh, :, q0:tq] + jnp.sum(p, axis=0, keepdims=True)
                acc_scr[lanes, q0:tq] = alpha * acc_scr[lanes, q0:tq] + _mm(vT_ref[lanes, keys], p)
                m_scr[hh, :, q0:tq] = m_new

        def chunks(j0, masked):
            for jj in range(per):
                scores(j0 + jj, masked, jj, jj * tkc if masked else 0)
            for jj in range(per):
                update(j0 + jj, jj, jj * tkc if masked else 0)

        def full_chunks(jb, carry):
            chunks(jb * per, False)
            return carry

        lax.fori_loop(0, i, full_chunks, 0)
        chunks(i * per, True)
        for hh in range(hg):
            lanes = slice(hh * dh, (hh + 1) * dh)
            acc_scr[lanes, :] = acc_scr[lanes, :] / l_scr[hh]
            lse_ref[hh:hh + 1, :] = m_scr[hh] + jnp.log(l_scr[hh])
        o = acc_scr[...].T
        o_ref[...] = o.astype(o_ref.dtype)
        o32_ref[...] = o

    return pl.pallas_call(
        body, name="flash_fwd", grid=(G, S // tq),
        in_specs=[pl.BlockSpec((128, tq), lambda g, i: (g, i)),
                  pl.BlockSpec((S, 128), lambda g, i: (0, g)),
                  pl.BlockSpec((128, S), lambda g, i: (g, 0)),
                  pl.BlockSpec((None, S, hg), lambda g, i: (g, 0, 0)),
                  pl.BlockSpec((None, hg, tq), lambda g, i: (g, 0, i))],
        out_specs=[pl.BlockSpec((tq, 128), lambda g, i: (i, g)),
                   pl.BlockSpec((tq, 128), lambda g, i: (i, g)),
                   pl.BlockSpec((None, hg, tq), lambda g, i: (g, 0, i))],
        out_shape=[jax.ShapeDtypeStruct((S, D), ACT_DTYPE), jax.ShapeDtypeStruct((S, D), F32),
                   jax.ShapeDtypeStruct((G, hg, S), F32)],
        scratch_shapes=([pltpu.VMEM((hg, 1, tq), F32), pltpu.VMEM((hg, 1, tq), F32), pltpu.VMEM((128, tq), F32),
                         pltpu.VMEM((per * hg, 1, tq), F32)]
                        + [pltpu.VMEM((hg, tkc, tq), F32)] * per),
        compiler_params=_params(2),
    )(qT, k, vT, c_col, c_row)


def loss_head(h, g, target, tm):
    S, D = h.shape
    nt = S // tm

    def body(h_ref, g_ref, t_ref, dh_ref, dg_ref, loss_ref, dg_acc, loss_acc):
        i = pl.program_id(0)

        @pl.when(i == 0)
        def _():
            dg_acc[...] = jnp.zeros_like(dg_acc)
            loss_acc[...] = jnp.zeros_like(loss_acc)

        x = h_ref[...]
        gg = g_ref[...]
        y, r = _rms(x, gg)
        e = y - t_ref[...]
        loss_acc[...] += 0.5 * jnp.sum(jnp.mean(e * e, axis=-1, keepdims=True), axis=0, keepdims=True)
        dx, dgr = _rms_bwd(x, r, gg, e / D)
        dh_ref[...] = dx
        dg_acc[...] += _colsum8(dgr)

        @pl.when(i == nt - 1)
        def _():
            dg_ref[...] = jnp.sum(dg_acc[...], axis=0, keepdims=True)
            loss_ref[...] = jnp.broadcast_to(loss_acc[...], loss_ref.shape)

    return pl.pallas_call(
        body, name="loss_head", grid=(nt,),
        in_specs=[_rows(tm, D), _whole((1, D)), _rows(tm, D)],
        out_specs=[_rows(tm, D), _whole((1, D)), _whole((1, 128))],
        out_shape=[jax.ShapeDtypeStruct((S, D), F32), jax.ShapeDtypeStruct((1, D), F32),
                   jax.ShapeDtypeStruct((1, 128), F32)],
        scratch_shapes=[pltpu.VMEM((8, D), F32), pltpu.VMEM((1, 1), F32)],
        compiler_params=_params(1),
    )(h, g, target)


def ple_bwd(d, h, g, wg, gate, p, wp, tm):
    S, D = h.shape
    E = p.shape[1]
    nt = S // tm

    def body(d_ref, h_ref, g_ref, wg_ref, gate_ref, p_ref, wp_ref, di_ref, dz_ref, dpp_ref, dg_ref, dg_acc):
        i = pl.program_id(0)

        @pl.when(i == 0)
        def _():
            dg_acc[...] = jnp.zeros_like(dg_acc)

        dd = d_ref[...]
        x = h_ref[...]
        gg = g_ref[...]
        gt = gate_ref[...]
        pp = _mm(p_ref[...], wp_ref[...])
        dpp_ref[...] = (dd * gt).astype(dpp_ref.dtype)
        dz = dd * pp * gt * (1.0 - gt)
        dz_ref[...] = dz.astype(dz_ref.dtype)
        r = lax.rsqrt(jnp.mean(x * x, axis=-1, keepdims=True) + EPS)
        dx, dgr = _rms_bwd(x, r, gg, _mm_nt(dz, wg_ref[...]))
        di_ref[...] = dd + dx
        dg_acc[...] += _colsum8(dgr)

        @pl.when(i == nt - 1)
        def _():
            dg_ref[...] = jnp.sum(dg_acc[...], axis=0, keepdims=True)

    return pl.pallas_call(
        body, name="ple_bwd", grid=(nt,),
        in_specs=[_rows(tm, D), _rows(tm, D), _whole((1, D)), _whole(wg.shape), _rows(tm, D), _rows(tm, E),
                  _whole(wp.shape)],
        out_specs=[_rows(tm, D), _rows(tm, D), _rows(tm, D), _whole((1, D))],
        out_shape=[jax.ShapeDtypeStruct((S, D), F32), jax.ShapeDtypeStruct((S, D), ACT_DTYPE),
                   jax.ShapeDtypeStruct((S, D), ACT_DTYPE), jax.ShapeDtypeStruct((1, D), F32)],
        scratch_shapes=[pltpu.VMEM((8, D), F32)],
        compiler_params=_params(1),
    )(d, h, g, wg, gate, p, wp)


def ffn_bwd(d, h, g, w1, w2, a, tm):
    S, D = h.shape
    FF = w1.shape[1]
    nt = S // tm

    def body(d_ref, h_ref, g_ref, w1_ref, w2_ref, a_ref, di_ref, da_ref, dg_ref, dg_acc):
        i = pl.program_id(0)

        @pl.when(i == 0)
        def _():
            dg_acc[...] = jnp.zeros_like(dg_acc)

        dd = d_ref[...]
        x = h_ref[...]
        da = _mm_nt(dd, w2_ref[...]) * (2.0 * jnp.maximum(a_ref[...], 0.0))
        da_ref[...] = da.astype(da_ref.dtype)
        r = lax.rsqrt(jnp.mean(x * x, axis=-1, keepdims=True) + EPS)
        dx, dgr = _rms_bwd(x, r, g_ref[...], _mm_nt(da, w1_ref[...]))
        di_ref[...] = dd + dx
        dg_acc[...] += _colsum8(dgr)

        @pl.when(i == nt - 1)
        def _():
            dg_ref[...] = jnp.sum(dg_acc[...], axis=0, keepdims=True)

    return pl.pallas_call(
        body, name="ffn_bwd", grid=(nt,),
        in_specs=[_rows(tm, D), _rows(tm, D), _whole((1, D)), _whole(w1.shape), _whole(w2.shape), _rows(tm, FF)],
        out_specs=[_rows(tm, D), _rows(tm, FF), _whole((1, D))],
        out_shape=[jax.ShapeDtypeStruct((S, D), F32), jax.ShapeDtypeStruct((S, FF), ACT_DTYPE),
                   jax.ShapeDtypeStruct((1, D), F32)],
        scratch_shapes=[pltpu.VMEM((8, D), F32)],
        compiler_params=_params(1),
    )(d, h, g, w1, w2, a)


def attn_out_bwd(d, wo, o32, dh, tm):
    S, D = d.shape
    H = D // dh

    def body(d_ref, wo_ref, o_ref, do_ref, doT_ref, delta_ref):
        do32 = _mm_nt(d_ref[...], wo_ref[...])
        do = do32.astype(do_ref.dtype)
        do_ref[...] = do
        doT_ref[...] = do32.T.astype(doT_ref.dtype)
        lane_head = lax.broadcasted_iota(jnp.int32, (D, H), 0) // dh
        seg = (lane_head == lax.broadcasted_iota(jnp.int32, (D, H), 1)).astype(MXU_DTYPE)
        hi, mid, lo = _split3(do.astype(F32) * o_ref[...])
        delta_ref[...] = (jnp.dot(lo, seg, preferred_element_type=F32) + jnp.dot(mid, seg, preferred_element_type=F32)
                          + jnp.dot(hi, seg, preferred_element_type=F32))

    return pl.pallas_call(
        body, name="attn_out_bwd", grid=(S // tm,),
        in_specs=[_rows(tm, D), _whole(wo.shape), _rows(tm, D)],
        out_specs=[_rows(tm, D), _cols(D, tm), _rows(tm, H)],
        out_shape=[jax.ShapeDtypeStruct((S, D), ACT_DTYPE), jax.ShapeDtypeStruct((D, S), ACT_DTYPE),
                   jax.ShapeDtypeStruct((S, H), F32)],
        compiler_params=_params(1),
    )(d, wo, o32)


def flash_bwd(q, qT, k, kT, vT, c_row, do, doT, qstat, dh, tk, tqc):
    S, D = q.shape
    hg = 128 // dh
    G = D // 128
    per = tk // tqc
    nchunk = S // tqc
    assert hg <= 8 and tk % tqc == 0 and S % tk == 0

    def body(q_ref, qT_ref, k_ref, kT_ref, vT_ref, crow_ref, do_ref, doT_ref, st_ref,
             dq_ref, dkT_ref, dvT_ref, dck_ref, dcq_ref):
        ki = pl.program_id(1)

        @pl.when(ki == 0)
        def _():
            dq_ref[...] = jnp.zeros_like(dq_ref)
            dcq_ref[...] = jnp.zeros_like(dcq_ref)

        dck_ref[...] = jnp.zeros_like(dck_ref)
        dkT_ref[...] = jnp.zeros_like(dkT_ref)
        dvT_ref[...] = jnp.zeros_like(dvT_ref)
        def chunk(jq, masked, nk):
            rows = pl.ds(pl.multiple_of(jq * tqc, tqc), tqc)
            st = st_ref[rows, :]
            for hh in range(hg):
                lanes = slice(hh * dh, (hh + 1) * dh)
                ck = crow_ref[hh:hh + 1, 0:nk]
                c0 = ck[:, 0:1]
                u = (st[:, hh:hh + 1] - c0) - st[:, hg + hh:hg + hh + 1]
                s = (_mm(q_ref[rows, lanes], kT_ref[lanes, 0:nk]) - (ck - c0)) + u
                if masked:
                    s = jnp.where(_causal_mask(ki * tk, jq * tqc, (tqc, nk), 1), s, NEG_BIG)
                p = jnp.exp(s)
                dvT_ref[lanes, 0:nk] += _mm(doT_ref[lanes, rows], p)
                ds = p * (_mm(do_ref[rows, lanes], vT_ref[lanes, 0:nk]) - st[:, 2 * hg + hh:2 * hg + hh + 1])
                dkT_ref[lanes, 0:nk] += _mm(qT_ref[lanes, rows], ds)
                dck_ref[hh:hh + 1, 0:nk] -= jnp.sum(ds, axis=0, keepdims=True)
                dcq_ref[rows, hh:hh + 1] += jnp.sum(ds, axis=1, keepdims=True)
                dq_ref[rows, lanes] += _mm(ds, k_ref[0:nk, lanes])

        for jj in range(per):
            chunk(ki * per + jj, True, (jj + 1) * tqc)

        def full_chunk(jq, carry):
            chunk(jq, False, tk)
            return carry

        lax.fori_loop((ki + 1) * per, nchunk, full_chunk, 0)

    return pl.pallas_call(
        body, name="flash_bwd", grid=(G, S // tk),
        in_specs=[pl.BlockSpec((S, 128), lambda g, j: (0, g)),
                  pl.BlockSpec((128, S), lambda g, j: (g, 0)),
                  pl.BlockSpec((tk, 128), lambda g, j: (j, g)),
                  pl.BlockSpec((128, tk), lambda g, j: (g, j)),
                  pl.BlockSpec((128, tk), lambda g, j: (g, j)),
                  pl.BlockSpec((None, hg, tk), lambda g, j: (g, 0, j)),
                  pl.BlockSpec((S, 128), lambda g, j: (0, g)),
                  pl.BlockSpec((128, S), lambda g, j: (g, 0)),
                  pl.BlockSpec((None, S, 3 * hg), lambda g, j: (g, 0, 0))],
        out_specs=[pl.BlockSpec((S, 128), lambda g, j: (0, g)),
                   pl.BlockSpec((128, tk), lambda g, j: (g, j)),
                   pl.BlockSpec((128, tk), lambda g, j: (g, j)),
                   pl.BlockSpec((None, 8, tk), lambda g, j: (g, 0, j)),
                   pl.BlockSpec((None, S, hg), lambda g, j: (g, 0, 0))],
        out_shape=[jax.ShapeDtypeStruct((S, D), F32), jax.ShapeDtypeStruct((D, S), F32),
                   jax.ShapeDtypeStruct((D, S), F32), jax.ShapeDtypeStruct((G, 8, S), F32),
                   jax.ShapeDtypeStruct((G, S, hg), F32)],
        compiler_params=_params(2),
    )(q, qT, k, kT, vT, c_row, do, doT, qstat)


def q_bwd(d, dq, h, g, wq, scale, tm):
    S, D = h.shape
    nt = S // tm

    def body(d_ref, dq_ref, h_ref, g_ref, wq_ref, di_ref, dqs_ref, dg_ref, dg_acc):
        i = pl.program_id(0)

        @pl.when(i == 0)
        def _():
            dg_acc[...] = jnp.zeros_like(dg_acc)

        x = h_ref[...]
        dqs = dq_ref[...] * scale
        dqs_ref[...] = dqs.astype(dqs_ref.dtype)
        r = lax.rsqrt(jnp.mean(x * x, axis=-1, keepdims=True) + EPS)
        dx, dgr = _rms_bwd(x, r, g_ref[...], _mm_nt(dqs, wq_ref[...]))
        di_ref[...] = d_ref[...] + dx
        dg_acc[...] += _colsum8(dgr)

        @pl.when(i == nt - 1)
        def _():
            dg_ref[...] = jnp.sum(dg_acc[...], axis=0, keepdims=True)

    return pl.pallas_call(
        body, name="q_bwd", grid=(nt,),
        in_specs=[_rows(tm, D), _rows(tm, D), _rows(tm, D), _whole((1, D)), _whole(wq.shape)],
        out_specs=[_rows(tm, D), _rows(tm, D), _whole((1, D))],
        out_shape=[jax.ShapeDtypeStruct((S, D), F32), jax.ShapeDtypeStruct((S, D), ACT_DTYPE),
                   jax.ShapeDtypeStruct((1, D), F32)],
        scratch_shapes=[pltpu.VMEM((8, D), F32)],
        compiler_params=_params(1),
    )(d, dq, h, g, wq)


def kv_bwd(d, dks, dvs, dcs, fl, h, g, wk, wv, wf, tm):
    S, D = h.shape
    H = wf.shape[1]
    nt = S // tm
    nl = len(dks)
    nc = len(dcs)

    def body(*refs):
        d_ref = refs[0]
        dk_refs = refs[1:1 + nl]
        dv_refs = refs[1 + nl:1 + 2 * nl]
        dc_refs = refs[1 + 2 * nl:1 + 2 * nl + nc]
        (fl_ref, h_ref, g_ref, wk_ref, wv_ref, wf_ref,
         di_ref, dk_ref, dv_ref, dfl_ref, dg_ref, dbf_ref, dg_acc, dbf_acc, carry) = refs[1 + 2 * nl + nc:]
        i = pl.program_id(0)

        @pl.when(i == 0)
        def _():
            dg_acc[...] = jnp.zeros_like(dg_acc)
            dbf_acc[...] = jnp.zeros_like(dbf_acc)
            carry[...] = jnp.zeros_like(carry)

        dkT = dk_refs[0][...]
        dvT = dv_refs[0][...]
        for l in range(1, nl):
            dkT = dkT + dk_refs[l][...]
            dvT = dvT + dv_refs[l][...]
        dk = dkT.T
        dv = dvT.T
        dk_ref[...] = dk.astype(dk_ref.dtype)
        dv_ref[...] = dv.astype(dv_ref.dtype)
        row = lax.broadcasted_iota(jnp.int32, (tm, tm), 0)
        col = lax.broadcasted_iota(jnp.int32, (tm, tm), 1)
        tri = (col >= row).astype(MXU_DTYPE)
        dc = dc_refs[0][...]
        for l in range(1, nc):
            dc = dc + dc_refs[l][...]
        dlogf = _tri_mm(tri, dc) + carry[...]
        carry[...] = dlogf[0:1, :]
        dfl = dlogf * _sigmoid(-fl_ref[...])
        dfl_ref[...] = dfl
        dbf_acc[...] += jnp.sum(dfl, axis=0, keepdims=True)
        x = h_ref[...]
        dn = _mm_nt(dk, wk_ref[...]) + _mm_nt(dv, wv_ref[...]) + _mm_nt(dfl, wf_ref[...])
        r = lax.rsqrt(jnp.mean(x * x, axis=-1, keepdims=True) + EPS)
        dx, dgr = _rms_bwd(x, r, g_ref[...], dn)
        di_ref[...] = d_ref[...] + dx
        dg_acc[...] += _colsum8(dgr)

        @pl.when(i == nt - 1)
        def _():
            dg_ref[...] = jnp.sum(dg_acc[...], axis=0, keepdims=True)
            dbf_ref[...] = dbf_acc[...]

    rev = lambda n: _rows_rev(tm, n, nt)
    return pl.pallas_call(
        body, name="kv_bwd", grid=(nt,),
        in_specs=([rev(D)] + [_cols_rev(D, tm, nt)] * (2 * nl)
                  + [rev(H)] * nc
                  + [rev(H), rev(D), _whole((1, D)), _whole(wk.shape), _whole(wv.shape), _whole(wf.shape)]),
        out_specs=[rev(D), rev(D), rev(D), rev(H), _whole((1, D)), _whole((1, H))],
        out_shape=[jax.ShapeDtypeStruct((S, D), F32), jax.ShapeDtypeStruct((S, D), ACT_DTYPE),
                   jax.ShapeDtypeStruct((S, D), ACT_DTYPE), jax.ShapeDtypeStruct((S, H), F32),
                   jax.ShapeDtypeStruct((1, D), F32), jax.ShapeDtypeStruct((1, H), F32)],
        scratch_shapes=[pltpu.VMEM((8, D), F32), pltpu.VMEM((1, H), F32), pltpu.VMEM((1, H), F32)],
        compiler_params=_params(1),
    )(d, *dks, *dvs, *dcs, fl, h, g, wk, wv, wf)


def conv_bwd(d, h, g, w1, wd, lg, lb, w2, u, z, tm):
    S, D = h.shape
    CW = wd.shape[0]
    nt = S // tm
    assert tm >= HALO and CW - 1 <= HALO

    def body(d_ref, h_ref, g_ref, w1_ref, wd_ref, lg_ref, lb_ref, w2_ref, u_ref, z_ref,
             di_ref, du_ref, db2_ref, dlg_ref, dlb_ref, dbd_ref, dwd_ref, db1_ref, dg_ref,
             ext, win, db2_acc, dlg_acc, dlb_acc, dbd_acc, dwd_acc, db1_acc, dg_acc):
        i = pl.program_id(0)

        @pl.when(i == 0)
        def _():
            ext[tm:tm + HALO, :] = jnp.zeros((HALO, D), F32)
            for acc in (db2_acc, dlg_acc, dlb_acc, dbd_acc, dwd_acc, db1_acc, dg_acc):
                acc[...] = jnp.zeros_like(acc)

        dd = d_ref[...]
        db2_acc[...] += _colsum8(dd)
        dsw = _mm_nt(dd, w2_ref[...])
        zz = z_ref[...]
        zc = zz - jnp.mean(zz, axis=-1, keepdims=True)
        rs = lax.rsqrt(jnp.mean(zc * zc, axis=-1, keepdims=True) + EPS)
        xh = zc * rs
        lgv = lg_ref[...]
        y = xh * lgv + lb_ref[...]
        sg = _sigmoid(y)
        dy = dsw * (sg * (1.0 + y * (1.0 - sg)))
        dlg_acc[...] += _colsum8(dy * xh)
        dlb_acc[...] += _colsum8(dy)
        dxh = dy * lgv
        dz = rs * (dxh - jnp.mean(dxh, axis=-1, keepdims=True) - xh * jnp.mean(dxh * xh, axis=-1, keepdims=True))
        dbd_acc[...] += _colsum8(dz)
        ext[0:tm, :] = dz
        uu = u_ref[...]
        a = uu[:, :D]
        sgg = _sigmoid(uu[:, D:])
        glu = a * sgg
        dglu = jnp.zeros((tm, D), F32)
        for b in range(8):
            amax = (CW - 1 - b) // 8
            win[0:tm + 8 * amax, :] = ext[b:b + tm + 8 * amax, :]
            for a8 in range(amax + 1):
                k = CW - 1 - (8 * a8 + b)
                sh = win[8 * a8:8 * a8 + tm, :]
                dglu = dglu + wd_ref[k:k + 1, :] * sh
                dwd_acc[k] += _colsum8(glu * sh)
        ext[tm:tm + HALO, :] = ext[0:HALO, :]
        da = dglu * sgg
        dgg = dglu * a * sgg * (1.0 - sgg)
        du_ref[:, :D] = da.astype(du_ref.dtype)
        du_ref[:, D:] = dgg.astype(du_ref.dtype)
        db1_acc[:, :D] += _colsum8(da)
        db1_acc[:, D:] += _colsum8(dgg)
        dn = _mm_nt(da, w1_ref[:, :D]) + _mm_nt(dgg, w1_ref[:, D:])
        x = h_ref[...]
        r = lax.rsqrt(jnp.mean(x * x, axis=-1, keepdims=True) + EPS)
        dx, dgr = _rms_bwd(x, r, g_ref[...], dn)
        di_ref[...] = dd + dx
        dg_acc[...] += _colsum8(dgr)

        @pl.when(i == nt - 1)
        def _():
            db2_ref[...] = jnp.sum(db2_acc[...], axis=0, keepdims=True)
            dlg_ref[...] = jnp.sum(dlg_acc[...], axis=0, keepdims=True)
            dlb_ref[...] = jnp.sum(dlb_acc[...], axis=0, keepdims=True)
            dbd_ref[...] = jnp.sum(dbd_acc[...], axis=0, keepdims=True)
            dwd_ref[...] = jnp.sum(dwd_acc[...], axis=1)
            db1_ref[...] = jnp.sum(db1_acc[...], axis=0, keepdims=True)
            dg_ref[...] = jnp.sum(dg_acc[...], axis=0, keepdims=True)

    rev = lambda n: _rows_rev(tm, n, nt)
    vec = jax.ShapeDtypeStruct((1, D), F32)
    return pl.pallas_call(
        body, name="conv_bwd", grid=(nt,),
        in_specs=[rev(D), rev(D), _whole((1, D)), _whole(w1.shape), _whole(wd.shape), _whole((1, D)),
                  _whole((1, D)), _whole(w2.shape), rev(2 * D), rev(D)],
        out_specs=[rev(D), rev(2 * D), _whole((1, D)), _whole((1, D)), _whole((1, D)), _whole((1, D)),
                   _whole((CW, D)), _whole((1, 2 * D)), _whole((1, D))],
        out_shape=[jax.ShapeDtypeStruct((S, D), F32), jax.ShapeDtypeStruct((S, 2 * D), ACT_DTYPE),
                   vec, vec, vec, vec, jax.ShapeDtypeStruct((CW, D), F32),
                   jax.ShapeDtypeStruct((1, 2 * D), F32), vec],
        scratch_shapes=[pltpu.VMEM((tm + HALO, D), F32), pltpu.VMEM((tm + HALO, D), F32),
                        pltpu.VMEM((8, D), F32), pltpu.VMEM((8, D), F32),
                        pltpu.VMEM((8, D), F32), pltpu.VMEM((8, D), F32), pltpu.VMEM((CW, 8, D), F32),
                        pltpu.VMEM((8, 2 * D), F32), pltpu.VMEM((8, D), F32)],
        compiler_params=_params(1),
    )(d, h, g, w1, wd, lg, lb, w2, u, z)


def weight_grad(a, b, ts, name, column_shards=None):
    S, M = a.shape
    N = b.shape[1]
    ta = M if M <= 1024 else 1024
    tb = N if N <= 1024 else 1024
    assert M % ta == 0 and N % tb == 0 and S % ts == 0
    if column_shards is None:
        width, per_tile = tb, 1
        out_spec = pl.BlockSpec((ta, tb), lambda i, j, s: (i, j))
        out_shape = jax.ShapeDtypeStruct((M, N), F32)
    else:
        width = N // column_shards
        per_tile = tb // width
        assert tb % width == 0
        out_spec = pl.BlockSpec((per_tile, ta, width), lambda i, j, s: (j, i, 0))
        out_shape = jax.ShapeDtypeStruct((column_shards, M, width), F32)

    def body(a_ref, b_ref, o_ref):
        @pl.when(pl.program_id(2) == 0)
        def _():
            o_ref[...] = jnp.zeros_like(o_ref)

        res = _mm_tn(a_ref[...], b_ref[...])
        if column_shards is None:
            o_ref[...] += res
        else:
            for d in range(per_tile):
                o_ref[d] += res[:, d * width:(d + 1) * width]

    return pl.pallas_call(
        body, name=name, grid=(M // ta, N // tb, S // ts),
        in_specs=[pl.BlockSpec((ts, ta), lambda i, j, s: (s, i)), pl.BlockSpec((ts, tb), lambda i, j, s: (s, j))],
        out_specs=out_spec, out_shape=out_shape,
        compiler_params=_params(3),
    )(a, b)


def _position():
    return lax.axis_index("x"), lax.axis_index("y"), lax.axis_index("c")


def all_gather(x, name):
    def body(x_ref, out_ref, send_sems, recv_sems, local_sem):
        x, y, c = _position()
        me, sibling = (x, y, c), (x, y, 1 - c)
        chips = [(1 - x, y), (x, 1 - y), (1 - x, 1 - y)]

        def slot(px, py, pc):
            return out_ref.at[4 * px + 2 * py + pc]

        def copy(k, block, to, src=None):
            return pltpu.make_async_remote_copy(
                src_ref=slot(*block) if src is None else src, dst_ref=slot(*block),
                send_sem=send_sems.at[k], recv_sem=recv_sems.at[k], device_id=to, device_id_type=MESH)

        mine = pltpu.make_async_copy(x_ref, slot(*me), local_sem)
        mine.start()
        first = [copy(0, me, sibling, src=x_ref)]
        first += [copy(1 + j, me, (*chip, c), src=x_ref) for j, chip in enumerate(chips)]
        for cp in first:
            cp.start()
        passed = [copy(4 + j, (*chip, c), sibling) for j, chip in enumerate(chips)]
        for j, chip in enumerate(chips):
            copy(1 + j, (*chip, c), me).wait_recv()
            passed[j].start()
        copy(0, sibling, me).wait_recv()
        for j, chip in enumerate(chips):
            copy(4 + j, (*chip, 1 - c), me).wait_recv()
        for cp in first + passed:
            cp.wait_send()
        mine.wait()

    return pl.pallas_call(
        body, name=name,
        in_specs=[pl.BlockSpec(memory_space=pl.ANY)], out_specs=pl.BlockSpec(memory_space=pl.ANY),
        out_shape=jax.ShapeDtypeStruct((N_DEV,) + x.shape, x.dtype),
        scratch_shapes=[pltpu.SemaphoreType.DMA((7,)), pltpu.SemaphoreType.DMA((7,)), pltpu.SemaphoreType.DMA],
    )(x)


def sibling_exchange(gs):
    n = len(gs)

    def body(*refs):
        g_refs, land_refs, (send_sems, recv_sems) = refs[:n], refs[n:2 * n], refs[2 * n:]
        x, y, c = _position()
        copies = [pltpu.make_async_remote_copy(
            src_ref=g_refs[a].at[2 * j + 1 - c], dst_ref=land_refs[a].at[j], send_sem=send_sems.at[N_CHIP * a + j],
            recv_sem=recv_sems.at[N_CHIP * a + j], device_id=(x, y, 1 - c), device_id_type=MESH)
            for a in range(n) for j in range(N_CHIP)]
        for cp in copies:
            cp.start()
        for cp in copies:
            cp.wait()

    return pl.pallas_call(
        body, name="grad_sibling_exchange",
        in_specs=[pl.BlockSpec(memory_space=pl.ANY)] * n, out_specs=[pl.BlockSpec(memory_space=pl.ANY)] * n,
        out_shape=[jax.ShapeDtypeStruct((N_CHIP,) + g.shape[1:], g.dtype) for g in gs],
        scratch_shapes=[pltpu.SemaphoreType.DMA((N_CHIP * n,)), pltpu.SemaphoreType.DMA((N_CHIP * n,))],
    )(*gs)


def chip_partial(g, land, core):
    _, R, C = g.shape
    tr = _row_tile(R, PACK_ROW_TILE)

    def body(c_ref, g_ref, l_ref, o_ref):
        o_ref[...] = (g_ref[...] + l_ref[...]).astype(o_ref.dtype)

    grid_spec = pltpu.PrefetchScalarGridSpec(
        num_scalar_prefetch=1, grid=(N_CHIP, R // tr),
        in_specs=[pl.BlockSpec((None, tr, C), lambda j, i, cr: (2 * j + cr[0], i, 0)),
                  pl.BlockSpec((None, tr, C), lambda j, i, cr: (j, i, 0))],
        out_specs=pl.BlockSpec((None, tr, C), lambda j, i, cr: (j, i, 0)))
    return pl.pallas_call(
        body, name="grad_chip_partial", grid_spec=grid_spec,
        out_shape=jax.ShapeDtypeStruct((N_CHIP, R, C), WIRE_DTYPE),
        compiler_params=_params(2),
    )(core, g, land)


def chip_exchange(parts):
    n = len(parts)

    def body(*refs):
        p_refs, land_refs, (send_sems, recv_sems, local_sems) = refs[:n], refs[n:2 * n], refs[2 * n:]
        x, y, c = _position()
        mychip = 2 * x + y
        chips = [(1 - x, y), (x, 1 - y), (1 - x, 1 - y)]

        def remote(a, k, slot):
            cx, cy = chips[k]
            return pltpu.make_async_remote_copy(
                src_ref=p_refs[a].at[2 * cx + cy], dst_ref=land_refs[a].at[slot], send_sem=send_sems.at[3 * a + k],
                recv_sem=recv_sems.at[3 * a + k], device_id=(cx, cy, c), device_id_type=MESH)

        mine = [pltpu.make_async_copy(p_refs[a].at[mychip], land_refs[a].at[mychip], local_sems.at[a])
                for a in range(n)]
        for cp in mine:
            cp.start()
        copies = [remote(a, k, mychip) for a in range(n) for k in range(3)]
        for cp in copies:
            cp.start()
        for a in range(n):
            for k, (cx, cy) in enumerate(chips):
                remote(a, k, 2 * cx + cy).wait_recv()
        for cp in copies:
            cp.wait_send()
        for cp in mine:
            cp.wait()

    return pl.pallas_call(
        body, name="grad_chip_exchange",
        in_specs=[pl.BlockSpec(memory_space=pl.ANY)] * n, out_specs=[pl.BlockSpec(memory_space=pl.ANY)] * n,
        out_shape=[jax.ShapeDtypeStruct(p.shape, p.dtype) for p in parts],
        scratch_shapes=[pltpu.SemaphoreType.DMA((3 * n,)), pltpu.SemaphoreType.DMA((3 * n,)),
                        pltpu.SemaphoreType.DMA((n,))],
    )(*parts)


def _adamw(w, g, m, v):
    m = ADAM_B1 * m + (1.0 - ADAM_B1) * g
    v = ADAM_B2 * v + (1.0 - ADAM_B2) * jnp.square(g)
    m_hat = m / (1.0 - ADAM_B1 ** ADAM_STEP)
    v_hat = v / (1.0 - ADAM_B2 ** ADAM_STEP)
    delta = -ADAM_LR * (m_hat / (jnp.sqrt(v_hat) + ADAM_EPS) + ADAM_WD * w)
    return delta, m, v


def adamw_sharded(parts, w, m, v):
    R, C = w.shape
    tr = _row_tile(R, PACK_ROW_TILE)

    def body(p_ref, w_ref, m_ref, v_ref, g_ref, d_ref, nm_ref, nv_ref):
        g = p_ref[0].astype(F32)
        for j in range(1, N_CHIP):
            g = g + p_ref[j].astype(F32)
        g_ref[...] = g
        d_ref[...], nm_ref[...], nv_ref[...] = _adamw(w_ref[...], g, m_ref[...], v_ref[...])

    out = jax.ShapeDtypeStruct((R, C), F32)
    return pl.pallas_call(
        body, name="adamw_sharded", grid=(R // tr,),
        in_specs=[pl.BlockSpec((N_CHIP, tr, C), lambda i: (0, i, 0)), _rows(tr, C), _rows(tr, C), _rows(tr, C)],
        out_specs=[_rows(tr, C)] * 4, out_shape=[out] * 4,
        compiler_params=_params(1),
    )(parts, w, m, v)


def adamw_replicated(gathered, w, m, v):
    R, C = w.shape

    def body(p_ref, w_ref, m_ref, v_ref, g_ref, d_ref, nm_ref, nv_ref):
        g = p_ref[0]
        for j in range(1, N_DEV):
            g = g + p_ref[j]
        g_ref[...] = g
        d_ref[...], nm_ref[...], nv_ref[...] = _adamw(w_ref[...], g, m_ref[...], v_ref[...])

    out = jax.ShapeDtypeStruct((R, C), F32)
    return pl.pallas_call(
        body, name="adamw_replicated", grid=(1,),
        in_specs=[_whole(gathered.shape), _whole((R, C)), _whole((R, C)), _whole((R, C))],
        out_specs=[_whole((R, C))] * 4, out_shape=[out] * 4,
        compiler_params=_params(1),
    )(gathered, w, m, v)


def _piece_rows(n):
    return -(-n // PACK_COLS)


def _as_rows(a, lead):
    flat = a.reshape(a.shape[:lead] + (-1,))
    fill = _piece_rows(flat.shape[-1]) * PACK_COLS - flat.shape[-1]
    if fill:
        flat = jnp.pad(flat, [(0, 0)] * lead + [(0, fill)])
    return flat.reshape(flat.shape[:-1] + (-1, PACK_COLS))


def _pack(arrays, rows_multiple, dtype=None, lead=0):
    pieces = [_as_rows(a if dtype is None else a.astype(dtype), lead) for a in arrays]
    extra = -sum(p.shape[lead] for p in pieces) % rows_multiple
    if extra:
        pieces.append(jnp.zeros(pieces[0].shape[:lead] + (extra, PACK_COLS), pieces[0].dtype))
    return jnp.concatenate(pieces, axis=lead)


def _unpack(packed, shapes, lead=0):
    out, r0 = [], 0
    for shp in shapes:
        n = int(np.prod(shp))
        rows = _piece_rows(n)
        seg = lax.slice_in_dim(packed, r0, r0 + rows, axis=lead).reshape(packed.shape[:lead] + (-1,))
        if rows * PACK_COLS != n:
            seg = seg[..., :n]
        out.append(seg.reshape(packed.shape[:lead] + tuple(shp)))
        r0 += rows
    return out


def _full_from_gathered(gathered, shard_shapes, axes):
    out = []
    for seg, shp, ax in zip(_unpack(gathered, shard_shapes, lead=1), shard_shapes, axes):
        seg = jnp.moveaxis(seg, 0, ax)
        out.append(seg.reshape(tuple(shp[:ax]) + (N_DEV * shp[ax],) + tuple(shp[ax + 1:])))
    return out


def kernel(x, p, mix_norm, conv_w_pw1, conv_b_pw1, conv_w_dw, conv_b_dw, conv_ln_g, conv_ln_b, conv_w_pw2, conv_b_pw2, kv_norm, w_kvf, b_f, attn_w_q, attn_w_o, ffn_norm, ffn_w1, ffn_w2, ple_norm, ple_w_gate, ple_w_proj, final_norm, loss_target, m_mix_norm, m_conv_w_pw1, m_conv_b_pw1, m_conv_w_dw, m_conv_b_dw, m_conv_ln_g, m_conv_ln_b, m_conv_w_pw2, m_conv_b_pw2, m_kv_norm, m_w_kvf, m_b_f, m_attn_w_q, m_attn_w_o, m_ffn_norm, m_ffn_w1, m_ffn_w2, m_ple_norm, m_ple_w_gate, m_ple_w_proj, m_final_norm, v_mix_norm, v_conv_w_pw1, v_conv_b_pw1, v_conv_w_dw, v_conv_b_dw, v_conv_ln_g, v_conv_ln_b, v_conv_w_pw2, v_conv_b_pw2, v_kv_norm, v_w_kvf, v_b_f, v_attn_w_q, v_attn_w_o, v_ffn_norm, v_ffn_w1, v_ffn_w2, v_ple_norm, v_ple_w_gate, v_ple_w_proj, v_final_norm):
    given = dict(locals())
    W = {n: given[n] for n in WEIGHTS}
    M = {n: given["m_" + n] for n in WEIGHTS}
    V = {n: given["v_" + n] for n in WEIGHTS}

    _, S, D = x.shape
    NA = conv_w_pw1.shape[0]
    NB = attn_w_q.shape[0]
    DEPTH = NA + NB
    H = b_f.shape[0]
    dh = D // H
    hg = 128 // dh
    G = D // 128
    scale = dh ** -0.5
    tm = _row_tile(S, 256)
    tq_f = _row_tile(S, FLASH_FWD_TILE[0])
    tkc_f = _row_tile(tq_f, FLASH_FWD_TILE[1])
    tk_b = _row_tile(S, FLASH_BWD_TILE[0])
    tqc_b = _row_tile(tk_b, FLASH_BWD_TILE[1])
    ts = _row_tile(S, 2048)
    xs = x[0]
    tgt = loss_target[0]
    ps = p[:, 0]
    row = lambda a: a.reshape(1, -1)

    big_names = list(SHARD_AXIS_BIG)
    small_names = list(SHARD_AXIS_SMALL)
    big = _full_from_gathered(
        all_gather(_pack([W[n] for n in big_names], 16, MXU_DTYPE), "weights_all_gather"),
        [W[n].shape for n in big_names], [SHARD_AXIS_BIG[n] for n in big_names])
    small = _full_from_gathered(
        all_gather(_pack([W[n] for n in small_names], 8), "vectors_all_gather"),
        [W[n].shape for n in small_names], [SHARD_AXIS_SMALL[n] for n in small_names])
    FW = dict(zip(big_names + small_names, big + small))
    wk, wv, wf = FW["w_kvf"][:, :D], FW["w_kvf"][:, D:2 * D], FW["w_kvf"][:, 2 * D:]

    saved = []
    h = xs
    kv = None
    for i in range(DEPTH):
        rec = {"h_in": h}
        if i < NA:
            h, rec["n"], rec["u"], rec["z"], rec["sw"] = conv_fwd(
                h, row(mix_norm[i]), FW["conv_w_pw1"][i], row(FW["conv_b_pw1"][i]), FW["conv_w_dw"][i],
                row(FW["conv_b_dw"][i]), row(FW["conv_ln_g"][i]), row(FW["conv_ln_b"][i]),
                FW["conv_w_pw2"][i], row(FW["conv_b_pw2"][i]), tm)
        else:
            j = i - NA
            if j == 0:
                k_, kT_, vT_, nkv, fl, c = kv_fwd(h, row(kv_norm), wk, wv, wf, row(b_f), tm)
                cg = c.reshape(S, G, hg)
                kv = dict(k=k_, kT=kT_, vT=vT_, n=nkv, fl=fl, h=h, c_col=jnp.transpose(cg, (1, 0, 2)),
                          c_row=jnp.transpose(cg, (1, 2, 0)))
            rec["n"], rec["q"], rec["qT"] = q_fwd(h, row(mix_norm[i]), FW["attn_w_q"][j], scale, tm)
            rec["o"], rec["o32"], rec["lse"] = flash_fwd(rec["qT"], kv["k"], kv["vT"], kv["c_col"], kv["c_row"], dh,
                                                         tq_f, tkc_f)
            h = attn_out_fwd(h, rec["o"], FW["attn_w_o"][j], tm)
        rec["h_ffn"] = h
        h, rec["n_ffn"], rec["a"], rec["s"] = ffn_fwd(h, row(ffn_norm[i]), FW["ffn_w1"][i], FW["ffn_w2"][i], tm)
        rec["h_ple"] = h
        h, rec["n_ple"], rec["gate"] = ple_fwd(h, row(ple_norm[i]), FW["ple_w_gate"][i], ps[i],
                                               FW["ple_w_proj"][i], tm)
        saved.append(rec)

    d, g_final, loss_part = loss_head(h, row(final_norm), tgt, tm)
    GW = {n: [None] * W[n].shape[0] for n in WEIGHTS if W[n].ndim > 1 and n != "w_kvf"}
    dks, dvs, dcs = [], [], []
    for i in reversed(range(DEPTH)):
        rec = saved[i]
        d_out = d
        d, dz, dpp, GW["ple_norm"][i] = ple_bwd(d_out, rec["h_ple"], row(ple_norm[i]), FW["ple_w_gate"][i],
                                                rec["gate"], ps[i], FW["ple_w_proj"][i], tm)
        GW["ple_w_gate"][i] = weight_grad(rec["n_ple"], dz, ts, "grad_ple_w_gate")
        GW["ple_w_proj"][i] = weight_grad(ps[i], dpp, ts, "grad_ple_w_proj", column_shards=N_DEV)
        d_out = d
        d, da, GW["ffn_norm"][i] = ffn_bwd(d_out, rec["h_ffn"], row(ffn_norm[i]), FW["ffn_w1"][i],
                                           FW["ffn_w2"][i], rec["a"], tm)
        GW["ffn_w2"][i] = weight_grad(rec["s"], d_out, ts, "grad_ffn_w2")
        GW["ffn_w1"][i] = weight_grad(rec["n_ffn"], da, ts, "grad_ffn_w1", column_shards=N_DEV)
        d_out = d
        if i >= NA:
            j = i - NA
            GW["attn_w_o"][j] = weight_grad(rec["o"], d_out, ts, "grad_attn_w_o")
            do, doT, delta = attn_out_bwd(d_out, FW["attn_w_o"][j], rec["o32"], dh, tm)
            qstat = jnp.concatenate([kv["c_col"], jnp.transpose(rec["lse"], (0, 2, 1)),
                                     jnp.transpose(delta.reshape(S, G, hg), (1, 0, 2))], axis=2)
            dq, dkT, dvT, dck, dcq = flash_bwd(rec["q"], rec["qT"], kv["k"], kv["kT"], kv["vT"], kv["c_row"], do, doT,
                                          qstat, dh, tk_b, tqc_b)
            dks.append(dkT)
            dvs.append(dvT)
            dcs.append(jnp.transpose(dck[:, :hg, :], (2, 0, 1)).reshape(S, H))
            dcs.append(jnp.transpose(dcq, (1, 0, 2)).reshape(S, H))
            d, dqs, GW["mix_norm"][i] = q_bwd(d_out, dq, rec["h_in"], row(mix_norm[i]), FW["attn_w_q"][j], scale, tm)
            GW["attn_w_q"][j] = weight_grad(rec["n"], dqs, ts, "grad_attn_w_q")
            if j == 0:
                d, dk_sum, dv_sum, dfl, g_kv_norm, g_b_f = kv_bwd(d, dks, dvs, dcs, kv["fl"], kv["h"],
                                                                  row(kv_norm), wk, wv, wf, tm)
                g_w_kvf = jnp.concatenate([weight_grad(kv["n"], dk_sum, ts, "grad_w_k"),
                                           weight_grad(kv["n"], dv_sum, ts, "grad_w_v"),
                                           weight_grad(kv["n"], dfl, ts, "grad_w_f")], axis=1)
        else:
            GW["conv_w_pw2"][i] = weight_grad(rec["sw"], d_out, ts, "grad_conv_w_pw2")
            (d, du, GW["conv_b_pw2"][i], GW["conv_ln_g"][i], GW["conv_ln_b"][i], GW["conv_b_dw"][i],
             GW["conv_w_dw"][i], GW["conv_b_pw1"][i], GW["mix_norm"][i]) = conv_bwd(
                d_out, rec["h_in"], row(mix_norm[i]), FW["conv_w_pw1"][i], FW["conv_w_dw"][i],
                row(FW["conv_ln_g"][i]), row(FW["conv_ln_b"][i]), FW["conv_w_pw2"][i], rec["u"], rec["z"], tm)
            GW["conv_w_pw1"][i] = weight_grad(rec["n"], du, ts, "grad_conv_w_pw1", column_shards=N_DEV)
    grad_x = d[None]

    sharded_names = big_names + small_names
    GW["w_kvf"] = [g_w_kvf]

    def device_major(n, g):
        width = W[n].shape[-1]
        if g.ndim == 3:
            return g
        if g.shape[-1] == width:
            return g.reshape(N_DEV, -1, width)
        return jnp.transpose(g.reshape(-1, N_DEV, width), (1, 0, 2))

    widths = sorted({W[n].shape[-1] for n in sharded_names}, reverse=True)
    groups = [[n for n in sharded_names if W[n].shape[-1] == width] for width in widths]

    def stack_rows(pieces, axis):
        rows = sum(p.shape[axis] for p in pieces)
        fill = -rows % (PACK_ROW_TILE if rows > PACK_ROW_TILE else 8)
        if fill:
            shape = list(pieces[0].shape)
            shape[axis] = fill
            pieces = pieces + [jnp.zeros(shape, pieces[0].dtype)]
        return jnp.concatenate(pieces, axis=axis)

    chunks = [stack_rows([device_major(n, g) for n in names for g in GW[n]], 1) for names in groups]
    core = lax.axis_index("c").astype(jnp.int32).reshape(1)
    landed = sibling_exchange(chunks)
    parts = chip_exchange([chip_partial(g, l, core) for g, l in zip(chunks, landed)])

    res = {}
    for names, width, part in zip(groups, widths, parts):
        group_rows = lambda src: stack_rows([src[n].reshape(-1, width) for n in names], 0)
        outs = adamw_sharded(part, group_rows(W), group_rows(M), group_rows(V))
        for kind, packed in zip(("grad", "delta", "new_m", "new_v"), outs):
            r0 = 0
            for n in names:
                nr = W[n].size // width
                res[kind, n] = packed[r0:r0 + nr].reshape(W[n].shape)
                r0 += nr

    rep_grads = {"mix_norm": jnp.concatenate(GW["mix_norm"], axis=0), "kv_norm": g_kv_norm,
                 "b_f": g_b_f, "ffn_norm": jnp.concatenate(GW["ffn_norm"], axis=0),
                 "ple_norm": jnp.concatenate(GW["ple_norm"], axis=0), "final_norm": g_final}

    def pack_rep(src, extra=None):
        rows_ = [jnp.pad(src[n].reshape(-1, src[n].shape[-1]), ((0, 0), (0, D - src[n].shape[-1])))
                 for n in REPLICATED]
        if extra is not None:
            rows_.append(jnp.pad(extra, ((0, 0), (0, D - extra.shape[-1]))))
        else:
            rows_.append(jnp.zeros((1, D), F32))
        flat = jnp.concatenate(rows_, axis=0)
        return jnp.pad(flat, ((0, -flat.shape[0] % 8), (0, 0)))

    rep_g = all_gather(pack_rep(rep_grads, loss_part), "replicated_all_gather")
    outs_rep = adamw_replicated(rep_g, pack_rep(W), pack_rep(M), pack_rep(V))
    n_rep_rows = sum(int(np.prod(W[n].shape[:-1])) for n in REPLICATED)
    for kind, packed in zip(("grad", "delta", "new_m", "new_v"), outs_rep):
        r0 = 0
        for n in REPLICATED:
            nr = int(np.prod(W[n].shape[:-1]))
            res[kind, n] = packed[r0:r0 + nr, :W[n].shape[-1]].reshape(W[n].shape)
            r0 += nr
    loss = outs_rep[0][n_rep_rows, 0]

    return (loss, grad_x, *[res["grad", n] for n in WEIGHTS], *[res["delta", n] for n in WEIGHTS],
            *[res["new_m", n] for n in WEIGHTS], *[res["new_v", n] for n in WEIGHTS])
```
